```python
import math
import jax, jax.numpy as jnp
from jax import lax
import numpy as np

D_MODEL = 2048
BATCH = 2
SEQ = 4096
DEPTH = 1

EPS = 1e-6
BLOCK_Q = 128
CONV_WIDTH = D_MODEL // 2
CONV_GROUPS = 8
CONV_TAPS = 3
FOX_HEAD_DIM = 128
FOX_HEADS = (D_MODEL // 2) // FOX_HEAD_DIM
FOX_WIDTH = FOX_HEADS * FOX_HEAD_DIM
MEM_TOKENS = 256
MEM_HEADS = 4
MEM_HEAD_DIM = (D_MODEL // 2) // MEM_HEADS
MEM_WIDTH = MEM_HEADS * MEM_HEAD_DIM
N_BRANCHES = 3
D_FF = 4 * D_MODEL
IN_COLS = 3 * CONV_WIDTH + 3 * FOX_WIDTH + FOX_HEADS + MEM_WIDTH + N_BRANCHES * D_MODEL

kernel_name = "hybrid_gated_conv_fox_memxattn_block"


def _rms(x, g):
    xf = x.astype(jnp.float32)
    y = xf * lax.rsqrt(jnp.mean(xf * xf, axis=-1, keepdims=True) + EPS)
    return (y * g.astype(jnp.float32)).astype(x.dtype)


def _causal_dwconv(u, w):
    s = u.shape[1]
    up = jnp.pad(u, ((0, 0), (CONV_TAPS - 1, 0), (0, 0)))
    y = w[0] * up[:, 0:s]
    for i in range(1, CONV_TAPS):
        y = y + w[i] * up[:, i:i + s]
    return y


def _forgetting_attention(q, k, v, log_f):
    b, s, h, d = q.shape
    nb = s // BLOCK_Q
    c = jnp.cumsum(log_f, axis=1).transpose(0, 2, 1)
    qh = q.transpose(0, 2, 1, 3)
    kh = k.transpose(0, 2, 1, 3)
    vh = v.transpose(0, 2, 1, 3)
    q_blocks = qh.reshape(b, h, nb, BLOCK_Q, d).transpose(2, 0, 1, 3, 4)
    c_blocks = c.reshape(b, h, nb, BLOCK_Q).transpose(2, 0, 1, 3)
    pos_blocks = jnp.arange(s).reshape(nb, BLOCK_Q)
    kpos = jnp.arange(s)
    scale = 1.0 / math.sqrt(d)

    def one_block(args):
        qb, cb, pb = args
        logits = jnp.einsum('bhqd,bhkd->bhqk', qb, kh,
                            preferred_element_type=jnp.float32) * scale
        logits = logits + (cb[..., None] - c[:, :, None, :])
        logits = jnp.where(pb[:, None] >= kpos[None, :], logits, -jnp.inf)
        p = jax.nn.softmax(logits, axis=-1)
        return jnp.einsum('bhqk,bhkd->bhqd', p.astype(vh.dtype), vh)

    out = lax.map(one_block, (q_blocks, c_blocks, pos_blocks))
    return out.transpose(1, 0, 3, 2, 4).reshape(b, s, h * d)


def _memory_attention(qm, mem, mem_norm_g, w_mem_kv, mem_q_g, mem_k_g):
    b, s, _, _ = qm.shape
    m = mem.shape[1]
    kv = _rms(mem, mem_norm_g) @ w_mem_kv
    km = kv[..., :MEM_WIDTH].reshape(b, m, MEM_HEADS, MEM_HEAD_DIM)
    vm = kv[..., MEM_WIDTH:].reshape(b, m, MEM_HEADS, MEM_HEAD_DIM)
    qm = _rms(qm, mem_q_g)
    km = _rms(km, mem_k_g)
    logits = jnp.einsum('bshd,bmhd->bhsm', qm, km,
                        preferred_element_type=jnp.float32) / math.sqrt(MEM_HEAD_DIM)
    p = jax.nn.softmax(logits, axis=-1)
    out = jnp.einsum('bhsm,bmhd->bshd', p.astype(vm.dtype), vm)
    return out.reshape(b, s, MEM_WIDTH)


def setup_inputs(seed: int = 0) -> dict:
    key = jax.random.key(seed)
    ks = jax.random.split(key, 20)
    f32 = jnp.float32

    def nrm(k, shape, fan_in):
        return jax.random.normal(k, shape, f32) * (fan_in ** -0.5)

    def gain(k, n):
        return 1.0 + 0.02 * jax.random.normal(k, (n,), f32)

    return {
        "x": jax.random.normal(ks[0], (BATCH, SEQ, D_MODEL), f32),
        "mem": jax.random.normal(ks[1], (BATCH, MEM_TOKENS, D_MODEL), f32),
        "norm1_g": gain(ks[2], D_MODEL),
        "w_in": nrm(ks[3], (D_MODEL, IN_COLS), D_MODEL),
        "b_f": jax.random.uniform(ks[4], (FOX_HEADS,), f32, minval=1.0, maxval=4.0),
        "conv_w": nrm(ks[5], (CONV_TAPS, CONV_WIDTH), CONV_TAPS),
        "fox_q_g": gain(ks[6], FOX_HEAD_DIM),
        "fox_k_g": gain(ks[7], FOX_HEAD_DIM),
        "mem_norm_g": gain(ks[8], D_MODEL),
        "w_mem_kv": nrm(ks[9], (D_MODEL, 2 * MEM_WIDTH), D_MODEL),
        "mem_q_g": gain(ks[10], MEM_HEAD_DIM),
        "mem_k_g": gain(ks[11], MEM_HEAD_DIM),
        "w_conv_out": nrm(ks[12], (CONV_WIDTH, D_MODEL), CONV_WIDTH),
        "w_fox_out": nrm(ks[13], (FOX_WIDTH, D_MODEL), FOX_WIDTH),
        "w_mem_out": nrm(ks[14], (MEM_WIDTH, D_MODEL), MEM_WIDTH),
        "w_out": nrm(ks[15], (D_MODEL, D_MODEL), D_MODEL),
        "norm2_g": gain(ks[16], D_MODEL),
        "w_up": nrm(ks[17], (D_MODEL, D_FF), D_MODEL),
        "w_down": nrm(ks[18], (D_FF, D_MODEL), D_FF),
    }


def reference(x, mem, norm1_g, w_in, b_f, conv_w, fox_q_g, fox_k_g, mem_norm_g,
              w_mem_kv, mem_q_g, mem_k_g, w_conv_out, w_fox_out, w_mem_out,
              w_out, norm2_g, w_up, w_down):
    b, s, _ = x.shape
    for _layer in range(DEPTH):
        h = _rms(x, norm1_g)
        proj = h @ w_in
        o = 0
        conv_bg = proj[..., o:o + CONV_WIDTH]; o += CONV_WIDTH
        conv_cg = proj[..., o:o + CONV_WIDTH]; o += CONV_WIDTH
        conv_v = proj[..., o:o + CONV_WIDTH]; o += CONV_WIDTH
        fq = proj[..., o:o + FOX_WIDTH]; o += FOX_WIDTH
        fk = proj[..., o:o + FOX_WIDTH]; o += FOX_WIDTH
        fv = proj[..., o:o + FOX_WIDTH]; o += FOX_WIDTH
        f_logit = proj[..., o:o + FOX_HEADS]; o += FOX_HEADS
        mq = proj[..., o:o + MEM_WIDTH]; o += MEM_WIDTH
        gate_logit = proj[..., o:o + N_BRANCHES * D_MODEL]

        y_conv = conv_bg * _causal_dwconv(conv_cg * conv_v, conv_w)
        o_conv = y_conv @ w_conv_out

        fq = _rms(fq.reshape(b, s, FOX_HEADS, FOX_HEAD_DIM), fox_q_g)
        fk = _rms(fk.reshape(b, s, FOX_HEADS, FOX_HEAD_DIM), fox_k_g)
        fv = fv.reshape(b, s, FOX_HEADS, FOX_HEAD_DIM)
        log_f = jax.nn.log_sigmoid(f_logit.astype(jnp.float32) + b_f.astype(jnp.float32))
        o_fox = _forgetting_attention(fq, fk, fv, log_f) @ w_fox_out

        mq = mq.reshape(b, s, MEM_HEADS, MEM_HEAD_DIM)
        o_mem = _memory_attention(mq, mem, mem_norm_g, w_mem_kv, mem_q_g, mem_k_g) @ w_mem_out

        g = jax.nn.sigmoid(gate_logit).reshape(b, s, N_BRANCHES, D_MODEL)
        merged = g[:, :, 0] * o_conv + g[:, :, 1] * o_fox + g[:, :, 2] * o_mem
        x = x + merged @ w_out

        h2 = _rms(x, norm2_g)
        x = x + jnp.square(jax.nn.relu(h2 @ w_up)) @ w_down
    return x
```

```python
import functools
import math

import jax
import jax.numpy as jnp
from jax import lax
from jax.experimental import pallas as pl
from jax.experimental.pallas import tpu as pltpu

F32 = jnp.float32
BF16 = jnp.bfloat16
HIGHEST = lax.Precision.HIGHEST

EPS = 1e-6
LANES = 128
CONV_TAPS = 3
FOX_HEADS = 8
FOX_HEAD_DIM = 128
MEM_HEADS = 4
MEM_HEAD_DIM = 256
N_BRANCHES = 3

VMEM_LIMIT_BYTES = 56 * 1024 * 1024


def _params(n_axes):
    return pltpu.CompilerParams(
        dimension_semantics=("arbitrary",) * n_axes,
        vmem_limit_bytes=VMEM_LIMIT_BYTES)


def _group_rms(a, gain, width, scale=1.0):
    outs = []
    for s in range(0, a.shape[1], width):
        blk = a[:, s:s + width]
        ms = jnp.mean(blk * blk, axis=-1, keepdims=True)
        outs.append(blk * lax.rsqrt(ms + EPS) * (gain * scale))
    return outs


def _split3(c):
    hi = c.astype(BF16).astype(F32)
    r1 = c - hi
    mid = r1.astype(BF16).astype(F32)
    lo = (r1 - mid).astype(BF16).astype(F32)
    return hi, mid, lo


def _norm1_kernel(x_ref, g_ref, wf_ref, bf_ref, h_ref, qa_ref, ka_ref, carry_ref):
    tm = x_ref.shape[0]

    @pl.when(pl.program_id(1) == 0)
    def _():
        carry_ref[...] = jnp.zeros_like(carry_ref)

    xf = x_ref[...]
    ms = jnp.mean(xf * xf, axis=-1, keepdims=True)
    y = xf * lax.rsqrt(ms + EPS) * g_ref[...]
    h_ref[...] = y.astype(BF16)

    z = jnp.dot(y, wf_ref[...], preferred_element_type=F32, precision=HIGHEST) + bf_ref[...]
    log_f = -(jnp.maximum(-z, 0.0) + jnp.log1p(jnp.exp(-jnp.abs(z))))
    row = lax.broadcasted_iota(jnp.int32, (tm, tm), 0)
    col = lax.broadcasted_iota(jnp.int32, (tm, tm), 1)
    tri = (row >= col).astype(F32)
    c = jnp.dot(tri, log_f, preferred_element_type=F32, precision=HIGHEST) + carry_ref[0:1, :]
    carry_ref[0:1, :] = c[tm - 1:tm, :]

    lane = lax.broadcasted_iota(jnp.int32, (tm, LANES), 1)
    for hd in range(FOX_HEADS):
        hi, mid, lo = _split3(c[:, hd:hd + 1])
        ones = jnp.where(lane < 6, 1.0, 0.0)
        qa = jnp.where(lane == 0, hi, jnp.where(lane == 1, mid, jnp.where(lane == 2, lo, ones)))
        ka = jnp.where(lane == 3, -hi, jnp.where(lane == 4, -mid, jnp.where(lane == 5, -lo, ones)))
        qa_ref[:, hd * LANES:(hd + 1) * LANES] = qa.astype(BF16)
        ka_ref[:, hd * LANES:(hd + 1) * LANES] = ka.astype(BF16)


def _norm1(x, g, wf, bf, tm=512):
    b, s, d = x.shape
    return pl.pallas_call(
        _norm1_kernel,
        grid=(b, s // tm),
        in_specs=[
            pl.BlockSpec((None, tm, d), lambda i, j: (i, j, 0)),
            pl.BlockSpec((1, d), lambda i, j: (0, 0)),
            pl.BlockSpec((d, LANES), lambda i, j: (0, 0)),
            pl.BlockSpec((1, LANES), lambda i, j: (0, 0)),
        ],
        out_specs=[
            pl.BlockSpec((None, tm, d), lambda i, j: (i, j, 0)),
            pl.BlockSpec((None, tm, FOX_HEADS * LANES), lambda i, j: (i, j, 0)),
            pl.BlockSpec((None, tm, FOX_HEADS * LANES), lambda i, j: (i, j, 0)),
        ],
        out_shape=[
            jax.ShapeDtypeStruct((b, s, d), BF16),
            jax.ShapeDtypeStruct((b, s, FOX_HEADS * LANES), BF16),
            jax.ShapeDtypeStruct((b, s, FOX_HEADS * LANES), BF16),
        ],
        scratch_shapes=[pltpu.VMEM((8, LANES), F32)],
        compiler_params=_params(2),
        name="norm1_decay",
    )(x, g, wf, bf)


def _proj_kernel(h_ref, w_ref, qg_ref, kg_ref, mg_ref, o_ref, wbf_ref, *, tile0):
    @pl.when(pl.program_id(1) == 0)
    def _():
        wbf_ref[...] = w_ref[...].astype(BF16)

    acc = jnp.dot(h_ref[...], wbf_ref[...], preferred_element_type=F32)
    t = pl.program_id(0) + tile0

    @pl.when(jnp.logical_or(t <= 2, t == 5))
    def _():
        o_ref[...] = acc.astype(BF16)

    @pl.when(t == 3)
    def _():
        parts = _group_rms(acc, qg_ref[...], FOX_HEAD_DIM, 1.0 / math.sqrt(FOX_HEAD_DIM))
        for i, p in enumerate(parts):
            o_ref[:, i * FOX_HEAD_DIM:(i + 1) * FOX_HEAD_DIM] = p.astype(BF16)

    @pl.when(t == 4)
    def _():
        parts = _group_rms(acc, kg_ref[...], FOX_HEAD_DIM)
        for i, p in enumerate(parts):
            o_ref[:, i * FOX_HEAD_DIM:(i + 1) * FOX_HEAD_DIM] = p.astype(BF16)

    @pl.when(t == 6)
    def _():
        parts = _group_rms(acc, mg_ref[...], MEM_HEAD_DIM, 1.0 / math.sqrt(MEM_HEAD_DIM))
        for i, p in enumerate(parts):
            o_ref[:, i * MEM_HEAD_DIM:(i + 1) * MEM_HEAD_DIM] = p.astype(BF16)

    @pl.when(t >= 7)
    def _():
        o_ref[...] = jax.nn.sigmoid(acc).astype(BF16)


def _proj(h, w, qg, kg, mg, n_tiles, tile0, tm=1024, tn=1024):
    m, k = h.shape
    return pl.pallas_call(
        functools.partial(_proj_kernel, tile0=tile0),
        grid=(n_tiles, m // tm),
        in_specs=[
            pl.BlockSpec((tm, k), lambda j, i: (i, 0)),
            pl.BlockSpec((k, tn), lambda j, i: (0, j)),
            pl.BlockSpec((1, FOX_HEAD_DIM), lambda j, i: (0, 0)),
            pl.BlockSpec((1, FOX_HEAD_DIM), lambda j, i: (0, 0)),
            pl.BlockSpec((1, MEM_HEAD_DIM), lambda j, i: (0, 0)),
        ],
        out_specs=pl.BlockSpec((tm, tn), lambda j, i: (i, j)),
        out_shape=jax.ShapeDtypeStruct((m, n_tiles * tn), BF16),
        scratch_shapes=[pltpu.VMEM((k, tn), BF16)],
        compiler_params=_params(2),
        name=f"in_proj_{tile0}",
    )(h, w, qg, kg, mg)


def _conv_kernel(bg_ref, cg_ref, v_ref, w_ref, o_ref, carry_ref):
    tm = bg_ref.shape[0]

    @pl.when(pl.program_id(1) == 0)
    def _():
        carry_ref[...] = jnp.zeros_like(carry_ref)

    u = cg_ref[...].astype(F32) * v_ref[...].astype(F32)
    row = lax.broadcasted_iota(jnp.int32, (tm, 1), 0)
    prev1 = carry_ref[1:2, :]
    prev2 = carry_ref[0:1, :]
    u1 = jnp.where(row == 0, prev1, pltpu.roll(u, 1, 0))
    u2 = jnp.where(row == 0, prev2, jnp.where(row == 1, prev1, pltpu.roll(u, 2, 0)))
    w = w_ref[...]
    y = bg_ref[...].astype(F32) * (w[0:1, :] * u2 + w[1:2, :] * u1 + w[2:3, :] * u)
    o_ref[...] = y.astype(BF16)
    carry_ref[0:2, :] = u[tm - 2:tm, :]


def _conv(proj, conv_w, tm=512):
    b, s, _ = proj.shape
    cw = conv_w.shape[1]
    return pl.pallas_call(
        _conv_kernel,
        grid=(b, s // tm),
        in_specs=[
            pl.BlockSpec((None, tm, cw), lambda i, j: (i, j, 0)),
            pl.BlockSpec((None, tm, cw), lambda i, j: (i, j, 1)),
            pl.BlockSpec((None, tm, cw), lambda i, j: (i, j, 2)),
            pl.BlockSpec((CONV_TAPS, cw), lambda i, j: (0, 0)),
        ],
        out_specs=pl.BlockSpec((None, tm, cw), lambda i, j: (i, j, 0)),
        out_shape=jax.ShapeDtypeStruct((b, s, cw), BF16),
        scratch_shapes=[pltpu.VMEM((8, cw), F32)],
        compiler_params=_params(2),
        name="gated_conv",
    )(proj, proj, proj, conv_w)


def _fox_kernel(q_ref, qa_ref, k_ref, ka_ref, v_ref, o_ref, m_ref, l_ref, acc_ref, *, tk):
    tq = q_ref.shape[0]
    qi = pl.program_id(2)
    q = jnp.concatenate([q_ref[...], qa_ref[...]], axis=1)
    m_ref[...] = jnp.full_like(m_ref, -jnp.inf)
    l_ref[...] = jnp.zeros_like(l_ref)
    acc_ref[...] = jnp.zeros_like(acc_ref)

    def step(kb, masked):
        rows = pl.ds(pl.multiple_of(kb * tk, tk), tk)
        k = jnp.concatenate([k_ref[rows, :], ka_ref[rows, :]], axis=1)
        s = lax.dot_general(q, k, (((1,), (1,)), ((), ())), preferred_element_type=F32)
        if masked:
            r = lax.broadcasted_iota(jnp.int32, (tq, tk), 0)
            c = lax.broadcasted_iota(jnp.int32, (tq, tk), 1)
            s = jnp.where(r >= c, s, -jnp.inf)
        m_prev = m_ref[...]
        m_new = jnp.maximum(m_prev, jnp.max(s, axis=-1, keepdims=True))
        alpha = jnp.exp(m_prev - m_new)
        p = jnp.exp(s - m_new)
        l_ref[...] = alpha * l_ref[...] + jnp.sum(p, axis=-1, keepdims=True)
        pv = jnp.dot(p.astype(BF16), v_ref[rows, :], preferred_element_type=F32)
        acc_ref[...] = alpha * acc_ref[...] + pv
        m_ref[...] = m_new

    def body(kb, carry):
        step(kb, False)
        return carry

    lax.fori_loop(0, qi, body, 0)
    step(qi, True)
    o_ref[...] = (acc_ref[...] / l_ref[...]).astype(BF16)


def _fox(proj, qa, ka, q_tile, k_tile, v_tile, tq=512):
    b, s, _ = proj.shape
    d = FOX_HEAD_DIM
    per_tile = 1024 // d
    return pl.pallas_call(
        functools.partial(_fox_kernel, tk=tq),
        grid=(b, FOX_HEADS, s // tq),
        in_specs=[
            pl.BlockSpec((None, tq, d), lambda bi, h, qi: (bi, qi, q_tile * per_tile + h)),
            pl.BlockSpec((None, tq, LANES), lambda bi, h, qi: (bi, qi, h)),
            pl.BlockSpec((None, s, d), lambda bi, h, qi: (bi, 0, k_tile * per_tile + h)),
            pl.BlockSpec((None, s, LANES), lambda bi, h, qi: (bi, 0, h)),
            pl.BlockSpec((None, s, d), lambda bi, h, qi: (bi, 0, v_tile * per_tile + h)),
        ],
        out_specs=pl.BlockSpec((None, tq, d), lambda bi, h, qi: (bi, qi, h)),
        out_shape=jax.ShapeDtypeStruct((b, s, FOX_HEADS * d), BF16),
        scratch_shapes=[pltpu.VMEM((tq, 1), F32), pltpu.VMEM((tq, 1), F32),
                        pltpu.VMEM((tq, d), F32)],
        compiler_params=_params(3),
        name="fox_attention",
    )(proj, qa, proj, ka, proj)


def _memkv_kernel(mem_ref, g_ref, w_ref, kg_ref, o_ref):
    xf = mem_ref[...]
    ms = jnp.mean(xf * xf, axis=-1, keepdims=True)
    y = (xf * lax.rsqrt(ms + EPS) * g_ref[...]).astype(BF16)
    acc = jnp.dot(y, w_ref[...].astype(BF16), preferred_element_type=F32)

    @pl.when(pl.program_id(0) == 0)
    def _():
        parts = _group_rms(acc, kg_ref[...], MEM_HEAD_DIM)
        for i, p in enumerate(parts):
            o_ref[:, i * MEM_HEAD_DIM:(i + 1) * MEM_HEAD_DIM] = p.astype(BF16)

    @pl.when(pl.program_id(0) == 1)
    def _():
        o_ref[...] = acc.astype(BF16)


def _memkv(mem2d, g, w, kg):
    m, d = mem2d.shape
    width = MEM_HEADS * MEM_HEAD_DIM
    return pl.pallas_call(
        _memkv_kernel,
        grid=(2,),
        in_specs=[
            pl.BlockSpec((m, d), lambda j: (0, 0)),
            pl.BlockSpec((1, d), lambda j: (0, 0)),
            pl.BlockSpec((d, width), lambda j: (0, j)),
            pl.BlockSpec((1, MEM_HEAD_DIM), lambda j: (0, 0)),
        ],
        out_specs=pl.BlockSpec((m, width), lambda j: (0, j)),
        out_shape=jax.ShapeDtypeStruct((m, 2 * width), BF16),
        compiler_params=_params(1),
        name="mem_kv",
    )(mem2d, g, w, kg)


def _memattn_kernel(q_ref, kv_ref, o_ref):
    width = MEM_HEADS * MEM_HEAD_DIM
    for h in range(MEM_HEADS):
        lo, hi = h * MEM_HEAD_DIM, (h + 1) * MEM_HEAD_DIM
        q = q_ref[:, lo:hi]
        k = kv_ref[:, lo:hi]
        v = kv_ref[:, width + lo:width + hi]
        s = lax.dot_general(q, k, (((1,), (1,)), ((), ())), preferred_element_type=F32)
        p = jnp.exp(s - jnp.max(s, axis=-1, keepdims=True))
        l = jnp.sum(p, axis=-1, keepdims=True)
        o = jnp.dot(p.astype(BF16), v, preferred_element_type=F32) / l
        o_ref[:, lo:hi] = o.astype(BF16)


def _memattn(proj, kv, q_tile, tm=512):
    b, s, _ = proj.shape
    width = MEM_HEADS * MEM_HEAD_DIM
    n_mem = kv.shape[1]
    return pl.pallas_call(
        _memattn_kernel,
        grid=(b, s // tm),
        in_specs=[
            pl.BlockSpec((None, tm, width), lambda i, j: (i, j, q_tile)),
            pl.BlockSpec((None, n_mem, 2 * width), lambda i, j: (i, 0, 0)),
        ],
        out_specs=pl.BlockSpec((None, tm, width), lambda i, j: (i, j, 0)),
        out_shape=jax.ShapeDtypeStruct((b, s, width), BF16),
        compiler_params=_params(2),
        name="mem_attention",
    )(proj, kv)


def _merge_kernel(a_ref, b_ref, c_ref, wa_ref, wb_ref, wc_ref, ga_ref, gb_ref, gc_ref,
                  o_ref, wabf_ref, wbbf_ref, wcbf_ref):
    @pl.when(pl.program_id(1) == 0)
    def _():
        wabf_ref[...] = wa_ref[...].astype(BF16)
        wbbf_ref[...] = wb_ref[...].astype(BF16)
        wcbf_ref[...] = wc_ref[...].astype(BF16)

    oa = jnp.dot(a_ref[...], wabf_ref[...], preferred_element_type=F32)
    ob = jnp.dot(b_ref[...], wbbf_ref[...], preferred_element_type=F32)
    oc = jnp.dot(c_ref[...], wcbf_ref[...], preferred_element_type=F32)
    merged = (ga_ref[...].astype(F32) * oa + gb_ref[...].astype(F32) * ob
              + gc_ref[...].astype(F32) * oc)
    o_ref[...] = merged.astype(BF16)


def _merge(ya, yb, yc, wa, wb, wc, gates, gate_col, d_model, tm=1024, tn=512):
    m, k = ya.shape
    nt = d_model // tn
    g0 = gate_col // tn
    lhs = pl.BlockSpec((tm, k), lambda j, i: (i, 0))
    wsp = pl.BlockSpec((k, tn), lambda j, i: (0, j))
    return pl.pallas_call(
        _merge_kernel,
        grid=(nt, m // tm),
        in_specs=[lhs, lhs, lhs, wsp, wsp, wsp,
                  pl.BlockSpec((tm, tn), lambda j, i: (i, g0 + j)),
                  pl.BlockSpec((tm, tn), lambda j, i: (i, g0 + nt + j)),
                  pl.BlockSpec((tm, tn), lambda j, i: (i, g0 + 2 * nt + j))],
        out_specs=pl.BlockSpec((tm, tn), lambda j, i: (i, j)),
        out_shape=jax.ShapeDtypeStruct((m, d_model), BF16),
        scratch_shapes=[pltpu.VMEM((k, tn), BF16)] * 3,
        compiler_params=_params(2),
        name="branch_merge",
    )(ya, yb, yc, wa, wb, wc, gates, gates, gates)


def _outproj_kernel(a_ref, w_ref, x_ref, g_ref, x2_ref, h2_ref, wbf_ref):
    @pl.when(pl.program_id(0) == 0)
    def _():
        wbf_ref[...] = w_ref[...].astype(BF16)

    x2 = x_ref[...] + jnp.dot(a_ref[...], wbf_ref[...], preferred_element_type=F32)
    x2_ref[...] = x2
    ms = jnp.mean(x2 * x2, axis=-1, keepdims=True)
    h2_ref[...] = (x2 * lax.rsqrt(ms + EPS) * g_ref[...]).astype(BF16)


def _outproj(a, w, x2d, g, tm=256):
    m, d = x2d.shape
    row = pl.BlockSpec((tm, d), lambda i: (i, 0))
    return pl.pallas_call(
        _outproj_kernel,
        grid=(m // tm,),
        in_specs=[row, pl.BlockSpec((d, d), lambda i: (0, 0)), row,
                  pl.BlockSpec((1, d), lambda i: (0, 0))],
        out_specs=[row, row],
        out_shape=[jax.ShapeDtypeStruct((m, d), F32), jax.ShapeDtypeStruct((m, d), BF16)],
        scratch_shapes=[pltpu.VMEM((d, d), BF16)],
        compiler_params=_params(1),
        name="out_proj_norm2",
    )(a, w, x2d, g)


def _mlp_kernel(h_ref, wu_ref, wd_ref, x_ref, o_ref):
    f = pl.program_id(1)
    rows = x_ref.shape[0]

    @pl.when(f == 0)
    def _():
        o_ref[...] = jnp.zeros_like(o_ref)

    a = jnp.dot(h_ref[...], wu_ref[...].astype(BF16), preferred_element_type=F32)
    a = jnp.square(jnp.maximum(a, 0.0)).astype(BF16)
    o_ref[...] += jnp.dot(a, wd_ref[...].astype(BF16), preferred_element_type=F32)
    slab = pl.ds(pl.multiple_of(f * rows, rows), rows)
    o_ref[slab, :] += x_ref[...]


def _mlp(h2, w_up, w_down, x2, tm=1024, tf=512):
    m, d = h2.shape
    d_ff = w_up.shape[1]
    nf = d_ff // tf
    rows = tm // nf
    return pl.pallas_call(
        _mlp_kernel,
        grid=(m // tm, nf),
        in_specs=[
            pl.BlockSpec((tm, d), lambda i, f: (i, 0)),
            pl.BlockSpec((d, tf), lambda i, f: (0, f)),
            pl.BlockSpec((tf, d), lambda i, f: (f, 0)),
            pl.BlockSpec((rows, d), lambda i, f: (i * nf + f, 0)),
        ],
        out_specs=pl.BlockSpec((tm, d), lambda i, f: (i, 0)),
        out_shape=jax.ShapeDtypeStruct((m, d), F32),
        compiler_params=_params(2),
        name="relu2_mlp",
    )(h2, w_up, w_down, x2)


def kernel(x, mem, norm1_g, w_in, b_f, conv_w, fox_q_g, fox_k_g, mem_norm_g, w_mem_kv, mem_q_g, mem_k_g, w_conv_out, w_fox_out, w_mem_out, w_out, norm2_g, w_up, w_down):
    b, s, d = x.shape
    m = b * s
    conv_width = conv_w.shape[1]
    fox_width = FOX_HEADS * FOX_HEAD_DIM
    mem_width = MEM_HEADS * MEM_HEAD_DIM
    f_col = 3 * conv_width + 3 * fox_width
    tail_col = f_col + FOX_HEADS
    n_head_tiles = f_col // 1024
    n_tail_tiles = (mem_width + N_BRANCHES * d) // 1024

    row = lambda v: v.reshape(1, -1)
    wf = jnp.pad(w_in[:, f_col:tail_col], ((0, 0), (0, LANES - FOX_HEADS)))
    bf = jnp.pad(b_f, (0, LANES - FOX_HEADS)).reshape(1, LANES)
    w_tail = w_in[:, tail_col:]

    h1, qa, ka = _norm1(x, row(norm1_g), wf, bf)
    h1 = h1.reshape(m, d)
    gains = (row(fox_q_g), row(fox_k_g), row(mem_q_g))
    proj_head = _proj(h1, w_in, *gains, n_tiles=n_head_tiles, tile0=0).reshape(b, s, -1)
    proj_tail = _proj(h1, w_tail, *gains, n_tiles=n_tail_tiles, tile0=n_head_tiles)

    y_conv = _conv(proj_head, conv_w)
    y_fox = _fox(proj_head, qa, ka, q_tile=3, k_tile=4, v_tile=5)
    kv = _memkv(mem.reshape(-1, d), row(mem_norm_g), w_mem_kv, row(mem_k_g))
    y_mem = _memattn(proj_tail.reshape(b, s, -1), kv.reshape(b, -1, 2 * mem_width), q_tile=0)

    merged = _merge(y_conv.reshape(m, -1), y_fox.reshape(m, -1), y_mem.reshape(m, -1),
                    w_conv_out, w_fox_out, w_mem_out, proj_tail, mem_width, d)
    x2, h2 = _outproj(merged, w_out, x.reshape(m, d), row(norm2_g))
    out = _mlp(h2, w_up, w_down, x2)
    return out.reshape(b, s, d)
```

```python
import functools
import math

import jax
import jax.numpy as jnp
import numpy as np
from jax import lax
from jax.experimental import pallas as pl
from jax.experimental.pallas import tpu as pltpu

F32 = jnp.float32
BF16 = jnp.bfloat16
HIGHEST = lax.Precision.HIGHEST

EPS = 1e-6
LANES = 128
CONV_TAPS = 3
FOX_HEADS = 8
FOX_HEAD_DIM = 128
MEM_HEADS = 4
MEM_HEAD_DIM = 256
N_BRANCHES = 3

VMEM_LIMIT_BYTES = 56 * 1024 * 1024


def _params(n_axes):
    return pltpu.CompilerParams(
        dimension_semantics=("arbitrary",) * n_axes,
        vmem_limit_bytes=VMEM_LIMIT_BYTES)


def _group_rms(a, gain, width, scale=1.0):
    outs = []
    for s in range(0, a.shape[1], width):
        blk = a[:, s:s + width]
        ms = jnp.mean(blk * blk, axis=-1, keepdims=True)
        outs.append(blk * lax.rsqrt(ms + EPS) * (gain * scale))
    return outs


def _split3(c):
    hi = c.astype(BF16).astype(F32)
    r1 = c - hi
    mid = r1.astype(BF16).astype(F32)
    lo = (r1 - mid).astype(BF16).astype(F32)
    return hi, mid, lo


def _decay_selector():
    sel = np.zeros((LANES, 2, FOX_HEADS, LANES), np.float32)
    one = 3 * FOX_HEADS
    for hd in range(FOX_HEADS):
        for piece in range(3):
            sel[piece * FOX_HEADS + hd, 0, hd, piece] = 1.0
            sel[one, 0, hd, 3 + piece] = 1.0
            sel[one, 1, hd, piece] = 1.0
            sel[piece * FOX_HEADS + hd, 1, hd, 3 + piece] = -1.0
    return jnp.asarray(sel.reshape(LANES, -1), BF16)


def _norm1_kernel(x_ref, g_ref, wf_ref, bf_ref, sel_ref, h_ref, qa_ref, ka_ref, carry_ref):
    tm = x_ref.shape[0]

    @pl.when(pl.program_id(1) == 0)
    def _():
        carry_ref[...] = jnp.zeros_like(carry_ref)

    xf = x_ref[...]
    ms = jnp.mean(xf * xf, axis=-1, keepdims=True)
    y = xf * lax.rsqrt(ms + EPS) * g_ref[...]
    y_hi = y.astype(BF16)
    h_ref[...] = y_hi
    y_lo = (y - y_hi.astype(F32)).astype(BF16)

    wf = wf_ref[...]
    wf_hi = wf.astype(BF16).astype(F32)
    w2 = jnp.concatenate(
        [wf_hi, wf - wf_hi, jnp.zeros((LANES - 2 * FOX_HEADS, wf.shape[1]), F32)], axis=0).astype(BF16)
    nt = (((1,), (1,)), ((), ()))
    z2 = (lax.dot_general(y_hi, w2, nt, preferred_element_type=F32)
          + lax.dot_general(y_lo, w2, nt, preferred_element_type=F32))
    z = z2 + pltpu.roll(z2, LANES - FOX_HEADS, 1) + bf_ref[...]
    log_f = -(jnp.maximum(-z, 0.0) + jnp.log1p(jnp.exp(-jnp.abs(z))))

    row = lax.broadcasted_iota(jnp.int32, (tm, tm), 0)
    col = lax.broadcasted_iota(jnp.int32, (tm, tm), 1)
    tri = jnp.where(row >= col, 1.0, 0.0).astype(BF16)
    f_hi, f_mid, f_lo = _split3(log_f)
    part = jnp.dot(tri, jnp.concatenate([f_hi, f_mid], axis=1).astype(BF16),
                   preferred_element_type=F32)
    c = (part[:, :LANES] + part[:, LANES:]
         + jnp.dot(tri, f_lo.astype(BF16), preferred_element_type=F32) + carry_ref[0:1, :])
    carry_ref[0:1, :] = c[tm - 1:tm, :]

    lane = lax.broadcasted_iota(jnp.int32, (tm, LANES), 1)
    c_hi, c_mid, c_lo = _split3(c)
    packed = jnp.where(
        lane < FOX_HEADS, c_hi,
        jnp.where(lane < 2 * FOX_HEADS, pltpu.roll(c_mid, FOX_HEADS, 1),
                  jnp.where(lane < 3 * FOX_HEADS, pltpu.roll(c_lo, 2 * FOX_HEADS, 1),
                            jnp.where(lane == 3 * FOX_HEADS, 1.0, 0.0))))
    aug = jnp.dot(packed.astype(BF16), sel_ref[...], preferred_element_type=F32)
    width = qa_ref.shape[1]
    qa_ref[...] = aug[:, :width].astype(BF16)
    ka_ref[...] = aug[:, width:].astype(BF16)


def _norm1(x, g, w_t, f_row, bf, tm=512):
    b, s, d = x.shape
    return pl.pallas_call(
        _norm1_kernel,
        grid=(b, s // tm),
        in_specs=[
            pl.BlockSpec((None, tm, d), lambda i, j: (i, j, 0)),
            pl.BlockSpec((1, d), lambda i, j: (0, 0)),
            pl.BlockSpec((FOX_HEADS, d), lambda i, j: (f_row // FOX_HEADS, 0)),
            pl.BlockSpec((1, LANES), lambda i, j: (0, 0)),
            pl.BlockSpec((LANES, 2 * FOX_HEADS * LANES), lambda i, j: (0, 0)),
        ],
        out_specs=[
            pl.BlockSpec((None, tm, d), lambda i, j: (i, j, 0)),
            pl.BlockSpec((None, tm, FOX_HEADS * LANES), lambda i, j: (i, j, 0)),
            pl.BlockSpec((None, tm, FOX_HEADS * LANES), lambda i, j: (i, j, 0)),
        ],
        out_shape=[
            jax.ShapeDtypeStruct((b, s, d), BF16),
            jax.ShapeDtypeStruct((b, s, FOX_HEADS * LANES), BF16),
            jax.ShapeDtypeStruct((b, s, FOX_HEADS * LANES), BF16),
        ],
        scratch_shapes=[pltpu.VMEM((8, LANES), F32)],
        compiler_params=_params(2),
        name="norm1_decay",
    )(x, g, w_t, bf, _decay_selector())


def _proj_kernel(h_ref, w_ref, qg_ref, kg_ref, mg_ref, o_ref, wbf_ref):
    @pl.when(pl.program_id(1) == 0)
    def _():
        wbf_ref[...] = w_ref[...].astype(BF16)

    acc = lax.dot_general(h_ref[...], wbf_ref[...], (((1,), (1,)), ((), ())),
                          preferred_element_type=F32)
    t = pl.program_id(0)

    @pl.when(jnp.logical_or(t <= 2, t == 5))
    def _():
        o_ref[...] = acc.astype(BF16)

    @pl.when(t == 3)
    def _():
        parts = _group_rms(acc, qg_ref[...], FOX_HEAD_DIM, 1.0 / math.sqrt(FOX_HEAD_DIM))
        for i, p in enumerate(parts):
            o_ref[:, i * FOX_HEAD_DIM:(i + 1) * FOX_HEAD_DIM] = p.astype(BF16)

    @pl.when(t == 4)
    def _():
        parts = _group_rms(acc, kg_ref[...], FOX_HEAD_DIM)
        for i, p in enumerate(parts):
            o_ref[:, i * FOX_HEAD_DIM:(i + 1) * FOX_HEAD_DIM] = p.astype(BF16)

    @pl.when(t == 6)
    def _():
        parts = _group_rms(acc, mg_ref[...], MEM_HEAD_DIM, 1.0 / math.sqrt(MEM_HEAD_DIM))
        for i, p in enumerate(parts):
            o_ref[:, i * MEM_HEAD_DIM:(i + 1) * MEM_HEAD_DIM] = p.astype(BF16)

    @pl.when(t >= 7)
    def _():
        o_ref[...] = (0.5 * jnp.tanh(0.5 * acc) + 0.5).astype(BF16)


def _proj(h, w_t, qg, kg, mg, skip_row, skip, tm=1024, tn=1024):
    m, k = h.shape
    n_tiles = (w_t.shape[0] - skip) // tn
    first_shifted = skip_row // tn

    def w_rows(j, i):
        return (pl.multiple_of(j * tn + jnp.where(j >= first_shifted, skip, 0), skip), 0)

    return pl.pallas_call(
        _proj_kernel,
        grid=(n_tiles, m // tm),
        in_specs=[
            pl.BlockSpec((tm, k), lambda j, i: (i, 0)),
            pl.BlockSpec((pl.Element(tn), pl.Element(k)), w_rows),
            pl.BlockSpec((1, FOX_HEAD_DIM), lambda j, i: (0, 0)),
            pl.BlockSpec((1, FOX_HEAD_DIM), lambda j, i: (0, 0)),
            pl.BlockSpec((1, MEM_HEAD_DIM), lambda j, i: (0, 0)),
        ],
        out_specs=pl.BlockSpec((tm, tn), lambda j, i: (i, j)),
        out_shape=jax.ShapeDtypeStruct((m, n_tiles * tn), BF16),
        scratch_shapes=[pltpu.VMEM((tn, k), BF16)],
        compiler_params=_params(2),
        name="in_proj",
    )(h, w_t, qg, kg, mg)


def _conv_kernel(bg_ref, cg_ref, v_ref, w_ref, o_ref, carry_ref):
    tm = bg_ref.shape[0]

    @pl.when(pl.program_id(1) == 0)
    def _():
        carry_ref[...] = jnp.zeros_like(carry_ref)

    u = cg_ref[...].astype(F32) * v_ref[...].astype(F32)
    row = lax.broadcasted_iota(jnp.int32, (tm, 1), 0)
    prev1 = carry_ref[1:2, :]
    prev2 = carry_ref[0:1, :]
    u1 = jnp.where(row == 0, prev1, pltpu.roll(u, 1, 0))
    u2 = jnp.where(row == 0, prev2, jnp.where(row == 1, prev1, pltpu.roll(u, 2, 0)))
    w = w_ref[...]
    y = bg_ref[...].astype(F32) * (w[0:1, :] * u2 + w[1:2, :] * u1 + w[2:3, :] * u)
    o_ref[...] = y.astype(BF16)
    carry_ref[0:2, :] = u[tm - 2:tm, :]


def _conv(proj, conv_w, tm=512):
    b, s, _ = proj.shape
    cw = conv_w.shape[1]
    return pl.pallas_call(
        _conv_kernel,
        grid=(b, s // tm),
        in_specs=[
            pl.BlockSpec((None, tm, cw), lambda i, j: (i, j, 0)),
            pl.BlockSpec((None, tm, cw), lambda i, j: (i, j, 1)),
            pl.BlockSpec((None, tm, cw), lambda i, j: (i, j, 2)),
            pl.BlockSpec((CONV_TAPS, cw), lambda i, j: (0, 0)),
        ],
        out_specs=pl.BlockSpec((None, tm, cw), lambda i, j: (i, j, 0)),
        out_shape=jax.ShapeDtypeStruct((b, s, cw), BF16),
        scratch_shapes=[pltpu.VMEM((8, cw), F32)],
        compiler_params=_params(2),
        name="gated_conv",
    )(proj, proj, proj, conv_w)


def _fox_kernel(q_ref, qa_ref, k_ref, ka_ref, v_ref, o_ref, m_ref, l_ref, acc_ref, *, tk):
    tq = q_ref.shape[0]
    qi = pl.program_id(2)
    q = jnp.concatenate([q_ref[...], qa_ref[...]], axis=1)
    m_ref[...] = jnp.full_like(m_ref, -jnp.inf)
    l_ref[...] = jnp.zeros_like(l_ref)
    acc_ref[...] = jnp.zeros_like(acc_ref)

    def step(kb, masked):
        rows = pl.ds(pl.multiple_of(kb * tk, tk), tk)
        k = jnp.concatenate([k_ref[rows, :], ka_ref[rows, :]], axis=1)
        s = lax.dot_general(q, k, (((1,), (1,)), ((), ())), preferred_element_type=F32)
        if masked:
            r = lax.broadcasted_iota(jnp.int32, (tq, tk), 0)
            c = lax.broadcasted_iota(jnp.int32, (tq, tk), 1)
            s = jnp.where(r >= c, s, -jnp.inf)
        m_prev = m_ref[...]
        m_new = jnp.maximum(m_prev, jnp.max(s, axis=-1, keepdims=True))
        alpha = jnp.exp(m_prev - m_new)
        p = jnp.exp(s - m_new)
        l_ref[...] = alpha * l_ref[...] + jnp.sum(p, axis=-1, keepdims=True)
        pv = jnp.dot(p.astype(BF16), v_ref[rows, :], preferred_element_type=F32)
        acc_ref[...] = alpha * acc_ref[...] + pv
        m_ref[...] = m_new

    def body(kb, carry):
        step(kb, False)
        return carry

    lax.fori_loop(0, qi, body, 0)
    step(qi, True)
    o_ref[...] = (acc_ref[...] / l_ref[...]).astype(BF16)


def _fox(proj, qa, ka, q_tile, k_tile, v_tile, tq=512):
    b, s, _ = proj.shape
    d = FOX_HEAD_DIM
    per_tile = 1024 // d
    return pl.pallas_call(
        functools.partial(_fox_kernel, tk=tq),
        grid=(b, FOX_HEADS, s // tq),
        in_specs=[
            pl.BlockSpec((None, tq, d), lambda bi, h, qi: (bi, qi, q_tile * per_tile + h)),
            pl.BlockSpec((None, tq, LANES), lambda bi, h, qi: (bi, qi, h)),
            pl.BlockSpec((None, s, d), lambda bi, h, qi: (bi, 0, k_tile * per_tile + h)),
            pl.BlockSpec((None, s, LANES), lambda bi, h, qi: (bi, 0, h)),
            pl.BlockSpec((None, s, d), lambda bi, h, qi: (bi, 0, v_tile * per_tile + h)),
        ],
        out_specs=pl.BlockSpec((None, tq, d), lambda bi, h, qi: (bi, qi, h)),
        out_shape=jax.ShapeDtypeStruct((b, s, FOX_HEADS * d), BF16),
        scratch_shapes=[pltpu.VMEM((tq, 1), F32), pltpu.VMEM((tq, 1), F32),
                        pltpu.VMEM((tq, d), F32)],
        compiler_params=_params(3),
        name="fox_attention",
    )(proj, qa, proj, ka, proj)


def _memkv_kernel(mem_ref, g_ref, w_ref, kg_ref, o_ref):
    xf = mem_ref[...]
    ms = jnp.mean(xf * xf, axis=-1, keepdims=True)
    y = (xf * lax.rsqrt(ms + EPS) * g_ref[...]).astype(BF16)
    acc = jnp.dot(y, w_ref[...].astype(BF16), preferred_element_type=F32)

    @pl.when(pl.program_id(0) == 0)
    def _():
        parts = _group_rms(acc, kg_ref[...], MEM_HEAD_DIM)
        for i, p in enumerate(parts):
            o_ref[:, i * MEM_HEAD_DIM:(i + 1) * MEM_HEAD_DIM] = p.astype(BF16)

    @pl.when(pl.program_id(0) == 1)
    def _():
        o_ref[...] = acc.astype(BF16)


def _memkv(mem2d, g, w, kg):
    m, d = mem2d.shape
    width = MEM_HEADS * MEM_HEAD_DIM
    return pl.pallas_call(
        _memkv_kernel,
        grid=(2,),
        in_specs=[
            pl.BlockSpec((m, d), lambda j: (0, 0)),
            pl.BlockSpec((1, d), lambda j: (0, 0)),
            pl.BlockSpec((d, width), lambda j: (0, j)),
            pl.BlockSpec((1, MEM_HEAD_DIM), lambda j: (0, 0)),
        ],
        out_specs=pl.BlockSpec((m, width), lambda j: (0, j)),
        out_shape=jax.ShapeDtypeStruct((m, 2 * width), BF16),
        compiler_params=_params(1),
        name="mem_kv",
    )(mem2d, g, w, kg)


def _memattn_kernel(q_ref, kv_ref, o_ref):
    width = MEM_HEADS * MEM_HEAD_DIM
    for h in range(MEM_HEADS):
        lo, hi = h * MEM_HEAD_DIM, (h + 1) * MEM_HEAD_DIM
        q = q_ref[:, lo:hi]
        k = kv_ref[:, lo:hi]
        v = kv_ref[:, width + lo:width + hi]
        s = lax.dot_general(q, k, (((1,), (1,)), ((), ())), preferred_element_type=F32)
        p = jnp.exp(s - jnp.max(s, axis=-1, keepdims=True))
        l = jnp.sum(p, axis=-1, keepdims=True)
        o = jnp.dot(p.astype(BF16), v, preferred_element_type=F32) / l
        o_ref[:, lo:hi] = o.astype(BF16)


def _memattn(proj, kv, q_tile, tm=512):
    b, s, _ = proj.shape
    width = MEM_HEADS * MEM_HEAD_DIM
    n_mem = kv.shape[1]
    return pl.pallas_call(
        _memattn_kernel,
        grid=(b, s // tm),
        in_specs=[
            pl.BlockSpec((None, tm, width), lambda i, j: (i, j, q_tile)),
            pl.BlockSpec((None, n_mem, 2 * width), lambda i, j: (i, 0, 0)),
        ],
        out_specs=pl.BlockSpec((None, tm, width), lambda i, j: (i, j, 0)),
        out_shape=jax.ShapeDtypeStruct((b, s, width), BF16),
        compiler_params=_params(2),
        name="mem_attention",
    )(proj, kv)


def _merge_kernel(a_ref, b_ref, c_ref, wa_ref, wb_ref, wc_ref, ga_ref, gb_ref, gc_ref,
                  o_ref, wabf_ref, wbbf_ref, wcbf_ref):
    @pl.when(pl.program_id(1) == 0)
    def _():
        wabf_ref[...] = wa_ref[...].astype(BF16)
        wbbf_ref[...] = wb_ref[...].astype(BF16)
        wcbf_ref[...] = wc_ref[...].astype(BF16)

    oa = jnp.dot(a_ref[...], wabf_ref[...], preferred_element_type=F32)
    ob = jnp.dot(b_ref[...], wbbf_ref[...], preferred_element_type=F32)
    oc = jnp.dot(c_ref[...], wcbf_ref[...], preferred_element_type=F32)
    merged = (ga_ref[...].astype(F32) * oa + gb_ref[...].astype(F32) * ob
              + gc_ref[...].astype(F32) * oc)
    o_ref[...] = merged.astype(BF16)


def _merge(ya, yb, yc, wa, wb, wc, gates, gate_col, d_model, tm=1024, tn=512):
    m, k = ya.shape
    nt = d_model // tn
    g0 = gate_col // tn
    lhs = pl.BlockSpec((tm, k), lambda j, i: (i, 0))
    wsp = pl.BlockSpec((k, tn), lambda j, i: (0, j))
    return pl.pallas_call(
        _merge_kernel,
        grid=(nt, m // tm),
        in_specs=[lhs, lhs, lhs, wsp, wsp, wsp,
                  pl.BlockSpec((tm, tn), lambda j, i: (i, g0 + j)),
                  pl.BlockSpec((tm, tn), lambda j, i: (i, g0 + nt + j)),
                  pl.BlockSpec((tm, tn), lambda j, i: (i, g0 + 2 * nt + j))],
        out_specs=pl.BlockSpec((tm, tn), lambda j, i: (i, j)),
        out_shape=jax.ShapeDtypeStruct((m, d_model), BF16),
        scratch_shapes=[pltpu.VMEM((k, tn), BF16)] * 3,
        compiler_params=_params(2),
        name="branch_merge",
    )(ya, yb, yc, wa, wb, wc, gates, gates, gates)


def _outproj_kernel(a_ref, w_ref, x_ref, g_ref, x2_ref, h2_ref, wbf_ref):
    @pl.when(pl.program_id(0) == 0)
    def _():
        wbf_ref[...] = w_ref[...].astype(BF16)

    x2 = x_ref[...] + jnp.dot(a_ref[...], wbf_ref[...], preferred_element_type=F32)
    x2_ref[...] = x2
    ms = jnp.mean(x2 * x2, axis=-1, keepdims=True)
    h2_ref[...] = (x2 * lax.rsqrt(ms + EPS) * g_ref[...]).astype(BF16)


def _outproj(a, w, x2d, g, tm=256):
    m, d = x2d.shape
    row = pl.BlockSpec((tm, d), lambda i: (i, 0))
    return pl.pallas_call(
        _outproj_kernel,
        grid=(m // tm,),
        in_specs=[row, pl.BlockSpec((d, d), lambda i: (0, 0)), row,
                  pl.BlockSpec((1, d), lambda i: (0, 0))],
        out_specs=[row, row],
        out_shape=[jax.ShapeDtypeStruct((m, d), F32), jax.ShapeDtypeStruct((m, d), BF16)],
        scratch_shapes=[pltpu.VMEM((d, d), BF16)],
        compiler_params=_params(1),
        name="out_proj_norm2",
    )(a, w, x2d, g)


def _mlp_kernel(h_ref, wu_ref, wd_ref, x_ref, o_ref):
    f = pl.program_id(1)
    rows = x_ref.shape[0]

    @pl.when(f == 0)
    def _():
        o_ref[...] = jnp.zeros_like(o_ref)

    a = jnp.dot(h_ref[...], wu_ref[...].astype(BF16), preferred_element_type=F32)
    a = jnp.square(jnp.maximum(a, 0.0)).astype(BF16)
    o_ref[...] += jnp.dot(a, wd_ref[...].astype(BF16), preferred_element_type=F32)
    slab = pl.ds(pl.multiple_of(f * rows, rows), rows)
    o_ref[slab, :] += x_ref[...]


def _mlp(h2, w_up, w_down, x2, tm=1024, tf=512):
    m, d = h2.shape
    d_ff = w_up.shape[1]
    nf = d_ff // tf
    rows = tm // nf
    return pl.pallas_call(
        _mlp_kernel,
        grid=(m // tm, nf),
        in_specs=[
            pl.BlockSpec((tm, d), lambda i, f: (i, 0)),
            pl.BlockSpec((d, tf), lambda i, f: (0, f)),
            pl.BlockSpec((tf, d), lambda i, f: (f, 0)),
            pl.BlockSpec((rows, d), lambda i, f: (i * nf + f, 0)),
        ],
        out_specs=pl.BlockSpec((tm, d), lambda i, f: (i, 0)),
        out_shape=jax.ShapeDtypeStruct((m, d), F32),
        compiler_params=_params(2),
        name="relu2_mlp",
    )(h2, w_up, w_down, x2)


def kernel(x, mem, norm1_g, w_in, b_f, conv_w, fox_q_g, fox_k_g, mem_norm_g, w_mem_kv, mem_q_g, mem_k_g, w_conv_out, w_fox_out, w_mem_out, w_out, norm2_g, w_up, w_down):
    b, s, d = x.shape
    m = b * s
    conv_width = conv_w.shape[1]
    fox_width = FOX_HEADS * FOX_HEAD_DIM
    mem_width = MEM_HEADS * MEM_HEAD_DIM
    f_col = 3 * conv_width + 3 * fox_width
    mq_tile = f_col // 1024
    gate_col = f_col + mem_width

    row = lambda v: v.reshape(1, -1)
    bf = jnp.pad(b_f, (0, LANES - FOX_HEADS)).reshape(1, LANES)
    w_t = w_in.T

    h1, qa, ka = _norm1(x, row(norm1_g), w_t, f_col, bf)
    proj = _proj(h1.reshape(m, d), w_t, row(fox_q_g), row(fox_k_g), row(mem_q_g),
                 skip_row=f_col, skip=FOX_HEADS)
    proj3 = proj.reshape(b, s, -1)

    y_conv = _conv(proj3, conv_w)
    y_fox = _fox(proj3, qa, ka, q_tile=3, k_tile=4, v_tile=5)
    kv = _memkv(mem.reshape(-1, d), row(mem_norm_g), w_mem_kv, row(mem_k_g))
    y_mem = _memattn(proj3, kv.reshape(b, -1, 2 * mem_width), q_tile=mq_tile)

    merged = _merge(y_conv.reshape(m, -1), y_fox.reshape(m, -1), y_mem.reshape(m, -1),
                    w_conv_out, w_fox_out, w_mem_out, proj, gate_col, d)
    x2, h2 = _outproj(merged, w_out, x.reshape(m, d), row(norm2_g))
    out = _mlp(h2, w_up, w_down, x2)
    return out.reshape(b, s, d)
```

```python
import functools
import math

import jax
import jax.numpy as jnp
import numpy as np
from jax import lax
from jax.experimental import pallas as pl
from jax.experimental.pallas import tpu as pltpu

F32 = jnp.float32
BF16 = jnp.bfloat16
HIGHEST = lax.Precision.HIGHEST

EPS = 1e-6
LOG2_E = math.log2(math.e)
LANES = 128
CONV_TAPS = 3
FOX_HEADS = 8
FOX_HEAD_DIM = 128
MEM_HEADS = 4
MEM_HEAD_DIM = 256
N_BRANCHES = 3

VMEM_LIMIT_BYTES = 56 * 1024 * 1024


def _params(n_axes):
    return pltpu.CompilerParams(
        dimension_semantics=("arbitrary",) * n_axes,
        vmem_limit_bytes=VMEM_LIMIT_BYTES)


def _group_rms(a, gain, width, scale=1.0):
    outs = []
    for s in range(0, a.shape[1], width):
        blk = a[:, s:s + width]
        ms = jnp.mean(blk * blk, axis=-1, keepdims=True)
        outs.append(blk * lax.rsqrt(ms + EPS) * (gain * scale))
    return outs


def _split3(c):
    hi = c.astype(BF16).astype(F32)
    r1 = c - hi
    mid = r1.astype(BF16).astype(F32)
    lo = (r1 - mid).astype(BF16).astype(F32)
    return hi, mid, lo


def _decay_selector():
    sel = np.zeros((LANES, 2, FOX_HEADS, LANES), np.float32)
    one = 3 * FOX_HEADS
    for hd in range(FOX_HEADS):
        for piece in range(3):
            sel[piece * FOX_HEADS + hd, 0, hd, piece] = 1.0
            sel[one, 0, hd, 3 + piece] = 1.0
            sel[one, 1, hd, piece] = 1.0
            sel[piece * FOX_HEADS + hd, 1, hd, 3 + piece] = -1.0
    return jnp.asarray(sel.reshape(LANES, -1), BF16)


def _norm1_kernel(x_ref, g_ref, wf_ref, bf_ref, sel_ref, h_ref, qa_ref, ka_ref, carry_ref):
    tm = x_ref.shape[0]

    @pl.when(pl.program_id(1) == 0)
    def _():
        carry_ref[...] = jnp.zeros_like(carry_ref)

    xf = x_ref[...]
    ms = jnp.mean(xf * xf, axis=-1, keepdims=True)
    y = xf * lax.rsqrt(ms + EPS) * g_ref[...]
    y_hi = y.astype(BF16)
    h_ref[...] = y_hi
    y_lo = (y - y_hi.astype(F32)).astype(BF16)

    wf = wf_ref[...]
    wf_hi = wf.astype(BF16).astype(F32)
    w2 = jnp.concatenate(
        [wf_hi, wf - wf_hi, jnp.zeros((LANES - 2 * FOX_HEADS, wf.shape[1]), F32)], axis=0).astype(BF16)
    nt = (((1,), (1,)), ((), ()))
    z2 = (lax.dot_general(y_hi, w2, nt, preferred_element_type=F32)
          + lax.dot_general(y_lo, w2, nt, preferred_element_type=F32))
    z = z2 + pltpu.roll(z2, LANES - FOX_HEADS, 1) + bf_ref[...]
    log_f = -LOG2_E * (jnp.maximum(-z, 0.0) + jnp.log1p(jnp.exp(-jnp.abs(z))))

    row = lax.broadcasted_iota(jnp.int32, (tm, tm), 0)
    col = lax.broadcasted_iota(jnp.int32, (tm, tm), 1)
    tri = jnp.where(row >= col, 1.0, 0.0).astype(BF16)
    f_hi, f_mid, f_lo = _split3(log_f)
    part = jnp.dot(tri, jnp.concatenate([f_hi, f_mid], axis=1).astype(BF16),
                   preferred_element_type=F32)
    c = (part[:, :LANES] + part[:, LANES:]
         + jnp.dot(tri, f_lo.astype(BF16), preferred_element_type=F32) + carry_ref[0:1, :])
    carry_ref[0:1, :] = c[tm - 1:tm, :]

    lane = lax.broadcasted_iota(jnp.int32, (tm, LANES), 1)
    c_hi, c_mid, c_lo = _split3(c)
    packed = jnp.where(
        lane < FOX_HEADS, c_hi,
        jnp.where(lane < 2 * FOX_HEADS, pltpu.roll(c_mid, FOX_HEADS, 1),
                  jnp.where(lane < 3 * FOX_HEADS, pltpu.roll(c_lo, 2 * FOX_HEADS, 1),
                            jnp.where(lane == 3 * FOX_HEADS, 1.0, 0.0))))
    aug = jnp.dot(packed.astype(BF16), sel_ref[...], preferred_element_type=F32)
    width = qa_ref.shape[1]
    qa_ref[...] = aug[:, :width].astype(BF16)
    ka_ref[...] = aug[:, width:].astype(BF16)


def _norm1(x, g, w_t, f_row, bf, tm=512):
    b, s, d = x.shape
    return pl.pallas_call(
        _norm1_kernel,
        grid=(b, s // tm),
        in_specs=[
            pl.BlockSpec((None, tm, d), lambda i, j: (i, j, 0)),
            pl.BlockSpec((1, d), lambda i, j: (0, 0)),
            pl.BlockSpec((FOX_HEADS, d), lambda i, j: (f_row // FOX_HEADS, 0)),
            pl.BlockSpec((1, LANES), lambda i, j: (0, 0)),
            pl.BlockSpec((LANES, 2 * FOX_HEADS * LANES), lambda i, j: (0, 0)),
        ],
        out_specs=[
            pl.BlockSpec((None, tm, d), lambda i, j: (i, j, 0)),
            pl.BlockSpec((None, tm, FOX_HEADS * LANES), lambda i, j: (i, j, 0)),
            pl.BlockSpec((None, tm, FOX_HEADS * LANES), lambda i, j: (i, j, 0)),
        ],
        out_shape=[
            jax.ShapeDtypeStruct((b, s, d), BF16),
            jax.ShapeDtypeStruct((b, s, FOX_HEADS * LANES), BF16),
            jax.ShapeDtypeStruct((b, s, FOX_HEADS * LANES), BF16),
        ],
        scratch_shapes=[pltpu.VMEM((8, LANES), F32)],
        compiler_params=_params(2),
        name="norm1_decay",
    )(x, g, w_t, bf, _decay_selector())


def _proj_kernel(h_ref, w_ref, qg_ref, kg_ref, mg_ref, o_ref, wbf_ref):
    @pl.when(pl.program_id(1) == 0)
    def _():
        wbf_ref[...] = w_ref[...].astype(BF16)

    acc = lax.dot_general(h_ref[...], wbf_ref[...], (((1,), (1,)), ((), ())),
                          preferred_element_type=F32)
    t = pl.program_id(0)

    @pl.when(jnp.logical_or(t <= 2, t == 5))
    def _():
        o_ref[...] = acc.astype(BF16)

    @pl.when(t == 3)
    def _():
        parts = _group_rms(acc, qg_ref[...], FOX_HEAD_DIM, LOG2_E / math.sqrt(FOX_HEAD_DIM))
        for i, p in enumerate(parts):
            o_ref[:, i * FOX_HEAD_DIM:(i + 1) * FOX_HEAD_DIM] = p.astype(BF16)

    @pl.when(t == 4)
    def _():
        parts = _group_rms(acc, kg_ref[...], FOX_HEAD_DIM)
        for i, p in enumerate(parts):
            o_ref[:, i * FOX_HEAD_DIM:(i + 1) * FOX_HEAD_DIM] = p.astype(BF16)

    @pl.when(t == 6)
    def _():
        parts = _group_rms(acc, mg_ref[...], MEM_HEAD_DIM, 1.0 / math.sqrt(MEM_HEAD_DIM))
        for i, p in enumerate(parts):
            o_ref[:, i * MEM_HEAD_DIM:(i + 1) * MEM_HEAD_DIM] = p.astype(BF16)

    @pl.when(t >= 7)
    def _():
        o_ref[...] = (0.5 * jnp.tanh(0.5 * acc) + 0.5).astype(BF16)


def _proj(h, w_t, qg, kg, mg, skip_row, skip, tm=1024, tn=1024):
    m, k = h.shape
    n_tiles = (w_t.shape[0] - skip) // tn
    first_shifted = skip_row // tn

    def w_rows(j, i):
        return (pl.multiple_of(j * tn + jnp.where(j >= first_shifted, skip, 0), skip), 0)

    return pl.pallas_call(
        _proj_kernel,
        grid=(n_tiles, m // tm),
        in_specs=[
            pl.BlockSpec((tm, k), lambda j, i: (i, 0)),
            pl.BlockSpec((pl.Element(tn), pl.Element(k)), w_rows),
            pl.BlockSpec((1, FOX_HEAD_DIM), lambda j, i: (0, 0)),
            pl.BlockSpec((1, FOX_HEAD_DIM), lambda j, i: (0, 0)),
            pl.BlockSpec((1, MEM_HEAD_DIM), lambda j, i: (0, 0)),
        ],
        out_specs=pl.BlockSpec((tm, tn), lambda j, i: (i, j)),
        out_shape=jax.ShapeDtypeStruct((m, n_tiles * tn), BF16),
        scratch_shapes=[pltpu.VMEM((tn, k), BF16)],
        compiler_params=_params(2),
        name="in_proj",
    )(h, w_t, qg, kg, mg)


def _conv_kernel(bg_ref, cg_ref, v_ref, w_ref, o_ref, carry_ref):
    tm = bg_ref.shape[0]

    @pl.when(pl.program_id(1) == 0)
    def _():
        carry_ref[...] = jnp.zeros_like(carry_ref)

    u = cg_ref[...].astype(F32) * v_ref[...].astype(F32)
    row = lax.broadcasted_iota(jnp.int32, (tm, 1), 0)
    prev1 = carry_ref[1:2, :]
    prev2 = carry_ref[0:1, :]
    u1 = jnp.where(row == 0, prev1, pltpu.roll(u, 1, 0))
    u2 = jnp.where(row == 0, prev2, jnp.where(row == 1, prev1, pltpu.roll(u, 2, 0)))
    w = w_ref[...]
    y = bg_ref[...].astype(F32) * (w[0:1, :] * u2 + w[1:2, :] * u1 + w[2:3, :] * u)
    o_ref[...] = y.astype(BF16)
    carry_ref[0:2, :] = u[tm - 2:tm, :]


def _conv(proj, conv_w, tm=512):
    b, s, _ = proj.shape
    cw = conv_w.shape[1]
    return pl.pallas_call(
        _conv_kernel,
        grid=(b, s // tm),
        in_specs=[
            pl.BlockSpec((None, tm, cw), lambda i, j: (i, j, 0)),
            pl.BlockSpec((None, tm, cw), lambda i, j: (i, j, 1)),
            pl.BlockSpec((None, tm, cw), lambda i, j: (i, j, 2)),
            pl.BlockSpec((CONV_TAPS, cw), lambda i, j: (0, 0)),
        ],
        out_specs=pl.BlockSpec((None, tm, cw), lambda i, j: (i, j, 0)),
        out_shape=jax.ShapeDtypeStruct((b, s, cw), BF16),
        scratch_shapes=[pltpu.VMEM((8, cw), F32)],
        compiler_params=_params(2),
        name="gated_conv",
    )(proj, proj, proj, conv_w)


def _fox_kernel(q_ref, qa_ref, k_ref, ka_ref, v_ref, o_ref, vt_ref, s_ref, m_ref, l_ref, acc_ref):
    t = q_ref.shape[0]
    d = FOX_HEAD_DIM
    n_heads = q_ref.shape[1] // d
    qi = pl.program_id(2)
    nt = (((1,), (1,)), ((), ()))

    @pl.when(qi == 0)
    def _():
        for g in range(n_heads):
            for j in range(v_ref.shape[0] // t):
                blk = v_ref[j * t:(j + 1) * t, g * d:(g + 1) * d]
                vt_ref[g, j] = blk.astype(F32).T.astype(BF16)

    m_ref[...] = jnp.full_like(m_ref, -jnp.inf)
    l_ref[...] = jnp.zeros_like(l_ref)
    acc_ref[...] = jnp.zeros_like(acc_ref)
    qs = [jnp.concatenate([q_ref[:, g * d:(g + 1) * d], qa_ref[:, g * LANES:(g + 1) * LANES]], axis=1)
          for g in range(n_heads)]

    def scores(kb, g):
        rows = pl.ds(pl.multiple_of(kb * t, t), t)
        k = jnp.concatenate([k_ref[rows, g * d:(g + 1) * d],
                             ka_ref[rows, g * LANES:(g + 1) * LANES]], axis=1)
        return lax.dot_general(k, qs[g], nt, preferred_element_type=F32)

    def consume(kb, slot, g, diagonal):
        s = s_ref[slot, g]
        if diagonal:
            kpos = lax.broadcasted_iota(jnp.int32, (t, t), 0)
            qpos = lax.broadcasted_iota(jnp.int32, (t, t), 1)
            s = jnp.where(qpos >= kpos, s, -jnp.inf)
        m_prev = m_ref[g]
        m_new = jnp.maximum(m_prev, jnp.max(s, axis=0, keepdims=True))
        alpha = jnp.exp2(m_prev - m_new)
        p = jnp.exp2(s - m_new)
        l_ref[g] = alpha * l_ref[g] + jnp.sum(p, axis=0, keepdims=True)
        pv = jnp.dot(vt_ref[g, kb], p.astype(BF16), preferred_element_type=F32)
        acc_ref[g] = alpha * acc_ref[g] + pv
        m_ref[g] = m_new

    def advance(kb, slot):
        for g in range(n_heads):
            s_ref[1 - slot, g] = scores(kb + 1, g)
            consume(kb, slot, g, False)

    def finish(slot):
        for g in range(n_heads):
            consume(qi, slot, g, True)
            o_ref[:, g * d:(g + 1) * d] = (acc_ref[g] / l_ref[g]).T.astype(BF16)

    for g in range(n_heads):
        s_ref[0, g] = scores(0, g)

    def body(pair, carry):
        advance(2 * pair, 0)
        advance(2 * pair + 1, 1)
        return carry

    lax.fori_loop(0, qi // 2, body, 0)

    @pl.when(qi % 2 == 1)
    def _():
        advance(qi - 1, 0)
        finish(1)

    @pl.when(qi % 2 == 0)
    def _():
        finish(0)


def _fox(proj, qa, ka, q_tile, k_tile, v_tile, tq=512, heads_per_step=4):
    b, s, _ = proj.shape
    d = FOX_HEAD_DIM
    w = heads_per_step * d
    per_tile = 1024 // w
    tk = tq
    return pl.pallas_call(
        _fox_kernel,
        grid=(b, FOX_HEADS // heads_per_step, s // tq),
        in_specs=[
            pl.BlockSpec((None, tq, w), lambda bi, h, qi: (bi, qi, q_tile * per_tile + h)),
            pl.BlockSpec((None, tq, w), lambda bi, h, qi: (bi, qi, h)),
            pl.BlockSpec((None, s, w), lambda bi, h, qi: (bi, 0, k_tile * per_tile + h)),
            pl.BlockSpec((None, s, w), lambda bi, h, qi: (bi, 0, h)),
            pl.BlockSpec((None, s, w), lambda bi, h, qi: (bi, 0, v_tile * per_tile + h)),
        ],
        out_specs=pl.BlockSpec((None, tq, w), lambda bi, h, qi: (bi, qi, h)),
        out_shape=jax.ShapeDtypeStruct((b, s, FOX_HEADS * d), BF16),
        scratch_shapes=[pltpu.VMEM((heads_per_step, s // tk, d, tk), BF16),
                        pltpu.VMEM((2, heads_per_step, tk, tq), F32),
                        pltpu.VMEM((heads_per_step, 1, tq), F32),
                        pltpu.VMEM((heads_per_step, 1, tq), F32),
                        pltpu.VMEM((heads_per_step, d, tq), F32)],
        compiler_params=_params(3),
        name="fox_attention",
    )(proj, qa, proj, ka, proj)


def _memkv_kernel(mem_ref, g_ref, w_ref, kg_ref, o_ref):
    xf = mem_ref[...]
    ms = jnp.mean(xf * xf, axis=-1, keepdims=True)
    y = (xf * lax.rsqrt(ms + EPS) * g_ref[...]).astype(BF16)
    acc = jnp.dot(y, w_ref[...].astype(BF16), preferred_element_type=F32)

    @pl.when(pl.program_id(0) == 0)
    def _():
        parts = _group_rms(acc, kg_ref[...], MEM_HEAD_DIM)
        for i, p in enumerate(parts):
            o_ref[:, i * MEM_HEAD_DIM:(i + 1) * MEM_HEAD_DIM] = p.astype(BF16)

    @pl.when(pl.program_id(0) == 1)
    def _():
        o_ref[...] = acc.astype(BF16)


def _memkv(mem2d, g, w, kg):
    m, d = mem2d.shape
    width = MEM_HEADS * MEM_HEAD_DIM
    return pl.pallas_call(
        _memkv_kernel,
        grid=(2,),
        in_specs=[
            pl.BlockSpec((m, d), lambda j: (0, 0)),
            pl.BlockSpec((1, d), lambda j: (0, 0)),
            pl.BlockSpec((d, width), lambda j: (0, j)),
            pl.BlockSpec((1, MEM_HEAD_DIM), lambda j: (0, 0)),
        ],
        out_specs=pl.BlockSpec((m, width), lambda j: (0, j)),
        out_shape=jax.ShapeDtypeStruct((m, 2 * width), BF16),
        compiler_params=_params(1),
        name="mem_kv",
    )(mem2d, g, w, kg)


def _memattn_kernel(q_ref, kv_ref, o_ref):
    width = MEM_HEADS * MEM_HEAD_DIM
    for h in range(MEM_HEADS):
        lo, hi = h * MEM_HEAD_DIM, (h + 1) * MEM_HEAD_DIM
        q = q_ref[:, lo:hi]
        k = kv_ref[:, lo:hi]
        v = kv_ref[:, width + lo:width + hi]
        s = lax.dot_general(q, k, (((1,), (1,)), ((), ())), preferred_element_type=F32)
        p = jnp.exp(s - jnp.max(s, axis=-1, keepdims=True))
        l = jnp.sum(p, axis=-1, keepdims=True)
        o = jnp.dot(p.astype(BF16), v, preferred_element_type=F32) / l
        o_ref[:, lo:hi] = o.astype(BF16)


def _memattn(proj, kv, q_tile, tm=512):
    b, s, _ = proj.shape
    width = MEM_HEADS * MEM_HEAD_DIM
    n_mem = kv.shape[1]
    return pl.pallas_call(
        _memattn_kernel,
        grid=(b, s // tm),
        in_specs=[
            pl.BlockSpec((None, tm, width), lambda i, j: (i, j, q_tile)),
            pl.BlockSpec((None, n_mem, 2 * width), lambda i, j: (i, 0, 0)),
        ],
        out_specs=pl.BlockSpec((None, tm, width), lambda i, j: (i, j, 0)),
        out_shape=jax.ShapeDtypeStruct((b, s, width), BF16),
        compiler_params=_params(2),
        name="mem_attention",
    )(proj, kv)


def _merge_kernel(a_ref, b_ref, c_ref, wa_ref, wb_ref, wc_ref, ga_ref, gb_ref, gc_ref,
                  o_ref, wabf_ref, wbbf_ref, wcbf_ref):
    @pl.when(pl.program_id(1) == 0)
    def _():
        wabf_ref[...] = wa_ref[...].astype(BF16)
        wbbf_ref[...] = wb_ref[...].astype(BF16)
        wcbf_ref[...] = wc_ref[...].astype(BF16)

    oa = jnp.dot(a_ref[...], wabf_ref[...], preferred_element_type=F32)
    ob = jnp.dot(b_ref[...], wbbf_ref[...], preferred_element_type=F32)
    oc = jnp.dot(c_ref[...], wcbf_ref[...], preferred_element_type=F32)
    merged = (ga_ref[...].astype(F32) * oa + gb_ref[...].astype(F32) * ob
              + gc_ref[...].astype(F32) * oc)
    o_ref[...] = merged.astype(BF16)


def _merge(ya, yb, yc, wa, wb, wc, gates, gate_col, d_model, tm=1024, tn=512):
    m, k = ya.shape
    nt = d_model // tn
    g0 = gate_col // tn
    lhs = pl.BlockSpec((tm, k), lambda j, i: (i, 0))
    wsp = pl.BlockSpec((k, tn), lambda j, i: (0, j))
    return pl.pallas_call(
        _merge_kernel,
        grid=(nt, m // tm),
        in_specs=[lhs, lhs, lhs, wsp, wsp, wsp,
                  pl.BlockSpec((tm, tn), lambda j, i: (i, g0 + j)),
                  pl.BlockSpec((tm, tn), lambda j, i: (i, g0 + nt + j)),
                  pl.BlockSpec((tm, tn), lambda j, i: (i, g0 + 2 * nt + j))],
        out_specs=pl.BlockSpec((tm, tn), lambda j, i: (i, j)),
        out_shape=jax.ShapeDtypeStruct((m, d_model), BF16),
        scratch_shapes=[pltpu.VMEM((k, tn), BF16)] * 3,
        compiler_params=_params(2),
        name="branch_merge",
    )(ya, yb, yc, wa, wb, wc, gates, gates, gates)


def _outproj_kernel(a_ref, w_ref, x_ref, g_ref, x2_ref, h2_ref, wbf_ref):
    @pl.when(pl.program_id(0) == 0)
    def _():
        wbf_ref[...] = w_ref[...].astype(BF16)

    x2 = x_ref[...] + jnp.dot(a_ref[...], wbf_ref[...], preferred_element_type=F32)
    x2_ref[...] = x2
    ms = jnp.mean(x2 * x2, axis=-1, keepdims=True)
    h2_ref[...] = (x2 * lax.rsqrt(ms + EPS) * g_ref[...]).astype(BF16)


def _outproj(a, w, x2d, g, tm=256):
    m, d = x2d.shape
    row = pl.BlockSpec((tm, d), lambda i: (i, 0))
    return pl.pallas_call(
        _outproj_kernel,
        grid=(m // tm,),
        in_specs=[row, pl.BlockSpec((d, d), lambda i: (0, 0)), row,
                  pl.BlockSpec((1, d), lambda i: (0, 0))],
        out_specs=[row, row],
        out_shape=[jax.ShapeDtypeStruct((m, d), F32), jax.ShapeDtypeStruct((m, d), BF16)],
        scratch_shapes=[pltpu.VMEM((d, d), BF16)],
        compiler_params=_params(1),
        name="out_proj_norm2",
    )(a, w, x2d, g)


def _mlp_kernel(h_ref, wu_ref, wd_ref, x_ref, o_ref):
    f = pl.program_id(1)
    rows = x_ref.shape[0]

    @pl.when(f == 0)
    def _():
        o_ref[...] = jnp.zeros_like(o_ref)

    a = jnp.dot(h_ref[...], wu_ref[...].astype(BF16), preferred_element_type=F32)
    a = jnp.square(jnp.maximum(a, 0.0)).astype(BF16)
    o_ref[...] += jnp.dot(a, wd_ref[...].astype(BF16), preferred_element_type=F32)
    slab = pl.ds(pl.multiple_of(f * rows, rows), rows)
    o_ref[slab, :] += x_ref[...]


def _mlp(h2, w_up, w_down, x2, tm=1024, tf=512):
    m, d = h2.shape
    d_ff = w_up.shape[1]
    nf = d_ff // tf
    rows = tm // nf
    return pl.pallas_call(
        _mlp_kernel,
        grid=(m // tm, nf),
        in_specs=[
            pl.BlockSpec((tm, d), lambda i, f: (i, 0)),
            pl.BlockSpec((d, tf), lambda i, f: (0, f)),
            pl.BlockSpec((tf, d), lambda i, f: (f, 0)),
            pl.BlockSpec((rows, d), lambda i, f: (i * nf + f, 0)),
        ],
        out_specs=pl.BlockSpec((tm, d), lambda i, f: (i, 0)),
        out_shape=jax.ShapeDtypeStruct((m, d), F32),
        compiler_params=_params(2),
        name="relu2_mlp",
    )(h2, w_up, w_down, x2)


def kernel(x, mem, norm1_g, w_in, b_f, conv_w, fox_q_g, fox_k_g, mem_norm_g, w_mem_kv, mem_q_g, mem_k_g, w_conv_out, w_fox_out, w_mem_out, w_out, norm2_g, w_up, w_down):
    b, s, d = x.shape
    m = b * s
    conv_width = conv_w.shape[1]
    fox_width = FOX_HEADS * FOX_HEAD_DIM
    mem_width = MEM_HEADS * MEM_HEAD_DIM
    f_col = 3 * conv_width + 3 * fox_width
    mq_tile = f_col // 1024
    gate_col = f_col + mem_width

    row = lambda v: v.reshape(1, -1)
    bf = jnp.pad(b_f, (0, LANES - FOX_HEADS)).reshape(1, LANES)
    w_t = w_in.T

    h1, qa, ka = _norm1(x, row(norm1_g), w_t, f_col, bf)
    proj = _proj(h1.reshape(m, d), w_t, row(fox_q_g), row(fox_k_g), row(mem_q_g),
                 skip_row=f_col, skip=FOX_HEADS)
    proj3 = proj.reshape(b, s, -1)

    y_conv = _conv(proj3, conv_w)
    y_fox = _fox(proj3, qa, ka, q_tile=3, k_tile=4, v_tile=5)
    kv = _memkv(mem.reshape(-1, d), row(mem_norm_g), w_mem_kv, row(mem_k_g))
    y_mem = _memattn(proj3, kv.reshape(b, -1, 2 * mem_width), q_tile=mq_tile)

    merged = _merge(y_conv.reshape(m, -1), y_fox.reshape(m, -1), y_mem.reshape(m, -1),
                    w_conv_out, w_fox_out, w_mem_out, proj, gate_col, d)
    x2, h2 = _outproj(merged, w_out, x.reshape(m, d), row(norm2_g))
    out = _mlp(h2, w_up, w_down, x2)
    return out.reshape(b, s, d)
```

```python
import functools
import math

import jax
import jax.numpy as jnp
import numpy as np
from jax import lax
from jax.experimental import pallas as pl
from jax.experimental.pallas import tpu as pltpu

F32 = jnp.float32
BF16 = jnp.bfloat16
HIGHEST = lax.Precision.HIGHEST

EPS = 1e-6
LOG2_E = math.log2(math.e)
LANES = 128
CONV_TAPS = 3
FOX_HEADS = 8
FOX_HEAD_DIM = 128
MEM_HEADS = 4
MEM_HEAD_DIM = 256
N_BRANCHES = 3

VMEM_LIMIT_BYTES = 56 * 1024 * 1024


def _params(n_axes):
    return pltpu.CompilerParams(
        dimension_semantics=("arbitrary",) * n_axes,
        vmem_limit_bytes=VMEM_LIMIT_BYTES)


def _group_rms(a, gain, width, scale=1.0):
    outs = []
    for s in range(0, a.shape[1], width):
        blk = a[:, s:s + width]
        ms = jnp.mean(blk * blk, axis=-1, keepdims=True)
        outs.append(blk * lax.rsqrt(ms + EPS) * (gain * scale))
    return outs


def _split3(c):
    hi = c.astype(BF16).astype(F32)
    r1 = c - hi
    mid = r1.astype(BF16).astype(F32)
    lo = (r1 - mid).astype(BF16).astype(F32)
    return hi, mid, lo


def _decay_selector():
    sel = np.zeros((LANES, 2, FOX_HEADS, LANES), np.float32)
    one = 3 * FOX_HEADS
    for hd in range(FOX_HEADS):
        for piece in range(3):
            sel[piece * FOX_HEADS + hd, 0, hd, piece] = 1.0
            sel[one, 0, hd, 3 + piece] = 1.0
            sel[one, 1, hd, piece] = 1.0
            sel[piece * FOX_HEADS + hd, 1, hd, 3 + piece] = -1.0
    return jnp.asarray(sel.reshape(LANES, -1), BF16)


def _norm1_kernel(x_ref, g_ref, wf_ref, bf_ref, sel_ref, h_ref, qa_ref, ka_ref, carry_ref):
    tm = x_ref.shape[0]

    @pl.when(pl.program_id(1) == 0)
    def _():
        carry_ref[...] = jnp.zeros_like(carry_ref)

    xf = x_ref[...]
    ms = jnp.mean(xf * xf, axis=-1, keepdims=True)
    y = xf * lax.rsqrt(ms + EPS) * g_ref[...]
    y_hi = y.astype(BF16)
    h_ref[...] = y_hi
    y_lo = (y - y_hi.astype(F32)).astype(BF16)

    wf = wf_ref[...]
    wf_hi = wf.astype(BF16).astype(F32)
    w2 = jnp.concatenate(
        [wf_hi, wf - wf_hi, jnp.zeros((LANES - 2 * FOX_HEADS, wf.shape[1]), F32)], axis=0).astype(BF16)
    nt = (((1,), (1,)), ((), ()))
    z2 = (lax.dot_general(y_hi, w2, nt, preferred_element_type=F32)
          + lax.dot_general(y_lo, w2, nt, preferred_element_type=F32))
    z = z2 + pltpu.roll(z2, LANES - FOX_HEADS, 1) + bf_ref[...]
    log_f = -LOG2_E * (jnp.maximum(-z, 0.0) + jnp.log1p(jnp.exp(-jnp.abs(z))))

    row = lax.broadcasted_iota(jnp.int32, (tm, tm), 0)
    col = lax.broadcasted_iota(jnp.int32, (tm, tm), 1)
    tri = jnp.where(row >= col, 1.0, 0.0).astype(BF16)
    f_hi, f_mid, f_lo = _split3(log_f)
    part = jnp.dot(tri, jnp.concatenate([f_hi, f_mid], axis=1).astype(BF16),
                   preferred_element_type=F32)
    c = (part[:, :LANES] + part[:, LANES:]
         + jnp.dot(tri, f_lo.astype(BF16), preferred_element_type=F32) + carry_ref[0:1, :])
    carry_ref[0:1, :] = c[tm - 1:tm, :]

    lane = lax.broadcasted_iota(jnp.int32, (tm, LANES), 1)
    c_hi, c_mid, c_lo = _split3(c)
    packed = jnp.where(
        lane < FOX_HEADS, c_hi,
        jnp.where(lane < 2 * FOX_HEADS, pltpu.roll(c_mid, FOX_HEADS, 1),
                  jnp.where(lane < 3 * FOX_HEADS, pltpu.roll(c_lo, 2 * FOX_HEADS, 1),
                            jnp.where(lane == 3 * FOX_HEADS, 1.0, 0.0))))
    aug = jnp.dot(packed.astype(BF16), sel_ref[...], preferred_element_type=F32)
    width = qa_ref.shape[1]
    qa_ref[...] = aug[:, :width].astype(BF16)
    ka_ref[...] = aug[:, width:].astype(BF16)


def _norm1(x, g, w_t, f_row, bf, tm=512):
    b, s, d = x.shape
    return pl.pallas_call(
        _norm1_kernel,
        grid=(b, s // tm),
        in_specs=[
            pl.BlockSpec((None, tm, d), lambda i, j: (i, j, 0)),
            pl.BlockSpec((1, d), lambda i, j: (0, 0)),
            pl.BlockSpec((FOX_HEADS, d), lambda i, j: (f_row // FOX_HEADS, 0)),
            pl.BlockSpec((1, LANES), lambda i, j: (0, 0)),
            pl.BlockSpec((LANES, 2 * FOX_HEADS * LANES), lambda i, j: (0, 0)),
        ],
        out_specs=[
            pl.BlockSpec((None, tm, d), lambda i, j: (i, j, 0)),
            pl.BlockSpec((None, tm, FOX_HEADS * LANES), lambda i, j: (i, j, 0)),
            pl.BlockSpec((None, tm, FOX_HEADS * LANES), lambda i, j: (i, j, 0)),
        ],
        out_shape=[
            jax.ShapeDtypeStruct((b, s, d), BF16),
            jax.ShapeDtypeStruct((b, s, FOX_HEADS * LANES), BF16),
            jax.ShapeDtypeStruct((b, s, FOX_HEADS * LANES), BF16),
        ],
        scratch_shapes=[pltpu.VMEM((8, LANES), F32)],
        compiler_params=_params(2),
        name="norm1_decay",
    )(x, g, w_t, bf, _decay_selector())


def _proj_kernel(h_ref, w_ref, *refs, group, scale, gate):
    o_ref, wbf_ref = refs[-2:]

    @pl.when(pl.program_id(1) == 0)
    def _():
        wbf_ref[...] = w_ref[...].astype(BF16)

    acc = lax.dot_general(h_ref[...], wbf_ref[...], (((1,), (1,)), ((), ())),
                          preferred_element_type=F32)
    if group:
        for i, p in enumerate(_group_rms(acc, refs[0][...], group, scale)):
            o_ref[:, i * group:(i + 1) * group] = p.astype(BF16)
    elif gate:
        o_ref[...] = (0.5 * jnp.tanh(0.5 * acc) + 0.5).astype(BF16)
    else:
        o_ref[...] = acc.astype(BF16)


def _proj(h, w_t, row_starts, name, gain=None, group=0, scale=1.0, gate=False, tm=1024, tn=1024):
    m, k = h.shape
    n_tiles = len(row_starts)
    assert (gain is not None) == bool(group)

    def w_rows(j, i):
        if n_tiles == 1:
            return (row_starts[0], 0)
        start = jnp.int32(row_starts[0])
        for t in range(1, n_tiles):
            start = jnp.where(j == t, row_starts[t], start)
        return (pl.multiple_of(start, 8), 0)

    gain_args = () if gain is None else (gain,)
    return pl.pallas_call(
        functools.partial(_proj_kernel, group=group, scale=scale, gate=gate),
        grid=(n_tiles, m // tm),
        in_specs=[
            pl.BlockSpec((tm, k), lambda j, i: (i, 0)),
            pl.BlockSpec((pl.Element(tn), pl.Element(k)), w_rows),
        ] + [pl.BlockSpec(g.shape, lambda j, i: (0, 0)) for g in gain_args],
        out_specs=pl.BlockSpec((tm, tn), lambda j, i: (i, j)),
        out_shape=jax.ShapeDtypeStruct((m, n_tiles * tn), BF16),
        scratch_shapes=[pltpu.VMEM((tn, k), BF16)],
        compiler_params=_params(2),
        name=name,
    )(h, w_t, *gain_args)


def _conv_kernel(bg_ref, cg_ref, v_ref, w_ref, o_ref, carry_ref):
    tm = bg_ref.shape[0]

    @pl.when(pl.program_id(1) == 0)
    def _():
        carry_ref[...] = jnp.zeros_like(carry_ref)

    u = cg_ref[...].astype(F32) * v_ref[...].astype(F32)
    row = lax.broadcasted_iota(jnp.int32, (tm, 1), 0)
    prev1 = carry_ref[1:2, :]
    prev2 = carry_ref[0:1, :]
    u1 = jnp.where(row == 0, prev1, pltpu.roll(u, 1, 0))
    u2 = jnp.where(row == 0, prev2, jnp.where(row == 1, prev1, pltpu.roll(u, 2, 0)))
    w = w_ref[...]
    y = bg_ref[...].astype(F32) * (w[0:1, :] * u2 + w[1:2, :] * u1 + w[2:3, :] * u)
    o_ref[...] = y.astype(BF16)
    carry_ref[0:2, :] = u[tm - 2:tm, :]


def _conv(proj, conv_w, tm=512):
    b, s, _ = proj.shape
    cw = conv_w.shape[1]
    return pl.pallas_call(
        _conv_kernel,
        grid=(b, s // tm),
        in_specs=[
            pl.BlockSpec((None, tm, cw), lambda i, j: (i, j, 0)),
            pl.BlockSpec((None, tm, cw), lambda i, j: (i, j, 1)),
            pl.BlockSpec((None, tm, cw), lambda i, j: (i, j, 2)),
            pl.BlockSpec((CONV_TAPS, cw), lambda i, j: (0, 0)),
        ],
        out_specs=pl.BlockSpec((None, tm, cw), lambda i, j: (i, j, 0)),
        out_shape=jax.ShapeDtypeStruct((b, s, cw), BF16),
        scratch_shapes=[pltpu.VMEM((8, cw), F32)],
        compiler_params=_params(2),
        name="gated_conv",
    )(proj, proj, proj, conv_w)


def _fox_kernel(q_ref, qa_ref, k_ref, ka_ref, v_ref, o_ref, vt_ref, s_ref, m_ref, l_ref, acc_ref):
    t = q_ref.shape[0]
    d = FOX_HEAD_DIM
    n_heads = q_ref.shape[1] // d
    qi = pl.program_id(2)
    nt = (((1,), (1,)), ((), ()))

    @pl.when(qi == 0)
    def _():
        for g in range(n_heads):
            for j in range(v_ref.shape[0] // t):
                blk = v_ref[j * t:(j + 1) * t, g * d:(g + 1) * d]
                vt_ref[g, j] = blk.astype(F32).T.astype(BF16)

    m_ref[...] = jnp.full_like(m_ref, -jnp.inf)
    l_ref[...] = jnp.zeros_like(l_ref)
    acc_ref[...] = jnp.zeros_like(acc_ref)
    qs = [jnp.concatenate([q_ref[:, g * d:(g + 1) * d], qa_ref[:, g * LANES:(g + 1) * LANES]], axis=1)
          for g in range(n_heads)]

    def scores(kb, g):
        rows = pl.ds(pl.multiple_of(kb * t, t), t)
        k = jnp.concatenate([k_ref[rows, g * d:(g + 1) * d],
                             ka_ref[rows, g * LANES:(g + 1) * LANES]], axis=1)
        return lax.dot_general(k, qs[g], nt, preferred_element_type=F32)

    def consume(kb, slot, g, diagonal):
        s = s_ref[slot, g]
        if diagonal:
            kpos = lax.broadcasted_iota(jnp.int32, (t, t), 0)
            qpos = lax.broadcasted_iota(jnp.int32, (t, t), 1)
            s = jnp.where(qpos >= kpos, s, -jnp.inf)
        m_prev = m_ref[g]
        m_new = jnp.maximum(m_prev, jnp.max(s, axis=0, keepdims=True))
        alpha = jnp.exp2(m_prev - m_new)
        p = jnp.exp2(s - m_new)
        l_ref[g] = alpha * l_ref[g] + jnp.sum(p, axis=0, keepdims=True)
        pv = jnp.dot(vt_ref[g, kb], p.astype(BF16), preferred_element_type=F32)
        acc_ref[g] = alpha * acc_ref[g] + pv
        m_ref[g] = m_new

    def advance(kb, slot):
        for g in range(n_heads):
            s_ref[1 - slot, g] = scores(kb + 1, g)
            consume(kb, slot, g, False)

    def finish(slot):
        for g in range(n_heads):
            consume(qi, slot, g, True)
            o_ref[:, g * d:(g + 1) * d] = (acc_ref[g] / l_ref[g]).T.astype(BF16)

    for g in range(n_heads):
        s_ref[0, g] = scores(0, g)

    def body(pair, carry):
        advance(2 * pair, 0)
        advance(2 * pair + 1, 1)
        return carry

    lax.fori_loop(0, qi // 2, body, 0)

    @pl.when(qi % 2 == 1)
    def _():
        advance(qi - 1, 0)
        finish(1)

    @pl.when(qi % 2 == 0)
    def _():
        finish(0)


def _fox(q, qa, k, ka, v, v_col, tq=512, heads_per_step=4):
    b, s, _ = q.shape
    d = FOX_HEAD_DIM
    w = heads_per_step * d
    v0 = v_col // w
    tk = tq
    return pl.pallas_call(
        _fox_kernel,
        grid=(b, FOX_HEADS // heads_per_step, s // tq),
        in_specs=[
            pl.BlockSpec((None, tq, w), lambda bi, h, qi: (bi, qi, h)),
            pl.BlockSpec((None, tq, w), lambda bi, h, qi: (bi, qi, h)),
            pl.BlockSpec((None, s, w), lambda bi, h, qi: (bi, 0, h)),
            pl.BlockSpec((None, s, w), lambda bi, h, qi: (bi, 0, h)),
            pl.BlockSpec((None, s, w), lambda bi, h, qi: (bi, 0, v0 + h)),
        ],
        out_specs=pl.BlockSpec((None, tq, w), lambda bi, h, qi: (bi, qi, h)),
        out_shape=jax.ShapeDtypeStruct((b, s, FOX_HEADS * d), BF16),
        scratch_shapes=[pltpu.VMEM((heads_per_step, s // tk, d, tk), BF16),
                        pltpu.VMEM((2, heads_per_step, tk, tq), F32),
                        pltpu.VMEM((heads_per_step, 1, tq), F32),
                        pltpu.VMEM((heads_per_step, 1, tq), F32),
                        pltpu.VMEM((heads_per_step, d, tq), F32)],
        compiler_params=_params(3),
        name="fox_attention",
    )(q, qa, k, ka, v)


def _memkv_kernel(mem_ref, g_ref, w_ref, kg_ref, o_ref):
    xf = mem_ref[...]
    ms = jnp.mean(xf * xf, axis=-1, keepdims=True)
    y = (xf * lax.rsqrt(ms + EPS) * g_ref[...]).astype(BF16)
    acc = jnp.dot(y, w_ref[...].astype(BF16), preferred_element_type=F32)

    @pl.when(pl.program_id(0) == 0)
    def _():
        parts = _group_rms(acc, kg_ref[...], MEM_HEAD_DIM)
        for i, p in enumerate(parts):
            o_ref[:, i * MEM_HEAD_DIM:(i + 1) * MEM_HEAD_DIM] = p.astype(BF16)

    @pl.when(pl.program_id(0) == 1)
    def _():
        o_ref[...] = acc.astype(BF16)


def _memkv(mem2d, g, w, kg):
    m, d = mem2d.shape
    width = MEM_HEADS * MEM_HEAD_DIM
    return pl.pallas_call(
        _memkv_kernel,
        grid=(2,),
        in_specs=[
            pl.BlockSpec((m, d), lambda j: (0, 0)),
            pl.BlockSpec((1, d), lambda j: (0, 0)),
            pl.BlockSpec((d, width), lambda j: (0, j)),
            pl.BlockSpec((1, MEM_HEAD_DIM), lambda j: (0, 0)),
        ],
        out_specs=pl.BlockSpec((m, width), lambda j: (0, j)),
        out_shape=jax.ShapeDtypeStruct((m, 2 * width), BF16),
        compiler_params=_params(1),
        name="mem_kv",
    )(mem2d, g, w, kg)


def _memattn_kernel(q_ref, kv_ref, o_ref):
    width = MEM_HEADS * MEM_HEAD_DIM
    for h in range(MEM_HEADS):
        lo, hi = h * MEM_HEAD_DIM, (h + 1) * MEM_HEAD_DIM
        q = q_ref[:, lo:hi]
        k = kv_ref[:, lo:hi]
        v = kv_ref[:, width + lo:width + hi]
        s = lax.dot_general(q, k, (((1,), (1,)), ((), ())), preferred_element_type=F32)
        p = jnp.exp(s - jnp.max(s, axis=-1, keepdims=True))
        l = jnp.sum(p, axis=-1, keepdims=True)
        o = jnp.dot(p.astype(BF16), v, preferred_element_type=F32) / l
        o_ref[:, lo:hi] = o.astype(BF16)


def _memattn(proj, kv, q_tile, tm=512):
    b, s, _ = proj.shape
    width = MEM_HEADS * MEM_HEAD_DIM
    n_mem = kv.shape[1]
    return pl.pallas_call(
        _memattn_kernel,
        grid=(b, s // tm),
        in_specs=[
            pl.BlockSpec((None, tm, width), lambda i, j: (i, j, q_tile)),
            pl.BlockSpec((None, n_mem, 2 * width), lambda i, j: (i, 0, 0)),
        ],
        out_specs=pl.BlockSpec((None, tm, width), lambda i, j: (i, j, 0)),
        out_shape=jax.ShapeDtypeStruct((b, s, width), BF16),
        compiler_params=_params(2),
        name="mem_attention",
    )(proj, kv)


def _merge_kernel(a_ref, b_ref, c_ref, wa_ref, wb_ref, wc_ref, ga_ref, gb_ref, gc_ref,
                  o_ref, wabf_ref, wbbf_ref, wcbf_ref):
    @pl.when(pl.program_id(1) == 0)
    def _():
        wabf_ref[...] = wa_ref[...].astype(BF16)
        wbbf_ref[...] = wb_ref[...].astype(BF16)
        wcbf_ref[...] = wc_ref[...].astype(BF16)

    oa = jnp.dot(a_ref[...], wabf_ref[...], preferred_element_type=F32)
    ob = jnp.dot(b_ref[...], wbbf_ref[...], preferred_element_type=F32)
    oc = jnp.dot(c_ref[...], wcbf_ref[...], preferred_element_type=F32)
    merged = (ga_ref[...].astype(F32) * oa + gb_ref[...].astype(F32) * ob
              + gc_ref[...].astype(F32) * oc)
    o_ref[...] = merged.astype(BF16)


def _merge(ya, yb, yc, wa, wb, wc, gates, gate_col, d_model, tm=1024, tn=512):
    m, k = ya.shape
    nt = d_model // tn
    g0 = gate_col // tn
    lhs = pl.BlockSpec((tm, k), lambda j, i: (i, 0))
    wsp = pl.BlockSpec((k, tn), lambda j, i: (0, j))
    return pl.pallas_call(
        _merge_kernel,
        grid=(nt, m // tm),
        in_specs=[lhs, lhs, lhs, wsp, wsp, wsp,
                  pl.BlockSpec((tm, tn), lambda j, i: (i, g0 + j)),
                  pl.BlockSpec((tm, tn), lambda j, i: (i, g0 + nt + j)),
                  pl.BlockSpec((tm, tn), lambda j, i: (i, g0 + 2 * nt + j))],
        out_specs=pl.BlockSpec((tm, tn), lambda j, i: (i, j)),
        out_shape=jax.ShapeDtypeStruct((m, d_model), BF16),
        scratch_shapes=[pltpu.VMEM((k, tn), BF16)] * 3,
        compiler_params=_params(2),
        name="branch_merge",
    )(ya, yb, yc, wa, wb, wc, gates, gates, gates)


def _outproj_kernel(a_ref, w_ref, x_ref, g_ref, x2_ref, h2_ref, wbf_ref):
    @pl.when(pl.program_id(0) == 0)
    def _():
        wbf_ref[...] = w_ref[...].astype(BF16)

    x2 = x_ref[...] + jnp.dot(a_ref[...], wbf_ref[...], preferred_element_type=F32)
    x2_ref[...] = x2
    ms = jnp.mean(x2 * x2, axis=-1, keepdims=True)
    h2_ref[...] = (x2 * lax.rsqrt(ms + EPS) * g_ref[...]).astype(BF16)


def _outproj(a, w, x2d, g, tm=256):
    m, d = x2d.shape
    row = pl.BlockSpec((tm, d), lambda i: (i, 0))
    return pl.pallas_call(
        _outproj_kernel,
        grid=(m // tm,),
        in_specs=[row, pl.BlockSpec((d, d), lambda i: (0, 0)), row,
                  pl.BlockSpec((1, d), lambda i: (0, 0))],
        out_specs=[row, row],
        out_shape=[jax.ShapeDtypeStruct((m, d), F32), jax.ShapeDtypeStruct((m, d), BF16)],
        scratch_shapes=[pltpu.VMEM((d, d), BF16)],
        compiler_params=_params(1),
        name="out_proj_norm2",
    )(a, w, x2d, g)


def _mlp_kernel(h_ref, wu_ref, wd_ref, x_ref, o_ref):
    f = pl.program_id(1)
    rows = x_ref.shape[0]

    @pl.when(f == 0)
    def _():
        o_ref[...] = jnp.zeros_like(o_ref)

    a = jnp.dot(h_ref[...], wu_ref[...].astype(BF16), preferred_element_type=F32)
    a = jnp.square(jnp.maximum(a, 0.0)).astype(BF16)
    o_ref[...] += jnp.dot(a, wd_ref[...].astype(BF16), preferred_element_type=F32)
    slab = pl.ds(pl.multiple_of(f * rows, rows), rows)
    o_ref[slab, :] += x_ref[...]


def _mlp(h2, w_up, w_down, x2, tm=1024, tf=512):
    m, d = h2.shape
    d_ff = w_up.shape[1]
    nf = d_ff // tf
    rows = tm // nf
    return pl.pallas_call(
        _mlp_kernel,
        grid=(m // tm, nf),
        in_specs=[
            pl.BlockSpec((tm, d), lambda i, f: (i, 0)),
            pl.BlockSpec((d, tf), lambda i, f: (0, f)),
            pl.BlockSpec((tf, d), lambda i, f: (f, 0)),
            pl.BlockSpec((rows, d), lambda i, f: (i * nf + f, 0)),
        ],
        out_specs=pl.BlockSpec((tm, d), lambda i, f: (i, 0)),
        out_shape=jax.ShapeDtypeStruct((m, d), F32),
        compiler_params=_params(2),
        name="relu2_mlp",
    )(h2, w_up, w_down, x2)


def kernel(x, mem, norm1_g, w_in, b_f, conv_w, fox_q_g, fox_k_g, mem_norm_g, w_mem_kv, mem_q_g, mem_k_g, w_conv_out, w_fox_out, w_mem_out, w_out, norm2_g, w_up, w_down):
    b, s, d = x.shape
    m = b * s
    conv_width = conv_w.shape[1]
    fox_width = FOX_HEADS * FOX_HEAD_DIM
    mem_width = MEM_HEADS * MEM_HEAD_DIM
    q_row = 3 * conv_width
    k_row = q_row + fox_width
    v_row = k_row + fox_width
    f_row = v_row + fox_width
    mq_row = f_row + FOX_HEADS
    gate_row = mq_row + mem_width

    row = lambda v: v.reshape(1, -1)
    tiles = lambda start, width: tuple(range(start, start + width, 1024))
    bf = jnp.pad(b_f, (0, LANES - FOX_HEADS)).reshape(1, LANES)
    w_t = w_in.T

    h1, qa, ka = _norm1(x, row(norm1_g), w_t, f_row, bf)
    h1 = h1.reshape(m, d)
    conv_v = _proj(h1, w_t, tiles(0, 3 * conv_width) + tiles(v_row, fox_width), "proj_conv_v")
    fq = _proj(h1, w_t, tiles(q_row, fox_width), "proj_fox_q", gain=row(fox_q_g),
               group=FOX_HEAD_DIM, scale=LOG2_E / math.sqrt(FOX_HEAD_DIM))
    fk = _proj(h1, w_t, tiles(k_row, fox_width), "proj_fox_k", gain=row(fox_k_g),
               group=FOX_HEAD_DIM)
    mq = _proj(h1, w_t, tiles(mq_row, mem_width), "proj_mem_q", gain=row(mem_q_g),
               group=MEM_HEAD_DIM, scale=1.0 / math.sqrt(MEM_HEAD_DIM))
    gates = _proj(h1, w_t, tiles(gate_row, N_BRANCHES * d), "proj_gates", gate=True)

    conv_v3 = conv_v.reshape(b, s, -1)
    y_conv = _conv(conv_v3, conv_w)
    y_fox = _fox(fq.reshape(b, s, -1), qa, fk.reshape(b, s, -1), ka, conv_v3, 3 * conv_width)
    kv = _memkv(mem.reshape(-1, d), row(mem_norm_g), w_mem_kv, row(mem_k_g))
    y_mem = _memattn(mq.reshape(b, s, -1), kv.reshape(b, -1, 2 * mem_width), q_tile=0)

    merged = _merge(y_conv.reshape(m, -1), y_fox.reshape(m, -1), y_mem.reshape(m, -1),
                    w_conv_out, w_fox_out, w_mem_out, gates, 0, d)
    x2, h2 = _outproj(merged, w_out, x.reshape(m, d), row(norm2_g))
    out = _mlp(h2, w_up, w_down, x2)
    return out.reshape(b, s, d)
```

```python
import functools
import math

import jax
import jax.numpy as jnp
import numpy as np
from jax import lax
from jax.experimental import pallas as pl
from jax.experimental.pallas import tpu as pltpu

F32 = jnp.float32
BF16 = jnp.bfloat16

EPS = 1e-6
LOG2_E = math.log2(math.e)
LANES = 128
CONV_TAPS = 3
FOX_HEADS = 8
FOX_HEAD_DIM = 128
MEM_HEADS = 4
MEM_HEAD_DIM = 256
N_BRANCHES = 3

VMEM_LIMIT_BYTES = 56 * 1024 * 1024
NT_DIMS = (((1,), (1,)), ((), ()))


def _params(n_axes):
    return pltpu.CompilerParams(
        dimension_semantics=("arbitrary",) * n_axes,
        vmem_limit_bytes=VMEM_LIMIT_BYTES)


def _group_rms(a, gain, width, scale=1.0):
    outs = []
    for s in range(0, a.shape[1], width):
        blk = a[:, s:s + width]
        ms = jnp.mean(blk * blk, axis=-1, keepdims=True)
        outs.append(blk * lax.rsqrt(ms + EPS) * (gain * scale))
    return outs


def _split3(c):
    hi = c.astype(BF16).astype(F32)
    r1 = c - hi
    mid = r1.astype(BF16).astype(F32)
    lo = (r1 - mid).astype(BF16).astype(F32)
    return hi, mid, lo


def _decay_selector():
    sel = np.zeros((LANES, 2, FOX_HEADS, LANES), np.float32)
    one = 3 * FOX_HEADS
    for hd in range(FOX_HEADS):
        for piece in range(3):
            sel[piece * FOX_HEADS + hd, 0, hd, piece] = 1.0
            sel[one, 0, hd, 3 + piece] = 1.0
            sel[one, 1, hd, piece] = 1.0
            sel[piece * FOX_HEADS + hd, 1, hd, 3 + piece] = -1.0
    return jnp.asarray(sel.reshape(LANES, -1), BF16)


def _norm1_kernel(x_ref, g_ref, wf_ref, bf_ref, sel_ref, h_ref, qa_ref, ka_ref, carry_ref):
    tm = x_ref.shape[0]

    @pl.when(pl.program_id(1) == 0)
    def _():
        carry_ref[...] = jnp.zeros_like(carry_ref)

    xf = x_ref[...]
    ms = jnp.mean(xf * xf, axis=-1, keepdims=True)
    y = xf * lax.rsqrt(ms + EPS) * g_ref[...]
    y_hi = y.astype(BF16)
    h_ref[...] = y_hi
    y_lo = (y - y_hi.astype(F32)).astype(BF16)

    wf = wf_ref[...]
    wf_hi = wf.astype(BF16).astype(F32)
    w2 = jnp.concatenate(
        [wf_hi, wf - wf_hi, jnp.zeros((LANES - 2 * FOX_HEADS, wf.shape[1]), F32)], axis=0).astype(BF16)
    z2 = (lax.dot_general(y_hi, w2, NT_DIMS, preferred_element_type=F32)
          + lax.dot_general(y_lo, w2, NT_DIMS, preferred_element_type=F32))
    z = z2 + pltpu.roll(z2, LANES - FOX_HEADS, 1) + bf_ref[...]
    log_f = -LOG2_E * (jnp.maximum(-z, 0.0) + jnp.log1p(jnp.exp(-jnp.abs(z))))

    row = lax.broadcasted_iota(jnp.int32, (tm, tm), 0)
    col = lax.broadcasted_iota(jnp.int32, (tm, tm), 1)
    tri = jnp.where(row >= col, 1.0, 0.0).astype(BF16)
    f_hi, f_mid, f_lo = _split3(log_f)
    part = jnp.dot(tri, jnp.concatenate([f_hi, f_mid], axis=1).astype(BF16),
                   preferred_element_type=F32)
    c = (part[:, :LANES] + part[:, LANES:]
         + jnp.dot(tri, f_lo.astype(BF16), preferred_element_type=F32) + carry_ref[0:1, :])
    carry_ref[0:1, :] = c[tm - 1:tm, :]

    lane = lax.broadcasted_iota(jnp.int32, (tm, LANES), 1)
    c_hi, c_mid, c_lo = _split3(c)
    packed = jnp.where(
        lane < FOX_HEADS, c_hi,
        jnp.where(lane < 2 * FOX_HEADS, pltpu.roll(c_mid, FOX_HEADS, 1),
                  jnp.where(lane < 3 * FOX_HEADS, pltpu.roll(c_lo, 2 * FOX_HEADS, 1),
                            jnp.where(lane == 3 * FOX_HEADS, 1.0, 0.0))))
    aug = jnp.dot(packed.astype(BF16), sel_ref[...], preferred_element_type=F32)
    width = qa_ref.shape[1]
    qa_ref[...] = aug[:, :width].astype(BF16)
    ka_ref[...] = aug[:, width:].astype(BF16)


def _norm1(x, g, w_t, f_row, bf, tm=512):
    b, s, d = x.shape
    return pl.pallas_call(
        _norm1_kernel,
        grid=(b, s // tm),
        in_specs=[
            pl.BlockSpec((None, tm, d), lambda i, j: (i, j, 0)),
            pl.BlockSpec((1, d), lambda i, j: (0, 0)),
            pl.BlockSpec((FOX_HEADS, d), lambda i, j: (f_row // FOX_HEADS, 0)),
            pl.BlockSpec((1, LANES), lambda i, j: (0, 0)),
            pl.BlockSpec((LANES, 2 * FOX_HEADS * LANES), lambda i, j: (0, 0)),
        ],
        out_specs=[
            pl.BlockSpec((None, tm, d), lambda i, j: (i, j, 0)),
            pl.BlockSpec((None, tm, FOX_HEADS * LANES), lambda i, j: (i, j, 0)),
            pl.BlockSpec((None, tm, FOX_HEADS * LANES), lambda i, j: (i, j, 0)),
        ],
        out_shape=[
            jax.ShapeDtypeStruct((b, s, d), BF16),
            jax.ShapeDtypeStruct((b, s, FOX_HEADS * LANES), BF16),
            jax.ShapeDtypeStruct((b, s, FOX_HEADS * LANES), BF16),
        ],
        scratch_shapes=[pltpu.VMEM((8, LANES), F32)],
        compiler_params=_params(2),
        name="norm1_decay",
    )(x, g, w_t, bf, _decay_selector())


def _proj_kernel(h_ref, w_ref, *refs, group, scale, gate):
    o_ref, wbf_ref = refs[-2:]

    @pl.when(pl.program_id(1) == 0)
    def _():
        wbf_ref[...] = w_ref[...].astype(BF16)

    acc = lax.dot_general(h_ref[...], wbf_ref[...], NT_DIMS, preferred_element_type=F32)
    if group:
        for i, p in enumerate(_group_rms(acc, refs[0][...], group, scale)):
            o_ref[:, i * group:(i + 1) * group] = p.astype(BF16)
    elif gate:
        o_ref[...] = (0.5 * jnp.tanh(0.5 * acc) + 0.5).astype(BF16)
    else:
        o_ref[...] = acc.astype(BF16)


def _proj(h, w_t, row_starts, name, gain=None, group=0, scale=1.0, gate=False, tm=1024, tn=1024):
    m, k = h.shape
    n_tiles = len(row_starts)
    assert (gain is not None) == bool(group)

    def w_rows(j, i):
        if n_tiles == 1:
            return (row_starts[0], 0)
        start = jnp.int32(row_starts[0])
        for t in range(1, n_tiles):
            start = jnp.where(j == t, row_starts[t], start)
        return (pl.multiple_of(start, 8), 0)

    gain_args = () if gain is None else (gain,)
    return pl.pallas_call(
        functools.partial(_proj_kernel, group=group, scale=scale, gate=gate),
        grid=(n_tiles, m // tm),
        in_specs=[
            pl.BlockSpec((tm, k), lambda j, i: (i, 0)),
            pl.BlockSpec((pl.Element(tn), pl.Element(k)), w_rows),
        ] + [pl.BlockSpec(g.shape, lambda j, i: (0, 0)) for g in gain_args],
        out_specs=pl.BlockSpec((tm, tn), lambda j, i: (i, j)),
        out_shape=jax.ShapeDtypeStruct((m, n_tiles * tn), BF16),
        scratch_shapes=[pltpu.VMEM((tn, k), BF16)],
        compiler_params=_params(2),
        name=name,
    )(h, w_t, *gain_args)


def _conv_kernel(bg_ref, cg_ref, v_ref, w_ref, o_ref, carry_ref):
    tm = bg_ref.shape[0]

    @pl.when(pl.program_id(1) == 0)
    def _():
        carry_ref[...] = jnp.zeros_like(carry_ref)

    u = cg_ref[...].astype(F32) * v_ref[...].astype(F32)
    row = lax.broadcasted_iota(jnp.int32, (tm, 1), 0)
    prev1 = carry_ref[1:2, :]
    prev2 = carry_ref[0:1, :]
    u1 = jnp.where(row == 0, prev1, pltpu.roll(u, 1, 0))
    u2 = jnp.where(row == 0, prev2, jnp.where(row == 1, prev1, pltpu.roll(u, 2, 0)))
    w = w_ref[...]
    y = bg_ref[...].astype(F32) * (w[0:1, :] * u2 + w[1:2, :] * u1 + w[2:3, :] * u)
    o_ref[...] = y.astype(BF16)
    carry_ref[0:2, :] = u[tm - 2:tm, :]


def _conv(proj, conv_w, tm=512):
    b, s, _ = proj.shape
    cw = conv_w.shape[1]
    return pl.pallas_call(
        _conv_kernel,
        grid=(b, s // tm),
        in_specs=[
            pl.BlockSpec((None, tm, cw), lambda i, j: (i, j, 0)),
            pl.BlockSpec((None, tm, cw), lambda i, j: (i, j, 1)),
            pl.BlockSpec((None, tm, cw), lambda i, j: (i, j, 2)),
            pl.BlockSpec((CONV_TAPS, cw), lambda i, j: (0, 0)),
        ],
        out_specs=pl.BlockSpec((None, tm, cw), lambda i, j: (i, j, 0)),
        out_shape=jax.ShapeDtypeStruct((b, s, cw), BF16),
        scratch_shapes=[pltpu.VMEM((8, cw), F32)],
        compiler_params=_params(2),
        name="gated_conv",
    )(proj, proj, proj, conv_w)


def _fox_kernel(q_ref, qa_ref, k_ref, ka_ref, v_ref, o_ref, vt_ref, s_ref, m_ref, l_ref, acc_ref):
    t = q_ref.shape[0] // 2
    d = FOX_HEAD_DIM
    n_heads = q_ref.shape[1] // d
    j = pl.program_id(2)

    @pl.when(j == 0)
    def _():
        for g in range(n_heads):
            for blk in range(v_ref.shape[0] // t):
                v = v_ref[blk * t:(blk + 1) * t, g * d:(g + 1) * d]
                vt_ref[g, blk] = v.astype(F32).T.astype(BF16)

    def scores(tile, kb, g):
        q = jnp.concatenate([q_ref[tile * t:(tile + 1) * t, g * d:(g + 1) * d],
                             qa_ref[tile * t:(tile + 1) * t, g * d:(g + 1) * d]], axis=1)
        rows = pl.ds(pl.multiple_of(kb * t, t), t)
        k = jnp.concatenate([k_ref[rows, g * d:(g + 1) * d], ka_ref[rows, g * d:(g + 1) * d]], axis=1)
        return lax.dot_general(k, q, NT_DIMS, preferred_element_type=F32)

    def reset(g):
        m_ref[g] = jnp.full(m_ref.shape[1:], -jnp.inf, F32)
        l_ref[g] = jnp.zeros(l_ref.shape[1:], F32)
        acc_ref[g] = jnp.zeros(acc_ref.shape[1:], F32)

    def consume(kb, slot, g, diagonal):
        s = s_ref[slot, g]
        if diagonal:
            kpos = lax.broadcasted_iota(jnp.int32, (t, t), 0)
            qpos = lax.broadcasted_iota(jnp.int32, (t, t), 1)
            s = jnp.where(qpos >= kpos, s, -jnp.inf)
        m_prev = m_ref[g]
        m_new = jnp.maximum(m_prev, jnp.max(s, axis=0, keepdims=True))
        alpha = jnp.exp2(m_prev - m_new)
        p = jnp.exp2(s - m_new)
        l_ref[g] = alpha * l_ref[g] + jnp.sum(p, axis=0, keepdims=True)
        pv = jnp.dot(vt_ref[g, kb], p.astype(BF16), preferred_element_type=F32)
        acc_ref[g] = alpha * acc_ref[g] + pv
        m_ref[g] = m_new

    def advance(tile, kb, slot, next_tile=None, next_kb=None, diagonal=False):
        for g in range(n_heads):
            s_ref[1 - slot, g] = scores(tile if next_tile is None else next_tile,
                                        kb + 1 if next_kb is None else next_kb, g)
            consume(kb, slot, g, diagonal)

    def emit(tile, g):
        o_ref[tile * t:(tile + 1) * t, g * d:(g + 1) * d] = (acc_ref[g] / l_ref[g]).T.astype(BF16)

    for g in range(n_heads):
        reset(g)
        s_ref[0, g] = scores(0, 0, g)

    def first_pairs(pair, carry):
        advance(0, 2 * pair, 0)
        advance(0, 2 * pair + 1, 1)
        return carry

    lax.fori_loop(0, j, first_pairs, 0)
    for g in range(n_heads):
        s_ref[1, g] = scores(1, 0, g)
        consume(2 * j, 0, g, True)
        emit(0, g)
        reset(g)

    def second_pairs(pair, carry):
        advance(1, 2 * pair, 1)
        advance(1, 2 * pair + 1, 0)
        return carry

    lax.fori_loop(0, j, second_pairs, 0)
    advance(1, 2 * j, 1)
    for g in range(n_heads):
        consume(2 * j + 1, 0, g, True)
        emit(1, g)


def _fox(q, qa, k, ka, v, v_col, t=512, heads_per_step=4):
    b, s, _ = q.shape
    d = FOX_HEAD_DIM
    w = heads_per_step * d
    v0 = v_col // w
    tile2 = pl.BlockSpec((None, 2 * t, w), lambda bi, h, j: (bi, j, h))
    seq = lambda col0: pl.BlockSpec((None, s, w), lambda bi, h, j: (bi, 0, col0 + h))
    return pl.pallas_call(
        _fox_kernel,
        grid=(b, FOX_HEADS // heads_per_step, s // (2 * t)),
        in_specs=[tile2, tile2, seq(0), seq(0), seq(v0)],
        out_specs=tile2,
        out_shape=jax.ShapeDtypeStruct((b, s, FOX_HEADS * d), BF16),
        scratch_shapes=[pltpu.VMEM((heads_per_step, s // t, d, t), BF16),
                        pltpu.VMEM((2, heads_per_step, t, t), F32),
                        pltpu.VMEM((heads_per_step, 1, t), F32),
                        pltpu.VMEM((heads_per_step, 1, t), F32),
                        pltpu.VMEM((heads_per_step, d, t), F32)],
        compiler_params=_params(3),
        name="fox_attention",
    )(q, qa, k, ka, v)


def _memkv_kernel(mem_ref, g_ref, w_ref, kg_ref, o_ref):
    xf = mem_ref[...]
    ms = jnp.mean(xf * xf, axis=-1, keepdims=True)
    y = (xf * lax.rsqrt(ms + EPS) * g_ref[...]).astype(BF16)
    acc = jnp.dot(y, w_ref[...].astype(BF16), preferred_element_type=F32)

    @pl.when(pl.program_id(0) == 0)
    def _():
        parts = _group_rms(acc, kg_ref[...], MEM_HEAD_DIM)
        for i, p in enumerate(parts):
            o_ref[:, i * MEM_HEAD_DIM:(i + 1) * MEM_HEAD_DIM] = p.astype(BF16)

    @pl.when(pl.program_id(0) == 1)
    def _():
        o_ref[...] = acc.astype(BF16)


def _memkv(mem2d, g, w, kg):
    m, d = mem2d.shape
    width = MEM_HEADS * MEM_HEAD_DIM
    return pl.pallas_call(
        _memkv_kernel,
        grid=(2,),
        in_specs=[
            pl.BlockSpec((m, d), lambda j: (0, 0)),
            pl.BlockSpec((1, d), lambda j: (0, 0)),
            pl.BlockSpec((d, width), lambda j: (0, j)),
            pl.BlockSpec((1, MEM_HEAD_DIM), lambda j: (0, 0)),
        ],
        out_specs=pl.BlockSpec((m, width), lambda j: (0, j)),
        out_shape=jax.ShapeDtypeStruct((m, 2 * width), BF16),
        compiler_params=_params(1),
        name="mem_kv",
    )(mem2d, g, w, kg)


def _memattn_kernel(q_ref, kv_ref, o_ref):
    width = MEM_HEADS * MEM_HEAD_DIM
    for h in range(MEM_HEADS):
        lo, hi = h * MEM_HEAD_DIM, (h + 1) * MEM_HEAD_DIM
        q = q_ref[:, lo:hi]
        k = kv_ref[:, lo:hi]
        v = kv_ref[:, width + lo:width + hi]
        s = lax.dot_general(q, k, NT_DIMS, preferred_element_type=F32)
        p = jnp.exp(s - jnp.max(s, axis=-1, keepdims=True))
        l = jnp.sum(p, axis=-1, keepdims=True)
        o = jnp.dot(p.astype(BF16), v, preferred_element_type=F32) / l
        o_ref[:, lo:hi] = o.astype(BF16)


def _memattn(mq, kv, tm=512):
    b, s, width = mq.shape
    n_mem = kv.shape[1]
    return pl.pallas_call(
        _memattn_kernel,
        grid=(b, s // tm),
        in_specs=[
            pl.BlockSpec((None, tm, width), lambda i, j: (i, j, 0)),
            pl.BlockSpec((None, n_mem, 2 * width), lambda i, j: (i, 0, 0)),
        ],
        out_specs=pl.BlockSpec((None, tm, width), lambda i, j: (i, j, 0)),
        out_shape=jax.ShapeDtypeStruct((b, s, width), BF16),
        compiler_params=_params(2),
        name="mem_attention",
    )(mq, kv)


def _merge_kernel(a_ref, b_ref, c_ref, wa_ref, wb_ref, wc_ref, ga_ref, gb_ref, gc_ref,
                  o_ref, wabf_ref, wbbf_ref, wcbf_ref):
    @pl.when(pl.program_id(1) == 0)
    def _():
        wabf_ref[...] = wa_ref[...].astype(BF16)
        wbbf_ref[...] = wb_ref[...].astype(BF16)
        wcbf_ref[...] = wc_ref[...].astype(BF16)

    oa = jnp.dot(a_ref[...], wabf_ref[...], preferred_element_type=F32)
    ob = jnp.dot(b_ref[...], wbbf_ref[...], preferred_element_type=F32)
    oc = jnp.dot(c_ref[...], wcbf_ref[...], preferred_element_type=F32)
    merged = (ga_ref[...].astype(F32) * oa + gb_ref[...].astype(F32) * ob
              + gc_ref[...].astype(F32) * oc)
    o_ref[...] = merged.astype(BF16)


def _merge(ya, yb, yc, wa, wb, wc, gates, d_model, tm=1024, tn=512):
    m, k = ya.shape
    nt = d_model // tn
    lhs = pl.BlockSpec((tm, k), lambda j, i: (i, 0))
    wsp = pl.BlockSpec((k, tn), lambda j, i: (0, j))
    return pl.pallas_call(
        _merge_kernel,
        grid=(nt, m // tm),
        in_specs=[lhs, lhs, lhs, wsp, wsp, wsp,
                  pl.BlockSpec((tm, tn), lambda j, i: (i, j)),
                  pl.BlockSpec((tm, tn), lambda j, i: (i, nt + j)),
                  pl.BlockSpec((tm, tn), lambda j, i: (i, 2 * nt + j))],
        out_specs=pl.BlockSpec((tm, tn), lambda j, i: (i, j)),
        out_shape=jax.ShapeDtypeStruct((m, d_model), BF16),
        scratch_shapes=[pltpu.VMEM((k, tn), BF16)] * 3,
        compiler_params=_params(2),
        name="branch_merge",
    )(ya, yb, yc, wa, wb, wc, gates, gates, gates)


def _outproj_kernel(a_ref, w_ref, x_ref, g_ref, x2_ref, h2_ref, wbf_ref):
    @pl.when(pl.program_id(0) == 0)
    def _():
        wbf_ref[...] = w_ref[...].astype(BF16)

    x2 = x_ref[...] + jnp.dot(a_ref[...], wbf_ref[...], preferred_element_type=F32)
    x2_ref[...] = x2
    ms = jnp.mean(x2 * x2, axis=-1, keepdims=True)
    h2_ref[...] = (x2 * lax.rsqrt(ms + EPS) * g_ref[...]).astype(BF16)


def _outproj(a, w, x2d, g, tm=256):
    m, d = x2d.shape
    row = pl.BlockSpec((tm, d), lambda i: (i, 0))
    return pl.pallas_call(
        _outproj_kernel,
        grid=(m // tm,),
        in_specs=[row, pl.BlockSpec((d, d), lambda i: (0, 0)), row,
                  pl.BlockSpec((1, d), lambda i: (0, 0))],
        out_specs=[row, row],
        out_shape=[jax.ShapeDtypeStruct((m, d), F32), jax.ShapeDtypeStruct((m, d), BF16)],
        scratch_shapes=[pltpu.VMEM((d, d), BF16)],
        compiler_params=_params(1),
        name="out_proj_norm2",
    )(a, w, x2d, g)


def _mlp_kernel(h_ref, wu_ref, wd_ref, x_ref, o_ref):
    f = pl.program_id(1)
    rows = x_ref.shape[0]

    @pl.when(f == 0)
    def _():
        o_ref[...] = jnp.zeros_like(o_ref)

    a = jnp.dot(h_ref[...], wu_ref[...].astype(BF16), preferred_element_type=F32)
    a = jnp.square(jnp.maximum(a, 0.0)).astype(BF16)
    o_ref[...] += jnp.dot(a, wd_ref[...].astype(BF16), preferred_element_type=F32)
    slab = pl.ds(pl.multiple_of(f * rows, rows), rows)
    o_ref[slab, :] += x_ref[...]


def _mlp(h2, w_up, w_down, x2, tm=1024, tf=512):
    m, d = h2.shape
    d_ff = w_up.shape[1]
    nf = d_ff // tf
    rows = tm // nf
    return pl.pallas_call(
        _mlp_kernel,
        grid=(m // tm, nf),
        in_specs=[
            pl.BlockSpec((tm, d), lambda i, f: (i, 0)),
            pl.BlockSpec((d, tf), lambda i, f: (0, f)),
            pl.BlockSpec((tf, d), lambda i, f: (f, 0)),
            pl.BlockSpec((rows, d), lambda i, f: (i * nf + f, 0)),
        ],
        out_specs=pl.BlockSpec((tm, d), lambda i, f: (i, 0)),
        out_shape=jax.ShapeDtypeStruct((m, d), F32),
        compiler_params=_params(2),
        name="relu2_mlp",
    )(h2, w_up, w_down, x2)


def kernel(x, mem, norm1_g, w_in, b_f, conv_w, fox_q_g, fox_k_g, mem_norm_g, w_mem_kv, mem_q_g, mem_k_g, w_conv_out, w_fox_out, w_mem_out, w_out, norm2_g, w_up, w_down):
    b, s, d = x.shape
    m = b * s
    conv_width = conv_w.shape[1]
    fox_width = FOX_HEADS * FOX_HEAD_DIM
    mem_width = MEM_HEADS * MEM_HEAD_DIM
    q_row = 3 * conv_width
    k_row = q_row + fox_width
    v_row = k_row + fox_width
    f_row = v_row + fox_width
    mq_row = f_row + FOX_HEADS
    gate_row = mq_row + mem_width

    row = lambda v: v.reshape(1, -1)
    tiles = lambda start, width: tuple(range(start, start + width, 1024))
    bf = jnp.pad(b_f, (0, LANES - FOX_HEADS)).reshape(1, LANES)
    w_t = w_in.T

    h1, qa, ka = _norm1(x, row(norm1_g), w_t, f_row, bf)
    h1 = h1.reshape(m, d)
    conv_v = _proj(h1, w_t, tiles(0, 3 * conv_width) + tiles(v_row, fox_width), "proj_conv_v")
    fq = _proj(h1, w_t, tiles(q_row, fox_width), "proj_fox_q", gain=row(fox_q_g),
               group=FOX_HEAD_DIM, scale=LOG2_E / math.sqrt(FOX_HEAD_DIM))
    fk = _proj(h1, w_t, tiles(k_row, fox_width), "proj_fox_k", gain=row(fox_k_g),
               group=FOX_HEAD_DIM)
    mq = _proj(h1, w_t, tiles(mq_row, mem_width), "proj_mem_q", gain=row(mem_q_g),
               group=MEM_HEAD_DIM, scale=1.0 / math.sqrt(MEM_HEAD_DIM))
    gates = _proj(h1, w_t, tiles(gate_row, N_BRANCHES * d), "proj_gates", gate=True)

    conv_v3 = conv_v.reshape(b, s, -1)
    y_conv = _conv(conv_v3, conv_w)
    y_fox = _fox(fq.reshape(b, s, -1), qa, fk.reshape(b, s, -1), ka, conv_v3, 3 * conv_width)
    kv = _memkv(mem.reshape(-1, d), row(mem_norm_g), w_mem_kv, row(mem_k_g))
    y_mem = _memattn(mq.reshape(b, s, -1), kv.reshape(b, -1, 2 * mem_width))

    merged = _merge(y_conv.reshape(m, -1), y_fox.reshape(m, -1), y_mem.reshape(m, -1),
                    w_conv_out, w_fox_out, w_mem_out, gates, d)
    x2, h2 = _outproj(merged, w_out, x.reshape(m, d), row(norm2_g))
    out = _mlp(h2, w_up, w_down, x2)
    return out.reshape(b, s, d)
```

```python
import functools
import math

import jax
import jax.numpy as jnp
import numpy as np
from jax import lax
from jax.experimental import pallas as pl
from jax.experimental.pallas import tpu as pltpu

F32 = jnp.float32
BF16 = jnp.bfloat16

EPS = 1e-6
LOG2_E = math.log2(math.e)
LANES = 128
CONV_TAPS = 3
FOX_HEADS = 8
FOX_HEAD_DIM = 128
MEM_HEADS = 4
MEM_HEAD_DIM = 256
N_BRANCHES = 3

VMEM_LIMIT_BYTES = 56 * 1024 * 1024
NT_DIMS = (((1,), (1,)), ((), ()))


def _params(n_axes):
    return pltpu.CompilerParams(
        dimension_semantics=("arbitrary",) * n_axes,
        vmem_limit_bytes=VMEM_LIMIT_BYTES)


def _group_rms(a, gain, width, scale=1.0):
    outs = []
    for s in range(0, a.shape[1], width):
        blk = a[:, s:s + width]
        ms = jnp.mean(blk * blk, axis=-1, keepdims=True)
        outs.append(blk * lax.rsqrt(ms + EPS) * (gain * scale))
    return outs


def _split3(c):
    hi = c.astype(BF16).astype(F32)
    r1 = c - hi
    mid = r1.astype(BF16).astype(F32)
    lo = (r1 - mid).astype(BF16).astype(F32)
    return hi, mid, lo


def _decay_selector():
    sel = np.zeros((LANES, 2, FOX_HEADS, LANES), np.float32)
    one = 3 * FOX_HEADS
    for hd in range(FOX_HEADS):
        for piece in range(3):
            sel[piece * FOX_HEADS + hd, 0, hd, piece] = 1.0
            sel[one, 0, hd, 3 + piece] = 1.0
            sel[one, 1, hd, piece] = 1.0
            sel[piece * FOX_HEADS + hd, 1, hd, 3 + piece] = -1.0
    return jnp.asarray(sel.reshape(LANES, -1), BF16)


def _norm1_kernel(x_ref, g_ref, wf_ref, bf_ref, sel_ref, h_ref, qa_ref, ka_ref, carry_ref):
    tm = x_ref.shape[0]

    @pl.when(pl.program_id(1) == 0)
    def _():
        carry_ref[...] = jnp.zeros_like(carry_ref)

    xf = x_ref[...]
    ms = jnp.mean(xf * xf, axis=-1, keepdims=True)
    y = xf * lax.rsqrt(ms + EPS) * g_ref[...]
    y_hi = y.astype(BF16)
    h_ref[...] = y_hi
    y_lo = (y - y_hi.astype(F32)).astype(BF16)

    wf = wf_ref[...]
    wf_hi = wf.astype(BF16).astype(F32)
    w2 = jnp.concatenate(
        [wf_hi, wf - wf_hi, jnp.zeros((LANES - 2 * FOX_HEADS, wf.shape[1]), F32)], axis=0).astype(BF16)
    z2 = (lax.dot_general(y_hi, w2, NT_DIMS, preferred_element_type=F32)
          + lax.dot_general(y_lo, w2, NT_DIMS, preferred_element_type=F32))
    z = z2 + pltpu.roll(z2, LANES - FOX_HEADS, 1) + bf_ref[...]
    log_f = -LOG2_E * (jnp.maximum(-z, 0.0) + jnp.log1p(jnp.exp(-jnp.abs(z))))

    row = lax.broadcasted_iota(jnp.int32, (tm, tm), 0)
    col = lax.broadcasted_iota(jnp.int32, (tm, tm), 1)
    tri = jnp.where(row >= col, 1.0, 0.0).astype(BF16)
    f_hi, f_mid, f_lo = _split3(log_f)
    part = jnp.dot(tri, jnp.concatenate([f_hi, f_mid], axis=1).astype(BF16),
                   preferred_element_type=F32)
    c = (part[:, :LANES] + part[:, LANES:]
         + jnp.dot(tri, f_lo.astype(BF16), preferred_element_type=F32) + carry_ref[0:1, :])
    carry_ref[0:1, :] = c[tm - 1:tm, :]

    lane = lax.broadcasted_iota(jnp.int32, (tm, LANES), 1)
    c_hi, c_mid, c_lo = _split3(c)
    packed = jnp.where(
        lane < FOX_HEADS, c_hi,
        jnp.where(lane < 2 * FOX_HEADS, pltpu.roll(c_mid, FOX_HEADS, 1),
                  jnp.where(lane < 3 * FOX_HEADS, pltpu.roll(c_lo, 2 * FOX_HEADS, 1),
                            jnp.where(lane == 3 * FOX_HEADS, 1.0, 0.0))))
    aug = jnp.dot(packed.astype(BF16), sel_ref[...], preferred_element_type=F32)
    width = qa_ref.shape[1]
    qa_ref[...] = aug[:, :width].astype(BF16)
    ka_ref[...] = aug[:, width:].astype(BF16)


def _norm1(x, g, w_t, f_row, bf, tm=512):
    b, s, d = x.shape
    return pl.pallas_call(
        _norm1_kernel,
        grid=(b, s // tm),
        in_specs=[
            pl.BlockSpec((None, tm, d), lambda i, j: (i, j, 0)),
            pl.BlockSpec((1, d), lambda i, j: (0, 0)),
            pl.BlockSpec((FOX_HEADS, d), lambda i, j: (f_row // FOX_HEADS, 0)),
            pl.BlockSpec((1, LANES), lambda i, j: (0, 0)),
            pl.BlockSpec((LANES, 2 * FOX_HEADS * LANES), lambda i, j: (0, 0)),
        ],
        out_specs=[
            pl.BlockSpec((None, tm, d), lambda i, j: (i, j, 0)),
            pl.BlockSpec((None, tm, FOX_HEADS * LANES), lambda i, j: (i, j, 0)),
            pl.BlockSpec((None, tm, FOX_HEADS * LANES), lambda i, j: (i, j, 0)),
        ],
        out_shape=[
            jax.ShapeDtypeStruct((b, s, d), BF16),
            jax.ShapeDtypeStruct((b, s, FOX_HEADS * LANES), BF16),
            jax.ShapeDtypeStruct((b, s, FOX_HEADS * LANES), BF16),
        ],
        scratch_shapes=[pltpu.VMEM((8, LANES), F32)],
        compiler_params=_params(2),
        name="norm1_decay",
    )(x, g, w_t, bf, _decay_selector())


def _proj_kernel(h_ref, w_ref, *refs, group, scale, gate):
    o_ref, wbf_ref = refs[-2:]

    @pl.when(pl.program_id(1) == 0)
    def _():
        wbf_ref[...] = w_ref[...].astype(BF16)

    acc = lax.dot_general(h_ref[...], wbf_ref[...], NT_DIMS, preferred_element_type=F32)
    if group:
        for i, p in enumerate(_group_rms(acc, refs[0][...], group, scale)):
            o_ref[:, i * group:(i + 1) * group] = p.astype(BF16)
    elif gate:
        o_ref[...] = (0.5 * jnp.tanh(0.5 * acc) + 0.5).astype(BF16)
    else:
        o_ref[...] = acc.astype(BF16)


def _proj(h, w_t, row_starts, name, gain=None, group=0, scale=1.0, gate=False, tm=1024, tn=1024):
    m, k = h.shape
    n_tiles = len(row_starts)
    assert (gain is not None) == bool(group)

    def w_rows(j, i):
        if n_tiles == 1:
            return (row_starts[0], 0)
        start = jnp.int32(row_starts[0])
        for t in range(1, n_tiles):
            start = jnp.where(j == t, row_starts[t], start)
        return (pl.multiple_of(start, 8), 0)

    gain_args = () if gain is None else (gain,)
    return pl.pallas_call(
        functools.partial(_proj_kernel, group=group, scale=scale, gate=gate),
        grid=(n_tiles, m // tm),
        in_specs=[
            pl.BlockSpec((tm, k), lambda j, i: (i, 0)),
            pl.BlockSpec((pl.Element(tn), pl.Element(k)), w_rows),
        ] + [pl.BlockSpec(g.shape, lambda j, i: (0, 0)) for g in gain_args],
        out_specs=pl.BlockSpec((tm, tn), lambda j, i: (i, j)),
        out_shape=jax.ShapeDtypeStruct((m, n_tiles * tn), BF16),
        scratch_shapes=[pltpu.VMEM((tn, k), BF16)],
        compiler_params=_params(2),
        name=name,
    )(h, w_t, *gain_args)


def _conv_kernel(bg_ref, cg_ref, v_ref, w_ref, o_ref, carry_ref):
    tm = bg_ref.shape[0]

    @pl.when(pl.program_id(1) == 0)
    def _():
        carry_ref[...] = jnp.zeros_like(carry_ref)

    u = cg_ref[...].astype(F32) * v_ref[...].astype(F32)
    row = lax.broadcasted_iota(jnp.int32, (tm, 1), 0)
    prev1 = carry_ref[1:2, :]
    prev2 = carry_ref[0:1, :]
    u1 = jnp.where(row == 0, prev1, pltpu.roll(u, 1, 0))
    u2 = jnp.where(row == 0, prev2, jnp.where(row == 1, prev1, pltpu.roll(u, 2, 0)))
    w = w_ref[...]
    y = bg_ref[...].astype(F32) * (w[0:1, :] * u2 + w[1:2, :] * u1 + w[2:3, :] * u)
    o_ref[...] = y.astype(BF16)
    carry_ref[0:2, :] = u[tm - 2:tm, :]


def _conv(proj, conv_w, tm=512):
    b, s, _ = proj.shape
    cw = conv_w.shape[1]
    return pl.pallas_call(
        _conv_kernel,
        grid=(b, s // tm),
        in_specs=[
            pl.BlockSpec((None, tm, cw), lambda i, j: (i, j, 0)),
            pl.BlockSpec((None, tm, cw), lambda i, j: (i, j, 1)),
            pl.BlockSpec((None, tm, cw), lambda i, j: (i, j, 2)),
            pl.BlockSpec((CONV_TAPS, cw), lambda i, j: (0, 0)),
        ],
        out_specs=pl.BlockSpec((None, tm, cw), lambda i, j: (i, j, 0)),
        out_shape=jax.ShapeDtypeStruct((b, s, cw), BF16),
        scratch_shapes=[pltpu.VMEM((8, cw), F32)],
        compiler_params=_params(2),
        name="gated_conv",
    )(proj, proj, proj, conv_w)


def _fox_kernel(q_ref, qa_ref, k_ref, ka_ref, v_ref, o_ref, vt_ref, s_ref, m_ref, l_ref, acc_ref):
    t = q_ref.shape[0] // 2
    d = FOX_HEAD_DIM
    n_heads = q_ref.shape[1] // d
    j = pl.program_id(2)

    @pl.when(j == 0)
    def _():
        for g in range(n_heads):
            for blk in range(v_ref.shape[0] // t):
                v = v_ref[blk * t:(blk + 1) * t, g * d:(g + 1) * d]
                vt_ref[g, blk, :d] = v.astype(F32).T.astype(BF16)
                vt_ref[g, blk, d:] = jnp.ones((vt_ref.shape[2] - d, t), BF16)

    def scores(tile, kb, g):
        q = jnp.concatenate([q_ref[tile * t:(tile + 1) * t, g * d:(g + 1) * d],
                             qa_ref[tile * t:(tile + 1) * t, g * d:(g + 1) * d]], axis=1)
        rows = pl.ds(pl.multiple_of(kb * t, t), t)
        k = jnp.concatenate([k_ref[rows, g * d:(g + 1) * d], ka_ref[rows, g * d:(g + 1) * d]], axis=1)
        return lax.dot_general(k, q, NT_DIMS, preferred_element_type=F32)

    def reset(g):
        m_ref[g] = jnp.full(m_ref.shape[1:], -jnp.inf, F32)
        l_ref[g] = jnp.zeros(l_ref.shape[1:], F32)
        acc_ref[g] = jnp.zeros(acc_ref.shape[1:], F32)

    def consume(kb, slot, g, diagonal):
        s = s_ref[slot, g]
        if diagonal:
            kpos = lax.broadcasted_iota(jnp.int32, (t, t), 0)
            qpos = lax.broadcasted_iota(jnp.int32, (t, t), 1)
            s = jnp.where(qpos >= kpos, s, -jnp.inf)
        m_prev = m_ref[g]
        m_new = jnp.maximum(m_prev, jnp.max(s, axis=0, keepdims=True))
        alpha = jnp.exp2(m_prev - m_new)
        p = jnp.exp2(s - m_new)
        pv = jnp.dot(vt_ref[g, kb], p.astype(BF16), preferred_element_type=F32)
        l_ref[g] = alpha * l_ref[g] + pv[d:d + 1]
        acc_ref[g] = alpha * acc_ref[g] + pv[:d]
        m_ref[g] = m_new

    def advance(tile, kb, slot, next_tile=None, next_kb=None, diagonal=False):
        for g in range(n_heads):
            s_ref[1 - slot, g] = scores(tile if next_tile is None else next_tile,
                                        kb + 1 if next_kb is None else next_kb, g)
            consume(kb, slot, g, diagonal)

    def emit(tile, g):
        o_ref[tile * t:(tile + 1) * t, g * d:(g + 1) * d] = (acc_ref[g] / l_ref[g]).T.astype(BF16)

    for g in range(n_heads):
        reset(g)
        s_ref[0, g] = scores(0, 0, g)

    def first_pairs(pair, carry):
        advance(0, 2 * pair, 0)
        advance(0, 2 * pair + 1, 1)
        return carry

    lax.fori_loop(0, j, first_pairs, 0)
    for g in range(n_heads):
        s_ref[1, g] = scores(1, 0, g)
        consume(2 * j, 0, g, True)
        emit(0, g)
        reset(g)

    def second_pairs(pair, carry):
        advance(1, 2 * pair, 1)
        advance(1, 2 * pair + 1, 0)
        return carry

    lax.fori_loop(0, j, second_pairs, 0)
    advance(1, 2 * j, 1)
    for g in range(n_heads):
        consume(2 * j + 1, 0, g, True)
        emit(1, g)


def _fox(q, qa, k, ka, v, v_col, t=512, heads_per_step=4):
    b, s, _ = q.shape
    d = FOX_HEAD_DIM
    w = heads_per_step * d
    v0 = v_col // w
    tile2 = pl.BlockSpec((None, 2 * t, w), lambda bi, h, j: (bi, j, h))
    seq = lambda col0: pl.BlockSpec((None, s, w), lambda bi, h, j: (bi, 0, col0 + h))
    return pl.pallas_call(
        _fox_kernel,
        grid=(b, FOX_HEADS // heads_per_step, s // (2 * t)),
        in_specs=[tile2, tile2, seq(0), seq(0), seq(v0)],
        out_specs=tile2,
        out_shape=jax.ShapeDtypeStruct((b, s, FOX_HEADS * d), BF16),
        scratch_shapes=[pltpu.VMEM((heads_per_step, s // t, d + 16, t), BF16),
                        pltpu.VMEM((2, heads_per_step, t, t), F32),
                        pltpu.VMEM((heads_per_step, 1, t), F32),
                        pltpu.VMEM((heads_per_step, 1, t), F32),
                        pltpu.VMEM((heads_per_step, d, t), F32)],
        compiler_params=_params(3),
        name="fox_attention",
    )(q, qa, k, ka, v)


def _memkv_kernel(mem_ref, g_ref, w_ref, kg_ref, o_ref):
    xf = mem_ref[...]
    ms = jnp.mean(xf * xf, axis=-1, keepdims=True)
    y = (xf * lax.rsqrt(ms + EPS) * g_ref[...]).astype(BF16)
    acc = jnp.dot(y, w_ref[...].astype(BF16), preferred_element_type=F32)

    @pl.when(pl.program_id(0) == 0)
    def _():
        parts = _group_rms(acc, kg_ref[...], MEM_HEAD_DIM)
        for i, p in enumerate(parts):
            o_ref[:, i * MEM_HEAD_DIM:(i + 1) * MEM_HEAD_DIM] = p.astype(BF16)

    @pl.when(pl.program_id(0) == 1)
    def _():
        o_ref[...] = acc.astype(BF16)


def _memkv(mem2d, g, w, kg):
    m, d = mem2d.shape
    width = MEM_HEADS * MEM_HEAD_DIM
    return pl.pallas_call(
        _memkv_kernel,
        grid=(2,),
        in_specs=[
            pl.BlockSpec((m, d), lambda j: (0, 0)),
            pl.BlockSpec((1, d), lambda j: (0, 0)),
            pl.BlockSpec((d, width), lambda j: (0, j)),
            pl.BlockSpec((1, MEM_HEAD_DIM), lambda j: (0, 0)),
        ],
        out_specs=pl.BlockSpec((m, width), lambda j: (0, j)),
        out_shape=jax.ShapeDtypeStruct((m, 2 * width), BF16),
        compiler_params=_params(1),
        name="mem_kv",
    )(mem2d, g, w, kg)


def _memattn_kernel(q_ref, kv_ref, o_ref):
    width = MEM_HEADS * MEM_HEAD_DIM
    for h in range(MEM_HEADS):
        lo, hi = h * MEM_HEAD_DIM, (h + 1) * MEM_HEAD_DIM
        q = q_ref[:, lo:hi]
        k = kv_ref[:, lo:hi]
        v = kv_ref[:, width + lo:width + hi]
        s = lax.dot_general(q, k, NT_DIMS, preferred_element_type=F32)
        p = jnp.exp(s - jnp.max(s, axis=-1, keepdims=True))
        l = jnp.sum(p, axis=-1, keepdims=True)
        o = jnp.dot(p.astype(BF16), v, preferred_element_type=F32) / l
        o_ref[:, lo:hi] = o.astype(BF16)


def _memattn(mq, kv, tm=512):
    b, s, width = mq.shape
    n_mem = kv.shape[1]
    return pl.pallas_call(
        _memattn_kernel,
        grid=(b, s // tm),
        in_specs=[
            pl.BlockSpec((None, tm, width), lambda i, j: (i, j, 0)),
            pl.BlockSpec((None, n_mem, 2 * width), lambda i, j: (i, 0, 0)),
        ],
        out_specs=pl.BlockSpec((None, tm, width), lambda i, j: (i, j, 0)),
        out_shape=jax.ShapeDtypeStruct((b, s, width), BF16),
        compiler_params=_params(2),
        name="mem_attention",
    )(mq, kv)


def _merge_kernel(a_ref, b_ref, c_ref, wa_ref, wb_ref, wc_ref, ga_ref, gb_ref, gc_ref,
                  o_ref, wabf_ref, wbbf_ref, wcbf_ref):
    @pl.when(pl.program_id(1) == 0)
    def _():
        wabf_ref[...] = wa_ref[...].astype(BF16)
        wbbf_ref[...] = wb_ref[...].astype(BF16)
        wcbf_ref[...] = wc_ref[...].astype(BF16)

    oa = jnp.dot(a_ref[...], wabf_ref[...], preferred_element_type=F32)
    ob = jnp.dot(b_ref[...], wbbf_ref[...], preferred_element_type=F32)
    oc = jnp.dot(c_ref[...], wcbf_ref[...], preferred_element_type=F32)
    merged = (ga_ref[...].astype(F32) * oa + gb_ref[...].astype(F32) * ob
              + gc_ref[...].astype(F32) * oc)
    o_ref[...] = merged.astype(BF16)


def _merge(ya, yb, yc, wa, wb, wc, gates, d_model, tm=512, tn=1024):
    m, k = ya.shape
    nt = d_model // tn
    lhs = pl.BlockSpec((tm, k), lambda j, i: (i, 0))
    wsp = pl.BlockSpec((k, tn), lambda j, i: (0, j))
    return pl.pallas_call(
        _merge_kernel,
        grid=(nt, m // tm),
        in_specs=[lhs, lhs, lhs, wsp, wsp, wsp,
                  pl.BlockSpec((tm, tn), lambda j, i: (i, j)),
                  pl.BlockSpec((tm, tn), lambda j, i: (i, nt + j)),
                  pl.BlockSpec((tm, tn), lambda j, i: (i, 2 * nt + j))],
        out_specs=pl.BlockSpec((tm, tn), lambda j, i: (i, j)),
        out_shape=jax.ShapeDtypeStruct((m, d_model), BF16),
        scratch_shapes=[pltpu.VMEM((k, tn), BF16)] * 3,
        compiler_params=_params(2),
        name="branch_merge",
    )(ya, yb, yc, wa, wb, wc, gates, gates, gates)


def _outproj_kernel(a_ref, w_ref, x_ref, g_ref, x2_ref, h2_ref, wbf_ref):
    @pl.when(pl.program_id(0) == 0)
    def _():
        wbf_ref[...] = w_ref[...].astype(BF16)

    x2 = x_ref[...] + jnp.dot(a_ref[...], wbf_ref[...], preferred_element_type=F32)
    x2_ref[...] = x2
    ms = jnp.mean(x2 * x2, axis=-1, keepdims=True)
    h2_ref[...] = (x2 * lax.rsqrt(ms + EPS) * g_ref[...]).astype(BF16)


def _outproj(a, w, x2d, g, tm=512):
    m, d = x2d.shape
    row = pl.BlockSpec((tm, d), lambda i: (i, 0))
    return pl.pallas_call(
        _outproj_kernel,
        grid=(m // tm,),
        in_specs=[row, pl.BlockSpec((d, d), lambda i: (0, 0), pipeline_mode=pl.Buffered(1)), row,
                  pl.BlockSpec((1, d), lambda i: (0, 0))],
        out_specs=[row, row],
        out_shape=[jax.ShapeDtypeStruct((m, d), F32), jax.ShapeDtypeStruct((m, d), BF16)],
        scratch_shapes=[pltpu.VMEM((d, d), BF16)],
        compiler_params=_params(1),
        name="out_proj_norm2",
    )(a, w, x2d, g)


def _mlp_kernel(h_ref, wu_ref, wd_ref, x_ref, o_ref):
    f = pl.program_id(1)
    rows = x_ref.shape[0]

    @pl.when(f == 0)
    def _():
        o_ref[...] = jnp.zeros_like(o_ref)

    a = jnp.dot(h_ref[...], wu_ref[...].astype(BF16), preferred_element_type=F32)
    a = jnp.square(jnp.maximum(a, 0.0)).astype(BF16)
    o_ref[...] += jnp.dot(a, wd_ref[...].astype(BF16), preferred_element_type=F32)
    slab = pl.ds(pl.multiple_of(f * rows, rows), rows)
    o_ref[slab, :] += x_ref[...]


def _mlp(h2, w_up, w_down, x2, tm=1024, tf=512):
    m, d = h2.shape
    d_ff = w_up.shape[1]
    nf = d_ff // tf
    rows = tm // nf
    return pl.pallas_call(
        _mlp_kernel,
        grid=(m // tm, nf),
        in_specs=[
            pl.BlockSpec((tm, d), lambda i, f: (i, 0)),
            pl.BlockSpec((d, tf), lambda i, f: (0, f)),
            pl.BlockSpec((tf, d), lambda i, f: (f, 0)),
            pl.BlockSpec((rows, d), lambda i, f: (i * nf + f, 0)),
        ],
        out_specs=pl.BlockSpec((tm, d), lambda i, f: (i, 0)),
        out_shape=jax.ShapeDtypeStruct((m, d), F32),
        compiler_params=_params(2),
        name="relu2_mlp",
    )(h2, w_up, w_down, x2)


def kernel(x, mem, norm1_g, w_in, b_f, conv_w, fox_q_g, fox_k_g, mem_norm_g, w_mem_kv, mem_q_g, mem_k_g, w_conv_out, w_fox_out, w_mem_out, w_out, norm2_g, w_up, w_down):
    b, s, d = x.shape
    m = b * s
    conv_width = conv_w.shape[1]
    fox_width = FOX_HEADS * FOX_HEAD_DIM
    mem_width = MEM_HEADS * MEM_HEAD_DIM
    q_row = 3 * conv_width
    k_row = q_row + fox_width
    v_row = k_row + fox_width
    f_row = v_row + fox_width
    mq_row = f_row + FOX_HEADS
    gate_row = mq_row + mem_width

    row = lambda v: v.reshape(1, -1)
    tiles = lambda start, width: tuple(range(start, start + width, 1024))
    bf = jnp.pad(b_f, (0, LANES - FOX_HEADS)).reshape(1, LANES)
    w_t = w_in.T

    h1, qa, ka = _norm1(x, row(norm1_g), w_t, f_row, bf)
    h1 = h1.reshape(m, d)
    conv_v = _proj(h1, w_t, tiles(0, 3 * conv_width) + tiles(v_row, fox_width), "proj_conv_v")
    fq = _proj(h1, w_t, tiles(q_row, fox_width), "proj_fox_q", gain=row(fox_q_g),
               group=FOX_HEAD_DIM, scale=LOG2_E / math.sqrt(FOX_HEAD_DIM))
    fk = _proj(h1, w_t, tiles(k_row, fox_width), "proj_fox_k", gain=row(fox_k_g),
               group=FOX_HEAD_DIM)
    mq = _proj(h1, w_t, tiles(mq_row, mem_width), "proj_mem_q", gain=row(mem_q_g),
               group=MEM_HEAD_DIM, scale=1.0 / math.sqrt(MEM_HEAD_DIM))
    gates = _proj(h1, w_t, tiles(gate_row, N_BRANCHES * d), "proj_gates", gate=True)

    conv_v3 = conv_v.reshape(b, s, -1)
    y_conv = _conv(conv_v3, conv_w)
    y_fox = _fox(fq.reshape(b, s, -1), qa, fk.reshape(b, s, -1), ka, conv_v3, 3 * conv_width)
    kv = _memkv(mem.reshape(-1, d), row(mem_norm_g), w_mem_kv, row(mem_k_g))
    y_mem = _memattn(mq.reshape(b, s, -1), kv.reshape(b, -1, 2 * mem_width))

    merged = _merge(y_conv.reshape(m, -1), y_fox.reshape(m, -1), y_mem.reshape(m, -1),
                    w_conv_out, w_fox_out, w_mem_out, gates, d)
    x2, h2 = _outproj(merged, w_out, x.reshape(m, d), row(norm2_g))
    out = _mlp(h2, w_up, w_down, x2)
    return out.reshape(b, s, d)
```

```python
import functools
import math

import jax
import jax.numpy as jnp
import numpy as np
from jax import lax
from jax.experimental import pallas as pl
from jax.experimental.pallas import tpu as pltpu

F32 = jnp.float32
BF16 = jnp.bfloat16

EPS = 1e-6
LOG2_E = math.log2(math.e)
LANES = 128
CONV_TAPS = 3
FOX_HEADS = 8
FOX_HEAD_DIM = 128
MEM_HEADS = 4
MEM_HEAD_DIM = 256
N_BRANCHES = 3

VMEM_LIMIT_BYTES = 56 * 1024 * 1024
NT_DIMS = (((1,), (1,)), ((), ()))


def _params(n_axes):
    return pltpu.CompilerParams(
        dimension_semantics=("arbitrary",) * n_axes,
        vmem_limit_bytes=VMEM_LIMIT_BYTES)


def _group_rms(a, gain, width, scale=1.0):
    outs = []
    for s in range(0, a.shape[1], width):
        blk = a[:, s:s + width]
        ms = jnp.mean(blk * blk, axis=-1, keepdims=True)
        outs.append(blk * lax.rsqrt(ms + EPS) * (gain * scale))
    return outs


def _split3(c):
    hi = c.astype(BF16).astype(F32)
    r1 = c - hi
    mid = r1.astype(BF16).astype(F32)
    lo = (r1 - mid).astype(BF16).astype(F32)
    return hi, mid, lo


def _decay_selector():
    sel = np.zeros((LANES, 2, FOX_HEADS, LANES), np.float32)
    one = 3 * FOX_HEADS
    for hd in range(FOX_HEADS):
        for piece in range(3):
            sel[piece * FOX_HEADS + hd, 0, hd, piece] = 1.0
            sel[one, 0, hd, 3 + piece] = 1.0
            sel[one, 1, hd, piece] = 1.0
            sel[piece * FOX_HEADS + hd, 1, hd, 3 + piece] = -1.0
    return jnp.asarray(sel.reshape(LANES, -1), BF16)


def _norm1_kernel(x_ref, g_ref, wf_ref, bf_ref, sel_ref, h_ref, qa_ref, ka_ref, carry_ref):
    tm = x_ref.shape[0]

    @pl.when(pl.program_id(1) == 0)
    def _():
        carry_ref[...] = jnp.zeros_like(carry_ref)

    xf = x_ref[...]
    ms = jnp.mean(xf * xf, axis=-1, keepdims=True)
    y = xf * lax.rsqrt(ms + EPS) * g_ref[...]
    y_hi = y.astype(BF16)
    h_ref[...] = y_hi
    y_lo = (y - y_hi.astype(F32)).astype(BF16)

    wf = wf_ref[...]
    wf_hi = wf.astype(BF16).astype(F32)
    w2 = jnp.concatenate(
        [wf_hi, wf - wf_hi, jnp.zeros((LANES - 2 * FOX_HEADS, wf.shape[1]), F32)], axis=0).astype(BF16)
    z2 = (lax.dot_general(y_hi, w2, NT_DIMS, preferred_element_type=F32)
          + lax.dot_general(y_lo, w2, NT_DIMS, preferred_element_type=F32))
    z = z2 + pltpu.roll(z2, LANES - FOX_HEADS, 1) + bf_ref[...]
    log_f = -LOG2_E * (jnp.maximum(-z, 0.0) + jnp.log1p(jnp.exp(-jnp.abs(z))))

    row = lax.broadcasted_iota(jnp.int32, (tm, tm), 0)
    col = lax.broadcasted_iota(jnp.int32, (tm, tm), 1)
    tri = jnp.where(row >= col, 1.0, 0.0).astype(BF16)
    f_hi, f_mid, f_lo = _split3(log_f)
    part = jnp.dot(tri, jnp.concatenate([f_hi, f_mid], axis=1).astype(BF16),
                   preferred_element_type=F32)
    c = (part[:, :LANES] + part[:, LANES:]
         + jnp.dot(tri, f_lo.astype(BF16), preferred_element_type=F32) + carry_ref[0:1, :])
    carry_ref[0:1, :] = c[tm - 1:tm, :]

    lane = lax.broadcasted_iota(jnp.int32, (tm, LANES), 1)
    c_hi, c_mid, c_lo = _split3(c)
    packed = jnp.where(
        lane < FOX_HEADS, c_hi,
        jnp.where(lane < 2 * FOX_HEADS, pltpu.roll(c_mid, FOX_HEADS, 1),
                  jnp.where(lane < 3 * FOX_HEADS, pltpu.roll(c_lo, 2 * FOX_HEADS, 1),
                            jnp.where(lane == 3 * FOX_HEADS, 1.0, 0.0))))
    aug = jnp.dot(packed.astype(BF16), sel_ref[...], preferred_element_type=F32)
    width = qa_ref.shape[1]
    qa_ref[...] = aug[:, :width].astype(BF16)
    ka_ref[...] = aug[:, width:].astype(BF16)


def _norm1(x, g, w_t, f_row, bf, tm=512):
    b, s, d = x.shape
    return pl.pallas_call(
        _norm1_kernel,
        grid=(b, s // tm),
        in_specs=[
            pl.BlockSpec((None, tm, d), lambda i, j: (i, j, 0)),
            pl.BlockSpec((1, d), lambda i, j: (0, 0)),
            pl.BlockSpec((FOX_HEADS, d), lambda i, j: (f_row // FOX_HEADS, 0)),
            pl.BlockSpec((1, LANES), lambda i, j: (0, 0)),
            pl.BlockSpec((LANES, 2 * FOX_HEADS * LANES), lambda i, j: (0, 0)),
        ],
        out_specs=[
            pl.BlockSpec((None, tm, d), lambda i, j: (i, j, 0)),
            pl.BlockSpec((None, tm, FOX_HEADS * LANES), lambda i, j: (i, j, 0)),
            pl.BlockSpec((None, tm, FOX_HEADS * LANES), lambda i, j: (i, j, 0)),
        ],
        out_shape=[
            jax.ShapeDtypeStruct((b, s, d), BF16),
            jax.ShapeDtypeStruct((b, s, FOX_HEADS * LANES), BF16),
            jax.ShapeDtypeStruct((b, s, FOX_HEADS * LANES), BF16),
        ],
        scratch_shapes=[pltpu.VMEM((8, LANES), F32)],
        compiler_params=_params(2),
        name="norm1_decay",
    )(x, g, w_t, bf, _decay_selector())


def _proj_kernel(h_ref, w_ref, *refs, group, gate):
    o_ref, wbf_ref = refs[-2:]

    @pl.when(pl.program_id(1) == 0)
    def _():
        wbf_ref[...] = w_ref[...].astype(BF16)

    acc = lax.dot_general(h_ref[...], wbf_ref[...], NT_DIMS, preferred_element_type=F32)
    if group:
        for i, p in enumerate(_group_rms(acc, refs[0][...], group)):
            o_ref[:, i * group:(i + 1) * group] = p.astype(BF16)
    elif gate:
        o_ref[...] = (0.5 * jnp.tanh(0.5 * acc) + 0.5).astype(BF16)
    else:
        o_ref[...] = acc.astype(BF16)


def _proj(h, w_t, row_starts, name, gains=None, gate=False, tm=1024, tn=1024):
    m, k = h.shape
    n_tiles = len(row_starts)
    group = 0 if gains is None else gains.shape[1]

    def w_rows(j, i):
        if n_tiles == 1:
            return (row_starts[0], 0)
        start = jnp.int32(row_starts[0])
        for t in range(1, n_tiles):
            start = jnp.where(j == t, row_starts[t], start)
        return (pl.multiple_of(start, 8), 0)

    gain_args = () if gains is None else (gains.reshape(n_tiles, 1, group),)
    return pl.pallas_call(
        functools.partial(_proj_kernel, group=group, gate=gate),
        grid=(n_tiles, m // tm),
        in_specs=[
            pl.BlockSpec((tm, k), lambda j, i: (i, 0)),
            pl.BlockSpec((pl.Element(tn), pl.Element(k)), w_rows),
        ] + [pl.BlockSpec((None, 1, group), lambda j, i: (j, 0, 0)) for _ in gain_args],
        out_specs=pl.BlockSpec((tm, tn), lambda j, i: (i, j)),
        out_shape=jax.ShapeDtypeStruct((m, n_tiles * tn), BF16),
        scratch_shapes=[pltpu.VMEM((tn, k), BF16)],
        compiler_params=_params(2),
        name=name,
    )(h, w_t, *gain_args)


def _conv_kernel(bg_ref, cg_ref, v_ref, w_ref, o_ref, carry_ref):
    tm = bg_ref.shape[0]

    @pl.when(pl.program_id(1) == 0)
    def _():
        carry_ref[...] = jnp.zeros_like(carry_ref)

    u = cg_ref[...].astype(F32) * v_ref[...].astype(F32)
    row = lax.broadcasted_iota(jnp.int32, (tm, 1), 0)
    prev1 = carry_ref[1:2, :]
    prev2 = carry_ref[0:1, :]
    u1 = jnp.where(row == 0, prev1, pltpu.roll(u, 1, 0))
    u2 = jnp.where(row == 0, prev2, jnp.where(row == 1, prev1, pltpu.roll(u, 2, 0)))
    w = w_ref[...]
    y = bg_ref[...].astype(F32) * (w[0:1, :] * u2 + w[1:2, :] * u1 + w[2:3, :] * u)
    o_ref[...] = y.astype(BF16)
    carry_ref[0:2, :] = u[tm - 2:tm, :]


def _conv(proj, conv_w, tm=1024):
    b, s, _ = proj.shape
    cw = conv_w.shape[1]
    return pl.pallas_call(
        _conv_kernel,
        grid=(b, s // tm),
        in_specs=[
            pl.BlockSpec((None, tm, cw), lambda i, j: (i, j, 0)),
            pl.BlockSpec((None, tm, cw), lambda i, j: (i, j, 1)),
            pl.BlockSpec((None, tm, cw), lambda i, j: (i, j, 2)),
            pl.BlockSpec((CONV_TAPS, cw), lambda i, j: (0, 0)),
        ],
        out_specs=pl.BlockSpec((None, tm, cw), lambda i, j: (i, j, 0)),
        out_shape=jax.ShapeDtypeStruct((b, s, cw), BF16),
        scratch_shapes=[pltpu.VMEM((8, cw), F32)],
        compiler_params=_params(2),
        name="gated_conv",
    )(proj, proj, proj, conv_w)


def _fox_kernel(q_ref, qa_ref, k_ref, ka_ref, v_ref, o_ref, vt_ref, s_ref, m_ref, l_ref, acc_ref):
    t = q_ref.shape[0] // 2
    d = FOX_HEAD_DIM
    n_heads = q_ref.shape[1] // d
    j = pl.program_id(2)

    @pl.when(j == 0)
    def _():
        for g in range(n_heads):
            for blk in range(v_ref.shape[0] // t):
                v = v_ref[blk * t:(blk + 1) * t, g * d:(g + 1) * d]
                vt_ref[g, blk, :d] = v.astype(F32).T.astype(BF16)
                vt_ref[g, blk, d:] = jnp.ones((vt_ref.shape[2] - d, t), BF16)

    def scores(tile, kb, g):
        q = jnp.concatenate([q_ref[tile * t:(tile + 1) * t, g * d:(g + 1) * d],
                             qa_ref[tile * t:(tile + 1) * t, g * d:(g + 1) * d]], axis=1)
        rows = pl.ds(pl.multiple_of(kb * t, t), t)
        k = jnp.concatenate([k_ref[rows, g * d:(g + 1) * d], ka_ref[rows, g * d:(g + 1) * d]], axis=1)
        return lax.dot_general(k, q, NT_DIMS, preferred_element_type=F32)

    def reset(g):
        m_ref[g] = jnp.full(m_ref.shape[1:], -jnp.inf, F32)
        l_ref[g] = jnp.zeros(l_ref.shape[1:], F32)
        acc_ref[g] = jnp.zeros(acc_ref.shape[1:], F32)

    def update(g, m_new, alpha, pv):
        l_ref[g] = alpha * l_ref[g] + pv[d:d + 1]
        acc_ref[g] = alpha * acc_ref[g] + pv[:d]
        m_ref[g] = m_new

    def consume(kb, slot, g):
        s = s_ref[slot, g]
        m_prev = m_ref[g]
        m_new = jnp.maximum(m_prev, jnp.max(s, axis=0, keepdims=True))
        p = jnp.exp2(s - m_new)
        pv = jnp.dot(vt_ref[g, kb], p.astype(BF16), preferred_element_type=F32)
        update(g, m_new, jnp.exp2(m_prev - m_new), pv)

    def causal(s):
        kpos = lax.broadcasted_iota(jnp.int32, s.shape, 0)
        qpos = lax.broadcasted_iota(jnp.int32, s.shape, 1)
        return jnp.where(qpos >= kpos, s, -jnp.inf)

    def consume_diagonal(kb, slot, g):
        h = t // 2
        s_up = causal(s_ref[slot, g, :h, :])
        s_lo = causal(s_ref[slot, g, h:, h:])
        m_up = jnp.max(s_up, axis=0, keepdims=True)
        m_blk = jnp.concatenate(
            [m_up[:, :h], jnp.maximum(m_up[:, h:], jnp.max(s_lo, axis=0, keepdims=True))], axis=1)
        m_prev = m_ref[g]
        m_new = jnp.maximum(m_prev, m_blk)
        p_up = jnp.exp2(s_up - m_new).astype(BF16)
        p_lo = jnp.exp2(s_lo - m_new[:, h:]).astype(BF16)
        vt = vt_ref[g, kb]
        pv_up = jnp.dot(vt[:, :h], p_up, preferred_element_type=F32)
        pv_lo = jnp.dot(vt[:, h:], p_lo, preferred_element_type=F32)
        pv = jnp.concatenate([pv_up[:, :h], pv_up[:, h:] + pv_lo], axis=1)
        update(g, m_new, jnp.exp2(m_prev - m_new), pv)

    def advance(tile, kb, slot):
        for g in range(n_heads):
            s_ref[1 - slot, g] = scores(tile, kb + 1, g)
            consume(kb, slot, g)

    def emit(tile, g):
        o_ref[tile * t:(tile + 1) * t, g * d:(g + 1) * d] = (acc_ref[g] / l_ref[g]).T.astype(BF16)

    for g in range(n_heads):
        reset(g)
        s_ref[0, g] = scores(0, 0, g)

    def first_pairs(pair, carry):
        advance(0, 2 * pair, 0)
        advance(0, 2 * pair + 1, 1)
        return carry

    lax.fori_loop(0, j, first_pairs, 0)
    for g in range(n_heads):
        s_ref[1, g] = scores(1, 0, g)
        consume_diagonal(2 * j, 0, g)
        emit(0, g)
        reset(g)

    def second_pairs(pair, carry):
        advance(1, 2 * pair, 1)
        advance(1, 2 * pair + 1, 0)
        return carry

    lax.fori_loop(0, j, second_pairs, 0)
    advance(1, 2 * j, 1)
    for g in range(n_heads):
        consume_diagonal(2 * j + 1, 0, g)
        emit(1, g)


def _fox(qk, qa, ka, v, k_col, v_col, t=512, heads_per_step=4):
    b, s, _ = qk.shape
    d = FOX_HEAD_DIM
    w = heads_per_step * d
    tile2 = pl.BlockSpec((None, 2 * t, w), lambda bi, h, j: (bi, j, h))
    seq = lambda col0: pl.BlockSpec((None, s, w), lambda bi, h, j: (bi, 0, col0 // w + h))
    return pl.pallas_call(
        _fox_kernel,
        grid=(b, FOX_HEADS // heads_per_step, s // (2 * t)),
        in_specs=[tile2, tile2, seq(k_col), seq(0), seq(v_col)],
        out_specs=tile2,
        out_shape=jax.ShapeDtypeStruct((b, s, FOX_HEADS * d), BF16),
        scratch_shapes=[pltpu.VMEM((heads_per_step, s // t, d + 16, t), BF16),
                        pltpu.VMEM((2, heads_per_step, t, t), F32),
                        pltpu.VMEM((heads_per_step, 1, t), F32),
                        pltpu.VMEM((heads_per_step, 1, t), F32),
                        pltpu.VMEM((heads_per_step, d, t), F32)],
        compiler_params=_params(3),
        name="fox_attention",
    )(qk, qa, qk, ka, v)


def _memkv_kernel(mem_ref, g_ref, w_ref, kg_ref, o_ref):
    xf = mem_ref[...]
    ms = jnp.mean(xf * xf, axis=-1, keepdims=True)
    y = (xf * lax.rsqrt(ms + EPS) * g_ref[...]).astype(BF16)
    acc = jnp.dot(y, w_ref[...].astype(BF16), preferred_element_type=F32)

    @pl.when(pl.program_id(0) == 0)
    def _():
        parts = _group_rms(acc, kg_ref[...], MEM_HEAD_DIM)
        for i, p in enumerate(parts):
            o_ref[:, i * MEM_HEAD_DIM:(i + 1) * MEM_HEAD_DIM] = p.astype(BF16)

    @pl.when(pl.program_id(0) == 1)
    def _():
        o_ref[...] = acc.astype(BF16)


def _memkv(mem2d, g, w, kg):
    m, d = mem2d.shape
    width = MEM_HEADS * MEM_HEAD_DIM
    return pl.pallas_call(
        _memkv_kernel,
        grid=(2,),
        in_specs=[
            pl.BlockSpec((m, d), lambda j: (0, 0)),
            pl.BlockSpec((1, d), lambda j: (0, 0)),
            pl.BlockSpec((d, width), lambda j: (0, j)),
            pl.BlockSpec((1, MEM_HEAD_DIM), lambda j: (0, 0)),
        ],
        out_specs=pl.BlockSpec((m, width), lambda j: (0, j)),
        out_shape=jax.ShapeDtypeStruct((m, 2 * width), BF16),
        compiler_params=_params(1),
        name="mem_kv",
    )(mem2d, g, w, kg)


def _memattn_kernel(q_ref, kv_ref, o_ref):
    width = MEM_HEADS * MEM_HEAD_DIM
    for h in range(MEM_HEADS):
        lo, hi = h * MEM_HEAD_DIM, (h + 1) * MEM_HEAD_DIM
        q = q_ref[:, lo:hi]
        k = kv_ref[:, lo:hi]
        v = kv_ref[:, width + lo:width + hi]
        s = lax.dot_general(q, k, NT_DIMS, preferred_element_type=F32)
        p = jnp.exp(s - jnp.max(s, axis=-1, keepdims=True))
        l = jnp.sum(p, axis=-1, keepdims=True)
        o = jnp.dot(p.astype(BF16), v, preferred_element_type=F32) / l
        o_ref[:, lo:hi] = o.astype(BF16)


def _memattn(mq, kv, tm=1024):
    b, s, width = mq.shape
    n_mem = kv.shape[1]
    return pl.pallas_call(
        _memattn_kernel,
        grid=(b, s // tm),
        in_specs=[
            pl.BlockSpec((None, tm, width), lambda i, j: (i, j, 0)),
            pl.BlockSpec((None, n_mem, 2 * width), lambda i, j: (i, 0, 0)),
        ],
        out_specs=pl.BlockSpec((None, tm, width), lambda i, j: (i, j, 0)),
        out_shape=jax.ShapeDtypeStruct((b, s, width), BF16),
        compiler_params=_params(2),
        name="mem_attention",
    )(mq, kv)


def _merge_kernel(a_ref, b_ref, c_ref, wa_ref, wb_ref, wc_ref, ga_ref, gb_ref, gc_ref,
                  o_ref, wabf_ref, wbbf_ref, wcbf_ref):
    @pl.when(pl.program_id(1) == 0)
    def _():
        wabf_ref[...] = wa_ref[...].astype(BF16)
        wbbf_ref[...] = wb_ref[...].astype(BF16)
        wcbf_ref[...] = wc_ref[...].astype(BF16)

    oa = jnp.dot(a_ref[...], wabf_ref[...], preferred_element_type=F32)
    ob = jnp.dot(b_ref[...], wbbf_ref[...], preferred_element_type=F32)
    oc = jnp.dot(c_ref[...], wcbf_ref[...], preferred_element_type=F32)
    merged = (ga_ref[...].astype(F32) * oa + gb_ref[...].astype(F32) * ob
              + gc_ref[...].astype(F32) * oc)
    o_ref[...] = merged.astype(BF16)


def _merge(ya, yb, yc, wa, wb, wc, gates, d_model, tm=512, tn=1024):
    m, k = ya.shape
    nt = d_model // tn
    lhs = pl.BlockSpec((tm, k), lambda j, i: (i, 0))
    wsp = pl.BlockSpec((k, tn), lambda j, i: (0, j))
    return pl.pallas_call(
        _merge_kernel,
        grid=(nt, m // tm),
        in_specs=[lhs, lhs, lhs, wsp, wsp, wsp,
                  pl.BlockSpec((tm, tn), lambda j, i: (i, j)),
                  pl.BlockSpec((tm, tn), lambda j, i: (i, nt + j)),
                  pl.BlockSpec((tm, tn), lambda j, i: (i, 2 * nt + j))],
        out_specs=pl.BlockSpec((tm, tn), lambda j, i: (i, j)),
        out_shape=jax.ShapeDtypeStruct((m, d_model), BF16),
        scratch_shapes=[pltpu.VMEM((k, tn), BF16)] * 3,
        compiler_params=_params(2),
        name="branch_merge",
    )(ya, yb, yc, wa, wb, wc, gates, gates, gates)


def _outproj_kernel(a_ref, w_ref, x_ref, g_ref, x2_ref, h2_ref, wbf_ref):
    @pl.when(pl.program_id(0) == 0)
    def _():
        wbf_ref[...] = w_ref[...].astype(BF16)

    x2 = x_ref[...] + jnp.dot(a_ref[...], wbf_ref[...], preferred_element_type=F32)
    x2_ref[...] = x2
    ms = jnp.mean(x2 * x2, axis=-1, keepdims=True)
    h2_ref[...] = (x2 * lax.rsqrt(ms + EPS) * g_ref[...]).astype(BF16)


def _outproj(a, w, x2d, g, tm=512):
    m, d = x2d.shape
    row = pl.BlockSpec((tm, d), lambda i: (i, 0))
    return pl.pallas_call(
        _outproj_kernel,
        grid=(m // tm,),
        in_specs=[row, pl.BlockSpec((d, d), lambda i: (0, 0), pipeline_mode=pl.Buffered(1)), row,
                  pl.BlockSpec((1, d), lambda i: (0, 0))],
        out_specs=[row, row],
        out_shape=[jax.ShapeDtypeStruct((m, d), F32), jax.ShapeDtypeStruct((m, d), BF16)],
        scratch_shapes=[pltpu.VMEM((d, d), BF16)],
        compiler_params=_params(1),
        name="out_proj_norm2",
    )(a, w, x2d, g)


def _mlp_kernel(h_ref, wu_ref, wd_ref, x_ref, o_ref):
    f = pl.program_id(1)
    rows = x_ref.shape[0]

    @pl.when(f == 0)
    def _():
        o_ref[...] = jnp.zeros_like(o_ref)

    a = jnp.dot(h_ref[...], wu_ref[...].astype(BF16), preferred_element_type=F32)
    a = jnp.square(jnp.maximum(a, 0.0)).astype(BF16)
    o_ref[...] += jnp.dot(a, wd_ref[...].astype(BF16), preferred_element_type=F32)
    slab = pl.ds(pl.multiple_of(f * rows, rows), rows)
    o_ref[slab, :] += x_ref[...]


def _mlp(h2, w_up, w_down, x2, tm=1024, tf=512):
    m, d = h2.shape
    d_ff = w_up.shape[1]
    nf = d_ff // tf
    rows = tm // nf
    return pl.pallas_call(
        _mlp_kernel,
        grid=(m // tm, nf),
        in_specs=[
            pl.BlockSpec((tm, d), lambda i, f: (i, 0)),
            pl.BlockSpec((d, tf), lambda i, f: (0, f)),
            pl.BlockSpec((tf, d), lambda i, f: (f, 0)),
            pl.BlockSpec((rows, d), lambda i, f: (i * nf + f, 0)),
        ],
        out_specs=pl.BlockSpec((tm, d), lambda i, f: (i, 0)),
        out_shape=jax.ShapeDtypeStruct((m, d), F32),
        compiler_params=_params(2),
        name="relu2_mlp",
    )(h2, w_up, w_down, x2)


def kernel(x, mem, norm1_g, w_in, b_f, conv_w, fox_q_g, fox_k_g, mem_norm_g, w_mem_kv, mem_q_g, mem_k_g, w_conv_out, w_fox_out, w_mem_out, w_out, norm2_g, w_up, w_down):
    b, s, d = x.shape
    m = b * s
    conv_width = conv_w.shape[1]
    fox_width = FOX_HEADS * FOX_HEAD_DIM
    mem_width = MEM_HEADS * MEM_HEAD_DIM
    q_row = 3 * conv_width
    k_row = q_row + fox_width
    v_row = k_row + fox_width
    f_row = v_row + fox_width
    mq_row = f_row + FOX_HEADS
    gate_row = mq_row + mem_width

    row = lambda v: v.reshape(1, -1)
    tiles = lambda start, width: tuple(range(start, start + width, 1024))
    bf = jnp.pad(b_f, (0, LANES - FOX_HEADS)).reshape(1, LANES)
    w_t = w_in.T

    h1, qa, ka = _norm1(x, row(norm1_g), w_t, f_row, bf)
    h1 = h1.reshape(m, d)
    conv_v = _proj(h1, w_t, tiles(0, 3 * conv_width) + tiles(v_row, fox_width), "proj_conv_v")
    qk_gains = jnp.stack([fox_q_g * (LOG2_E / math.sqrt(FOX_HEAD_DIM)), fox_k_g])
    fqk = _proj(h1, w_t, tiles(q_row, 2 * fox_width), "proj_fox_qk", gains=qk_gains)
    mq = _proj(h1, w_t, tiles(mq_row, mem_width), "proj_mem_q",
               gains=row(mem_q_g) * (1.0 / math.sqrt(MEM_HEAD_DIM)))
    gates = _proj(h1, w_t, tiles(gate_row, N_BRANCHES * d), "proj_gates", gate=True)

    conv_v3 = conv_v.reshape(b, s, -1)
    y_conv = _conv(conv_v3, conv_w)
    y_fox = _fox(fqk.reshape(b, s, -1), qa, ka, conv_v3, k_col=fox_width, v_col=3 * conv_width)
    kv = _memkv(mem.reshape(-1, d), row(mem_norm_g), w_mem_kv, row(mem_k_g))
    y_mem = _memattn(mq.reshape(b, s, -1), kv.reshape(b, -1, 2 * mem_width))

    merged = _merge(y_conv.reshape(m, -1), y_fox.reshape(m, -1), y_mem.reshape(m, -1),
                    w_conv_out, w_fox_out, w_mem_out, gates, d)
    x2, h2 = _outproj(merged, w_out, x.reshape(m, d), row(norm2_g))
    out = _mlp(h2, w_up, w_down, x2)
    return out.reshape(b, s, d)
```

```python
import functools
import math

import jax
import jax.numpy as jnp
import numpy as np
from jax import lax
from jax.experimental import pallas as pl
from jax.experimental.pallas import tpu as pltpu

F32 = jnp.float32
BF16 = jnp.bfloat16

EPS = 1e-6
LOG2_E = math.log2(math.e)
LANES = 128
CONV_TAPS = 3
FOX_HEADS = 8
FOX_HEAD_DIM = 128
MEM_HEADS = 4
MEM_HEAD_DIM = 256
N_BRANCHES = 3

VMEM_LIMIT_BYTES = 56 * 1024 * 1024
NT_DIMS = (((1,), (1,)), ((), ()))


def _params(n_axes):
    return pltpu.CompilerParams(
        dimension_semantics=("arbitrary",) * n_axes,
        vmem_limit_bytes=VMEM_LIMIT_BYTES)


def _group_rms(a, gain, width, scale=1.0):
    outs = []
    for s in range(0, a.shape[1], width):
        blk = a[:, s:s + width]
        ms = jnp.mean(blk * blk, axis=-1, keepdims=True)
        outs.append(blk * lax.rsqrt(ms + EPS) * (gain * scale))
    return outs


def _split3(c):
    hi = c.astype(BF16).astype(F32)
    r1 = c - hi
    mid = r1.astype(BF16).astype(F32)
    lo = (r1 - mid).astype(BF16).astype(F32)
    return hi, mid, lo


def _decay_selector():
    sel = np.zeros((LANES, 2, FOX_HEADS, LANES), np.float32)
    one = 3 * FOX_HEADS
    for hd in range(FOX_HEADS):
        for piece in range(3):
            sel[piece * FOX_HEADS + hd, 0, hd, piece] = 1.0
            sel[one, 0, hd, 3 + piece] = 1.0
            sel[one, 1, hd, piece] = 1.0
            sel[piece * FOX_HEADS + hd, 1, hd, 3 + piece] = -1.0
    return jnp.asarray(sel.reshape(LANES, -1), BF16)


def _norm1_kernel(x_ref, g_ref, wf_ref, bf_ref, sel_ref, h_ref, qa_ref, ka_ref, carry_ref):
    tm = x_ref.shape[0]

    @pl.when(pl.program_id(1) == 0)
    def _():
        carry_ref[...] = jnp.zeros_like(carry_ref)

    xf = x_ref[...]
    ms = jnp.mean(xf * xf, axis=-1, keepdims=True)
    y = xf * lax.rsqrt(ms + EPS) * g_ref[...]
    y_hi = y.astype(BF16)
    h_ref[...] = y_hi
    y_lo = (y - y_hi.astype(F32)).astype(BF16)

    wf = wf_ref[...]
    wf_hi = wf.astype(BF16).astype(F32)
    w2 = jnp.concatenate(
        [wf_hi, wf - wf_hi, jnp.zeros((LANES - 2 * FOX_HEADS, wf.shape[1]), F32)], axis=0).astype(BF16)
    z2 = (lax.dot_general(y_hi, w2, NT_DIMS, preferred_element_type=F32)
          + lax.dot_general(y_lo, w2, NT_DIMS, preferred_element_type=F32))
    z = z2 + pltpu.roll(z2, LANES - FOX_HEADS, 1) + bf_ref[...]
    log_f = -LOG2_E * (jnp.maximum(-z, 0.0) + jnp.log1p(jnp.exp(-jnp.abs(z))))

    row = lax.broadcasted_iota(jnp.int32, (tm, tm), 0)
    col = lax.broadcasted_iota(jnp.int32, (tm, tm), 1)
    tri = jnp.where(row >= col, 1.0, 0.0).astype(BF16)
    f_hi, f_mid, f_lo = _split3(log_f)
    part = jnp.dot(tri, jnp.concatenate([f_hi, f_mid], axis=1).astype(BF16),
                   preferred_element_type=F32)
    c = (part[:, :LANES] + part[:, LANES:]
         + jnp.dot(tri, f_lo.astype(BF16), preferred_element_type=F32) + carry_ref[0:1, :])
    carry_ref[0:1, :] = c[tm - 1:tm, :]

    lane = lax.broadcasted_iota(jnp.int32, (tm, LANES), 1)
    c_hi, c_mid, c_lo = _split3(c)
    packed = jnp.where(
        lane < FOX_HEADS, c_hi,
        jnp.where(lane < 2 * FOX_HEADS, pltpu.roll(c_mid, FOX_HEADS, 1),
                  jnp.where(lane < 3 * FOX_HEADS, pltpu.roll(c_lo, 2 * FOX_HEADS, 1),
                            jnp.where(lane == 3 * FOX_HEADS, 1.0, 0.0))))
    aug = jnp.dot(packed.astype(BF16), sel_ref[...], preferred_element_type=F32)
    width = qa_ref.shape[1]
    qa_ref[...] = aug[:, :width].astype(BF16)
    ka_ref[...] = aug[:, width:].astype(BF16)


def _norm1(x, g, w_t, f_row, bf, tm=512):
    b, s, d = x.shape
    return pl.pallas_call(
        _norm1_kernel,
        grid=(b, s // tm),
        in_specs=[
            pl.BlockSpec((None, tm, d), lambda i, j: (i, j, 0)),
            pl.BlockSpec((1, d), lambda i, j: (0, 0)),
            pl.BlockSpec((FOX_HEADS, d), lambda i, j: (f_row // FOX_HEADS, 0)),
            pl.BlockSpec((1, LANES), lambda i, j: (0, 0)),
            pl.BlockSpec((LANES, 2 * FOX_HEADS * LANES), lambda i, j: (0, 0)),
        ],
        out_specs=[
            pl.BlockSpec((None, tm, d), lambda i, j: (i, j, 0)),
            pl.BlockSpec((None, tm, FOX_HEADS * LANES), lambda i, j: (i, j, 0)),
            pl.BlockSpec((None, tm, FOX_HEADS * LANES), lambda i, j: (i, j, 0)),
        ],
        out_shape=[
            jax.ShapeDtypeStruct((b, s, d), BF16),
            jax.ShapeDtypeStruct((b, s, FOX_HEADS * LANES), BF16),
            jax.ShapeDtypeStruct((b, s, FOX_HEADS * LANES), BF16),
        ],
        scratch_shapes=[pltpu.VMEM((8, LANES), F32)],
        compiler_params=_params(2),
        name="norm1_decay",
    )(x, g, w_t, bf, _decay_selector())


def _proj_kernel(h_ref, w_ref, *refs, group, gate):
    o_ref, wbf_ref = refs[-2:]

    @pl.when(pl.program_id(1) == 0)
    def _():
        wbf_ref[...] = w_ref[...].astype(BF16)

    acc = lax.dot_general(h_ref[...], wbf_ref[...], NT_DIMS, preferred_element_type=F32)
    if group:
        for i, p in enumerate(_group_rms(acc, refs[0][...], group)):
            o_ref[:, i * group:(i + 1) * group] = p.astype(BF16)
    elif gate:
        o_ref[...] = (0.5 * jnp.tanh(0.5 * acc) + 0.5).astype(BF16)
    else:
        o_ref[...] = acc.astype(BF16)


def _proj(h, w_t, row_starts, name, gains=None, gate=False, tm=1024, tn=1024):
    m, k = h.shape
    n_tiles = len(row_starts)
    group = 0 if gains is None else gains.shape[1]

    def w_rows(j, i):
        if n_tiles == 1:
            return (row_starts[0], 0)
        start = jnp.int32(row_starts[0])
        for t in range(1, n_tiles):
            start = jnp.where(j == t, row_starts[t], start)
        return (pl.multiple_of(start, 8), 0)

    gain_args = () if gains is None else (gains.reshape(n_tiles, 1, group),)
    return pl.pallas_call(
        functools.partial(_proj_kernel, group=group, gate=gate),
        grid=(n_tiles, m // tm),
        in_specs=[
            pl.BlockSpec((tm, k), lambda j, i: (i, 0)),
            pl.BlockSpec((pl.Element(tn), pl.Element(k)), w_rows),
        ] + [pl.BlockSpec((None, 1, group), lambda j, i: (j, 0, 0)) for _ in gain_args],
        out_specs=pl.BlockSpec((tm, tn), lambda j, i: (i, j)),
        out_shape=jax.ShapeDtypeStruct((m, n_tiles * tn), BF16),
        scratch_shapes=[pltpu.VMEM((tn, k), BF16)],
        compiler_params=_params(2),
        name=name,
    )(h, w_t, *gain_args)


def _gated_conv_tile(bg_ref, cg_ref, v_ref, w_ref, carry_ref):
    tm = bg_ref.shape[0]
    u = cg_ref[...].astype(F32) * v_ref[...].astype(F32)
    w = w_ref[...]
    taps = lambda u2, u1, u0: w[0:1, :] * u2 + w[1:2, :] * u1 + w[2:3, :] * u0
    conv = taps(pltpu.roll(u, 2, 0), pltpu.roll(u, 1, 0), u)
    top = u[0:8, :]
    row = lax.broadcasted_iota(jnp.int32, (8, 1), 0)
    prev1 = carry_ref[1:2, :]
    prev2 = carry_ref[0:1, :]
    top1 = jnp.where(row == 0, prev1, pltpu.roll(top, 1, 0))
    top2 = jnp.where(row == 0, prev2, jnp.where(row == 1, prev1, pltpu.roll(top, 2, 0)))
    conv = jnp.concatenate([taps(top2, top1, top), conv[8:, :]], axis=0)
    carry_ref[0:2, :] = u[tm - 2:tm, :]
    return bg_ref[...].astype(F32) * conv


def _fox_kernel(q_ref, qa_ref, k_ref, ka_ref, v_ref, o_ref, vt_ref, s_ref, m_ref, l_ref, acc_ref):
    t = q_ref.shape[0] // 2
    d = FOX_HEAD_DIM
    n_heads = q_ref.shape[1] // d
    j = pl.program_id(2)

    @pl.when(j == 0)
    def _():
        for g in range(n_heads):
            for blk in range(v_ref.shape[0] // t):
                v = v_ref[blk * t:(blk + 1) * t, g * d:(g + 1) * d]
                vt_ref[g, blk, :d] = v.astype(F32).T.astype(BF16)
                vt_ref[g, blk, d:] = jnp.ones((vt_ref.shape[2] - d, t), BF16)

    def scores(tile, kb, g):
        q = jnp.concatenate([q_ref[tile * t:(tile + 1) * t, g * d:(g + 1) * d],
                             qa_ref[tile * t:(tile + 1) * t, g * d:(g + 1) * d]], axis=1)
        rows = pl.ds(pl.multiple_of(kb * t, t), t)
        k = jnp.concatenate([k_ref[rows, g * d:(g + 1) * d], ka_ref[rows, g * d:(g + 1) * d]], axis=1)
        return lax.dot_general(k, q, NT_DIMS, preferred_element_type=F32)

    def reset(g):
        m_ref[g] = jnp.full(m_ref.shape[1:], -jnp.inf, F32)
        l_ref[g] = jnp.zeros(l_ref.shape[1:], F32)
        acc_ref[g] = jnp.zeros(acc_ref.shape[1:], F32)

    def update(g, m_new, alpha, pv):
        l_ref[g] = alpha * l_ref[g] + pv[d:d + 1]
        acc_ref[g] = alpha * acc_ref[g] + pv[:d]
        m_ref[g] = m_new

    def consume(kb, slot, g):
        s = s_ref[slot, g]
        m_prev = m_ref[g]
        m_new = jnp.maximum(m_prev, jnp.max(s, axis=0, keepdims=True))
        p = jnp.exp2(s - m_new)
        pv = jnp.dot(vt_ref[g, kb], p.astype(BF16), preferred_element_type=F32)
        update(g, m_new, jnp.exp2(m_prev - m_new), pv)

    def causal(s):
        kpos = lax.broadcasted_iota(jnp.int32, s.shape, 0)
        qpos = lax.broadcasted_iota(jnp.int32, s.shape, 1)
        return jnp.where(qpos >= kpos, s, -jnp.inf)

    def consume_diagonal(kb, slot, g):
        h = t // 2
        s_up = causal(s_ref[slot, g, :h, :])
        s_lo = causal(s_ref[slot, g, h:, h:])
        m_up = jnp.max(s_up, axis=0, keepdims=True)
        m_blk = jnp.concatenate(
            [m_up[:, :h], jnp.maximum(m_up[:, h:], jnp.max(s_lo, axis=0, keepdims=True))], axis=1)
        m_prev = m_ref[g]
        m_new = jnp.maximum(m_prev, m_blk)
        p_up = jnp.exp2(s_up - m_new).astype(BF16)
        p_lo = jnp.exp2(s_lo - m_new[:, h:]).astype(BF16)
        vt = vt_ref[g, kb]
        pv_up = jnp.dot(vt[:, :h], p_up, preferred_element_type=F32)
        pv_lo = jnp.dot(vt[:, h:], p_lo, preferred_element_type=F32)
        pv = jnp.concatenate([pv_up[:, :h], pv_up[:, h:] + pv_lo], axis=1)
        update(g, m_new, jnp.exp2(m_prev - m_new), pv)

    def advance(tile, kb, slot):
        for g in range(n_heads):
            s_ref[1 - slot, g] = scores(tile, kb + 1, g)
            consume(kb, slot, g)

    def emit(tile, g):
        o_ref[tile * t:(tile + 1) * t, g * d:(g + 1) * d] = (acc_ref[g] / l_ref[g]).T.astype(BF16)

    for g in range(n_heads):
        reset(g)
        s_ref[0, g] = scores(0, 0, g)

    def first_pairs(pair, carry):
        advance(0, 2 * pair, 0)
        advance(0, 2 * pair + 1, 1)
        return carry

    lax.fori_loop(0, j, first_pairs, 0)
    for g in range(n_heads):
        s_ref[1, g] = scores(1, 0, g)
        consume_diagonal(2 * j, 0, g)
        emit(0, g)
        reset(g)

    def second_pairs(pair, carry):
        advance(1, 2 * pair, 1)
        advance(1, 2 * pair + 1, 0)
        return carry

    lax.fori_loop(0, j, second_pairs, 0)
    advance(1, 2 * j, 1)
    for g in range(n_heads):
        consume_diagonal(2 * j + 1, 0, g)
        emit(1, g)


def _fox(qk, qa, ka, v, k_col, v_col, t=512, heads_per_step=4):
    b, s, _ = qk.shape
    d = FOX_HEAD_DIM
    w = heads_per_step * d
    tile2 = pl.BlockSpec((None, 2 * t, w), lambda bi, h, j: (bi, j, h))
    seq = lambda col0: pl.BlockSpec((None, s, w), lambda bi, h, j: (bi, 0, col0 // w + h))
    return pl.pallas_call(
        _fox_kernel,
        grid=(b, FOX_HEADS // heads_per_step, s // (2 * t)),
        in_specs=[tile2, tile2, seq(k_col), seq(0), seq(v_col)],
        out_specs=tile2,
        out_shape=jax.ShapeDtypeStruct((b, s, FOX_HEADS * d), BF16),
        scratch_shapes=[pltpu.VMEM((heads_per_step, s // t, d + 16, t), BF16),
                        pltpu.VMEM((2, heads_per_step, t, t), F32),
                        pltpu.VMEM((heads_per_step, 1, t), F32),
                        pltpu.VMEM((heads_per_step, 1, t), F32),
                        pltpu.VMEM((heads_per_step, d, t), F32)],
        compiler_params=_params(3),
        name="fox_attention",
    )(qk, qa, qk, ka, v)


def _memkv_kernel(mem_ref, g_ref, w_ref, kg_ref, o_ref):
    xf = mem_ref[...]
    ms = jnp.mean(xf * xf, axis=-1, keepdims=True)
    y = (xf * lax.rsqrt(ms + EPS) * g_ref[...]).astype(BF16)
    acc = jnp.dot(y, w_ref[...].astype(BF16), preferred_element_type=F32)

    @pl.when(pl.program_id(0) == 0)
    def _():
        parts = _group_rms(acc, kg_ref[...], MEM_HEAD_DIM)
        for i, p in enumerate(parts):
            o_ref[:, i * MEM_HEAD_DIM:(i + 1) * MEM_HEAD_DIM] = p.astype(BF16)

    @pl.when(pl.program_id(0) == 1)
    def _():
        o_ref[...] = acc.astype(BF16)


def _memkv(mem2d, g, w, kg):
    m, d = mem2d.shape
    width = MEM_HEADS * MEM_HEAD_DIM
    return pl.pallas_call(
        _memkv_kernel,
        grid=(2,),
        in_specs=[
            pl.BlockSpec((m, d), lambda j: (0, 0)),
            pl.BlockSpec((1, d), lambda j: (0, 0)),
            pl.BlockSpec((d, width), lambda j: (0, j)),
            pl.BlockSpec((1, MEM_HEAD_DIM), lambda j: (0, 0)),
        ],
        out_specs=pl.BlockSpec((m, width), lambda j: (0, j)),
        out_shape=jax.ShapeDtypeStruct((m, 2 * width), BF16),
        compiler_params=_params(1),
        name="mem_kv",
    )(mem2d, g, w, kg)


def _memattn_kernel(q_ref, kv_ref, o_ref):
    width = MEM_HEADS * MEM_HEAD_DIM
    for h in range(MEM_HEADS):
        lo, hi = h * MEM_HEAD_DIM, (h + 1) * MEM_HEAD_DIM
        q = q_ref[:, lo:hi]
        k = kv_ref[:, lo:hi]
        v = kv_ref[:, width + lo:width + hi]
        s = lax.dot_general(q, k, NT_DIMS, preferred_element_type=F32)
        p = jnp.exp(s - jnp.max(s, axis=-1, keepdims=True))
        l = jnp.sum(p, axis=-1, keepdims=True)
        o = jnp.dot(p.astype(BF16), v, preferred_element_type=F32) / l
        o_ref[:, lo:hi] = o.astype(BF16)


def _memattn(mq, kv, tm=1024):
    b, s, width = mq.shape
    n_mem = kv.shape[1]
    return pl.pallas_call(
        _memattn_kernel,
        grid=(b, s // tm),
        in_specs=[
            pl.BlockSpec((None, tm, width), lambda i, j: (i, j, 0)),
            pl.BlockSpec((None, n_mem, 2 * width), lambda i, j: (i, 0, 0)),
        ],
        out_specs=pl.BlockSpec((None, tm, width), lambda i, j: (i, j, 0)),
        out_shape=jax.ShapeDtypeStruct((b, s, width), BF16),
        compiler_params=_params(2),
        name="mem_attention",
    )(mq, kv)


def _merge_kernel(bg_ref, cg_ref, cv_ref, cw_ref, b_ref, c_ref, wa_ref, wb_ref, wc_ref,
                  ga_ref, gb_ref, gc_ref, o_ref, wabf_ref, wbbf_ref, wcbf_ref, carry_ref, *,
                  tiles_per_seq):
    i = pl.program_id(1)

    @pl.when(i == 0)
    def _():
        wabf_ref[...] = wa_ref[...].astype(BF16)
        wbbf_ref[...] = wb_ref[...].astype(BF16)
        wcbf_ref[...] = wc_ref[...].astype(BF16)

    @pl.when(i % tiles_per_seq == 0)
    def _():
        carry_ref[...] = jnp.zeros_like(carry_ref)

    y_conv = _gated_conv_tile(bg_ref, cg_ref, cv_ref, cw_ref, carry_ref).astype(BF16)
    oa = jnp.dot(y_conv, wabf_ref[...], preferred_element_type=F32)
    ob = jnp.dot(b_ref[...], wbbf_ref[...], preferred_element_type=F32)
    oc = jnp.dot(c_ref[...], wcbf_ref[...], preferred_element_type=F32)
    merged = (ga_ref[...].astype(F32) * oa + gb_ref[...].astype(F32) * ob
              + gc_ref[...].astype(F32) * oc)
    o_ref[...] = merged.astype(BF16)


def _merge(conv_in, conv_w, yb, yc, wa, wb, wc, gates, seq_len, d_model, tm=512, tn=1024):
    m, k = yb.shape
    nt = d_model // tn
    lhs = lambda col: pl.BlockSpec((tm, k), lambda j, i: (i, col))
    wsp = pl.BlockSpec((k, tn), lambda j, i: (0, j))
    return pl.pallas_call(
        functools.partial(_merge_kernel, tiles_per_seq=seq_len // tm),
        grid=(nt, m // tm),
        in_specs=[lhs(0), lhs(1), lhs(2), pl.BlockSpec((CONV_TAPS, k), lambda j, i: (0, 0)),
                  lhs(0), lhs(0), wsp, wsp, wsp,
                  pl.BlockSpec((tm, tn), lambda j, i: (i, j)),
                  pl.BlockSpec((tm, tn), lambda j, i: (i, nt + j)),
                  pl.BlockSpec((tm, tn), lambda j, i: (i, 2 * nt + j))],
        out_specs=pl.BlockSpec((tm, tn), lambda j, i: (i, j)),
        out_shape=jax.ShapeDtypeStruct((m, d_model), BF16),
        scratch_shapes=[pltpu.VMEM((k, tn), BF16)] * 3 + [pltpu.VMEM((8, k), F32)],
        compiler_params=_params(2),
        name="branch_merge",
    )(conv_in, conv_in, conv_in, conv_w, yb, yc, wa, wb, wc, gates, gates, gates)


def _outproj_kernel(a_ref, w_ref, x_ref, g_ref, x2_ref, h2_ref, wbf_ref):
    @pl.when(pl.program_id(0) == 0)
    def _():
        wbf_ref[...] = w_ref[...].astype(BF16)

    x2 = x_ref[...] + jnp.dot(a_ref[...], wbf_ref[...], preferred_element_type=F32)
    x2_ref[...] = x2
    ms = jnp.mean(x2 * x2, axis=-1, keepdims=True)
    h2_ref[...] = (x2 * lax.rsqrt(ms + EPS) * g_ref[...]).astype(BF16)


def _outproj(a, w, x2d, g, tm=512):
    m, d = x2d.shape
    row = pl.BlockSpec((tm, d), lambda i: (i, 0))
    return pl.pallas_call(
        _outproj_kernel,
        grid=(m // tm,),
        in_specs=[row, pl.BlockSpec((d, d), lambda i: (0, 0), pipeline_mode=pl.Buffered(1)), row,
                  pl.BlockSpec((1, d), lambda i: (0, 0))],
        out_specs=[row, row],
        out_shape=[jax.ShapeDtypeStruct((m, d), F32), jax.ShapeDtypeStruct((m, d), BF16)],
        scratch_shapes=[pltpu.VMEM((d, d), BF16)],
        compiler_params=_params(1),
        name="out_proj_norm2",
    )(a, w, x2d, g)


def _mlp_kernel(h_ref, wu_ref, wd_ref, x_ref, o_ref):
    f = pl.program_id(1)
    rows = x_ref.shape[0]

    @pl.when(f == 0)
    def _():
        o_ref[...] = jnp.zeros_like(o_ref)

    a = jnp.dot(h_ref[...], wu_ref[...].astype(BF16), preferred_element_type=F32)
    a = jnp.square(jnp.maximum(a, 0.0)).astype(BF16)
    o_ref[...] += jnp.dot(a, wd_ref[...].astype(BF16), preferred_element_type=F32)
    slab = pl.ds(pl.multiple_of(f * rows, rows), rows)
    o_ref[slab, :] += x_ref[...]


def _mlp(h2, w_up, w_down, x2, tm=1024, tf=512):
    m, d = h2.shape
    d_ff = w_up.shape[1]
    nf = d_ff // tf
    rows = tm // nf
    return pl.pallas_call(
        _mlp_kernel,
        grid=(m // tm, nf),
        in_specs=[
            pl.BlockSpec((tm, d), lambda i, f: (i, 0)),
            pl.BlockSpec((d, tf), lambda i, f: (0, f)),
            pl.BlockSpec((tf, d), lambda i, f: (f, 0)),
            pl.BlockSpec((rows, d), lambda i, f: (i * nf + f, 0)),
        ],
        out_specs=pl.BlockSpec((tm, d), lambda i, f: (i, 0)),
        out_shape=jax.ShapeDtypeStruct((m, d), F32),
        compiler_params=_params(2),
        name="relu2_mlp",
    )(h2, w_up, w_down, x2)


def kernel(x, mem, norm1_g, w_in, b_f, conv_w, fox_q_g, fox_k_g, mem_norm_g, w_mem_kv, mem_q_g, mem_k_g, w_conv_out, w_fox_out, w_mem_out, w_out, norm2_g, w_up, w_down):
    b, s, d = x.shape
    m = b * s
    conv_width = conv_w.shape[1]
    fox_width = FOX_HEADS * FOX_HEAD_DIM
    mem_width = MEM_HEADS * MEM_HEAD_DIM
    q_row = 3 * conv_width
    k_row = q_row + fox_width
    v_row = k_row + fox_width
    f_row = v_row + fox_width
    mq_row = f_row + FOX_HEADS
    gate_row = mq_row + mem_width

    row = lambda v: v.reshape(1, -1)
    tiles = lambda start, width: tuple(range(start, start + width, 1024))
    bf = jnp.pad(b_f, (0, LANES - FOX_HEADS)).reshape(1, LANES)
    w_t = w_in.T

    h1, qa, ka = _norm1(x, row(norm1_g), w_t, f_row, bf)
    h1 = h1.reshape(m, d)
    conv_v = _proj(h1, w_t, tiles(0, 3 * conv_width) + tiles(v_row, fox_width), "proj_conv_v")
    qk_gains = jnp.stack([fox_q_g * (LOG2_E / math.sqrt(FOX_HEAD_DIM)), fox_k_g])
    fqk = _proj(h1, w_t, tiles(q_row, 2 * fox_width), "proj_fox_qk", gains=qk_gains)
    mq = _proj(h1, w_t, tiles(mq_row, mem_width), "proj_mem_q",
               gains=row(mem_q_g) * (1.0 / math.sqrt(MEM_HEAD_DIM)))
    gates = _proj(h1, w_t, tiles(gate_row, N_BRANCHES * d), "proj_gates", gate=True)

    y_fox = _fox(fqk.reshape(b, s, -1), qa, ka, conv_v.reshape(b, s, -1),
                 k_col=fox_width, v_col=3 * conv_width)
    kv = _memkv(mem.reshape(-1, d), row(mem_norm_g), w_mem_kv, row(mem_k_g))
    y_mem = _memattn(mq.reshape(b, s, -1), kv.reshape(b, -1, 2 * mem_width))

    merged = _merge(conv_v, conv_w, y_fox.reshape(m, -1), y_mem.reshape(m, -1),
                    w_conv_out, w_fox_out, w_mem_out, gates, s, d)
    x2, h2 = _outproj(merged, w_out, x.reshape(m, d), row(norm2_g))
    out = _mlp(h2, w_up, w_down, x2)
    return out.reshape(b, s, d)
```

```python
import functools
import math

import jax
import jax.numpy as jnp
import numpy as np
from jax import lax
from jax.experimental import pallas as pl
from jax.experimental.pallas import tpu as pltpu

F32 = jnp.float32
BF16 = jnp.bfloat16

EPS = 1e-6
LOG2_E = math.log2(math.e)
LANES = 128
CONV_TAPS = 3
FOX_HEADS = 8
FOX_HEAD_DIM = 128
MEM_HEADS = 4
MEM_HEAD_DIM = 256
N_BRANCHES = 3

VMEM_LIMIT_BYTES = 56 * 1024 * 1024
NT_DIMS = (((1,), (1,)), ((), ()))


def _params(n_axes):
    return pltpu.CompilerParams(
        dimension_semantics=("arbitrary",) * n_axes,
        vmem_limit_bytes=VMEM_LIMIT_BYTES)


def _group_rms(a, gain, width, scale=1.0):
    outs = []
    for s in range(0, a.shape[1], width):
        blk = a[:, s:s + width]
        ms = jnp.mean(blk * blk, axis=-1, keepdims=True)
        outs.append(blk * lax.rsqrt(ms + EPS) * (gain * scale))
    return outs


def _split3(c):
    hi = c.astype(BF16).astype(F32)
    r1 = c - hi
    mid = r1.astype(BF16).astype(F32)
    lo = (r1 - mid).astype(BF16).astype(F32)
    return hi, mid, lo


def _decay_selector():
    sel = np.zeros((LANES, 2, FOX_HEADS, LANES), np.float32)
    one = 3 * FOX_HEADS
    for hd in range(FOX_HEADS):
        for piece in range(3):
            sel[piece * FOX_HEADS + hd, 0, hd, piece] = 1.0
            sel[one, 0, hd, 3 + piece] = 1.0
            sel[one, 1, hd, piece] = 1.0
            sel[piece * FOX_HEADS + hd, 1, hd, 3 + piece] = -1.0
    return jnp.asarray(sel.reshape(LANES, -1), BF16)


def _norm1_kernel(x_ref, g_ref, wf_ref, bf_ref, sel_ref, h_ref, qa_ref, ka_ref, carry_ref):
    tm = x_ref.shape[0]

    @pl.when(pl.program_id(1) == 0)
    def _():
        carry_ref[...] = jnp.zeros_like(carry_ref)

    xf = x_ref[...]
    ms = jnp.mean(xf * xf, axis=-1, keepdims=True)
    y = xf * lax.rsqrt(ms + EPS) * g_ref[...]
    y_hi = y.astype(BF16)
    h_ref[...] = y_hi
    y_lo = (y - y_hi.astype(F32)).astype(BF16)

    wf = wf_ref[...]
    wf_hi = wf.astype(BF16).astype(F32)
    w2 = jnp.concatenate(
        [wf_hi, wf - wf_hi, jnp.zeros((LANES - 2 * FOX_HEADS, wf.shape[1]), F32)], axis=0).astype(BF16)
    z2 = (lax.dot_general(y_hi, w2, NT_DIMS, preferred_element_type=F32)
          + lax.dot_general(y_lo, w2, NT_DIMS, preferred_element_type=F32))
    z = z2 + pltpu.roll(z2, LANES - FOX_HEADS, 1) + bf_ref[...]
    log_f = -LOG2_E * (jnp.maximum(-z, 0.0) + jnp.log1p(jnp.exp(-jnp.abs(z))))

    row = lax.broadcasted_iota(jnp.int32, (tm, tm), 0)
    col = lax.broadcasted_iota(jnp.int32, (tm, tm), 1)
    tri = jnp.where(row >= col, 1.0, 0.0).astype(BF16)
    f_hi, f_mid, f_lo = _split3(log_f)
    part = jnp.dot(tri, jnp.concatenate([f_hi, f_mid], axis=1).astype(BF16),
                   preferred_element_type=F32)
    c = (part[:, :LANES] + part[:, LANES:]
         + jnp.dot(tri, f_lo.astype(BF16), preferred_element_type=F32) + carry_ref[0:1, :])
    carry_ref[0:1, :] = c[tm - 1:tm, :]

    lane = lax.broadcasted_iota(jnp.int32, (tm, LANES), 1)
    c_hi, c_mid, c_lo = _split3(c)
    packed = jnp.where(
        lane < FOX_HEADS, c_hi,
        jnp.where(lane < 2 * FOX_HEADS, pltpu.roll(c_mid, FOX_HEADS, 1),
                  jnp.where(lane < 3 * FOX_HEADS, pltpu.roll(c_lo, 2 * FOX_HEADS, 1),
                            jnp.where(lane == 3 * FOX_HEADS, 1.0, 0.0))))
    aug = jnp.dot(packed.astype(BF16), sel_ref[...], preferred_element_type=F32)
    width = qa_ref.shape[1]
    qa_ref[...] = aug[:, :width].astype(BF16)
    ka_ref[...] = aug[:, width:].astype(BF16)


def _norm1(x, g, w_t, f_row, bf, tm=512):
    b, s, d = x.shape
    return pl.pallas_call(
        _norm1_kernel,
        grid=(b, s // tm),
        in_specs=[
            pl.BlockSpec((None, tm, d), lambda i, j: (i, j, 0)),
            pl.BlockSpec((1, d), lambda i, j: (0, 0)),
            pl.BlockSpec((FOX_HEADS, d), lambda i, j: (f_row // FOX_HEADS, 0)),
            pl.BlockSpec((1, LANES), lambda i, j: (0, 0)),
            pl.BlockSpec((LANES, 2 * FOX_HEADS * LANES), lambda i, j: (0, 0)),
        ],
        out_specs=[
            pl.BlockSpec((None, tm, d), lambda i, j: (i, j, 0)),
            pl.BlockSpec((None, tm, FOX_HEADS * LANES), lambda i, j: (i, j, 0)),
            pl.BlockSpec((None, tm, FOX_HEADS * LANES), lambda i, j: (i, j, 0)),
        ],
        out_shape=[
            jax.ShapeDtypeStruct((b, s, d), BF16),
            jax.ShapeDtypeStruct((b, s, FOX_HEADS * LANES), BF16),
            jax.ShapeDtypeStruct((b, s, FOX_HEADS * LANES), BF16),
        ],
        scratch_shapes=[pltpu.VMEM((8, LANES), F32)],
        compiler_params=_params(2),
        name="norm1_decay",
    )(x, g, w_t, bf, _decay_selector())


def _proj_kernel(h_ref, w_ref, *refs, group, gate):
    o_ref, wbf_ref = refs[-2:]

    @pl.when(pl.program_id(1) == 0)
    def _():
        wbf_ref[...] = w_ref[...].astype(BF16)

    acc = lax.dot_general(h_ref[...], wbf_ref[...], NT_DIMS, preferred_element_type=F32)
    if group:
        for i, p in enumerate(_group_rms(acc, refs[0][...], group)):
            o_ref[:, i * group:(i + 1) * group] = p.astype(BF16)
    elif gate:
        o_ref[...] = (0.5 * jnp.tanh(0.5 * acc) + 0.5).astype(BF16)
    else:
        o_ref[...] = acc.astype(BF16)


def _proj(h, w_t, row_starts, name, gains=None, gate=False, tm=1024, tn=1024):
    m, k = h.shape
    n_tiles = len(row_starts)
    group = 0 if gains is None else gains.shape[1]

    def w_rows(j, i):
        if n_tiles == 1:
            return (row_starts[0], 0)
        start = jnp.int32(row_starts[0])
        for t in range(1, n_tiles):
            start = jnp.where(j == t, row_starts[t], start)
        return (pl.multiple_of(start, 8), 0)

    gain_args = () if gains is None else (gains.reshape(n_tiles, 1, group),)
    return pl.pallas_call(
        functools.partial(_proj_kernel, group=group, gate=gate),
        grid=(n_tiles, m // tm),
        in_specs=[
            pl.BlockSpec((tm, k), lambda j, i: (i, 0)),
            pl.BlockSpec((pl.Element(tn), pl.Element(k)), w_rows),
        ] + [pl.BlockSpec((None, 1, group), lambda j, i: (j, 0, 0)) for _ in gain_args],
        out_specs=pl.BlockSpec((tm, tn), lambda j, i: (i, j)),
        out_shape=jax.ShapeDtypeStruct((m, n_tiles * tn), BF16),
        scratch_shapes=[pltpu.VMEM((tn, k), BF16)],
        compiler_params=_params(2),
        name=name,
    )(h, w_t, *gain_args)


def _gated_conv_tile(bg_ref, cg_ref, v_ref, w_ref, carry_ref):
    tm = bg_ref.shape[0]
    u = cg_ref[...].astype(F32) * v_ref[...].astype(F32)
    w = w_ref[...]
    taps = lambda u2, u1, u0: w[0:1, :] * u2 + w[1:2, :] * u1 + w[2:3, :] * u0
    conv = taps(pltpu.roll(u, 2, 0), pltpu.roll(u, 1, 0), u)
    top = u[0:8, :]
    row = lax.broadcasted_iota(jnp.int32, (8, 1), 0)
    prev1 = carry_ref[1:2, :]
    prev2 = carry_ref[0:1, :]
    top1 = jnp.where(row == 0, prev1, pltpu.roll(top, 1, 0))
    top2 = jnp.where(row == 0, prev2, jnp.where(row == 1, prev1, pltpu.roll(top, 2, 0)))
    conv = jnp.concatenate([taps(top2, top1, top), conv[8:, :]], axis=0)
    carry_ref[0:2, :] = u[tm - 2:tm, :]
    return bg_ref[...].astype(F32) * conv


def _fox_kernel(q_ref, qa_ref, k_ref, ka_ref, v_ref, o_ref, vt_ref, s_ref, m_ref, l_ref, acc_ref):
    t = q_ref.shape[0] // 2
    d = FOX_HEAD_DIM
    n_heads = q_ref.shape[1] // d
    j = pl.program_id(2)

    @pl.when(j == 0)
    def _():
        for g in range(n_heads):
            for blk in range(v_ref.shape[0] // t):
                v = v_ref[blk * t:(blk + 1) * t, g * d:(g + 1) * d]
                vt_ref[g, blk, :d] = v.astype(F32).T.astype(BF16)
                vt_ref[g, blk, d:] = jnp.ones((vt_ref.shape[2] - d, t), BF16)

    def scores(tile, kb, g):
        q = jnp.concatenate([q_ref[tile * t:(tile + 1) * t, g * d:(g + 1) * d],
                             qa_ref[tile * t:(tile + 1) * t, g * d:(g + 1) * d]], axis=1)
        rows = pl.ds(pl.multiple_of(kb * t, t), t)
        k = jnp.concatenate([k_ref[rows, g * d:(g + 1) * d], ka_ref[rows, g * d:(g + 1) * d]], axis=1)
        return lax.dot_general(k, q, NT_DIMS, preferred_element_type=F32)

    def reset(g):
        m_ref[g] = jnp.full(m_ref.shape[1:], -jnp.inf, F32)
        l_ref[g] = jnp.zeros(l_ref.shape[1:], F32)
        acc_ref[g] = jnp.zeros(acc_ref.shape[1:], F32)

    def update(g, m_new, alpha, pv):
        l_ref[g] = alpha * l_ref[g] + pv[d:d + 1]
        acc_ref[g] = alpha * acc_ref[g] + pv[:d]
        m_ref[g] = m_new

    def consume(kb, slot, g):
        s = s_ref[slot, g]
        m_prev = m_ref[g]
        m_new = jnp.maximum(m_prev, jnp.max(s, axis=0, keepdims=True))
        p = jnp.exp2(s - m_new)
        pv = jnp.dot(vt_ref[g, kb], p.astype(BF16), preferred_element_type=F32)
        update(g, m_new, jnp.exp2(m_prev - m_new), pv)

    def causal(s):
        kpos = lax.broadcasted_iota(jnp.int32, s.shape, 0)
        qpos = lax.broadcasted_iota(jnp.int32, s.shape, 1)
        return jnp.where(qpos >= kpos, s, -jnp.inf)

    def consume_diagonal(kb, slot, g):
        h = t // 2
        s_up = causal(s_ref[slot, g, :h, :])
        s_lo = causal(s_ref[slot, g, h:, h:])
        m_up = jnp.max(s_up, axis=0, keepdims=True)
        m_blk = jnp.concatenate(
            [m_up[:, :h], jnp.maximum(m_up[:, h:], jnp.max(s_lo, axis=0, keepdims=True))], axis=1)
        m_prev = m_ref[g]
        m_new = jnp.maximum(m_prev, m_blk)
        p_up = jnp.exp2(s_up - m_new).astype(BF16)
        p_lo = jnp.exp2(s_lo - m_new[:, h:]).astype(BF16)
        vt = vt_ref[g, kb]
        pv_up = jnp.dot(vt[:, :h], p_up, preferred_element_type=F32)
        pv_lo = jnp.dot(vt[:, h:], p_lo, preferred_element_type=F32)
        pv = jnp.concatenate([pv_up[:, :h], pv_up[:, h:] + pv_lo], axis=1)
        update(g, m_new, jnp.exp2(m_prev - m_new), pv)

    def advance(tile, kb, slot):
        for g in range(n_heads):
            s_ref[1 - slot, g] = scores(tile, kb + 1, g)
            consume(kb, slot, g)

    def emit(tile, g):
        o_ref[tile * t:(tile + 1) * t, g * d:(g + 1) * d] = (acc_ref[g] / l_ref[g]).T.astype(BF16)

    for g in range(n_heads):
        reset(g)
        s_ref[0, g] = scores(0, 0, g)

    def first_pairs(pair, carry):
        advance(0, 2 * pair, 0)
        advance(0, 2 * pair + 1, 1)
        return carry

    lax.fori_loop(0, j, first_pairs, 0)
    for g in range(n_heads):
        s_ref[1, g] = scores(1, 0, g)
        consume_diagonal(2 * j, 0, g)
        emit(0, g)
        reset(g)

    def second_pairs(pair, carry):
        advance(1, 2 * pair, 1)
        advance(1, 2 * pair + 1, 0)
        return carry

    lax.fori_loop(0, j, second_pairs, 0)
    advance(1, 2 * j, 1)
    for g in range(n_heads):
        consume_diagonal(2 * j + 1, 0, g)
        emit(1, g)


def _fox(qk, qa, ka, v, k_col, v_col, t=1024, heads_per_step=2):
    b, s, _ = qk.shape
    d = FOX_HEAD_DIM
    w = heads_per_step * d
    tile2 = pl.BlockSpec((None, 2 * t, w), lambda bi, h, j: (bi, j, h))
    seq = lambda col0: pl.BlockSpec((None, s, w), lambda bi, h, j: (bi, 0, col0 // w + h))
    return pl.pallas_call(
        _fox_kernel,
        grid=(b, FOX_HEADS // heads_per_step, s // (2 * t)),
        in_specs=[tile2, tile2, seq(k_col), seq(0), seq(v_col)],
        out_specs=tile2,
        out_shape=jax.ShapeDtypeStruct((b, s, FOX_HEADS * d), BF16),
        scratch_shapes=[pltpu.VMEM((heads_per_step, s // t, d + 16, t), BF16),
                        pltpu.VMEM((2, heads_per_step, t, t), F32),
                        pltpu.VMEM((heads_per_step, 1, t), F32),
                        pltpu.VMEM((heads_per_step, 1, t), F32),
                        pltpu.VMEM((heads_per_step, d, t), F32)],
        compiler_params=_params(3),
        name="fox_attention",
    )(qk, qa, qk, ka, v)


def _memkv_kernel(mem_ref, g_ref, w_ref, kg_ref, o_ref):
    xf = mem_ref[...]
    ms = jnp.mean(xf * xf, axis=-1, keepdims=True)
    y = (xf * lax.rsqrt(ms + EPS) * g_ref[...]).astype(BF16)
    acc = jnp.dot(y, w_ref[...].astype(BF16), preferred_element_type=F32)

    @pl.when(pl.program_id(0) == 0)
    def _():
        parts = _group_rms(acc, kg_ref[...], MEM_HEAD_DIM)
        for i, p in enumerate(parts):
            o_ref[:, i * MEM_HEAD_DIM:(i + 1) * MEM_HEAD_DIM] = p.astype(BF16)

    @pl.when(pl.program_id(0) == 1)
    def _():
        o_ref[...] = acc.astype(BF16)


def _memkv(mem2d, g, w, kg):
    m, d = mem2d.shape
    width = MEM_HEADS * MEM_HEAD_DIM
    return pl.pallas_call(
        _memkv_kernel,
        grid=(2,),
        in_specs=[
            pl.BlockSpec((m, d), lambda j: (0, 0)),
            pl.BlockSpec((1, d), lambda j: (0, 0)),
            pl.BlockSpec((d, width), lambda j: (0, j)),
            pl.BlockSpec((1, MEM_HEAD_DIM), lambda j: (0, 0)),
        ],
        out_specs=pl.BlockSpec((m, width), lambda j: (0, j)),
        out_shape=jax.ShapeDtypeStruct((m, 2 * width), BF16),
        compiler_params=_params(1),
        name="mem_kv",
    )(mem2d, g, w, kg)


def _memattn_kernel(q_ref, kv_ref, o_ref):
    width = MEM_HEADS * MEM_HEAD_DIM
    for h in range(MEM_HEADS):
        lo, hi = h * MEM_HEAD_DIM, (h + 1) * MEM_HEAD_DIM
        q = q_ref[:, lo:hi]
        k = kv_ref[:, lo:hi]
        v = kv_ref[:, width + lo:width + hi]
        s = lax.dot_general(q, k, NT_DIMS, preferred_element_type=F32)
        p = jnp.exp(s - jnp.max(s, axis=-1, keepdims=True))
        l = jnp.sum(p, axis=-1, keepdims=True)
        o = jnp.dot(p.astype(BF16), v, preferred_element_type=F32) / l
        o_ref[:, lo:hi] = o.astype(BF16)


def _memattn(mq, kv, tm=1024):
    b, s, width = mq.shape
    n_mem = kv.shape[1]
    return pl.pallas_call(
        _memattn_kernel,
        grid=(b, s // tm),
        in_specs=[
            pl.BlockSpec((None, tm, width), lambda i, j: (i, j, 0)),
            pl.BlockSpec((None, n_mem, 2 * width), lambda i, j: (i, 0, 0)),
        ],
        out_specs=pl.BlockSpec((None, tm, width), lambda i, j: (i, j, 0)),
        out_shape=jax.ShapeDtypeStruct((b, s, width), BF16),
        compiler_params=_params(2),
        name="mem_attention",
    )(mq, kv)


def _merge_kernel(bg_ref, cg_ref, cv_ref, cw_ref, b_ref, c_ref, wa_ref, wb_ref, wc_ref,
                  ga_ref, gb_ref, gc_ref, o_ref, wabf_ref, wbbf_ref, wcbf_ref, carry_ref, *,
                  tiles_per_seq):
    i = pl.program_id(1)

    @pl.when(i == 0)
    def _():
        wabf_ref[...] = wa_ref[...].astype(BF16)
        wbbf_ref[...] = wb_ref[...].astype(BF16)
        wcbf_ref[...] = wc_ref[...].astype(BF16)

    @pl.when(i % tiles_per_seq == 0)
    def _():
        carry_ref[...] = jnp.zeros_like(carry_ref)

    y_conv = _gated_conv_tile(bg_ref, cg_ref, cv_ref, cw_ref, carry_ref).astype(BF16)
    oa = jnp.dot(y_conv, wabf_ref[...], preferred_element_type=F32)
    ob = jnp.dot(b_ref[...], wbbf_ref[...], preferred_element_type=F32)
    oc = jnp.dot(c_ref[...], wcbf_ref[...], preferred_element_type=F32)
    merged = (ga_ref[...].astype(F32) * oa + gb_ref[...].astype(F32) * ob
              + gc_ref[...].astype(F32) * oc)
    o_ref[...] = merged.astype(BF16)


def _merge(conv_in, conv_w, yb, yc, wa, wb, wc, gates, seq_len, d_model, tm=512, tn=1024):
    m, k = yb.shape
    nt = d_model // tn
    lhs = lambda col: pl.BlockSpec((tm, k), lambda j, i: (i, col))
    wsp = pl.BlockSpec((k, tn), lambda j, i: (0, j))
    return pl.pallas_call(
        functools.partial(_merge_kernel, tiles_per_seq=seq_len // tm),
        grid=(nt, m // tm),
        in_specs=[lhs(0), lhs(1), lhs(2), pl.BlockSpec((CONV_TAPS, k), lambda j, i: (0, 0)),
                  lhs(0), lhs(0), wsp, wsp, wsp,
                  pl.BlockSpec((tm, tn), lambda j, i: (i, j)),
                  pl.BlockSpec((tm, tn), lambda j, i: (i, nt + j)),
                  pl.BlockSpec((tm, tn), lambda j, i: (i, 2 * nt + j))],
        out_specs=pl.BlockSpec((tm, tn), lambda j, i: (i, j)),
        out_shape=jax.ShapeDtypeStruct((m, d_model), BF16),
        scratch_shapes=[pltpu.VMEM((k, tn), BF16)] * 3 + [pltpu.VMEM((8, k), F32)],
        compiler_params=_params(2),
        name="branch_merge",
    )(conv_in, conv_in, conv_in, conv_w, yb, yc, wa, wb, wc, gates, gates, gates)


def _outproj_kernel(a_ref, w_ref, x_ref, g_ref, x2_ref, h2_ref, wbf_ref):
    @pl.when(pl.program_id(0) == 0)
    def _():
        wbf_ref[...] = w_ref[...].astype(BF16)

    x2 = x_ref[...] + jnp.dot(a_ref[...], wbf_ref[...], preferred_element_type=F32)
    x2_ref[...] = x2
    ms = jnp.mean(x2 * x2, axis=-1, keepdims=True)
    h2_ref[...] = (x2 * lax.rsqrt(ms + EPS) * g_ref[...]).astype(BF16)


def _outproj(a, w, x2d, g, tm=512):
    m, d = x2d.shape
    row = pl.BlockSpec((tm, d), lambda i: (i, 0))
    return pl.pallas_call(
        _outproj_kernel,
        grid=(m // tm,),
        in_specs=[row, pl.BlockSpec((d, d), lambda i: (0, 0), pipeline_mode=pl.Buffered(1)), row,
                  pl.BlockSpec((1, d), lambda i: (0, 0))],
        out_specs=[row, row],
        out_shape=[jax.ShapeDtypeStruct((m, d), F32), jax.ShapeDtypeStruct((m, d), BF16)],
        scratch_shapes=[pltpu.VMEM((d, d), BF16)],
        compiler_params=_params(1),
        name="out_proj_norm2",
    )(a, w, x2d, g)


def _mlp_kernel(h_ref, wu_ref, wd_ref, x_ref, o_ref):
    f = pl.program_id(1)
    rows = x_ref.shape[0]

    @pl.when(f == 0)
    def _():
        o_ref[...] = jnp.zeros_like(o_ref)

    a = jnp.dot(h_ref[...], wu_ref[...].astype(BF16), preferred_element_type=F32)
    a = jnp.square(jnp.maximum(a, 0.0)).astype(BF16)
    o_ref[...] += jnp.dot(a, wd_ref[...].astype(BF16), preferred_element_type=F32)
    slab = pl.ds(pl.multiple_of(f * rows, rows), rows)
    o_ref[slab, :] += x_ref[...]


def _mlp(h2, w_up, w_down, x2, tm=1024, tf=512):
    m, d = h2.shape
    d_ff = w_up.shape[1]
    nf = d_ff // tf
    rows = tm // nf
    return pl.pallas_call(
        _mlp_kernel,
        grid=(m // tm, nf),
        in_specs=[
            pl.BlockSpec((tm, d), lambda i, f: (i, 0)),
            pl.BlockSpec((d, tf), lambda i, f: (0, f)),
            pl.BlockSpec((tf, d), lambda i, f: (f, 0)),
            pl.BlockSpec((rows, d), lambda i, f: (i * nf + f, 0)),
        ],
        out_specs=pl.BlockSpec((tm, d), lambda i, f: (i, 0)),
        out_shape=jax.ShapeDtypeStruct((m, d), F32),
        compiler_params=_params(2),
        name="relu2_mlp",
    )(h2, w_up, w_down, x2)


def kernel(x, mem, norm1_g, w_in, b_f, conv_w, fox_q_g, fox_k_g, mem_norm_g, w_mem_kv, mem_q_g, mem_k_g, w_conv_out, w_fox_out, w_mem_out, w_out, norm2_g, w_up, w_down):
    b, s, d = x.shape
    m = b * s
    conv_width = conv_w.shape[1]
    fox_width = FOX_HEADS * FOX_HEAD_DIM
    mem_width = MEM_HEADS * MEM_HEAD_DIM
    q_row = 3 * conv_width
    k_row = q_row + fox_width
    v_row = k_row + fox_width
    f_row = v_row + fox_width
    mq_row = f_row + FOX_HEADS
    gate_row = mq_row + mem_width

    row = lambda v: v.reshape(1, -1)
    tiles = lambda start, width: tuple(range(start, start + width, 1024))
    bf = jnp.pad(b_f, (0, LANES - FOX_HEADS)).reshape(1, LANES)
    w_t = w_in.T

    h1, qa, ka = _norm1(x, row(norm1_g), w_t, f_row, bf)
    h1 = h1.reshape(m, d)
    conv_v = _proj(h1, w_t, tiles(0, 3 * conv_width) + tiles(v_row, fox_width), "proj_conv_v")
    qk_gains = jnp.stack([fox_q_g * (LOG2_E / math.sqrt(FOX_HEAD_DIM)), fox_k_g])
    fqk = _proj(h1, w_t, tiles(q_row, 2 * fox_width), "proj_fox_qk", gains=qk_gains)
    mq = _proj(h1, w_t, tiles(mq_row, mem_width), "proj_mem_q",
               gains=row(mem_q_g) * (1.0 / math.sqrt(MEM_HEAD_DIM)))
    gates = _proj(h1, w_t, tiles(gate_row, N_BRANCHES * d), "proj_gates", gate=True)

    y_fox = _fox(fqk.reshape(b, s, -1), qa, ka, conv_v.reshape(b, s, -1),
                 k_col=fox_width, v_col=3 * conv_width)
    kv = _memkv(mem.reshape(-1, d), row(mem_norm_g), w_mem_kv, row(mem_k_g))
    y_mem = _memattn(mq.reshape(b, s, -1), kv.reshape(b, -1, 2 * mem_width))

    merged = _merge(conv_v, conv_w, y_fox.reshape(m, -1), y_mem.reshape(m, -1),
                    w_conv_out, w_fox_out, w_mem_out, gates, s, d)
    x2, h2 = _outproj(merged, w_out, x.reshape(m, d), row(norm2_g))
    out = _mlp(h2, w_up, w_down, x2)
    return out.reshape(b, s, d)
```

```python
import functools
import math

import jax
import jax.numpy as jnp
import numpy as np
from jax import lax
from jax.experimental import pallas as pl
from jax.experimental.pallas import tpu as pltpu

F32 = jnp.float32
BF16 = jnp.bfloat16

EPS = 1e-6
LOG2_E = math.log2(math.e)
LANES = 128
CONV_TAPS = 3
FOX_HEADS = 8
FOX_HEAD_DIM = 128
MEM_HEADS = 4
MEM_HEAD_DIM = 256
N_BRANCHES = 3

VMEM_LIMIT_BYTES = 56 * 1024 * 1024
NT_DIMS = (((1,), (1,)), ((), ()))


def _params(n_axes):
    return pltpu.CompilerParams(
        dimension_semantics=("arbitrary",) * n_axes,
        vmem_limit_bytes=VMEM_LIMIT_BYTES)


def _group_rms(a, gain, width, scale=1.0):
    outs = []
    for s in range(0, a.shape[1], width):
        blk = a[:, s:s + width]
        ms = jnp.mean(blk * blk, axis=-1, keepdims=True)
        outs.append(blk * lax.rsqrt(ms + EPS) * (gain * scale))
    return outs


def _split3(c):
    hi = c.astype(BF16).astype(F32)
    r1 = c - hi
    mid = r1.astype(BF16).astype(F32)
    lo = (r1 - mid).astype(BF16).astype(F32)
    return hi, mid, lo


def _decay_selector():
    sel = np.zeros((LANES, 2, FOX_HEADS, LANES), np.float32)
    one = 3 * FOX_HEADS
    for hd in range(FOX_HEADS):
        for piece in range(3):
            sel[piece * FOX_HEADS + hd, 0, hd, piece] = 1.0
            sel[one, 0, hd, 3 + piece] = 1.0
            sel[one, 1, hd, piece] = 1.0
            sel[piece * FOX_HEADS + hd, 1, hd, 3 + piece] = -1.0
    return jnp.asarray(sel.reshape(LANES, -1), BF16)


def _norm1_kernel(x_ref, g_ref, wf_ref, bf_ref, sel_ref, h_ref, qa_ref, ka_ref, carry_ref):
    tm = x_ref.shape[0]

    @pl.when(pl.program_id(1) == 0)
    def _():
        carry_ref[...] = jnp.zeros_like(carry_ref)

    xf = x_ref[...]
    ms = jnp.mean(xf * xf, axis=-1, keepdims=True)
    y = xf * lax.rsqrt(ms + EPS) * g_ref[...]
    y_hi = y.astype(BF16)
    h_ref[...] = y_hi
    y_lo = (y - y_hi.astype(F32)).astype(BF16)

    wf = wf_ref[...]
    wf_hi = wf.astype(BF16).astype(F32)
    w2 = jnp.concatenate(
        [wf_hi, wf - wf_hi, jnp.zeros((LANES - 2 * FOX_HEADS, wf.shape[1]), F32)], axis=0).astype(BF16)
    z2 = (lax.dot_general(y_hi, w2, NT_DIMS, preferred_element_type=F32)
          + lax.dot_general(y_lo, w2, NT_DIMS, preferred_element_type=F32))
    z = z2 + pltpu.roll(z2, LANES - FOX_HEADS, 1) + bf_ref[...]
    log_f = -LOG2_E * (jnp.maximum(-z, 0.0) + jnp.log1p(jnp.exp(-jnp.abs(z))))

    row = lax.broadcasted_iota(jnp.int32, (tm, tm), 0)
    col = lax.broadcasted_iota(jnp.int32, (tm, tm), 1)
    tri = jnp.where(row >= col, 1.0, 0.0).astype(BF16)
    f_hi, f_mid, f_lo = _split3(log_f)
    part = jnp.dot(tri, jnp.concatenate([f_hi, f_mid], axis=1).astype(BF16),
                   preferred_element_type=F32)
    c = (part[:, :LANES] + part[:, LANES:]
         + jnp.dot(tri, f_lo.astype(BF16), preferred_element_type=F32) + carry_ref[0:1, :])
    carry_ref[0:1, :] = c[tm - 1:tm, :]

    lane = lax.broadcasted_iota(jnp.int32, (tm, LANES), 1)
    c_hi, c_mid, c_lo = _split3(c)
    packed = jnp.where(
        lane < FOX_HEADS, c_hi,
        jnp.where(lane < 2 * FOX_HEADS, pltpu.roll(c_mid, FOX_HEADS, 1),
                  jnp.where(lane < 3 * FOX_HEADS, pltpu.roll(c_lo, 2 * FOX_HEADS, 1),
                            jnp.where(lane == 3 * FOX_HEADS, 1.0, 0.0))))
    aug = jnp.dot(packed.astype(BF16), sel_ref[...], preferred_element_type=F32)
    width = qa_ref.shape[1]
    qa_ref[...] = aug[:, :width].astype(BF16)
    ka_ref[...] = aug[:, width:].astype(BF16)


def _norm1(x, g, w_t, f_row, bf, tm=512):
    b, s, d = x.shape
    return pl.pallas_call(
        _norm1_kernel,
        grid=(b, s // tm),
        in_specs=[
            pl.BlockSpec((None, tm, d), lambda i, j: (i, j, 0)),
            pl.BlockSpec((1, d), lambda i, j: (0, 0)),
            pl.BlockSpec((FOX_HEADS, d), lambda i, j: (f_row // FOX_HEADS, 0)),
            pl.BlockSpec((1, LANES), lambda i, j: (0, 0)),
            pl.BlockSpec((LANES, 2 * FOX_HEADS * LANES), lambda i, j: (0, 0)),
        ],
        out_specs=[
            pl.BlockSpec((None, tm, d), lambda i, j: (i, j, 0)),
            pl.BlockSpec((None, tm, FOX_HEADS * LANES), lambda i, j: (i, j, 0)),
            pl.BlockSpec((None, tm, FOX_HEADS * LANES), lambda i, j: (i, j, 0)),
        ],
        out_shape=[
            jax.ShapeDtypeStruct((b, s, d), BF16),
            jax.ShapeDtypeStruct((b, s, FOX_HEADS * LANES), BF16),
            jax.ShapeDtypeStruct((b, s, FOX_HEADS * LANES), BF16),
        ],
        scratch_shapes=[pltpu.VMEM((8, LANES), F32)],
        compiler_params=_params(2),
        name="norm1_decay",
    )(x, g, w_t, bf, _decay_selector())


PLAIN, GATE = 0, -1


def _proj_kernel(h_ref, w_ref, gain_ref, o_ref, wbf_ref, *, kinds):
    @pl.when(pl.program_id(1) == 0)
    def _():
        wbf_ref[...] = w_ref[...].astype(BF16)

    def tile(kind):
        acc = lax.dot_general(h_ref[...], wbf_ref[...], NT_DIMS, preferred_element_type=F32)
        if kind == PLAIN:
            o_ref[...] = acc.astype(BF16)
        elif kind == GATE:
            o_ref[...] = (0.5 * jnp.tanh(0.5 * acc) + 0.5).astype(BF16)
        else:
            gain = gain_ref[...]
            for s in range(0, acc.shape[1], kind):
                blk = acc[:, s:s + kind]
                ms = jnp.mean(blk * blk, axis=-1, keepdims=True)
                o_ref[:, s:s + kind] = (blk * lax.rsqrt(ms + EPS) * gain[:, s:s + kind]).astype(BF16)

    j = pl.program_id(0)
    for kind in sorted(set(kinds)):
        hit = functools.reduce(jnp.logical_or, [j == t for t, kd in enumerate(kinds) if kd == kind])
        pl.when(hit)(functools.partial(tile, kind))


def _proj(h, w_t, row_starts, kinds, gains, tm=1024, tn=1024):
    m, k = h.shape
    n_tiles = len(row_starts)

    def w_rows(j, i):
        start = jnp.int32(row_starts[0])
        for t in range(1, n_tiles):
            start = jnp.where(j == t, row_starts[t], start)
        return (pl.multiple_of(start, 8), 0)

    return pl.pallas_call(
        functools.partial(_proj_kernel, kinds=tuple(kinds)),
        grid=(n_tiles, m // tm),
        in_specs=[
            pl.BlockSpec((tm, k), lambda j, i: (i, 0)),
            pl.BlockSpec((pl.Element(tn), pl.Element(k)), w_rows),
            pl.BlockSpec((None, 1, tn), lambda j, i: (j, 0, 0)),
        ],
        out_specs=pl.BlockSpec((tm, tn), lambda j, i: (i, j)),
        out_shape=jax.ShapeDtypeStruct((m, n_tiles * tn), BF16),
        scratch_shapes=[pltpu.VMEM((tn, k), BF16)],
        compiler_params=_params(2),
        name="in_proj",
    )(h, w_t, gains.reshape(n_tiles, 1, tn))


def _gated_conv_tile(bg_ref, cg_ref, v_ref, w_ref, carry_ref):
    tm = bg_ref.shape[0]
    u = cg_ref[...].astype(F32) * v_ref[...].astype(F32)
    w = w_ref[...]
    taps = lambda u2, u1, u0: w[0:1, :] * u2 + w[1:2, :] * u1 + w[2:3, :] * u0
    conv = taps(pltpu.roll(u, 2, 0), pltpu.roll(u, 1, 0), u)
    top = u[0:8, :]
    row = lax.broadcasted_iota(jnp.int32, (8, 1), 0)
    prev1 = carry_ref[1:2, :]
    prev2 = carry_ref[0:1, :]
    top1 = jnp.where(row == 0, prev1, pltpu.roll(top, 1, 0))
    top2 = jnp.where(row == 0, prev2, jnp.where(row == 1, prev1, pltpu.roll(top, 2, 0)))
    conv = jnp.concatenate([taps(top2, top1, top), conv[8:, :]], axis=0)
    carry_ref[0:2, :] = u[tm - 2:tm, :]
    return bg_ref[...].astype(F32) * conv


def _fox_kernel(q_ref, qa_ref, k_ref, ka_ref, v_ref, o_ref, vt_ref, s_ref, m_ref, l_ref, acc_ref):
    t = q_ref.shape[0] // 2
    d = FOX_HEAD_DIM
    n_heads = q_ref.shape[1] // d
    j = pl.program_id(2)

    @pl.when(j == 0)
    def _():
        for g in range(n_heads):
            for blk in range(v_ref.shape[0] // t):
                v = v_ref[blk * t:(blk + 1) * t, g * d:(g + 1) * d]
                vt_ref[g, blk, :d] = v.astype(F32).T.astype(BF16)
                vt_ref[g, blk, d:] = jnp.ones((vt_ref.shape[2] - d, t), BF16)

    def scores(tile, kb, g):
        q = jnp.concatenate([q_ref[tile * t:(tile + 1) * t, g * d:(g + 1) * d],
                             qa_ref[tile * t:(tile + 1) * t, g * d:(g + 1) * d]], axis=1)
        rows = pl.ds(pl.multiple_of(kb * t, t), t)
        k = jnp.concatenate([k_ref[rows, g * d:(g + 1) * d], ka_ref[rows, g * d:(g + 1) * d]], axis=1)
        return lax.dot_general(k, q, NT_DIMS, preferred_element_type=F32)

    def reset(g):
        m_ref[g] = jnp.full(m_ref.shape[1:], -jnp.inf, F32)
        l_ref[g] = jnp.zeros(l_ref.shape[1:], F32)
        acc_ref[g] = jnp.zeros(acc_ref.shape[1:], F32)

    def update(g, m_new, alpha, pv):
        l_ref[g] = alpha * l_ref[g] + pv[d:d + 1]
        acc_ref[g] = alpha * acc_ref[g] + pv[:d]
        m_ref[g] = m_new

    def consume(kb, slot, g):
        s = s_ref[slot, g]
        m_prev = m_ref[g]
        m_new = jnp.maximum(m_prev, jnp.max(s, axis=0, keepdims=True))
        p = jnp.exp2(s - m_new)
        pv = jnp.dot(vt_ref[g, kb], p.astype(BF16), preferred_element_type=F32)
        update(g, m_new, jnp.exp2(m_prev - m_new), pv)

    def causal(s):
        kpos = lax.broadcasted_iota(jnp.int32, s.shape, 0)
        qpos = lax.broadcasted_iota(jnp.int32, s.shape, 1)
        return jnp.where(qpos >= kpos, s, -jnp.inf)

    def consume_diagonal(kb, slot, g):
        h = t // 2
        s_up = causal(s_ref[slot, g, :h, :])
        s_lo = causal(s_ref[slot, g, h:, h:])
        m_up = jnp.max(s_up, axis=0, keepdims=True)
        m_blk = jnp.concatenate(
            [m_up[:, :h], jnp.maximum(m_up[:, h:], jnp.max(s_lo, axis=0, keepdims=True))], axis=1)
        m_prev = m_ref[g]
        m_new = jnp.maximum(m_prev, m_blk)
        p_up = jnp.exp2(s_up - m_new).astype(BF16)
        p_lo = jnp.exp2(s_lo - m_new[:, h:]).astype(BF16)
        vt = vt_ref[g, kb]
        pv_up = jnp.dot(vt[:, :h], p_up, preferred_element_type=F32)
        pv_lo = jnp.dot(vt[:, h:], p_lo, preferred_element_type=F32)
        pv = jnp.concatenate([pv_up[:, :h], pv_up[:, h:] + pv_lo], axis=1)
        update(g, m_new, jnp.exp2(m_prev - m_new), pv)

    def advance(tile, kb, slot):
        for g in range(n_heads):
            s_ref[1 - slot, g] = scores(tile, kb + 1, g)
            consume(kb, slot, g)

    def emit(tile, g):
        o_ref[tile * t:(tile + 1) * t, g * d:(g + 1) * d] = (acc_ref[g] / l_ref[g]).T.astype(BF16)

    for g in range(n_heads):
        reset(g)
        s_ref[0, g] = scores(0, 0, g)

    def first_pairs(pair, carry):
        advance(0, 2 * pair, 0)
        advance(0, 2 * pair + 1, 1)
        return carry

    lax.fori_loop(0, j, first_pairs, 0)
    for g in range(n_heads):
        s_ref[1, g] = scores(1, 0, g)
        consume_diagonal(2 * j, 0, g)
        emit(0, g)
        reset(g)

    def second_pairs(pair, carry):
        advance(1, 2 * pair, 1)
        advance(1, 2 * pair + 1, 0)
        return carry

    lax.fori_loop(0, j, second_pairs, 0)
    advance(1, 2 * j, 1)
    for g in range(n_heads):
        consume_diagonal(2 * j + 1, 0, g)
        emit(1, g)


def _fox(proj, qa, ka, q_col, k_col, v_col, t=512, heads_per_step=4):
    b, s, _ = proj.shape
    d = FOX_HEAD_DIM
    w = heads_per_step * d
    tile2 = lambda col0: pl.BlockSpec((None, 2 * t, w), lambda bi, h, j: (bi, j, col0 // w + h))
    seq = lambda col0: pl.BlockSpec((None, s, w), lambda bi, h, j: (bi, 0, col0 // w + h))
    return pl.pallas_call(
        _fox_kernel,
        grid=(b, FOX_HEADS // heads_per_step, s // (2 * t)),
        in_specs=[tile2(q_col), tile2(0), seq(k_col), seq(0), seq(v_col)],
        out_specs=tile2(0),
        out_shape=jax.ShapeDtypeStruct((b, s, FOX_HEADS * d), BF16),
        scratch_shapes=[pltpu.VMEM((heads_per_step, s // t, d + 16, t), BF16),
                        pltpu.VMEM((2, heads_per_step, t, t), F32),
                        pltpu.VMEM((heads_per_step, 1, t), F32),
                        pltpu.VMEM((heads_per_step, 1, t), F32),
                        pltpu.VMEM((heads_per_step, d, t), F32)],
        compiler_params=_params(3),
        name="fox_attention",
    )(proj, qa, proj, ka, proj)


def _memkv_kernel(mem_ref, g_ref, w_ref, kg_ref, o_ref):
    xf = mem_ref[...]
    ms = jnp.mean(xf * xf, axis=-1, keepdims=True)
    y = (xf * lax.rsqrt(ms + EPS) * g_ref[...]).astype(BF16)
    acc = jnp.dot(y, w_ref[...].astype(BF16), preferred_element_type=F32)

    @pl.when(pl.program_id(0) == 0)
    def _():
        parts = _group_rms(acc, kg_ref[...], MEM_HEAD_DIM)
        for i, p in enumerate(parts):
            o_ref[:, i * MEM_HEAD_DIM:(i + 1) * MEM_HEAD_DIM] = p.astype(BF16)

    @pl.when(pl.program_id(0) == 1)
    def _():
        o_ref[...] = acc.astype(BF16)


def _memkv(mem2d, g, w, kg):
    m, d = mem2d.shape
    width = MEM_HEADS * MEM_HEAD_DIM
    return pl.pallas_call(
        _memkv_kernel,
        grid=(2,),
        in_specs=[
            pl.BlockSpec((m, d), lambda j: (0, 0)),
            pl.BlockSpec((1, d), lambda j: (0, 0)),
            pl.BlockSpec((d, width), lambda j: (0, j)),
            pl.BlockSpec((1, MEM_HEAD_DIM), lambda j: (0, 0)),
        ],
        out_specs=pl.BlockSpec((m, width), lambda j: (0, j)),
        out_shape=jax.ShapeDtypeStruct((m, 2 * width), BF16),
        compiler_params=_params(1),
        name="mem_kv",
    )(mem2d, g, w, kg)


def _memattn_kernel(q_ref, kv_ref, o_ref):
    width = MEM_HEADS * MEM_HEAD_DIM
    for h in range(MEM_HEADS):
        lo, hi = h * MEM_HEAD_DIM, (h + 1) * MEM_HEAD_DIM
        q = q_ref[:, lo:hi]
        k = kv_ref[:, lo:hi]
        v = kv_ref[:, width + lo:width + hi]
        s = lax.dot_general(q, k, NT_DIMS, preferred_element_type=F32)
        p = jnp.exp(s - jnp.max(s, axis=-1, keepdims=True))
        l = jnp.sum(p, axis=-1, keepdims=True)
        o = jnp.dot(p.astype(BF16), v, preferred_element_type=F32) / l
        o_ref[:, lo:hi] = o.astype(BF16)


def _memattn(proj, q_col, kv, tm=1024):
    b, s, _ = proj.shape
    width = MEM_HEADS * MEM_HEAD_DIM
    n_mem = kv.shape[1]
    return pl.pallas_call(
        _memattn_kernel,
        grid=(b, s // tm),
        in_specs=[
            pl.BlockSpec((None, tm, width), lambda i, j: (i, j, q_col // width)),
            pl.BlockSpec((None, n_mem, 2 * width), lambda i, j: (i, 0, 0)),
        ],
        out_specs=pl.BlockSpec((None, tm, width), lambda i, j: (i, j, 0)),
        out_shape=jax.ShapeDtypeStruct((b, s, width), BF16),
        compiler_params=_params(2),
        name="mem_attention",
    )(proj, kv)


def _merge_kernel(bg_ref, cg_ref, cv_ref, cw_ref, b_ref, c_ref, wa_ref, wb_ref, wc_ref,
                  ga_ref, gb_ref, gc_ref, o_ref, wabf_ref, wbbf_ref, wcbf_ref, carry_ref, *,
                  tiles_per_seq):
    i = pl.program_id(1)

    @pl.when(i == 0)
    def _():
        wabf_ref[...] = wa_ref[...].astype(BF16)
        wbbf_ref[...] = wb_ref[...].astype(BF16)
        wcbf_ref[...] = wc_ref[...].astype(BF16)

    @pl.when(i % tiles_per_seq == 0)
    def _():
        carry_ref[...] = jnp.zeros_like(carry_ref)

    y_conv = _gated_conv_tile(bg_ref, cg_ref, cv_ref, cw_ref, carry_ref).astype(BF16)
    oa = jnp.dot(y_conv, wabf_ref[...], preferred_element_type=F32)
    ob = jnp.dot(b_ref[...], wbbf_ref[...], preferred_element_type=F32)
    oc = jnp.dot(c_ref[...], wcbf_ref[...], preferred_element_type=F32)
    merged = (ga_ref[...].astype(F32) * oa + gb_ref[...].astype(F32) * ob
              + gc_ref[...].astype(F32) * oc)
    o_ref[...] = merged.astype(BF16)


def _merge(proj, gate_col, conv_w, yb, yc, wa, wb, wc, seq_len, d_model, tm=512, tn=1024):
    m, k = yb.shape
    nt = d_model // tn
    lhs = lambda col: pl.BlockSpec((tm, k), lambda j, i: (i, col))
    wsp = pl.BlockSpec((k, tn), lambda j, i: (0, j))
    gate = lambda br: pl.BlockSpec((tm, tn), lambda j, i: (i, gate_col // tn + br * nt + j))
    return pl.pallas_call(
        functools.partial(_merge_kernel, tiles_per_seq=seq_len // tm),
        grid=(nt, m // tm),
        in_specs=[lhs(0), lhs(1), lhs(2), pl.BlockSpec((CONV_TAPS, k), lambda j, i: (0, 0)),
                  lhs(0), lhs(0), wsp, wsp, wsp, gate(0), gate(1), gate(2)],
        out_specs=pl.BlockSpec((tm, tn), lambda j, i: (i, j)),
        out_shape=jax.ShapeDtypeStruct((m, d_model), BF16),
        scratch_shapes=[pltpu.VMEM((k, tn), BF16)] * 3 + [pltpu.VMEM((8, k), F32)],
        compiler_params=_params(2),
        name="branch_merge",
    )(proj, proj, proj, conv_w, yb, yc, wa, wb, wc, proj, proj, proj)


def _outproj_kernel(a_ref, w_ref, x_ref, g_ref, x2_ref, h2_ref, wbf_ref):
    @pl.when(pl.program_id(0) == 0)
    def _():
        wbf_ref[...] = w_ref[...].astype(BF16)

    x2 = x_ref[...] + jnp.dot(a_ref[...], wbf_ref[...], preferred_element_type=F32)
    x2_ref[...] = x2
    ms = jnp.mean(x2 * x2, axis=-1, keepdims=True)
    h2_ref[...] = (x2 * lax.rsqrt(ms + EPS) * g_ref[...]).astype(BF16)


def _outproj(a, w, x2d, g, tm=512):
    m, d = x2d.shape
    row = pl.BlockSpec((tm, d), lambda i: (i, 0))
    return pl.pallas_call(
        _outproj_kernel,
        grid=(m // tm,),
        in_specs=[row, pl.BlockSpec((d, d), lambda i: (0, 0), pipeline_mode=pl.Buffered(1)), row,
                  pl.BlockSpec((1, d), lambda i: (0, 0))],
        out_specs=[row, row],
        out_shape=[jax.ShapeDtypeStruct((m, d), F32), jax.ShapeDtypeStruct((m, d), BF16)],
        scratch_shapes=[pltpu.VMEM((d, d), BF16)],
        compiler_params=_params(1),
        name="out_proj_norm2",
    )(a, w, x2d, g)


def _mlp_kernel(h_ref, wu_ref, wd_ref, x_ref, o_ref):
    f = pl.program_id(1)
    rows = x_ref.shape[0]

    @pl.when(f == 0)
    def _():
        o_ref[...] = jnp.zeros_like(o_ref)

    a = jnp.dot(h_ref[...], wu_ref[...].astype(BF16), preferred_element_type=F32)
    a = jnp.square(jnp.maximum(a, 0.0)).astype(BF16)
    o_ref[...] += jnp.dot(a, wd_ref[...].astype(BF16), preferred_element_type=F32)
    slab = pl.ds(pl.multiple_of(f * rows, rows), rows)
    o_ref[slab, :] += x_ref[...]


def _mlp(h2, w_up, w_down, x2, tm=1024, tf=512):
    m, d = h2.shape
    d_ff = w_up.shape[1]
    nf = d_ff // tf
    rows = tm // nf
    return pl.pallas_call(
        _mlp_kernel,
        grid=(m // tm, nf),
        in_specs=[
            pl.BlockSpec((tm, d), lambda i, f: (i, 0)),
            pl.BlockSpec((d, tf), lambda i, f: (0, f)),
            pl.BlockSpec((tf, d), lambda i, f: (f, 0)),
            pl.BlockSpec((rows, d), lambda i, f: (i * nf + f, 0)),
        ],
        out_specs=pl.BlockSpec((tm, d), lambda i, f: (i, 0)),
        out_shape=jax.ShapeDtypeStruct((m, d), F32),
        compiler_params=_params(2),
        name="relu2_mlp",
    )(h2, w_up, w_down, x2)


def kernel(x, mem, norm1_g, w_in, b_f, conv_w, fox_q_g, fox_k_g, mem_norm_g, w_mem_kv, mem_q_g, mem_k_g, w_conv_out, w_fox_out, w_mem_out, w_out, norm2_g, w_up, w_down):
    b, s, d = x.shape
    m = b * s
    conv_width = conv_w.shape[1]
    fox_width = FOX_HEADS * FOX_HEAD_DIM
    mem_width = MEM_HEADS * MEM_HEAD_DIM
    tile = 1024
    q_col = 3 * conv_width
    k_col = q_col + fox_width
    v_col = k_col + fox_width
    mq_col = v_col + fox_width
    gate_col = mq_col + mem_width
    n_cols = gate_col + N_BRANCHES * d
    row_starts = tuple(c if c < mq_col else c + FOX_HEADS for c in range(0, n_cols, tile))
    kinds = ([PLAIN] * (q_col // tile) + [FOX_HEAD_DIM] * (2 * fox_width // tile)
             + [PLAIN] * (fox_width // tile) + [MEM_HEAD_DIM] * (mem_width // tile)
             + [GATE] * (N_BRANCHES * d // tile))
    gains = jnp.ones((n_cols,), F32)
    gains = gains.at[q_col:k_col].set(jnp.tile(fox_q_g * (LOG2_E / math.sqrt(FOX_HEAD_DIM)), FOX_HEADS))
    gains = gains.at[k_col:v_col].set(jnp.tile(fox_k_g, FOX_HEADS))
    gains = gains.at[mq_col:gate_col].set(jnp.tile(mem_q_g * (1.0 / math.sqrt(MEM_HEAD_DIM)), MEM_HEADS))

    row = lambda v: v.reshape(1, -1)
    bf = jnp.pad(b_f, (0, LANES - FOX_HEADS)).reshape(1, LANES)
    w_t = w_in.T

    h1, qa, ka = _norm1(x, row(norm1_g), w_t, mq_col, bf)
    proj = _proj(h1.reshape(m, d), w_t, row_starts, kinds, gains.reshape(-1, tile))
    proj3 = proj.reshape(b, s, -1)

    y_fox = _fox(proj3, qa, ka, q_col, k_col, v_col)
    kv = _memkv(mem.reshape(-1, d), row(mem_norm_g), w_mem_kv, row(mem_k_g))
    y_mem = _memattn(proj3, mq_col, kv.reshape(b, -1, 2 * mem_width))

    merged = _merge(proj, gate_col, conv_w, y_fox.reshape(m, -1), y_mem.reshape(m, -1),
                    w_conv_out, w_fox_out, w_mem_out, s, d)
    x2, h2 = _outproj(merged, w_out, x.reshape(m, d), row(norm2_g))
    out = _mlp(h2, w_up, w_down, x2)
    return out.reshape(b, s, d)
```

```python
import functools
import math

import jax
import jax.numpy as jnp
import numpy as np
from jax import lax
from jax.experimental import pallas as pl
from jax.experimental.pallas import tpu as pltpu

F32 = jnp.float32
BF16 = jnp.bfloat16

EPS = 1e-6
LOG2_E = math.log2(math.e)
LANES = 128
CONV_TAPS = 3
FOX_HEADS = 8
FOX_HEAD_DIM = 128
MEM_HEADS = 4
MEM_HEAD_DIM = 256
N_BRANCHES = 3

VMEM_LIMIT_BYTES = 56 * 1024 * 1024
NT_DIMS = (((1,), (1,)), ((), ()))


def _params(n_axes):
    return pltpu.CompilerParams(
        dimension_semantics=("arbitrary",) * n_axes,
        vmem_limit_bytes=VMEM_LIMIT_BYTES)


def _group_rms(a, gain, width, scale=1.0):
    outs = []
    for s in range(0, a.shape[1], width):
        blk = a[:, s:s + width]
        ms = jnp.mean(blk * blk, axis=-1, keepdims=True)
        outs.append(blk * lax.rsqrt(ms + EPS) * (gain * scale))
    return outs


def _split3(c):
    hi = c.astype(BF16).astype(F32)
    r1 = c - hi
    mid = r1.astype(BF16).astype(F32)
    lo = (r1 - mid).astype(BF16).astype(F32)
    return hi, mid, lo


def _decay_selector():
    sel = np.zeros((LANES, 2, FOX_HEADS, LANES), np.float32)
    one = 3 * FOX_HEADS
    for hd in range(FOX_HEADS):
        for piece in range(3):
            sel[piece * FOX_HEADS + hd, 0, hd, piece] = 1.0
            sel[one, 0, hd, 3 + piece] = 1.0
            sel[one, 1, hd, piece] = 1.0
            sel[piece * FOX_HEADS + hd, 1, hd, 3 + piece] = -1.0
    return jnp.asarray(sel.reshape(LANES, -1), BF16)


def _norm1_kernel(x_ref, g_ref, wf_ref, bf_ref, sel_ref, h_ref, qa_ref, ka_ref, carry_ref):
    tm = x_ref.shape[0]

    @pl.when(pl.program_id(1) == 0)
    def _():
        carry_ref[...] = jnp.zeros_like(carry_ref)

    xf = x_ref[...]
    ms = jnp.mean(xf * xf, axis=-1, keepdims=True)
    y = xf * lax.rsqrt(ms + EPS) * g_ref[...]
    y_hi = y.astype(BF16)
    h_ref[...] = y_hi
    y_lo = (y - y_hi.astype(F32)).astype(BF16)

    wf = wf_ref[...]
    wf_hi = wf.astype(BF16).astype(F32)
    w2 = jnp.concatenate(
        [wf_hi, wf - wf_hi, jnp.zeros((LANES - 2 * FOX_HEADS, wf.shape[1]), F32)], axis=0).astype(BF16)
    z2 = (lax.dot_general(y_hi, w2, NT_DIMS, preferred_element_type=F32)
          + lax.dot_general(y_lo, w2, NT_DIMS, preferred_element_type=F32))
    z = z2 + pltpu.roll(z2, LANES - FOX_HEADS, 1) + bf_ref[...]
    log_f = -LOG2_E * (jnp.maximum(-z, 0.0) + jnp.log1p(jnp.exp(-jnp.abs(z))))

    row = lax.broadcasted_iota(jnp.int32, (tm, tm), 0)
    col = lax.broadcasted_iota(jnp.int32, (tm, tm), 1)
    tri = jnp.where(row >= col, 1.0, 0.0).astype(BF16)
    f_hi, f_mid, f_lo = _split3(log_f)
    part = jnp.dot(tri, jnp.concatenate([f_hi, f_mid], axis=1).astype(BF16),
                   preferred_element_type=F32)
    c = (part[:, :LANES] + part[:, LANES:]
         + jnp.dot(tri, f_lo.astype(BF16), preferred_element_type=F32) + carry_ref[0:1, :])
    carry_ref[0:1, :] = c[tm - 1:tm, :]

    lane = lax.broadcasted_iota(jnp.int32, (tm, LANES), 1)
    c_hi, c_mid, c_lo = _split3(c)
    packed = jnp.where(
        lane < FOX_HEADS, c_hi,
        jnp.where(lane < 2 * FOX_HEADS, pltpu.roll(c_mid, FOX_HEADS, 1),
                  jnp.where(lane < 3 * FOX_HEADS, pltpu.roll(c_lo, 2 * FOX_HEADS, 1),
                            jnp.where(lane == 3 * FOX_HEADS, 1.0, 0.0))))
    aug = jnp.dot(packed.astype(BF16), sel_ref[...], preferred_element_type=F32)
    width = qa_ref.shape[1]
    qa_ref[...] = aug[:, :width].astype(BF16)
    ka_ref[...] = aug[:, width:].astype(BF16)


def _norm1(x, g, w_t, f_row, bf, tm=512):
    b, s, d = x.shape
    return pl.pallas_call(
        _norm1_kernel,
        grid=(b, s // tm),
        in_specs=[
            pl.BlockSpec((None, tm, d), lambda i, j: (i, j, 0)),
            pl.BlockSpec((1, d), lambda i, j: (0, 0)),
            pl.BlockSpec((FOX_HEADS, d), lambda i, j: (f_row // FOX_HEADS, 0)),
            pl.BlockSpec((1, LANES), lambda i, j: (0, 0)),
            pl.BlockSpec((LANES, 2 * FOX_HEADS * LANES), lambda i, j: (0, 0)),
        ],
        out_specs=[
            pl.BlockSpec((None, tm, d), lambda i, j: (i, j, 0)),
            pl.BlockSpec((None, tm, FOX_HEADS * LANES), lambda i, j: (i, j, 0)),
            pl.BlockSpec((None, tm, FOX_HEADS * LANES), lambda i, j: (i, j, 0)),
        ],
        out_shape=[
            jax.ShapeDtypeStruct((b, s, d), BF16),
            jax.ShapeDtypeStruct((b, s, FOX_HEADS * LANES), BF16),
            jax.ShapeDtypeStruct((b, s, FOX_HEADS * LANES), BF16),
        ],
        scratch_shapes=[pltpu.VMEM((8, LANES), F32)],
        compiler_params=_params(2),
        name="norm1_decay",
    )(x, g, w_t, bf, _decay_selector())


PLAIN, GATE = 0, -1


def _proj_kernel(h_ref, w_ref, gain_ref, o_ref, wbf_ref, *, kinds):
    @pl.when(pl.program_id(1) == 0)
    def _():
        wbf_ref[...] = w_ref[...].astype(BF16)

    def tile(kind):
        acc = lax.dot_general(h_ref[...], wbf_ref[...], NT_DIMS, preferred_element_type=F32)
        if kind == PLAIN:
            o_ref[...] = acc.astype(BF16)
        elif kind == GATE:
            o_ref[...] = (0.5 * jnp.tanh(0.5 * acc) + 0.5).astype(BF16)
        else:
            gain = gain_ref[...]
            for s in range(0, acc.shape[1], kind):
                blk = acc[:, s:s + kind]
                ms = jnp.mean(blk * blk, axis=-1, keepdims=True)
                o_ref[:, s:s + kind] = (blk * lax.rsqrt(ms + EPS) * gain[:, s:s + kind]).astype(BF16)

    j = pl.program_id(0)
    for kind in sorted(set(kinds)):
        hit = functools.reduce(jnp.logical_or, [j == t for t, kd in enumerate(kinds) if kd == kind])
        pl.when(hit)(functools.partial(tile, kind))


def _proj(h, w_t, row_starts, kinds, gains, tm=1024, tn=1024):
    m, k = h.shape
    n_tiles = len(row_starts)

    def w_rows(j, i):
        start = jnp.int32(row_starts[0])
        for t in range(1, n_tiles):
            start = jnp.where(j == t, row_starts[t], start)
        return (pl.multiple_of(start, 8), 0)

    return pl.pallas_call(
        functools.partial(_proj_kernel, kinds=tuple(kinds)),
        grid=(n_tiles, m // tm),
        in_specs=[
            pl.BlockSpec((tm, k), lambda j, i: (i, 0)),
            pl.BlockSpec((pl.Element(tn), pl.Element(k)), w_rows),
            pl.BlockSpec((None, 1, tn), lambda j, i: (j, 0, 0)),
        ],
        out_specs=pl.BlockSpec((tm, tn), lambda j, i: (i, j)),
        out_shape=jax.ShapeDtypeStruct((m, n_tiles * tn), BF16),
        scratch_shapes=[pltpu.VMEM((tn, k), BF16)],
        compiler_params=_params(2),
        name="in_proj",
    )(h, w_t, gains.reshape(n_tiles, 1, tn))


def _gated_conv_tile(conv_ref, w_ref, carry_ref):
    tm = conv_ref.shape[0]
    k = w_ref.shape[1]
    u = conv_ref[:, k:2 * k].astype(F32) * conv_ref[:, 2 * k:3 * k].astype(F32)
    w = w_ref[...]
    taps = lambda u2, u1, u0: w[0:1, :] * u2 + w[1:2, :] * u1 + w[2:3, :] * u0
    conv = taps(pltpu.roll(u, 2, 0), pltpu.roll(u, 1, 0), u)
    top = u[0:8, :]
    row = lax.broadcasted_iota(jnp.int32, (8, 1), 0)
    prev1 = carry_ref[1:2, :]
    prev2 = carry_ref[0:1, :]
    top1 = jnp.where(row == 0, prev1, pltpu.roll(top, 1, 0))
    top2 = jnp.where(row == 0, prev2, jnp.where(row == 1, prev1, pltpu.roll(top, 2, 0)))
    conv = jnp.concatenate([taps(top2, top1, top), conv[8:, :]], axis=0)
    carry_ref[0:2, :] = u[tm - 2:tm, :]
    return conv_ref[:, 0:k].astype(F32) * conv


def _fox_kernel(q_ref, qa_ref, k_ref, ka_ref, v_ref, o_ref, vt_ref, s_ref, m_ref, l_ref, acc_ref):
    t = q_ref.shape[0] // 2
    d = FOX_HEAD_DIM
    n_heads = q_ref.shape[1] // d
    j = pl.program_id(2)

    @pl.when(j == 0)
    def _():
        for g in range(n_heads):
            for blk in range(v_ref.shape[0] // t):
                v = v_ref[blk * t:(blk + 1) * t, g * d:(g + 1) * d]
                vt_ref[g, blk, :d] = v.astype(F32).T.astype(BF16)
                vt_ref[g, blk, d:] = jnp.ones((vt_ref.shape[2] - d, t), BF16)

    def scores(tile, kb, g):
        q = jnp.concatenate([q_ref[tile * t:(tile + 1) * t, g * d:(g + 1) * d],
                             qa_ref[tile * t:(tile + 1) * t, g * d:(g + 1) * d]], axis=1)
        rows = pl.ds(pl.multiple_of(kb * t, t), t)
        k = jnp.concatenate([k_ref[rows, g * d:(g + 1) * d], ka_ref[rows, g * d:(g + 1) * d]], axis=1)
        return lax.dot_general(k, q, NT_DIMS, preferred_element_type=F32)

    def reset(g):
        m_ref[g] = jnp.full(m_ref.shape[1:], -jnp.inf, F32)
        l_ref[g] = jnp.zeros(l_ref.shape[1:], F32)
        acc_ref[g] = jnp.zeros(acc_ref.shape[1:], F32)

    def update(g, m_new, alpha, pv):
        l_ref[g] = alpha * l_ref[g] + pv[d:d + 1]
        acc_ref[g] = alpha * acc_ref[g] + pv[:d]
        m_ref[g] = m_new

    def consume(kb, slot, g):
        s = s_ref[slot, g]
        m_prev = m_ref[g]
        m_new = jnp.maximum(m_prev, jnp.max(s, axis=0, keepdims=True))
        p = jnp.exp2(s - m_new)
        pv = jnp.dot(vt_ref[g, kb], p.astype(BF16), preferred_element_type=F32)
        update(g, m_new, jnp.exp2(m_prev - m_new), pv)

    def causal(s):
        kpos = lax.broadcasted_iota(jnp.int32, s.shape, 0)
        qpos = lax.broadcasted_iota(jnp.int32, s.shape, 1)
        return jnp.where(qpos >= kpos, s, -jnp.inf)

    def consume_diagonal(kb, slot, g):
        h = t // 2
        s_up = causal(s_ref[slot, g, :h, :])
        s_lo = causal(s_ref[slot, g, h:, h:])
        m_up = jnp.max(s_up, axis=0, keepdims=True)
        m_blk = jnp.concatenate(
            [m_up[:, :h], jnp.maximum(m_up[:, h:], jnp.max(s_lo, axis=0, keepdims=True))], axis=1)
        m_prev = m_ref[g]
        m_new = jnp.maximum(m_prev, m_blk)
        p_up = jnp.exp2(s_up - m_new).astype(BF16)
        p_lo = jnp.exp2(s_lo - m_new[:, h:]).astype(BF16)
        vt = vt_ref[g, kb]
        pv_up = jnp.dot(vt[:, :h], p_up, preferred_element_type=F32)
        pv_lo = jnp.dot(vt[:, h:], p_lo, preferred_element_type=F32)
        pv = jnp.concatenate([pv_up[:, :h], pv_up[:, h:] + pv_lo], axis=1)
        update(g, m_new, jnp.exp2(m_prev - m_new), pv)

    def advance(tile, kb, slot):
        for g in range(n_heads):
            s_ref[1 - slot, g] = scores(tile, kb + 1, g)
            consume(kb, slot, g)

    def emit(tile, g):
        o_ref[tile * t:(tile + 1) * t, g * d:(g + 1) * d] = (acc_ref[g] / l_ref[g]).T.astype(BF16)

    for g in range(n_heads):
        reset(g)
        s_ref[0, g] = scores(0, 0, g)

    def first_pairs(pair, carry):
        advance(0, 2 * pair, 0)
        advance(0, 2 * pair + 1, 1)
        return carry

    lax.fori_loop(0, j, first_pairs, 0)
    for g in range(n_heads):
        s_ref[1, g] = scores(1, 0, g)
        consume_diagonal(2 * j, 0, g)
        emit(0, g)
        reset(g)

    def second_pairs(pair, carry):
        advance(1, 2 * pair, 1)
        advance(1, 2 * pair + 1, 0)
        return carry

    lax.fori_loop(0, j, second_pairs, 0)
    advance(1, 2 * j, 1)
    for g in range(n_heads):
        consume_diagonal(2 * j + 1, 0, g)
        emit(1, g)


def _fox(proj, qa, ka, q_col, k_col, v_col, t=512, heads_per_step=4):
    b, s, _ = proj.shape
    d = FOX_HEAD_DIM
    w = heads_per_step * d
    tile2 = lambda col0: pl.BlockSpec((None, 2 * t, w), lambda bi, h, j: (bi, j, col0 // w + h))
    seq = lambda col0: pl.BlockSpec((None, s, w), lambda bi, h, j: (bi, 0, col0 // w + h))
    return pl.pallas_call(
        _fox_kernel,
        grid=(b, FOX_HEADS // heads_per_step, s // (2 * t)),
        in_specs=[tile2(q_col), tile2(0), seq(k_col), seq(0), seq(v_col)],
        out_specs=tile2(0),
        out_shape=jax.ShapeDtypeStruct((b, s, FOX_HEADS * d), BF16),
        scratch_shapes=[pltpu.VMEM((heads_per_step, s // t, d + 16, t), BF16),
                        pltpu.VMEM((2, heads_per_step, t, t), F32),
                        pltpu.VMEM((heads_per_step, 1, t), F32),
                        pltpu.VMEM((heads_per_step, 1, t), F32),
                        pltpu.VMEM((heads_per_step, d, t), F32)],
        compiler_params=_params(3),
        name="fox_attention",
    )(proj, qa, proj, ka, proj)


def _memkv_kernel(mem_ref, g_ref, w_ref, kg_ref, o_ref):
    xf = mem_ref[...]
    ms = jnp.mean(xf * xf, axis=-1, keepdims=True)
    y = (xf * lax.rsqrt(ms + EPS) * g_ref[...]).astype(BF16)
    acc = jnp.dot(y, w_ref[...].astype(BF16), preferred_element_type=F32)

    @pl.when(pl.program_id(0) == 0)
    def _():
        parts = _group_rms(acc, kg_ref[...], MEM_HEAD_DIM)
        for i, p in enumerate(parts):
            o_ref[:, i * MEM_HEAD_DIM:(i + 1) * MEM_HEAD_DIM] = p.astype(BF16)

    @pl.when(pl.program_id(0) == 1)
    def _():
        o_ref[...] = acc.astype(BF16)


def _memkv(mem2d, g, w, kg):
    m, d = mem2d.shape
    width = MEM_HEADS * MEM_HEAD_DIM
    return pl.pallas_call(
        _memkv_kernel,
        grid=(2,),
        in_specs=[
            pl.BlockSpec((m, d), lambda j: (0, 0)),
            pl.BlockSpec((1, d), lambda j: (0, 0)),
            pl.BlockSpec((d, width), lambda j: (0, j)),
            pl.BlockSpec((1, MEM_HEAD_DIM), lambda j: (0, 0)),
        ],
        out_specs=pl.BlockSpec((m, width), lambda j: (0, j)),
        out_shape=jax.ShapeDtypeStruct((m, 2 * width), BF16),
        compiler_params=_params(1),
        name="mem_kv",
    )(mem2d, g, w, kg)


def _memattn_kernel(q_ref, kv_ref, o_ref):
    width = MEM_HEADS * MEM_HEAD_DIM
    for h in range(MEM_HEADS):
        lo, hi = h * MEM_HEAD_DIM, (h + 1) * MEM_HEAD_DIM
        q = q_ref[:, lo:hi]
        k = kv_ref[:, lo:hi]
        v = kv_ref[:, width + lo:width + hi]
        s = lax.dot_general(q, k, NT_DIMS, preferred_element_type=F32)
        p = jnp.exp(s - jnp.max(s, axis=-1, keepdims=True))
        l = jnp.sum(p, axis=-1, keepdims=True)
        o = jnp.dot(p.astype(BF16), v, preferred_element_type=F32) / l
        o_ref[:, lo:hi] = o.astype(BF16)


def _memattn(proj, q_col, kv, tm=1024):
    b, s, _ = proj.shape
    width = MEM_HEADS * MEM_HEAD_DIM
    n_mem = kv.shape[1]
    return pl.pallas_call(
        _memattn_kernel,
        grid=(b, s // tm),
        in_specs=[
            pl.BlockSpec((None, tm, width), lambda i, j: (i, j, q_col // width)),
            pl.BlockSpec((None, n_mem, 2 * width), lambda i, j: (i, 0, 0)),
        ],
        out_specs=pl.BlockSpec((None, tm, width), lambda i, j: (i, j, 0)),
        out_shape=jax.ShapeDtypeStruct((b, s, width), BF16),
        compiler_params=_params(2),
        name="mem_attention",
    )(proj, kv)


def _merge_kernel(conv_ref, cw_ref, b_ref, c_ref, wa_ref, wb_ref, wc_ref, g_ref,
                  o_ref, wabf_ref, wbbf_ref, wcbf_ref, carry_ref, *, tiles_per_seq):
    i = pl.program_id(1)
    tn = o_ref.shape[1]

    @pl.when(i == 0)
    def _():
        wabf_ref[...] = wa_ref[...].astype(BF16)
        wbbf_ref[...] = wb_ref[...].astype(BF16)
        wcbf_ref[...] = wc_ref[...].astype(BF16)

    @pl.when(i % tiles_per_seq == 0)
    def _():
        carry_ref[...] = jnp.zeros_like(carry_ref)

    y_conv = _gated_conv_tile(conv_ref, cw_ref, carry_ref).astype(BF16)
    oa = jnp.dot(y_conv, wabf_ref[...], preferred_element_type=F32)
    ob = jnp.dot(b_ref[...], wbbf_ref[...], preferred_element_type=F32)
    oc = jnp.dot(c_ref[...], wcbf_ref[...], preferred_element_type=F32)
    gate = lambda br: g_ref[:, br * tn:(br + 1) * tn].astype(F32)
    o_ref[...] = (gate(0) * oa + gate(1) * ob + gate(2) * oc).astype(BF16)


def _merge(proj, conv_col, conv_w, yb, yc, wa, wb, wc, seq_len, d_model, tm=512, tn=1024):
    m, k = yb.shape
    lhs = pl.BlockSpec((tm, k), lambda j, i: (i, 0))
    wsp = pl.BlockSpec((k, tn), lambda j, i: (0, j))
    return pl.pallas_call(
        functools.partial(_merge_kernel, tiles_per_seq=seq_len // tm),
        grid=(d_model // tn, m // tm),
        in_specs=[pl.BlockSpec((tm, 3 * k), lambda j, i: (i, conv_col // (3 * k))),
                  pl.BlockSpec((CONV_TAPS, k), lambda j, i: (0, 0)),
                  lhs, lhs, wsp, wsp, wsp,
                  pl.BlockSpec((tm, 3 * tn), lambda j, i: (i, j))],
        out_specs=pl.BlockSpec((tm, tn), lambda j, i: (i, j)),
        out_shape=jax.ShapeDtypeStruct((m, d_model), BF16),
        scratch_shapes=[pltpu.VMEM((k, tn), BF16)] * 3 + [pltpu.VMEM((8, k), F32)],
        compiler_params=_params(2),
        name="branch_merge",
    )(proj, conv_w, yb, yc, wa, wb, wc, proj)


def _outproj_kernel(a_ref, w_ref, x_ref, g_ref, x2_ref, h2_ref, wbf_ref):
    @pl.when(pl.program_id(0) == 0)
    def _():
        wbf_ref[...] = w_ref[...].astype(BF16)

    x2 = x_ref[...] + jnp.dot(a_ref[...], wbf_ref[...], preferred_element_type=F32)
    x2_ref[...] = x2
    ms = jnp.mean(x2 * x2, axis=-1, keepdims=True)
    h2_ref[...] = (x2 * lax.rsqrt(ms + EPS) * g_ref[...]).astype(BF16)


def _outproj(a, w, x2d, g, tm=512):
    m, d = x2d.shape
    row = pl.BlockSpec((tm, d), lambda i: (i, 0))
    return pl.pallas_call(
        _outproj_kernel,
        grid=(m // tm,),
        in_specs=[row, pl.BlockSpec((d, d), lambda i: (0, 0), pipeline_mode=pl.Buffered(1)), row,
                  pl.BlockSpec((1, d), lambda i: (0, 0))],
        out_specs=[row, row],
        out_shape=[jax.ShapeDtypeStruct((m, d), F32), jax.ShapeDtypeStruct((m, d), BF16)],
        scratch_shapes=[pltpu.VMEM((d, d), BF16)],
        compiler_params=_params(1),
        name="out_proj_norm2",
    )(a, w, x2d, g)


def _mlp_kernel(h_ref, wu_ref, wd_ref, x_ref, o_ref):
    f = pl.program_id(1)
    rows = x_ref.shape[0]

    @pl.when(f == 0)
    def _():
        o_ref[...] = jnp.zeros_like(o_ref)

    a = jnp.dot(h_ref[...], wu_ref[...].astype(BF16), preferred_element_type=F32)
    a = jnp.square(jnp.maximum(a, 0.0)).astype(BF16)
    o_ref[...] += jnp.dot(a, wd_ref[...].astype(BF16), preferred_element_type=F32)
    slab = pl.ds(pl.multiple_of(f * rows, rows), rows)
    o_ref[slab, :] += x_ref[...]


def _mlp(h2, w_up, w_down, x2, tm=1024, tf=512):
    m, d = h2.shape
    d_ff = w_up.shape[1]
    nf = d_ff // tf
    rows = tm // nf
    return pl.pallas_call(
        _mlp_kernel,
        grid=(m // tm, nf),
        in_specs=[
            pl.BlockSpec((tm, d), lambda i, f: (i, 0)),
            pl.BlockSpec((d, tf), lambda i, f: (0, f)),
            pl.BlockSpec((tf, d), lambda i, f: (f, 0)),
            pl.BlockSpec((rows, d), lambda i, f: (i * nf + f, 0)),
        ],
        out_specs=pl.BlockSpec((tm, d), lambda i, f: (i, 0)),
        out_shape=jax.ShapeDtypeStruct((m, d), F32),
        compiler_params=_params(2),
        name="relu2_mlp",
    )(h2, w_up, w_down, x2)


def kernel(x, mem, norm1_g, w_in, b_f, conv_w, fox_q_g, fox_k_g, mem_norm_g, w_mem_kv, mem_q_g, mem_k_g, w_conv_out, w_fox_out, w_mem_out, w_out, norm2_g, w_up, w_down):
    b, s, d = x.shape
    m = b * s
    conv_width = conv_w.shape[1]
    fox_width = FOX_HEADS * FOX_HEAD_DIM
    mem_width = MEM_HEADS * MEM_HEAD_DIM
    tile = 1024
    q_row = 3 * conv_width
    k_row = q_row + fox_width
    v_row = k_row + fox_width
    f_row = v_row + fox_width
    mq_row = f_row + FOX_HEADS
    gate_row = mq_row + mem_width
    n_gate = N_BRANCHES * d
    ch_tiles = d // tile
    gate_rows = [gate_row + br * d + n * tile for n in range(ch_tiles) for br in range(N_BRANCHES)]
    rest_rows = [r for r in range(0, f_row, tile)] + [r for r in range(mq_row, gate_row, tile)]
    row_starts = tuple(gate_rows + rest_rows)
    conv_col, q_col, k_col, v_col, mq_col = (n_gate + r for r in (0, q_row, k_row, v_row, f_row))
    kinds = ([GATE] * len(gate_rows) + [PLAIN] * (q_row // tile) + [FOX_HEAD_DIM] * (2 * fox_width // tile)
             + [PLAIN] * (fox_width // tile) + [MEM_HEAD_DIM] * (mem_width // tile))
    gains = jnp.ones((len(row_starts) * tile,), F32)
    gains = gains.at[q_col:k_col].set(jnp.tile(fox_q_g * (LOG2_E / math.sqrt(FOX_HEAD_DIM)), FOX_HEADS))
    gains = gains.at[k_col:v_col].set(jnp.tile(fox_k_g, FOX_HEADS))
    gains = gains.at[mq_col:].set(jnp.tile(mem_q_g * (1.0 / math.sqrt(MEM_HEAD_DIM)), MEM_HEADS))

    row = lambda v: v.reshape(1, -1)
    bf = jnp.pad(b_f, (0, LANES - FOX_HEADS)).reshape(1, LANES)
    w_t = w_in.T

    h1, qa, ka = _norm1(x, row(norm1_g), w_t, f_row, bf)
    proj = _proj(h1.reshape(m, d), w_t, row_starts, kinds, gains.reshape(-1, tile))
    proj3 = proj.reshape(b, s, -1)

    y_fox = _fox(proj3, qa, ka, q_col, k_col, v_col)
    kv = _memkv(mem.reshape(-1, d), row(mem_norm_g), w_mem_kv, row(mem_k_g))
    y_mem = _memattn(proj3, mq_col, kv.reshape(b, -1, 2 * mem_width))

    merged = _merge(proj, conv_col, conv_w, y_fox.reshape(m, -1), y_mem.reshape(m, -1),
                    w_conv_out, w_fox_out, w_mem_out, s, d)
    x2, h2 = _outproj(merged, w_out, x.reshape(m, d), row(norm2_g))
    out = _mlp(h2, w_up, w_down, x2)
    return out.reshape(b, s, d)
```

```python
import functools
import math

import jax
import jax.numpy as jnp
import numpy as np
from jax import lax
from jax.experimental import pallas as pl
from jax.experimental.pallas import tpu as pltpu

F32 = jnp.float32
BF16 = jnp.bfloat16

EPS = 1e-6
LOG2_E = math.log2(math.e)
LANES = 128
SUBLANES = 8
BF16_ROWS = 2 * SUBLANES
CONV_TAPS = 3
FOX_HEADS = 8
FOX_HEAD_DIM = 128
MEM_HEADS = 4
MEM_HEAD_DIM = 256
N_BRANCHES = 3

VMEM_LIMIT_BYTES = 56 * 1024 * 1024
COL_TILE = 1024
NT_DIMS = (((1,), (1,)), ((), ()))


def _params(n_axes):
    return pltpu.CompilerParams(
        dimension_semantics=("arbitrary",) * n_axes,
        vmem_limit_bytes=VMEM_LIMIT_BYTES)


def _store_group_rms(o_ref, a, gain, width):
    for s in range(0, a.shape[1], width):
        blk = a[:, s:s + width]
        ms = jnp.mean(blk * blk, axis=-1, keepdims=True)
        o_ref[:, s:s + width] = (blk * lax.rsqrt(ms + EPS) * gain[:, s:s + width]).astype(o_ref.dtype)


def _split3(c):
    hi = c.astype(BF16).astype(F32)
    r1 = c - hi
    mid = r1.astype(BF16).astype(F32)
    lo = (r1 - mid).astype(BF16).astype(F32)
    return hi, mid, lo


ONE_LANE = 3 * FOX_HEADS


def _decay_selector():
    sel = np.zeros((LANES, FOX_HEADS, LANES), np.float32)
    for hd in range(FOX_HEADS):
        for piece in range(3):
            sel[ONE_LANE, hd, piece * FOX_HEADS + hd] = 1.0
            sel[piece * FOX_HEADS + hd, hd, ONE_LANE + piece] = -1.0
    return jnp.asarray(sel.reshape(LANES, -1), BF16)


def _norm1_kernel(x_ref, g_ref, wf_ref, bf_ref, sel_ref, h_ref, qa_ref, ka_ref, carry_ref):
    tm = x_ref.shape[0]

    @pl.when(pl.program_id(1) == 0)
    def _():
        carry_ref[...] = jnp.zeros_like(carry_ref)

    xf = x_ref[...]
    ms = jnp.mean(xf * xf, axis=-1, keepdims=True)
    y = xf * lax.rsqrt(ms + EPS) * g_ref[...]
    y_hi = y.astype(BF16)
    h_ref[...] = y_hi

    wf = wf_ref[...]
    wf_hi = wf.astype(BF16).astype(F32)
    w2 = jnp.concatenate(
        [wf_hi, wf - wf_hi, jnp.zeros((LANES - 2 * FOX_HEADS, wf.shape[1]), F32)], axis=0).astype(BF16)
    z2 = lax.dot_general(y_hi, w2, NT_DIMS, preferred_element_type=F32)
    z = z2 + pltpu.roll(z2, LANES - FOX_HEADS, 1) + bf_ref[...]
    log_f = -LOG2_E * (jnp.maximum(-z, 0.0) + jnp.log1p(jnp.exp(-jnp.abs(z))))

    row = lax.broadcasted_iota(jnp.int32, (tm, tm), 0)
    col = lax.broadcasted_iota(jnp.int32, (tm, tm), 1)
    tri = jnp.where(row >= col, 1.0, 0.0).astype(BF16)
    f_hi, f_mid, f_lo = _split3(log_f)
    part = jnp.dot(tri, jnp.concatenate([f_hi, f_mid], axis=1).astype(BF16),
                   preferred_element_type=F32)
    c = (part[:, :LANES] + part[:, LANES:]
         + jnp.dot(tri, f_lo.astype(BF16), preferred_element_type=F32) + carry_ref[0:1, :])
    carry_ref[0:1, :] = c[tm - 1:tm, :]

    lane = lax.broadcasted_iota(jnp.int32, (tm, LANES), 1)
    c_hi, c_mid, c_lo = _split3(c)
    packed = jnp.where(
        lane < FOX_HEADS, c_hi,
        jnp.where(lane < 2 * FOX_HEADS, pltpu.roll(c_mid, FOX_HEADS, 1),
                  jnp.where(lane < ONE_LANE, pltpu.roll(c_lo, 2 * FOX_HEADS, 1),
                            jnp.where(lane < ONE_LANE + 3, 1.0, 0.0)))).astype(BF16)
    qa_ref[...] = packed
    ka_ref[...] = jnp.dot(packed, sel_ref[...], preferred_element_type=F32).astype(BF16)


def _norm1(x, g, w_t, f_row, bf, tm=512):
    b, s, d = x.shape
    return pl.pallas_call(
        _norm1_kernel,
        grid=(b, s // tm),
        in_specs=[
            pl.BlockSpec((None, tm, d), lambda i, j: (i, j, 0)),
            pl.BlockSpec((1, d), lambda i, j: (0, 0)),
            pl.BlockSpec((FOX_HEADS, d), lambda i, j: (f_row // FOX_HEADS, 0)),
            pl.BlockSpec((1, LANES), lambda i, j: (0, 0)),
            pl.BlockSpec((LANES, FOX_HEADS * LANES), lambda i, j: (0, 0)),
        ],
        out_specs=[
            pl.BlockSpec((None, tm, d), lambda i, j: (i, j, 0)),
            pl.BlockSpec((None, tm, LANES), lambda i, j: (i, j, 0)),
            pl.BlockSpec((None, tm, FOX_HEADS * LANES), lambda i, j: (i, j, 0)),
        ],
        out_shape=[
            jax.ShapeDtypeStruct((b, s, d), BF16),
            jax.ShapeDtypeStruct((b, s, LANES), BF16),
            jax.ShapeDtypeStruct((b, s, FOX_HEADS * LANES), BF16),
        ],
        scratch_shapes=[pltpu.VMEM((SUBLANES, LANES), F32)],
        compiler_params=_params(2),
        name="norm1_decay",
    )(x, g, w_t, bf, _decay_selector())


PLAIN, GATE = 0, -1


def _proj_kernel(h_ref, w_ref, gain_ref, o_ref, wbf_ref, *, kinds):
    @pl.when(pl.program_id(1) == 0)
    def _():
        wbf_ref[...] = w_ref[...].astype(BF16)

    def tile(kind):
        acc = lax.dot_general(h_ref[...], wbf_ref[...], NT_DIMS, preferred_element_type=F32)
        if kind == PLAIN:
            o_ref[...] = acc.astype(BF16)
        elif kind == GATE:
            o_ref[...] = (0.5 * jnp.tanh(0.5 * acc) + 0.5).astype(BF16)
        else:
            _store_group_rms(o_ref, acc, gain_ref[...], kind)

    j = pl.program_id(0)
    for kind in sorted(set(kinds)):
        hit = functools.reduce(jnp.logical_or, [j == t for t, kd in enumerate(kinds) if kd == kind])
        pl.when(hit)(functools.partial(tile, kind))


def _proj(h, w_t, row_starts, kinds, gains, tm=1024, tn=COL_TILE):
    m, k = h.shape
    n_tiles = len(row_starts)

    def w_rows(j, i):
        start = jnp.int32(row_starts[0])
        for t in range(1, n_tiles):
            start = jnp.where(j == t, row_starts[t], start)
        return (pl.multiple_of(start, 8), 0)

    return pl.pallas_call(
        functools.partial(_proj_kernel, kinds=tuple(kinds)),
        grid=(n_tiles, m // tm),
        in_specs=[
            pl.BlockSpec((tm, k), lambda j, i: (i, 0)),
            pl.BlockSpec((pl.Element(tn), pl.Element(k)), w_rows),
            pl.BlockSpec((None, 1, tn), lambda j, i: (j, 0, 0)),
        ],
        out_specs=pl.BlockSpec((tm, tn), lambda j, i: (i, j)),
        out_shape=jax.ShapeDtypeStruct((m, n_tiles * tn), BF16),
        scratch_shapes=[pltpu.VMEM((tn, k), BF16)],
        compiler_params=_params(2),
        name="in_proj",
    )(h, w_t, gains.reshape(n_tiles, 1, tn))


def _gated_conv_tile(conv_ref, w_ref, carry_ref):
    tm = conv_ref.shape[0]
    k = w_ref.shape[1]
    u = conv_ref[:, k:2 * k].astype(F32) * conv_ref[:, 2 * k:3 * k].astype(F32)
    w = w_ref[...]
    taps = lambda u2, u1, u0: w[0:1, :] * u2 + w[1:2, :] * u1 + w[2:3, :] * u0
    conv = taps(pltpu.roll(u, 2, 0), pltpu.roll(u, 1, 0), u)
    top = u[0:SUBLANES, :]
    row = lax.broadcasted_iota(jnp.int32, (SUBLANES, 1), 0)
    prev1 = carry_ref[1:2, :]
    prev2 = carry_ref[0:1, :]
    top1 = jnp.where(row == 0, prev1, pltpu.roll(top, 1, 0))
    top2 = jnp.where(row == 0, prev2, jnp.where(row == 1, prev1, pltpu.roll(top, 2, 0)))
    conv = jnp.concatenate([taps(top2, top1, top), conv[SUBLANES:, :]], axis=0)
    carry_ref[0:2, :] = u[tm - 2:tm, :]
    return conv_ref[:, 0:k].astype(F32) * conv


def _fox_kernel(q_ref, qa_ref, k_ref, ka_ref, v_ref, o_ref, vt_ref, s_ref, m_ref, l_ref, acc_ref, *,
                t):
    n_tiles = q_ref.shape[0] // t
    assert n_tiles % 2 == 0
    d = FOX_HEAD_DIM
    n_heads = q_ref.shape[1] // d
    j = pl.program_id(2)

    @pl.when(j == 0)
    def _():
        for g in range(n_heads):
            for blk in range(v_ref.shape[0] // t):
                v = v_ref[blk * t:(blk + 1) * t, g * d:(g + 1) * d]
                vt_ref[g, blk, :d] = v.astype(F32).T.astype(BF16)
                vt_ref[g, blk, d:] = jnp.ones((vt_ref.shape[2] - d, t), BF16)

    def scores(tile, kb, g):
        q = jnp.concatenate([q_ref[tile * t:(tile + 1) * t, g * d:(g + 1) * d],
                             qa_ref[tile * t:(tile + 1) * t, :]], axis=1)
        rows = pl.ds(pl.multiple_of(kb * t, t), t)
        k = jnp.concatenate([k_ref[rows, g * d:(g + 1) * d], ka_ref[rows, g * d:(g + 1) * d]], axis=1)
        return lax.dot_general(k, q, NT_DIMS, preferred_element_type=F32)

    def reset(g):
        m_ref[g] = jnp.full(m_ref.shape[1:], -jnp.inf, F32)
        l_ref[g] = jnp.zeros(l_ref.shape[1:], F32)
        acc_ref[g] = jnp.zeros(acc_ref.shape[1:], F32)

    def update(g, m_new, alpha, pv):
        l_ref[g] = alpha * l_ref[g] + pv[d:d + 1]
        acc_ref[g] = alpha * acc_ref[g] + pv[:d]
        m_ref[g] = m_new

    def consume(kb, slot, g):
        s = s_ref[slot, g]
        m_prev = m_ref[g]
        m_new = jnp.maximum(m_prev, jnp.max(s, axis=0, keepdims=True))
        p = jnp.exp2(s - m_new)
        pv = jnp.dot(vt_ref[g, kb], p.astype(BF16), preferred_element_type=F32)
        update(g, m_new, jnp.exp2(m_prev - m_new), pv)

    def causal(s):
        kpos = lax.broadcasted_iota(jnp.int32, s.shape, 0)
        qpos = lax.broadcasted_iota(jnp.int32, s.shape, 1)
        return jnp.where(qpos >= kpos, s, -jnp.inf)

    def consume_diagonal(kb, slot, g):
        h = t // 2
        s_up = causal(s_ref[slot, g, :h, :])
        s_lo = causal(s_ref[slot, g, h:, h:])
        m_up = jnp.max(s_up, axis=0, keepdims=True)
        m_blk = jnp.concatenate(
            [m_up[:, :h], jnp.maximum(m_up[:, h:], jnp.max(s_lo, axis=0, keepdims=True))], axis=1)
        m_prev = m_ref[g]
        m_new = jnp.maximum(m_prev, m_blk)
        p_up = jnp.exp2(s_up - m_new).astype(BF16)
        p_lo = jnp.exp2(s_lo - m_new[:, h:]).astype(BF16)
        vt = vt_ref[g, kb]
        pv_up = jnp.dot(vt[:, :h], p_up, preferred_element_type=F32)
        pv_lo = jnp.dot(vt[:, h:], p_lo, preferred_element_type=F32)
        pv = jnp.concatenate([pv_up[:, :h], pv_up[:, h:] + pv_lo], axis=1)
        update(g, m_new, jnp.exp2(m_prev - m_new), pv)

    def advance(tile, kb, slot):
        for g in range(n_heads):
            s_ref[1 - slot, g] = scores(tile, kb + 1, g)
            consume(kb, slot, g)

    def emit(tile, g):
        o_ref[tile * t:(tile + 1) * t, g * d:(g + 1) * d] = (acc_ref[g] / l_ref[g]).T.astype(BF16)

    slot = 0
    for g in range(n_heads):
        reset(g)
        s_ref[slot, g] = scores(0, 0, g)
    for r in range(n_tiles):
        def pairs(pair, carry, r=r, slot=slot):
            advance(r, 2 * pair, slot)
            advance(r, 2 * pair + 1, 1 - slot)
            return carry

        lax.fori_loop(0, (n_tiles // 2) * j + r // 2, pairs, 0)
        diagonal = n_tiles * j + r
        if r % 2 == 1:
            advance(r, diagonal - 1, slot)
            slot = 1 - slot
        for g in range(n_heads):
            if r + 1 < n_tiles:
                s_ref[1 - slot, g] = scores(r + 1, 0, g)
            consume_diagonal(diagonal, slot, g)
            emit(r, g)
            if r + 1 < n_tiles:
                reset(g)
        slot = 1 - slot


def _fox(proj, qa, ka, q_col, k_col, v_col, t=512, tiles_per_step=4, heads_per_step=4):
    b, s, _ = proj.shape
    d = FOX_HEAD_DIM
    w = heads_per_step * d
    rows = tiles_per_step * t
    tiles = lambda col0: pl.BlockSpec((None, rows, w), lambda bi, h, j: (bi, j, col0 // w + h))
    seq = lambda col0: pl.BlockSpec((None, s, w), lambda bi, h, j: (bi, 0, col0 // w + h))
    return pl.pallas_call(
        functools.partial(_fox_kernel, t=t),
        grid=(b, FOX_HEADS // heads_per_step, s // rows),
        in_specs=[tiles(q_col), pl.BlockSpec((None, rows, LANES), lambda bi, h, j: (bi, j, 0)),
                  seq(k_col), seq(0), seq(v_col)],
        out_specs=tiles(0),
        out_shape=jax.ShapeDtypeStruct((b, s, FOX_HEADS * d), BF16),
        scratch_shapes=[pltpu.VMEM((heads_per_step, s // t, d + BF16_ROWS, t), BF16),
                        pltpu.VMEM((2, heads_per_step, t, t), F32),
                        pltpu.VMEM((heads_per_step, 1, t), F32),
                        pltpu.VMEM((heads_per_step, 1, t), F32),
                        pltpu.VMEM((heads_per_step, d, t), F32)],
        compiler_params=_params(3),
        name="fox_attention",
    )(proj, qa, proj, ka, proj)


def _memkv_kernel(mem_ref, g_ref, w_ref, kg_ref, o_ref):
    xf = mem_ref[...]
    ms = jnp.mean(xf * xf, axis=-1, keepdims=True)
    y = (xf * lax.rsqrt(ms + EPS) * g_ref[...]).astype(BF16)
    acc = jnp.dot(y, w_ref[...].astype(BF16), preferred_element_type=F32)

    @pl.when(pl.program_id(0) == 0)
    def _():
        _store_group_rms(o_ref, acc, kg_ref[...], MEM_HEAD_DIM)

    @pl.when(pl.program_id(0) == 1)
    def _():
        o_ref[...] = acc.astype(BF16)


def _memkv(mem2d, g, w, kg):
    m, d = mem2d.shape
    width = MEM_HEADS * MEM_HEAD_DIM
    return pl.pallas_call(
        _memkv_kernel,
        grid=(2,),
        in_specs=[
            pl.BlockSpec((m, d), lambda j: (0, 0)),
            pl.BlockSpec((1, d), lambda j: (0, 0)),
            pl.BlockSpec((d, width), lambda j: (0, j)),
            pl.BlockSpec((1, width), lambda j: (0, 0)),
        ],
        out_specs=pl.BlockSpec((m, width), lambda j: (0, j)),
        out_shape=jax.ShapeDtypeStruct((m, 2 * width), BF16),
        compiler_params=_params(1),
        name="mem_kv",
    )(mem2d, g, w, kg)


def _memattn_kernel(q_ref, kv_ref, o_ref):
    width = MEM_HEADS * MEM_HEAD_DIM
    for h in range(MEM_HEADS):
        lo, hi = h * MEM_HEAD_DIM, (h + 1) * MEM_HEAD_DIM
        q = q_ref[:, lo:hi]
        k = kv_ref[:, lo:hi]
        v = kv_ref[:, width + lo:width + hi]
        s = lax.dot_general(q, k, NT_DIMS, preferred_element_type=F32)
        p = jnp.exp(s - jnp.max(s, axis=-1, keepdims=True))
        l = jnp.sum(p, axis=-1, keepdims=True)
        o = jnp.dot(p.astype(BF16), v, preferred_element_type=F32) / l
        o_ref[:, lo:hi] = o.astype(BF16)


def _memattn(proj, q_col, kv, tm=1024):
    b, s, _ = proj.shape
    width = MEM_HEADS * MEM_HEAD_DIM
    n_mem = kv.shape[1]
    return pl.pallas_call(
        _memattn_kernel,
        grid=(b, s // tm),
        in_specs=[
            pl.BlockSpec((None, tm, width), lambda i, j: (i, j, q_col // width)),
            pl.BlockSpec((None, n_mem, 2 * width), lambda i, j: (i, 0, 0)),
        ],
        out_specs=pl.BlockSpec((None, tm, width), lambda i, j: (i, j, 0)),
        out_shape=jax.ShapeDtypeStruct((b, s, width), BF16),
        compiler_params=_params(2),
        name="mem_attention",
    )(proj, kv)


def _merge_kernel(conv_ref, cw_ref, b_ref, c_ref, wa_ref, wb_ref, wc_ref, g_ref,
                  o_ref, wabf_ref, wbbf_ref, wcbf_ref, carry_ref, *, tiles_per_seq):
    i = pl.program_id(1)
    tn = o_ref.shape[1]

    @pl.when(i == 0)
    def _():
        wabf_ref[...] = wa_ref[...].astype(BF16)
        wbbf_ref[...] = wb_ref[...].astype(BF16)
        wcbf_ref[...] = wc_ref[...].astype(BF16)

    @pl.when(i % tiles_per_seq == 0)
    def _():
        carry_ref[...] = jnp.zeros_like(carry_ref)

    y_conv = _gated_conv_tile(conv_ref, cw_ref, carry_ref).astype(BF16)
    oa = jnp.dot(y_conv, wabf_ref[...], preferred_element_type=F32)
    ob = jnp.dot(b_ref[...], wbbf_ref[...], preferred_element_type=F32)
    oc = jnp.dot(c_ref[...], wcbf_ref[...], preferred_element_type=F32)
    gate = lambda br: g_ref[:, br * tn:(br + 1) * tn].astype(F32)
    o_ref[...] = (gate(0) * oa + gate(1) * ob + gate(2) * oc).astype(BF16)


def _merge(proj, conv_col, conv_w, yb, yc, wa, wb, wc, seq_len, d_model, tm=512, tn=COL_TILE):
    m, k = yb.shape
    lhs = pl.BlockSpec((tm, k), lambda j, i: (i, 0))
    wsp = pl.BlockSpec((k, tn), lambda j, i: (0, j))
    return pl.pallas_call(
        functools.partial(_merge_kernel, tiles_per_seq=seq_len // tm),
        grid=(d_model // tn, m // tm),
        in_specs=[pl.BlockSpec((tm, 3 * k), lambda j, i: (i, conv_col // (3 * k))),
                  pl.BlockSpec((CONV_TAPS, k), lambda j, i: (0, 0)),
                  lhs, lhs, wsp, wsp, wsp,
                  pl.BlockSpec((tm, 3 * tn), lambda j, i: (i, j))],
        out_specs=pl.BlockSpec((tm, tn), lambda j, i: (i, j)),
        out_shape=jax.ShapeDtypeStruct((m, d_model), BF16),
        scratch_shapes=[pltpu.VMEM((k, tn), BF16)] * 3 + [pltpu.VMEM((SUBLANES, k), F32)],
        compiler_params=_params(2),
        name="branch_merge",
    )(proj, conv_w, yb, yc, wa, wb, wc, proj)


def _outproj_kernel(a_ref, w_ref, x_ref, g_ref, x2_ref, h2_ref, wbf_ref):
    @pl.when(pl.program_id(0) == 0)
    def _():
        wbf_ref[...] = w_ref[...].astype(BF16)

    x2 = x_ref[...] + jnp.dot(a_ref[...], wbf_ref[...], preferred_element_type=F32)
    x2_ref[...] = x2
    ms = jnp.mean(x2 * x2, axis=-1, keepdims=True)
    h2_ref[...] = (x2 * lax.rsqrt(ms + EPS) * g_ref[...]).astype(BF16)


def _outproj(a, w, x2d, g, tm=512):
    m, d = x2d.shape
    row = pl.BlockSpec((tm, d), lambda i: (i, 0))
    return pl.pallas_call(
        _outproj_kernel,
        grid=(m // tm,),
        in_specs=[row, pl.BlockSpec((d, d), lambda i: (0, 0), pipeline_mode=pl.Buffered(1)), row,
                  pl.BlockSpec((1, d), lambda i: (0, 0))],
        out_specs=[row, row],
        out_shape=[jax.ShapeDtypeStruct((m, d), F32), jax.ShapeDtypeStruct((m, d), BF16)],
        scratch_shapes=[pltpu.VMEM((d, d), BF16)],
        compiler_params=_params(1),
        name="out_proj_norm2",
    )(a, w, x2d, g)


def _mlp_kernel(h_ref, wu_ref, wd_ref, x_ref, o_ref):
    f = pl.program_id(1)
    rows = x_ref.shape[0]

    @pl.when(f == 0)
    def _():
        o_ref[...] = jnp.zeros_like(o_ref)

    a = jnp.dot(h_ref[...], wu_ref[...].astype(BF16), preferred_element_type=F32)
    a = jnp.square(jnp.maximum(a, 0.0)).astype(BF16)
    o_ref[...] += jnp.dot(a, wd_ref[...].astype(BF16), preferred_element_type=F32)
    slab = pl.ds(pl.multiple_of(f * rows, rows), rows)
    o_ref[slab, :] += x_ref[...]


def _mlp(h2, w_up, w_down, x2, tm=1024, tf=512):
    m, d = h2.shape
    d_ff = w_up.shape[1]
    nf = d_ff // tf
    rows = tm // nf
    return pl.pallas_call(
        _mlp_kernel,
        grid=(m // tm, nf),
        in_specs=[
            pl.BlockSpec((tm, d), lambda i, f: (i, 0)),
            pl.BlockSpec((d, tf), lambda i, f: (0, f)),
            pl.BlockSpec((tf, d), lambda i, f: (f, 0)),
            pl.BlockSpec((rows, d), lambda i, f: (i * nf + f, 0)),
        ],
        out_specs=pl.BlockSpec((tm, d), lambda i, f: (i, 0)),
        out_shape=jax.ShapeDtypeStruct((m, d), F32),
        compiler_params=_params(2),
        name="relu2_mlp",
    )(h2, w_up, w_down, x2)


def kernel(x, mem, norm1_g, w_in, b_f, conv_w, fox_q_g, fox_k_g, mem_norm_g, w_mem_kv, mem_q_g, mem_k_g, w_conv_out, w_fox_out, w_mem_out, w_out, norm2_g, w_up, w_down):
    b, s, d = x.shape
    m = b * s
    conv_width = conv_w.shape[1]
    fox_width = FOX_HEADS * FOX_HEAD_DIM
    mem_width = MEM_HEADS * MEM_HEAD_DIM
    tile = COL_TILE
    q_row = 3 * conv_width
    k_row = q_row + fox_width
    v_row = k_row + fox_width
    f_row = v_row + fox_width
    mq_row = f_row + FOX_HEADS
    gate_row = mq_row + mem_width
    n_gate = N_BRANCHES * d
    ch_tiles = d // tile
    gate_rows = [gate_row + br * d + n * tile for n in range(ch_tiles) for br in range(N_BRANCHES)]
    rest_rows = [r for r in range(0, f_row, tile)] + [r for r in range(mq_row, gate_row, tile)]
    row_starts = tuple(gate_rows + rest_rows)
    conv_col, q_col, k_col, v_col, mq_col = (n_gate + r for r in (0, q_row, k_row, v_row, f_row))
    kinds = ([GATE] * len(gate_rows) + [PLAIN] * (q_row // tile) + [FOX_HEAD_DIM] * (2 * fox_width // tile)
             + [PLAIN] * (fox_width // tile) + [MEM_HEAD_DIM] * (mem_width // tile))
    gains = jnp.ones((len(row_starts) * tile,), F32)
    gains = gains.at[q_col:k_col].set(jnp.tile(fox_q_g * (LOG2_E / math.sqrt(FOX_HEAD_DIM)), FOX_HEADS))
    gains = gains.at[k_col:v_col].set(jnp.tile(fox_k_g, FOX_HEADS))
    gains = gains.at[mq_col:].set(jnp.tile(mem_q_g * (1.0 / math.sqrt(MEM_HEAD_DIM)), MEM_HEADS))

    row = lambda v: v.reshape(1, -1)
    bf = jnp.pad(b_f, (0, LANES - FOX_HEADS)).reshape(1, LANES)
    w_t = w_in.T

    h1, qa, ka = _norm1(x, row(norm1_g), w_t, f_row, bf)
    proj = _proj(h1.reshape(m, d), w_t, row_starts, kinds, gains.reshape(-1, tile))
    proj3 = proj.reshape(b, s, -1)

    y_fox = _fox(proj3, qa, ka, q_col, k_col, v_col)
    kv = _memkv(mem.reshape(-1, d), row(mem_norm_g), w_mem_kv, row(jnp.tile(mem_k_g, MEM_HEADS)))
    y_mem = _memattn(proj3, mq_col, kv.reshape(b, -1, 2 * mem_width))

    merged = _merge(proj, conv_col, conv_w, y_fox.reshape(m, -1), y_mem.reshape(m, -1),
                    w_conv_out, w_fox_out, w_mem_out, s, d)
    x2, h2 = _outproj(merged, w_out, x.reshape(m, d), row(norm2_g))
    out = _mlp(h2, w_up, w_down, x2)
    return out.reshape(b, s, d)
```

```python
import functools
import math

import jax
import jax.numpy as jnp
import numpy as np
from jax import lax
from jax.experimental import pallas as pl
from jax.experimental.pallas import tpu as pltpu

F32 = jnp.float32
BF16 = jnp.bfloat16

EPS = 1e-6
LOG2_E = math.log2(math.e)
LANES = 128
SUBLANES = 8
BF16_ROWS = 2 * SUBLANES
CONV_TAPS = 3
FOX_HEADS = 8
FOX_HEAD_DIM = 128
MEM_HEADS = 4
MEM_HEAD_DIM = 256
N_BRANCHES = 3

VMEM_LIMIT_BYTES = 56 * 1024 * 1024
COL_TILE = 1024
NT_DIMS = (((1,), (1,)), ((), ()))


def _params(n_axes):
    return pltpu.CompilerParams(
        dimension_semantics=("arbitrary",) * n_axes,
        vmem_limit_bytes=VMEM_LIMIT_BYTES)


def _store_group_rms(o_ref, a, gain, width):
    for s in range(0, a.shape[1], width):
        blk = a[:, s:s + width]
        ms = jnp.mean(blk * blk, axis=-1, keepdims=True)
        o_ref[:, s:s + width] = (blk * lax.rsqrt(ms + EPS) * gain[:, s:s + width]).astype(o_ref.dtype)


def _split3(c):
    hi = c.astype(BF16).astype(F32)
    r1 = c - hi
    mid = r1.astype(BF16).astype(F32)
    lo = (r1 - mid).astype(BF16).astype(F32)
    return hi, mid, lo


ONE_LANE = 3 * FOX_HEADS


def _decay_selector():
    sel = np.zeros((LANES, FOX_HEADS, LANES), np.float32)
    for hd in range(FOX_HEADS):
        for piece in range(3):
            sel[ONE_LANE, hd, piece * FOX_HEADS + hd] = 1.0
            sel[piece * FOX_HEADS + hd, hd, ONE_LANE + piece] = -1.0
    return jnp.asarray(sel.reshape(LANES, -1), BF16)


def _norm1_kernel(x_ref, g_ref, wf_ref, bf_ref, sel_ref, h_ref, qa_ref, ka_ref, carry_ref):
    tm = x_ref.shape[0]

    @pl.when(pl.program_id(1) == 0)
    def _():
        carry_ref[...] = jnp.zeros_like(carry_ref)

    xf = x_ref[...]
    ms = jnp.mean(xf * xf, axis=-1, keepdims=True)
    y = xf * lax.rsqrt(ms + EPS) * g_ref[...]
    y_hi = y.astype(BF16)
    h_ref[...] = y_hi

    wf = wf_ref[...]
    wf_hi = wf.astype(BF16).astype(F32)
    w2 = jnp.concatenate(
        [wf_hi, wf - wf_hi, jnp.zeros((LANES - 2 * FOX_HEADS, wf.shape[1]), F32)], axis=0).astype(BF16)
    z2 = lax.dot_general(y_hi, w2, NT_DIMS, preferred_element_type=F32)
    z = z2 + pltpu.roll(z2, LANES - FOX_HEADS, 1) + bf_ref[...]
    log_f = -LOG2_E * (jnp.maximum(-z, 0.0) + jnp.log1p(jnp.exp(-jnp.abs(z))))

    row = lax.broadcasted_iota(jnp.int32, (tm, tm), 0)
    col = lax.broadcasted_iota(jnp.int32, (tm, tm), 1)
    tri = jnp.where(row >= col, 1.0, 0.0).astype(BF16)
    f_hi, f_mid, f_lo = _split3(log_f)
    part = jnp.dot(tri, jnp.concatenate([f_hi, f_mid], axis=1).astype(BF16),
                   preferred_element_type=F32)
    c = (part[:, :LANES] + part[:, LANES:]
         + jnp.dot(tri, f_lo.astype(BF16), preferred_element_type=F32) + carry_ref[0:1, :])
    carry_ref[0:1, :] = c[tm - 1:tm, :]

    lane = lax.broadcasted_iota(jnp.int32, (tm, LANES), 1)
    c_hi, c_mid, c_lo = _split3(c)
    packed = jnp.where(
        lane < FOX_HEADS, c_hi,
        jnp.where(lane < 2 * FOX_HEADS, pltpu.roll(c_mid, FOX_HEADS, 1),
                  jnp.where(lane < ONE_LANE, pltpu.roll(c_lo, 2 * FOX_HEADS, 1),
                            jnp.where(lane < ONE_LANE + 3, 1.0, 0.0)))).astype(BF16)
    qa_ref[...] = packed
    ka_ref[...] = jnp.dot(packed, sel_ref[...], preferred_element_type=F32).astype(BF16)


def _norm1(x, g, w_t, f_row, bf, tm=512):
    b, s, d = x.shape
    return pl.pallas_call(
        _norm1_kernel,
        grid=(b, s // tm),
        in_specs=[
            pl.BlockSpec((None, tm, d), lambda i, j: (i, j, 0)),
            pl.BlockSpec((1, d), lambda i, j: (0, 0)),
            pl.BlockSpec((FOX_HEADS, d), lambda i, j: (f_row // FOX_HEADS, 0)),
            pl.BlockSpec((1, LANES), lambda i, j: (0, 0)),
            pl.BlockSpec((LANES, FOX_HEADS * LANES), lambda i, j: (0, 0)),
        ],
        out_specs=[
            pl.BlockSpec((None, tm, d), lambda i, j: (i, j, 0)),
            pl.BlockSpec((None, tm, LANES), lambda i, j: (i, j, 0)),
            pl.BlockSpec((None, tm, FOX_HEADS * LANES), lambda i, j: (i, j, 0)),
        ],
        out_shape=[
            jax.ShapeDtypeStruct((b, s, d), BF16),
            jax.ShapeDtypeStruct((b, s, LANES), BF16),
            jax.ShapeDtypeStruct((b, s, FOX_HEADS * LANES), BF16),
        ],
        scratch_shapes=[pltpu.VMEM((SUBLANES, LANES), F32)],
        compiler_params=_params(2),
        name="norm1_decay",
    )(x, g, w_t, bf, _decay_selector())


PLAIN, GATE = 0, -1


def _proj_kernel(h_ref, w_ref, gain_ref, o_ref, wbf_ref, *, kinds):
    @pl.when(pl.program_id(1) == 0)
    def _():
        wbf_ref[...] = w_ref[...].astype(BF16)

    def tile(kind):
        acc = lax.dot_general(h_ref[...], wbf_ref[...], NT_DIMS, preferred_element_type=F32)
        if kind == PLAIN:
            o_ref[...] = acc.astype(BF16)
        elif kind == GATE:
            o_ref[...] = (0.5 * jnp.tanh(0.5 * acc) + 0.5).astype(BF16)
        else:
            _store_group_rms(o_ref, acc, gain_ref[...], kind)

    j = pl.program_id(0)
    for kind in sorted(set(kinds)):
        hit = functools.reduce(jnp.logical_or, [j == t for t, kd in enumerate(kinds) if kd == kind])
        pl.when(hit)(functools.partial(tile, kind))


def _proj(h, w_t, row_starts, kinds, gains, tm=1024, tn=COL_TILE):
    m, k = h.shape
    n_tiles = len(row_starts)

    def w_rows(j, i):
        start = jnp.int32(row_starts[0])
        for t in range(1, n_tiles):
            start = jnp.where(j == t, row_starts[t], start)
        return (pl.multiple_of(start, 8), 0)

    return pl.pallas_call(
        functools.partial(_proj_kernel, kinds=tuple(kinds)),
        grid=(n_tiles, m // tm),
        in_specs=[
            pl.BlockSpec((tm, k), lambda j, i: (i, 0)),
            pl.BlockSpec((pl.Element(tn), pl.Element(k)), w_rows),
            pl.BlockSpec((None, 1, tn), lambda j, i: (j, 0, 0)),
        ],
        out_specs=pl.BlockSpec((tm, tn), lambda j, i: (i, j)),
        out_shape=jax.ShapeDtypeStruct((m, n_tiles * tn), BF16),
        scratch_shapes=[pltpu.VMEM((tn, k), BF16)],
        compiler_params=_params(2),
        name="in_proj",
    )(h, w_t, gains.reshape(n_tiles, 1, tn))


def _gated_conv_tile(conv_ref, w_ref, carry_ref):
    tm = conv_ref.shape[0]
    k = w_ref.shape[1]
    u = conv_ref[:, k:2 * k].astype(F32) * conv_ref[:, 2 * k:3 * k].astype(F32)
    w = w_ref[...]
    taps = lambda u2, u1, u0: w[0:1, :] * u2 + w[1:2, :] * u1 + w[2:3, :] * u0
    conv = taps(pltpu.roll(u, 2, 0), pltpu.roll(u, 1, 0), u)
    top = u[0:SUBLANES, :]
    row = lax.broadcasted_iota(jnp.int32, (SUBLANES, 1), 0)
    prev1 = carry_ref[1:2, :]
    prev2 = carry_ref[0:1, :]
    top1 = jnp.where(row == 0, prev1, pltpu.roll(top, 1, 0))
    top2 = jnp.where(row == 0, prev2, jnp.where(row == 1, prev1, pltpu.roll(top, 2, 0)))
    conv = jnp.concatenate([taps(top2, top1, top), conv[SUBLANES:, :]], axis=0)
    carry_ref[0:2, :] = u[tm - 2:tm, :]
    return conv_ref[:, 0:k].astype(F32) * conv


def _fox_kernel(q_ref, qa_ref, k_ref, ka_ref, v_ref, o_ref, vt_ref, s_ref, m_ref, l_ref, acc_ref, *,
                t):
    n_tiles = q_ref.shape[0] // t
    assert n_tiles % 2 == 0
    d = FOX_HEAD_DIM
    n_heads = q_ref.shape[1] // d
    j = pl.program_id(2)

    @pl.when(j == 0)
    def _():
        for g in range(n_heads):
            for blk in range(v_ref.shape[0] // t):
                v = v_ref[blk * t:(blk + 1) * t, g * d:(g + 1) * d]
                vt_ref[g, blk, :d] = v.astype(F32).T.astype(BF16)
                vt_ref[g, blk, d:] = jnp.ones((vt_ref.shape[2] - d, t), BF16)

    def scores(tile, kb, g):
        q = jnp.concatenate([q_ref[tile * t:(tile + 1) * t, g * d:(g + 1) * d],
                             qa_ref[tile * t:(tile + 1) * t, :]], axis=1)
        rows = pl.ds(pl.multiple_of(kb * t, t), t)
        k = jnp.concatenate([k_ref[rows, g * d:(g + 1) * d], ka_ref[rows, g * d:(g + 1) * d]], axis=1)
        return lax.dot_general(k, q, NT_DIMS, preferred_element_type=F32)

    def reset(g):
        m_ref[g] = jnp.full(m_ref.shape[1:], -jnp.inf, F32)
        l_ref[g] = jnp.zeros(l_ref.shape[1:], F32)
        acc_ref[g] = jnp.zeros(acc_ref.shape[1:], F32)

    def update(g, m_new, alpha, pv):
        l_ref[g] = alpha * l_ref[g] + pv[d:d + 1]
        acc_ref[g] = alpha * acc_ref[g] + pv[:d]
        m_ref[g] = m_new

    def consume(kb, slot, g):
        s = s_ref[slot, g]
        m_prev = m_ref[g]
        m_new = jnp.maximum(m_prev, jnp.max(s, axis=0, keepdims=True))
        p = jnp.exp2(s - m_new)
        pv = jnp.dot(vt_ref[g, kb], p.astype(BF16), preferred_element_type=F32)
        update(g, m_new, jnp.exp2(m_prev - m_new), pv)

    def causal(s):
        kpos = lax.broadcasted_iota(jnp.int32, s.shape, 0)
        qpos = lax.broadcasted_iota(jnp.int32, s.shape, 1)
        return jnp.where(qpos >= kpos, s, -jnp.inf)

    def consume_diagonal(kb, slot, g):
        h = t // 2
        s_up = causal(s_ref[slot, g, :h, :])
        s_lo = causal(s_ref[slot, g, h:, h:])
        m_up = jnp.max(s_up, axis=0, keepdims=True)
        m_blk = jnp.concatenate(
            [m_up[:, :h], jnp.maximum(m_up[:, h:], jnp.max(s_lo, axis=0, keepdims=True))], axis=1)
        m_prev = m_ref[g]
        m_new = jnp.maximum(m_prev, m_blk)
        p_up = jnp.exp2(s_up - m_new).astype(BF16)
        p_lo = jnp.exp2(s_lo - m_new[:, h:]).astype(BF16)
        vt = vt_ref[g, kb]
        pv_up = jnp.dot(vt[:, :h], p_up, preferred_element_type=F32)
        pv_lo = jnp.dot(vt[:, h:], p_lo, preferred_element_type=F32)
        pv = jnp.concatenate([pv_up[:, :h], pv_up[:, h:] + pv_lo], axis=1)
        update(g, m_new, jnp.exp2(m_prev - m_new), pv)

    def advance(tile, kb, slot):
        for g in range(n_heads):
            s_ref[1 - slot, g] = scores(tile, kb + 1, g)
            consume(kb, slot, g)

    def emit(tile, g):
        o_ref[tile * t:(tile + 1) * t, g * d:(g + 1) * d] = (acc_ref[g] / l_ref[g]).T.astype(BF16)

    slot = 0
    for g in range(n_heads):
        reset(g)
        s_ref[slot, g] = scores(0, 0, g)
    for r in range(n_tiles):
        def pairs(pair, carry, r=r, slot=slot):
            advance(r, 2 * pair, slot)
            advance(r, 2 * pair + 1, 1 - slot)
            return carry

        lax.fori_loop(0, (n_tiles // 2) * j + r // 2, pairs, 0)
        diagonal = n_tiles * j + r
        if r % 2 == 1:
            advance(r, diagonal - 1, slot)
            slot = 1 - slot
        for g in range(n_heads):
            if r + 1 < n_tiles:
                s_ref[1 - slot, g] = scores(r + 1, 0, g)
            consume_diagonal(diagonal, slot, g)
            emit(r, g)
            if r + 1 < n_tiles:
                reset(g)
        slot = 1 - slot


def _fox(proj, qa, ka, q_col, k_col, v_col, t=512, tiles_per_step=2, heads_per_step=4):
    b, s, _ = proj.shape
    d = FOX_HEAD_DIM
    w = heads_per_step * d
    rows = tiles_per_step * t
    tiles = lambda col0: pl.BlockSpec((None, rows, w), lambda bi, h, j: (bi, j, col0 // w + h))
    seq = lambda col0: pl.BlockSpec((None, s, w), lambda bi, h, j: (bi, 0, col0 // w + h))
    return pl.pallas_call(
        functools.partial(_fox_kernel, t=t),
        grid=(b, FOX_HEADS // heads_per_step, s // rows),
        in_specs=[tiles(q_col), pl.BlockSpec((None, rows, LANES), lambda bi, h, j: (bi, j, 0)),
                  seq(k_col), seq(0), seq(v_col)],
        out_specs=tiles(0),
        out_shape=jax.ShapeDtypeStruct((b, s, FOX_HEADS * d), BF16),
        scratch_shapes=[pltpu.VMEM((heads_per_step, s // t, d + BF16_ROWS, t), BF16),
                        pltpu.VMEM((2, heads_per_step, t, t), F32),
                        pltpu.VMEM((heads_per_step, 1, t), F32),
                        pltpu.VMEM((heads_per_step, 1, t), F32),
                        pltpu.VMEM((heads_per_step, d, t), F32)],
        compiler_params=_params(3),
        name="fox_attention",
    )(proj, qa, proj, ka, proj)


def _memkv_kernel(mem_ref, g_ref, w_ref, kg_ref, o_ref):
    xf = mem_ref[...]
    ms = jnp.mean(xf * xf, axis=-1, keepdims=True)
    y = (xf * lax.rsqrt(ms + EPS) * g_ref[...]).astype(BF16)
    acc = jnp.dot(y, w_ref[...].astype(BF16), preferred_element_type=F32)

    @pl.when(pl.program_id(0) == 0)
    def _():
        _store_group_rms(o_ref, acc, kg_ref[...], MEM_HEAD_DIM)

    @pl.when(pl.program_id(0) == 1)
    def _():
        o_ref[...] = acc.astype(BF16)


def _memkv(mem2d, g, w, kg):
    m, d = mem2d.shape
    width = MEM_HEADS * MEM_HEAD_DIM
    return pl.pallas_call(
        _memkv_kernel,
        grid=(2,),
        in_specs=[
            pl.BlockSpec((m, d), lambda j: (0, 0)),
            pl.BlockSpec((1, d), lambda j: (0, 0)),
            pl.BlockSpec((d, width), lambda j: (0, j)),
            pl.BlockSpec((1, width), lambda j: (0, 0)),
        ],
        out_specs=pl.BlockSpec((m, width), lambda j: (0, j)),
        out_shape=jax.ShapeDtypeStruct((m, 2 * width), BF16),
        compiler_params=_params(1),
        name="mem_kv",
    )(mem2d, g, w, kg)


def _memattn_kernel(q_ref, kv_ref, o_ref):
    width = MEM_HEADS * MEM_HEAD_DIM
    for h in range(MEM_HEADS):
        lo, hi = h * MEM_HEAD_DIM, (h + 1) * MEM_HEAD_DIM
        q = q_ref[:, lo:hi]
        k = kv_ref[:, lo:hi]
        v = kv_ref[:, width + lo:width + hi]
        s = lax.dot_general(q, k, NT_DIMS, preferred_element_type=F32)
        p = jnp.exp(s - jnp.max(s, axis=-1, keepdims=True))
        l = jnp.sum(p, axis=-1, keepdims=True)
        o = jnp.dot(p.astype(BF16), v, preferred_element_type=F32) / l
        o_ref[:, lo:hi] = o.astype(BF16)


def _memattn(proj, q_col, kv, tm=1024):
    b, s, _ = proj.shape
    width = MEM_HEADS * MEM_HEAD_DIM
    n_mem = kv.shape[1]
    return pl.pallas_call(
        _memattn_kernel,
        grid=(b, s // tm),
        in_specs=[
            pl.BlockSpec((None, tm, width), lambda i, j: (i, j, q_col // width)),
            pl.BlockSpec((None, n_mem, 2 * width), lambda i, j: (i, 0, 0)),
        ],
        out_specs=pl.BlockSpec((None, tm, width), lambda i, j: (i, j, 0)),
        out_shape=jax.ShapeDtypeStruct((b, s, width), BF16),
        compiler_params=_params(2),
        name="mem_attention",
    )(proj, kv)


def _merge_kernel(conv_ref, cw_ref, b_ref, c_ref, wa_ref, wb_ref, wc_ref, g_ref,
                  o_ref, wabf_ref, wbbf_ref, wcbf_ref, carry_ref, *, tiles_per_seq):
    i = pl.program_id(1)
    tn = o_ref.shape[1]

    @pl.when(i == 0)
    def _():
        wabf_ref[...] = wa_ref[...].astype(BF16)
        wbbf_ref[...] = wb_ref[...].astype(BF16)
        wcbf_ref[...] = wc_ref[...].astype(BF16)

    @pl.when(i % tiles_per_seq == 0)
    def _():
        carry_ref[...] = jnp.zeros_like(carry_ref)

    y_conv = _gated_conv_tile(conv_ref, cw_ref, carry_ref).astype(BF16)
    oa = jnp.dot(y_conv, wabf_ref[...], preferred_element_type=F32)
    ob = jnp.dot(b_ref[...], wbbf_ref[...], preferred_element_type=F32)
    oc = jnp.dot(c_ref[...], wcbf_ref[...], preferred_element_type=F32)
    gate = lambda br: g_ref[:, br * tn:(br + 1) * tn].astype(F32)
    o_ref[...] = (gate(0) * oa + gate(1) * ob + gate(2) * oc).astype(BF16)


def _merge(proj, conv_col, conv_w, yb, yc, wa, wb, wc, seq_len, d_model, tm=512, tn=COL_TILE):
    m, k = yb.shape
    lhs = pl.BlockSpec((tm, k), lambda j, i: (i, 0))
    wsp = pl.BlockSpec((k, tn), lambda j, i: (0, j))
    return pl.pallas_call(
        functools.partial(_merge_kernel, tiles_per_seq=seq_len // tm),
        grid=(d_model // tn, m // tm),
        in_specs=[pl.BlockSpec((tm, 3 * k), lambda j, i: (i, conv_col // (3 * k))),
                  pl.BlockSpec((CONV_TAPS, k), lambda j, i: (0, 0)),
                  lhs, lhs, wsp, wsp, wsp,
                  pl.BlockSpec((tm, 3 * tn), lambda j, i: (i, j))],
        out_specs=pl.BlockSpec((tm, tn), lambda j, i: (i, j)),
        out_shape=jax.ShapeDtypeStruct((m, d_model), BF16),
        scratch_shapes=[pltpu.VMEM((k, tn), BF16)] * 3 + [pltpu.VMEM((SUBLANES, k), F32)],
        compiler_params=_params(2),
        name="branch_merge",
    )(proj, conv_w, yb, yc, wa, wb, wc, proj)


def _outproj_kernel(a_ref, w_ref, x_ref, g_ref, x2_ref, h2_ref, wbf_ref):
    @pl.when(pl.program_id(0) == 0)
    def _():
        wbf_ref[...] = w_ref[...].astype(BF16)

    x2 = x_ref[...] + jnp.dot(a_ref[...], wbf_ref[...], preferred_element_type=F32)
    x2_ref[...] = x2
    ms = jnp.mean(x2 * x2, axis=-1, keepdims=True)
    h2_ref[...] = (x2 * lax.rsqrt(ms + EPS) * g_ref[...]).astype(BF16)


def _outproj(a, w, x2d, g, tm=512):
    m, d = x2d.shape
    row = pl.BlockSpec((tm, d), lambda i: (i, 0))
    return pl.pallas_call(
        _outproj_kernel,
        grid=(m // tm,),
        in_specs=[row, pl.BlockSpec((d, d), lambda i: (0, 0), pipeline_mode=pl.Buffered(1)), row,
                  pl.BlockSpec((1, d), lambda i: (0, 0))],
        out_specs=[row, row],
        out_shape=[jax.ShapeDtypeStruct((m, d), F32), jax.ShapeDtypeStruct((m, d), BF16)],
        scratch_shapes=[pltpu.VMEM((d, d), BF16)],
        compiler_params=_params(1),
        name="out_proj_norm2",
    )(a, w, x2d, g)


def _mlp_kernel(h_ref, wu_ref, wd_ref, x_ref, o_ref):
    f = pl.program_id(1)
    rows = x_ref.shape[0]

    @pl.when(f == 0)
    def _():
        o_ref[...] = jnp.zeros_like(o_ref)

    a = jnp.dot(h_ref[...], wu_ref[...].astype(BF16), preferred_element_type=F32)
    a = jnp.square(jnp.maximum(a, 0.0)).astype(BF16)
    o_ref[...] += jnp.dot(a, wd_ref[...].astype(BF16), preferred_element_type=F32)
    slab = pl.ds(pl.multiple_of(f * rows, rows), rows)
    o_ref[slab, :] += x_ref[...]


def _mlp(h2, w_up, w_down, x2, tm=1024, tf=512):
    m, d = h2.shape
    d_ff = w_up.shape[1]
    nf = d_ff // tf
    rows = tm // nf
    return pl.pallas_call(
        _mlp_kernel,
        grid=(m // tm, nf),
        in_specs=[
            pl.BlockSpec((tm, d), lambda i, f: (i, 0)),
            pl.BlockSpec((d, tf), lambda i, f: (0, f)),
            pl.BlockSpec((tf, d), lambda i, f: (f, 0)),
            pl.BlockSpec((rows, d), lambda i, f: (i * nf + f, 0)),
        ],
        out_specs=pl.BlockSpec((tm, d), lambda i, f: (i, 0)),
        out_shape=jax.ShapeDtypeStruct((m, d), F32),
        compiler_params=_params(2),
        name="relu2_mlp",
    )(h2, w_up, w_down, x2)


def kernel(x, mem, norm1_g, w_in, b_f, conv_w, fox_q_g, fox_k_g, mem_norm_g, w_mem_kv, mem_q_g, mem_k_g, w_conv_out, w_fox_out, w_mem_out, w_out, norm2_g, w_up, w_down):
    b, s, d = x.shape
    m = b * s
    conv_width = conv_w.shape[1]
    fox_width = FOX_HEADS * FOX_HEAD_DIM
    mem_width = MEM_HEADS * MEM_HEAD_DIM
    tile = COL_TILE
    q_row = 3 * conv_width
    k_row = q_row + fox_width
    v_row = k_row + fox_width
    f_row = v_row + fox_width
    mq_row = f_row + FOX_HEADS
    gate_row = mq_row + mem_width
    n_gate = N_BRANCHES * d
    ch_tiles = d // tile
    gate_rows = [gate_row + br * d + n * tile for n in range(ch_tiles) for br in range(N_BRANCHES)]
    rest_rows = [r for r in range(0, f_row, tile)] + [r for r in range(mq_row, gate_row, tile)]
    row_starts = tuple(gate_rows + rest_rows)
    conv_col, q_col, k_col, v_col, mq_col = (n_gate + r for r in (0, q_row, k_row, v_row, f_row))
    kinds = ([GATE] * len(gate_rows) + [PLAIN] * (q_row // tile) + [FOX_HEAD_DIM] * (2 * fox_width // tile)
             + [PLAIN] * (fox_width // tile) + [MEM_HEAD_DIM] * (mem_width // tile))
    gains = jnp.ones((len(row_starts) * tile,), F32)
    gains = gains.at[q_col:k_col].set(jnp.tile(fox_q_g * (LOG2_E / math.sqrt(FOX_HEAD_DIM)), FOX_HEADS))
    gains = gains.at[k_col:v_col].set(jnp.tile(fox_k_g, FOX_HEADS))
    gains = gains.at[mq_col:].set(jnp.tile(mem_q_g * (1.0 / math.sqrt(MEM_HEAD_DIM)), MEM_HEADS))

    row = lambda v: v.reshape(1, -1)
    bf = jnp.pad(b_f, (0, LANES - FOX_HEADS)).reshape(1, LANES)
    w_t = w_in.T

    h1, qa, ka = _norm1(x, row(norm1_g), w_t, f_row, bf)
    proj = _proj(h1.reshape(m, d), w_t, row_starts, kinds, gains.reshape(-1, tile))
    proj3 = proj.reshape(b, s, -1)

    y_fox = _fox(proj3, qa, ka, q_col, k_col, v_col)
    kv = _memkv(mem.reshape(-1, d), row(mem_norm_g), w_mem_kv, row(jnp.tile(mem_k_g, MEM_HEADS)))
    y_mem = _memattn(proj3, mq_col, kv.reshape(b, -1, 2 * mem_width))

    merged = _merge(proj, conv_col, conv_w, y_fox.reshape(m, -1), y_mem.reshape(m, -1),
                    w_conv_out, w_fox_out, w_mem_out, s, d)
    x2, h2 = _outproj(merged, w_out, x.reshape(m, d), row(norm2_g))
    out = _mlp(h2, w_up, w_down, x2)
    return out.reshape(b, s, d)
```

```python
import functools
import math

import jax
import jax.numpy as jnp
import numpy as np
from jax import lax
from jax.experimental import pallas as pl
from jax.experimental.pallas import tpu as pltpu

F32 = jnp.float32
BF16 = jnp.bfloat16

EPS = 1e-6
LOG2_E = math.log2(math.e)
LANES = 128
SUBLANES = 8
BF16_ROWS = 2 * SUBLANES
CONV_TAPS = 3
FOX_HEADS = 8
FOX_HEAD_DIM = 128
MEM_HEADS = 4
MEM_HEAD_DIM = 256
N_BRANCHES = 3

VMEM_LIMIT_BYTES = 56 * 1024 * 1024
COL_TILE = 1024
NT_DIMS = (((1,), (1,)), ((), ()))


def _params(n_axes):
    return pltpu.CompilerParams(
        dimension_semantics=("arbitrary",) * n_axes,
        vmem_limit_bytes=VMEM_LIMIT_BYTES)


def _store_group_rms(o_ref, a, gain, width):
    for s in range(0, a.shape[1], width):
        blk = a[:, s:s + width]
        ms = jnp.mean(blk * blk, axis=-1, keepdims=True)
        o_ref[:, s:s + width] = (blk * lax.rsqrt(ms + EPS) * gain[:, s:s + width]).astype(o_ref.dtype)


def _split3(c):
    hi = c.astype(BF16).astype(F32)
    r1 = c - hi
    mid = r1.astype(BF16).astype(F32)
    lo = (r1 - mid).astype(BF16).astype(F32)
    return hi, mid, lo


ONE_LANE = 3 * FOX_HEADS


def _decay_selector():
    sel = np.zeros((LANES, FOX_HEADS, LANES), np.float32)
    for hd in range(FOX_HEADS):
        for piece in range(3):
            sel[ONE_LANE, hd, piece * FOX_HEADS + hd] = 1.0
            sel[piece * FOX_HEADS + hd, hd, ONE_LANE + piece] = -1.0
    return jnp.asarray(sel.reshape(LANES, -1), BF16)


def _norm1_kernel(x_ref, g_ref, wf_ref, bf_ref, sel_ref, h_ref, qa_ref, ka_ref, carry_ref):
    tm = x_ref.shape[0]

    @pl.when(pl.program_id(1) == 0)
    def _():
        carry_ref[...] = jnp.zeros_like(carry_ref)

    xf = x_ref[...]
    ms = jnp.mean(xf * xf, axis=-1, keepdims=True)
    y = xf * lax.rsqrt(ms + EPS) * g_ref[...]
    y_hi = y.astype(BF16)
    h_ref[...] = y_hi

    wf = wf_ref[...]
    wf_hi = wf.astype(BF16).astype(F32)
    w2 = jnp.concatenate(
        [wf_hi, wf - wf_hi, jnp.zeros((LANES - 2 * FOX_HEADS, wf.shape[1]), F32)], axis=0).astype(BF16)
    z2 = lax.dot_general(y_hi, w2, NT_DIMS, preferred_element_type=F32)
    z = z2 + pltpu.roll(z2, LANES - FOX_HEADS, 1) + bf_ref[...]
    log_f = -LOG2_E * (jnp.maximum(-z, 0.0) + jnp.log1p(jnp.exp(-jnp.abs(z))))

    row = lax.broadcasted_iota(jnp.int32, (tm, tm), 0)
    col = lax.broadcasted_iota(jnp.int32, (tm, tm), 1)
    tri = jnp.where(row >= col, 1.0, 0.0).astype(BF16)
    f_hi, f_mid, f_lo = _split3(log_f)
    part = jnp.dot(tri, jnp.concatenate([f_hi, f_mid], axis=1).astype(BF16),
                   preferred_element_type=F32)
    c = (part[:, :LANES] + part[:, LANES:]
         + jnp.dot(tri, f_lo.astype(BF16), preferred_element_type=F32) + carry_ref[0:1, :])
    carry_ref[0:1, :] = c[tm - 1:tm, :]

    lane = lax.broadcasted_iota(jnp.int32, (tm, LANES), 1)
    c_hi, c_mid, c_lo = _split3(c)
    packed = jnp.where(
        lane < FOX_HEADS, c_hi,
        jnp.where(lane < 2 * FOX_HEADS, pltpu.roll(c_mid, FOX_HEADS, 1),
                  jnp.where(lane < ONE_LANE, pltpu.roll(c_lo, 2 * FOX_HEADS, 1),
                            jnp.where(lane < ONE_LANE + 3, 1.0, 0.0)))).astype(BF16)
    qa_ref[...] = packed
    ka_ref[...] = jnp.dot(packed, sel_ref[...], preferred_element_type=F32).astype(BF16)


def _norm1(x, g, w_t, f_row, bf, tm=512):
    b, s, d = x.shape
    return pl.pallas_call(
        _norm1_kernel,
        grid=(b, s // tm),
        in_specs=[
            pl.BlockSpec((None, tm, d), lambda i, j: (i, j, 0)),
            pl.BlockSpec((1, d), lambda i, j: (0, 0)),
            pl.BlockSpec((FOX_HEADS, d), lambda i, j: (f_row // FOX_HEADS, 0)),
            pl.BlockSpec((1, LANES), lambda i, j: (0, 0)),
            pl.BlockSpec((LANES, FOX_HEADS * LANES), lambda i, j: (0, 0)),
        ],
        out_specs=[
            pl.BlockSpec((None, tm, d), lambda i, j: (i, j, 0)),
            pl.BlockSpec((None, tm, LANES), lambda i, j: (i, j, 0)),
            pl.BlockSpec((None, tm, FOX_HEADS * LANES), lambda i, j: (i, j, 0)),
        ],
        out_shape=[
            jax.ShapeDtypeStruct((b, s, d), BF16),
            jax.ShapeDtypeStruct((b, s, LANES), BF16),
            jax.ShapeDtypeStruct((b, s, FOX_HEADS * LANES), BF16),
        ],
        scratch_shapes=[pltpu.VMEM((SUBLANES, LANES), F32)],
        compiler_params=_params(2),
        name="norm1_decay",
    )(x, g, w_t, bf, _decay_selector())


PLAIN, GATE = 0, -1


def _proj_kernel(h_ref, w_ref, gain_ref, o_ref, wbf_ref, *, kinds):
    @pl.when(pl.program_id(1) == 0)
    def _():
        wbf_ref[...] = w_ref[...].astype(BF16)

    def tile(kind):
        acc = lax.dot_general(h_ref[...], wbf_ref[...], NT_DIMS, preferred_element_type=F32)
        if kind == PLAIN:
            o_ref[...] = acc.astype(BF16)
        elif kind == GATE:
            o_ref[...] = (0.5 * jnp.tanh(0.5 * acc) + 0.5).astype(BF16)
        else:
            _store_group_rms(o_ref, acc, gain_ref[...], kind)

    j = pl.program_id(0)
    for kind in sorted(set(kinds)):
        hit = functools.reduce(jnp.logical_or, [j == t for t, kd in enumerate(kinds) if kd == kind])
        pl.when(hit)(functools.partial(tile, kind))


def _proj(h, w_t, row_starts, kinds, gains, tm=1024, tn=COL_TILE):
    m, k = h.shape
    n_tiles = len(row_starts)

    def w_rows(j, i):
        start = jnp.int32(row_starts[0])
        for t in range(1, n_tiles):
            start = jnp.where(j == t, row_starts[t], start)
        return (pl.multiple_of(start, 8), 0)

    return pl.pallas_call(
        functools.partial(_proj_kernel, kinds=tuple(kinds)),
        grid=(n_tiles, m // tm),
        in_specs=[
            pl.BlockSpec((tm, k), lambda j, i: (i, 0)),
            pl.BlockSpec((pl.Element(tn), pl.Element(k)), w_rows),
            pl.BlockSpec((None, 1, tn), lambda j, i: (j, 0, 0)),
        ],
        out_specs=pl.BlockSpec((tm, tn), lambda j, i: (i, j)),
        out_shape=jax.ShapeDtypeStruct((m, n_tiles * tn), BF16),
        scratch_shapes=[pltpu.VMEM((tn, k), BF16)],
        compiler_params=_params(2),
        name="in_proj",
    )(h, w_t, gains.reshape(n_tiles, 1, tn))


def _gated_conv_tile(conv_ref, w_ref, carry_ref):
    tm = conv_ref.shape[0]
    k = w_ref.shape[1]
    u = conv_ref[:, k:2 * k].astype(F32) * conv_ref[:, 2 * k:3 * k].astype(F32)
    w = w_ref[...]
    taps = lambda u2, u1, u0: w[0:1, :] * u2 + w[1:2, :] * u1 + w[2:3, :] * u0
    conv = taps(pltpu.roll(u, 2, 0), pltpu.roll(u, 1, 0), u)
    top = u[0:SUBLANES, :]
    row = lax.broadcasted_iota(jnp.int32, (SUBLANES, 1), 0)
    prev1 = carry_ref[1:2, :]
    prev2 = carry_ref[0:1, :]
    top1 = jnp.where(row == 0, prev1, pltpu.roll(top, 1, 0))
    top2 = jnp.where(row == 0, prev2, jnp.where(row == 1, prev1, pltpu.roll(top, 2, 0)))
    conv = jnp.concatenate([taps(top2, top1, top), conv[SUBLANES:, :]], axis=0)
    carry_ref[0:2, :] = u[tm - 2:tm, :]
    return conv_ref[:, 0:k].astype(F32) * conv


def _fox_kernel(q_ref, qa_ref, k_ref, ka_ref, v_ref, o_ref, vt_ref, s_ref, m_ref, l_ref, acc_ref, *,
                t):
    n_tiles = q_ref.shape[0] // t
    assert n_tiles % 2 == 0
    d = FOX_HEAD_DIM
    n_heads = q_ref.shape[1] // d
    j = pl.program_id(2)

    @pl.when(j == 0)
    def _():
        for g in range(n_heads):
            for blk in range(v_ref.shape[0] // t):
                v = v_ref[blk * t:(blk + 1) * t, g * d:(g + 1) * d]
                vt_ref[g, blk, :d] = v.astype(F32).T.astype(BF16)
                vt_ref[g, blk, d:] = jnp.ones((vt_ref.shape[2] - d, t), BF16)

    def scores(tile, kb, g):
        q = jnp.concatenate([q_ref[tile * t:(tile + 1) * t, g * d:(g + 1) * d],
                             qa_ref[tile * t:(tile + 1) * t, :]], axis=1)
        rows = pl.ds(pl.multiple_of(kb * t, t), t)
        k = jnp.concatenate([k_ref[rows, g * d:(g + 1) * d], ka_ref[rows, g * d:(g + 1) * d]], axis=1)
        return lax.dot_general(k, q, NT_DIMS, preferred_element_type=F32)

    def reset(g):
        m_ref[g] = jnp.full(m_ref.shape[1:], -jnp.inf, F32)
        l_ref[g] = jnp.zeros(l_ref.shape[1:], F32)
        acc_ref[g] = jnp.zeros(acc_ref.shape[1:], F32)

    def update(g, m_new, alpha, pv):
        l_ref[g] = alpha * l_ref[g] + pv[d:d + 1]
        acc_ref[g] = alpha * acc_ref[g] + pv[:d]
        m_ref[g] = m_new

    def consume(kb, slot, g):
        s = s_ref[slot, g]
        m_prev = m_ref[g]
        m_new = jnp.maximum(m_prev, jnp.max(s, axis=0, keepdims=True))
        p = jnp.exp2(s - m_new)
        pv = jnp.dot(vt_ref[g, kb], p.astype(BF16), preferred_element_type=F32)
        update(g, m_new, jnp.exp2(m_prev - m_new), pv)

    def causal(s):
        kpos = lax.broadcasted_iota(jnp.int32, s.shape, 0)
        qpos = lax.broadcasted_iota(jnp.int32, s.shape, 1)
        return jnp.where(qpos >= kpos, s, -jnp.inf)

    def consume_diagonal(kb, slot, g):
        h = t // 2
        s_up = causal(s_ref[slot, g, :h, :])
        s_lo = causal(s_ref[slot, g, h:, h:])
        m_up = jnp.max(s_up, axis=0, keepdims=True)
        m_blk = jnp.concatenate(
            [m_up[:, :h], jnp.maximum(m_up[:, h:], jnp.max(s_lo, axis=0, keepdims=True))], axis=1)
        m_prev = m_ref[g]
        m_new = jnp.maximum(m_prev, m_blk)
        p_up = jnp.exp2(s_up - m_new).astype(BF16)
        p_lo = jnp.exp2(s_lo - m_new[:, h:]).astype(BF16)
        vt = vt_ref[g, kb]
        pv_up = jnp.dot(vt[:, :h], p_up, preferred_element_type=F32)
        pv_lo = jnp.dot(vt[:, h:], p_lo, preferred_element_type=F32)
        pv = jnp.concatenate([pv_up[:, :h], pv_up[:, h:] + pv_lo], axis=1)
        update(g, m_new, jnp.exp2(m_prev - m_new), pv)

    def advance(tile, kb, slot):
        for g in range(n_heads):
            s_ref[1 - slot, g] = scores(tile, kb + 1, g)
            consume(kb, slot, g)

    def emit(tile, g):
        o_ref[tile * t:(tile + 1) * t, g * d:(g + 1) * d] = (acc_ref[g] / l_ref[g]).T.astype(BF16)

    slot = 0
    for g in range(n_heads):
        reset(g)
        s_ref[slot, g] = scores(0, 0, g)
    for r in range(n_tiles):
        def pairs(pair, carry, r=r, slot=slot):
            advance(r, 2 * pair, slot)
            advance(r, 2 * pair + 1, 1 - slot)
            return carry

        lax.fori_loop(0, (n_tiles // 2) * j + r // 2, pairs, 0)
        diagonal = n_tiles * j + r
        if r % 2 == 1:
            advance(r, diagonal - 1, slot)
            slot = 1 - slot
        for g in range(n_heads):
            if r + 1 < n_tiles:
                s_ref[1 - slot, g] = scores(r + 1, 0, g)
            consume_diagonal(diagonal, slot, g)
            emit(r, g)
            if r + 1 < n_tiles:
                reset(g)
        slot = 1 - slot


def _fox(proj, qa, ka, q_col, k_col, v_col, t=512, tiles_per_step=2, heads_per_step=2):
    b, s, _ = proj.shape
    d = FOX_HEAD_DIM
    w = heads_per_step * d
    rows = tiles_per_step * t
    tiles = lambda col0: pl.BlockSpec((None, rows, w), lambda bi, h, j: (bi, j, col0 // w + h))
    seq = lambda col0: pl.BlockSpec((None, s, w), lambda bi, h, j: (bi, 0, col0 // w + h))
    return pl.pallas_call(
        functools.partial(_fox_kernel, t=t),
        grid=(b, FOX_HEADS // heads_per_step, s // rows),
        in_specs=[tiles(q_col), pl.BlockSpec((None, rows, LANES), lambda bi, h, j: (bi, j, 0)),
                  seq(k_col), seq(0), seq(v_col)],
        out_specs=tiles(0),
        out_shape=jax.ShapeDtypeStruct((b, s, FOX_HEADS * d), BF16),
        scratch_shapes=[pltpu.VMEM((heads_per_step, s // t, d + BF16_ROWS, t), BF16),
                        pltpu.VMEM((2, heads_per_step, t, t), F32),
                        pltpu.VMEM((heads_per_step, 1, t), F32),
                        pltpu.VMEM((heads_per_step, 1, t), F32),
                        pltpu.VMEM((heads_per_step, d, t), F32)],
        compiler_params=_params(3),
        name="fox_attention",
    )(proj, qa, proj, ka, proj)


def _memkv_kernel(mem_ref, g_ref, w_ref, kg_ref, o_ref):
    xf = mem_ref[...]
    ms = jnp.mean(xf * xf, axis=-1, keepdims=True)
    y = (xf * lax.rsqrt(ms + EPS) * g_ref[...]).astype(BF16)
    acc = jnp.dot(y, w_ref[...].astype(BF16), preferred_element_type=F32)

    @pl.when(pl.program_id(0) == 0)
    def _():
        _store_group_rms(o_ref, acc, kg_ref[...], MEM_HEAD_DIM)

    @pl.when(pl.program_id(0) == 1)
    def _():
        o_ref[...] = acc.astype(BF16)


def _memkv(mem2d, g, w, kg):
    m, d = mem2d.shape
    width = MEM_HEADS * MEM_HEAD_DIM
    return pl.pallas_call(
        _memkv_kernel,
        grid=(2,),
        in_specs=[
            pl.BlockSpec((m, d), lambda j: (0, 0)),
            pl.BlockSpec((1, d), lambda j: (0, 0)),
            pl.BlockSpec((d, width), lambda j: (0, j)),
            pl.BlockSpec((1, width), lambda j: (0, 0)),
        ],
        out_specs=pl.BlockSpec((m, width), lambda j: (0, j)),
        out_shape=jax.ShapeDtypeStruct((m, 2 * width), BF16),
        compiler_params=_params(1),
        name="mem_kv",
    )(mem2d, g, w, kg)


def _memattn_kernel(q_ref, kv_ref, o_ref):
    width = MEM_HEADS * MEM_HEAD_DIM
    for h in range(MEM_HEADS):
        lo, hi = h * MEM_HEAD_DIM, (h + 1) * MEM_HEAD_DIM
        q = q_ref[:, lo:hi]
        k = kv_ref[:, lo:hi]
        v = kv_ref[:, width + lo:width + hi]
        s = lax.dot_general(q, k, NT_DIMS, preferred_element_type=F32)
        p = jnp.exp(s - jnp.max(s, axis=-1, keepdims=True))
        l = jnp.sum(p, axis=-1, keepdims=True)
        o = jnp.dot(p.astype(BF16), v, preferred_element_type=F32) / l
        o_ref[:, lo:hi] = o.astype(BF16)


def _memattn(proj, q_col, kv, tm=1024):
    b, s, _ = proj.shape
    width = MEM_HEADS * MEM_HEAD_DIM
    n_mem = kv.shape[1]
    return pl.pallas_call(
        _memattn_kernel,
        grid=(b, s // tm),
        in_specs=[
            pl.BlockSpec((None, tm, width), lambda i, j: (i, j, q_col // width)),
            pl.BlockSpec((None, n_mem, 2 * width), lambda i, j: (i, 0, 0)),
        ],
        out_specs=pl.BlockSpec((None, tm, width), lambda i, j: (i, j, 0)),
        out_shape=jax.ShapeDtypeStruct((b, s, width), BF16),
        compiler_params=_params(2),
        name="mem_attention",
    )(proj, kv)


def _merge_kernel(conv_ref, cw_ref, b_ref, c_ref, wa_ref, wb_ref, wc_ref, g_ref,
                  o_ref, wabf_ref, wbbf_ref, wcbf_ref, carry_ref, *, tiles_per_seq):
    i = pl.program_id(1)
    tn = o_ref.shape[1]

    @pl.when(i == 0)
    def _():
        wabf_ref[...] = wa_ref[...].astype(BF16)
        wbbf_ref[...] = wb_ref[...].astype(BF16)
        wcbf_ref[...] = wc_ref[...].astype(BF16)

    @pl.when(i % tiles_per_seq == 0)
    def _():
        carry_ref[...] = jnp.zeros_like(carry_ref)

    y_conv = _gated_conv_tile(conv_ref, cw_ref, carry_ref).astype(BF16)
    oa = jnp.dot(y_conv, wabf_ref[...], preferred_element_type=F32)
    ob = jnp.dot(b_ref[...], wbbf_ref[...], preferred_element_type=F32)
    oc = jnp.dot(c_ref[...], wcbf_ref[...], preferred_element_type=F32)
    gate = lambda br: g_ref[:, br * tn:(br + 1) * tn].astype(F32)
    o_ref[...] = (gate(0) * oa + gate(1) * ob + gate(2) * oc).astype(BF16)


def _merge(proj, conv_col, conv_w, yb, yc, wa, wb, wc, seq_len, d_model, tm=512, tn=COL_TILE):
    m, k = yb.shape
    lhs = pl.BlockSpec((tm, k), lambda j, i: (i, 0))
    wsp = pl.BlockSpec((k, tn), lambda j, i: (0, j))
    return pl.pallas_call(
        functools.partial(_merge_kernel, tiles_per_seq=seq_len // tm),
        grid=(d_model // tn, m // tm),
        in_specs=[pl.BlockSpec((tm, 3 * k), lambda j, i: (i, conv_col // (3 * k))),
                  pl.BlockSpec((CONV_TAPS, k), lambda j, i: (0, 0)),
                  lhs, lhs, wsp, wsp, wsp,
                  pl.BlockSpec((tm, 3 * tn), lambda j, i: (i, j))],
        out_specs=pl.BlockSpec((tm, tn), lambda j, i: (i, j)),
        out_shape=jax.ShapeDtypeStruct((m, d_model), BF16),
        scratch_shapes=[pltpu.VMEM((k, tn), BF16)] * 3 + [pltpu.VMEM((SUBLANES, k), F32)],
        compiler_params=_params(2),
        name="branch_merge",
    )(proj, conv_w, yb, yc, wa, wb, wc, proj)


def _outproj_kernel(a_ref, w_ref, x_ref, g_ref, x2_ref, h2_ref, wbf_ref):
    @pl.when(pl.program_id(0) == 0)
    def _():
        wbf_ref[...] = w_ref[...].astype(BF16)

    x2 = x_ref[...] + jnp.dot(a_ref[...], wbf_ref[...], preferred_element_type=F32)
    x2_ref[...] = x2
    ms = jnp.mean(x2 * x2, axis=-1, keepdims=True)
    h2_ref[...] = (x2 * lax.rsqrt(ms + EPS) * g_ref[...]).astype(BF16)


def _outproj(a, w, x2d, g, tm=512):
    m, d = x2d.shape
    row = pl.BlockSpec((tm, d), lambda i: (i, 0))
    return pl.pallas_call(
        _outproj_kernel,
        grid=(m // tm,),
        in_specs=[row, pl.BlockSpec((d, d), lambda i: (0, 0), pipeline_mode=pl.Buffered(1)), row,
                  pl.BlockSpec((1, d), lambda i: (0, 0))],
        out_specs=[row, row],
        out_shape=[jax.ShapeDtypeStruct((m, d), F32), jax.ShapeDtypeStruct((m, d), BF16)],
        scratch_shapes=[pltpu.VMEM((d, d), BF16)],
        compiler_params=_params(1),
        name="out_proj_norm2",
    )(a, w, x2d, g)


def _mlp_kernel(h_ref, wu_ref, wd_ref, x_ref, o_ref):
    f = pl.program_id(1)
    rows = x_ref.shape[0]

    @pl.when(f == 0)
    def _():
        o_ref[...] = jnp.zeros_like(o_ref)

    a = jnp.dot(h_ref[...], wu_ref[...].astype(BF16), preferred_element_type=F32)
    a = jnp.square(jnp.maximum(a, 0.0)).astype(BF16)
    o_ref[...] += jnp.dot(a, wd_ref[...].astype(BF16), preferred_element_type=F32)
    slab = pl.ds(pl.multiple_of(f * rows, rows), rows)
    o_ref[slab, :] += x_ref[...]


def _mlp(h2, w_up, w_down, x2, tm=1024, tf=512):
    m, d = h2.shape
    d_ff = w_up.shape[1]
    nf = d_ff // tf
    rows = tm // nf
    return pl.pallas_call(
        _mlp_kernel,
        grid=(m // tm, nf),
        in_specs=[
            pl.BlockSpec((tm, d), lambda i, f: (i, 0)),
            pl.BlockSpec((d, tf), lambda i, f: (0, f)),
            pl.BlockSpec((tf, d), lambda i, f: (f, 0)),
            pl.BlockSpec((rows, d), lambda i, f: (i * nf + f, 0)),
        ],
        out_specs=pl.BlockSpec((tm, d), lambda i, f: (i, 0)),
        out_shape=jax.ShapeDtypeStruct((m, d), F32),
        compiler_params=_params(2),
        name="relu2_mlp",
    )(h2, w_up, w_down, x2)


def kernel(x, mem, norm1_g, w_in, b_f, conv_w, fox_q_g, fox_k_g, mem_norm_g, w_mem_kv, mem_q_g, mem_k_g, w_conv_out, w_fox_out, w_mem_out, w_out, norm2_g, w_up, w_down):
    b, s, d = x.shape
    m = b * s
    conv_width = conv_w.shape[1]
    fox_width = FOX_HEADS * FOX_HEAD_DIM
    mem_width = MEM_HEADS * MEM_HEAD_DIM
    tile = COL_TILE
    q_row = 3 * conv_width
    k_row = q_row + fox_width
    v_row = k_row + fox_width
    f_row = v_row + fox_width
    mq_row = f_row + FOX_HEADS
    gate_row = mq_row + mem_width
    n_gate = N_BRANCHES * d
    ch_tiles = d // tile
    gate_rows = [gate_row + br * d + n * tile for n in range(ch_tiles) for br in range(N_BRANCHES)]
    rest_rows = [r for r in range(0, f_row, tile)] + [r for r in range(mq_row, gate_row, tile)]
    row_starts = tuple(gate_rows + rest_rows)
    conv_col, q_col, k_col, v_col, mq_col = (n_gate + r for r in (0, q_row, k_row, v_row, f_row))
    kinds = ([GATE] * len(gate_rows) + [PLAIN] * (q_row // tile) + [FOX_HEAD_DIM] * (2 * fox_width // tile)
             + [PLAIN] * (fox_width // tile) + [MEM_HEAD_DIM] * (mem_width // tile))
    gains = jnp.ones((len(row_starts) * tile,), F32)
    gains = gains.at[q_col:k_col].set(jnp.tile(fox_q_g * (LOG2_E / math.sqrt(FOX_HEAD_DIM)), FOX_HEADS))
    gains = gains.at[k_col:v_col].set(jnp.tile(fox_k_g, FOX_HEADS))
    gains = gains.at[mq_col:].set(jnp.tile(mem_q_g * (1.0 / math.sqrt(MEM_HEAD_DIM)), MEM_HEADS))

    row = lambda v: v.reshape(1, -1)
    bf = jnp.pad(b_f, (0, LANES - FOX_HEADS)).reshape(1, LANES)
    w_t = w_in.T

    h1, qa, ka = _norm1(x, row(norm1_g), w_t, f_row, bf)
    proj = _proj(h1.reshape(m, d), w_t, row_starts, kinds, gains.reshape(-1, tile))
    proj3 = proj.reshape(b, s, -1)

    y_fox = _fox(proj3, qa, ka, q_col, k_col, v_col)
    kv = _memkv(mem.reshape(-1, d), row(mem_norm_g), w_mem_kv, row(jnp.tile(mem_k_g, MEM_HEADS)))
    y_mem = _memattn(proj3, mq_col, kv.reshape(b, -1, 2 * mem_width))

    merged = _merge(proj, conv_col, conv_w, y_fox.reshape(m, -1), y_mem.reshape(m, -1),
                    w_conv_out, w_fox_out, w_mem_out, s, d)
    x2, h2 = _outproj(merged, w_out, x.reshape(m, d), row(norm2_g))
    out = _mlp(h2, w_up, w_down, x2)
    return out.reshape(b, s, d)
```

```python
import functools
import math

import jax
import jax.numpy as jnp
import numpy as np
from jax import lax
from jax.experimental import pallas as pl
from jax.experimental.pallas import tpu as pltpu

F32 = jnp.float32
BF16 = jnp.bfloat16

EPS = 1e-6
LOG2_E = math.log2(math.e)
LANES = 128
SUBLANES = 8
BF16_ROWS = 2 * SUBLANES
CONV_TAPS = 3
FOX_HEADS = 8
FOX_HEAD_DIM = 128
MEM_HEADS = 4
MEM_HEAD_DIM = 256
N_BRANCHES = 3

VMEM_LIMIT_BYTES = 56 * 1024 * 1024
COL_TILE = 1024
NT_DIMS = (((1,), (1,)), ((), ()))


def _params(n_axes):
    return pltpu.CompilerParams(
        dimension_semantics=("arbitrary",) * n_axes,
        vmem_limit_bytes=VMEM_LIMIT_BYTES)


def _store_group_rms(o_ref, a, gain, width):
    for s in range(0, a.shape[1], width):
        blk = a[:, s:s + width]
        ms = jnp.mean(blk * blk, axis=-1, keepdims=True)
        o_ref[:, s:s + width] = (blk * lax.rsqrt(ms + EPS) * gain[:, s:s + width]).astype(o_ref.dtype)


def _split3(c):
    hi = c.astype(BF16).astype(F32)
    r1 = c - hi
    mid = r1.astype(BF16).astype(F32)
    lo = (r1 - mid).astype(BF16).astype(F32)
    return hi, mid, lo


ONE_LANE = 3 * FOX_HEADS


def _decay_selector():
    sel = np.zeros((LANES, FOX_HEADS, LANES), np.float32)
    for hd in range(FOX_HEADS):
        for piece in range(3):
            sel[ONE_LANE, hd, piece * FOX_HEADS + hd] = 1.0
            sel[piece * FOX_HEADS + hd, hd, ONE_LANE + piece] = -1.0
    return jnp.asarray(sel.reshape(LANES, -1), BF16)


def _norm1_kernel(x_ref, g_ref, wf_ref, bf_ref, sel_ref, h_ref, qa_ref, ka_ref, carry_ref):
    tm = x_ref.shape[0]

    @pl.when(pl.program_id(1) == 0)
    def _():
        carry_ref[...] = jnp.zeros_like(carry_ref)

    xf = x_ref[...]
    ms = jnp.mean(xf * xf, axis=-1, keepdims=True)
    y = xf * lax.rsqrt(ms + EPS) * g_ref[...]
    y_hi = y.astype(BF16)
    h_ref[...] = y_hi

    wf = wf_ref[...]
    wf_hi = wf.astype(BF16).astype(F32)
    w2 = jnp.concatenate(
        [wf_hi, wf - wf_hi, jnp.zeros((LANES - 2 * FOX_HEADS, wf.shape[1]), F32)], axis=0).astype(BF16)
    z2 = lax.dot_general(y_hi, w2, NT_DIMS, preferred_element_type=F32)
    z = z2 + pltpu.roll(z2, LANES - FOX_HEADS, 1) + bf_ref[...]
    log_f = -LOG2_E * (jnp.maximum(-z, 0.0) + jnp.log1p(jnp.exp(-jnp.abs(z))))

    row = lax.broadcasted_iota(jnp.int32, (tm, tm), 0)
    col = lax.broadcasted_iota(jnp.int32, (tm, tm), 1)
    tri = jnp.where(row >= col, 1.0, 0.0).astype(BF16)
    f_hi, f_mid, f_lo = _split3(log_f)
    part = jnp.dot(tri, jnp.concatenate([f_hi, f_mid], axis=1).astype(BF16),
                   preferred_element_type=F32)
    c = (part[:, :LANES] + part[:, LANES:]
         + jnp.dot(tri, f_lo.astype(BF16), preferred_element_type=F32) + carry_ref[0:1, :])
    carry_ref[0:1, :] = c[tm - 1:tm, :]

    lane = lax.broadcasted_iota(jnp.int32, (tm, LANES), 1)
    c_hi, c_mid, c_lo = _split3(c)
    packed = jnp.where(
        lane < FOX_HEADS, c_hi,
        jnp.where(lane < 2 * FOX_HEADS, pltpu.roll(c_mid, FOX_HEADS, 1),
                  jnp.where(lane < ONE_LANE, pltpu.roll(c_lo, 2 * FOX_HEADS, 1),
                            jnp.where(lane < ONE_LANE + 3, 1.0, 0.0)))).astype(BF16)
    qa_ref[...] = packed
    ka_ref[...] = jnp.dot(packed, sel_ref[...], preferred_element_type=F32).astype(BF16)


def _norm1(x, g, w_t, f_row, bf, tm=1024):
    b, s, d = x.shape
    return pl.pallas_call(
        _norm1_kernel,
        grid=(b, s // tm),
        in_specs=[
            pl.BlockSpec((None, tm, d), lambda i, j: (i, j, 0)),
            pl.BlockSpec((1, d), lambda i, j: (0, 0)),
            pl.BlockSpec((FOX_HEADS, d), lambda i, j: (f_row // FOX_HEADS, 0)),
            pl.BlockSpec((1, LANES), lambda i, j: (0, 0)),
            pl.BlockSpec((LANES, FOX_HEADS * LANES), lambda i, j: (0, 0)),
        ],
        out_specs=[
            pl.BlockSpec((None, tm, d), lambda i, j: (i, j, 0)),
            pl.BlockSpec((None, tm, LANES), lambda i, j: (i, j, 0)),
            pl.BlockSpec((None, tm, FOX_HEADS * LANES), lambda i, j: (i, j, 0)),
        ],
        out_shape=[
            jax.ShapeDtypeStruct((b, s, d), BF16),
            jax.ShapeDtypeStruct((b, s, LANES), BF16),
            jax.ShapeDtypeStruct((b, s, FOX_HEADS * LANES), BF16),
        ],
        scratch_shapes=[pltpu.VMEM((SUBLANES, LANES), F32)],
        compiler_params=_params(2),
        name="norm1_decay",
    )(x, g, w_t, bf, _decay_selector())


PLAIN, GATE = 0, -1


def _proj_kernel(h_ref, w_ref, gain_ref, o_ref, wbf_ref, *, kinds):
    @pl.when(pl.program_id(1) == 0)
    def _():
        wbf_ref[...] = w_ref[...].astype(BF16)

    def tile(kind):
        acc = lax.dot_general(h_ref[...], wbf_ref[...], NT_DIMS, preferred_element_type=F32)
        if kind == PLAIN:
            o_ref[...] = acc.astype(BF16)
        elif kind == GATE:
            o_ref[...] = (0.5 * jnp.tanh(0.5 * acc) + 0.5).astype(BF16)
        else:
            _store_group_rms(o_ref, acc, gain_ref[...], kind)

    j = pl.program_id(0)
    for kind in sorted(set(kinds)):
        hit = functools.reduce(jnp.logical_or, [j == t for t, kd in enumerate(kinds) if kd == kind])
        pl.when(hit)(functools.partial(tile, kind))


def _proj(h, w_t, row_starts, kinds, gains, tm=1024, tn=COL_TILE):
    m, k = h.shape
    n_tiles = len(row_starts)
    assert all(r % SUBLANES == 0 for r in row_starts)

    def w_rows(j, i):
        start = jnp.int32(row_starts[0])
        for t in range(1, n_tiles):
            start = jnp.where(j == t, row_starts[t], start)
        return (pl.multiple_of(start, SUBLANES), 0)

    return pl.pallas_call(
        functools.partial(_proj_kernel, kinds=tuple(kinds)),
        grid=(n_tiles, m // tm),
        in_specs=[
            pl.BlockSpec((tm, k), lambda j, i: (i, 0)),
            pl.BlockSpec((pl.Element(tn), pl.Element(k)), w_rows),
            pl.BlockSpec((None, 1, tn), lambda j, i: (j, 0, 0)),
        ],
        out_specs=pl.BlockSpec((tm, tn), lambda j, i: (i, j)),
        out_shape=jax.ShapeDtypeStruct((m, n_tiles * tn), BF16),
        scratch_shapes=[pltpu.VMEM((tn, k), BF16)],
        compiler_params=_params(2),
        name="in_proj",
    )(h, w_t, gains.reshape(n_tiles, 1, tn))


def _gated_conv_tile(conv_ref, w_ref, carry_ref):
    tm = conv_ref.shape[0]
    k = w_ref.shape[1]
    u = conv_ref[:, k:2 * k].astype(F32) * conv_ref[:, 2 * k:3 * k].astype(F32)
    w = w_ref[...]
    taps = lambda u2, u1, u0: w[0:1, :] * u2 + w[1:2, :] * u1 + w[2:3, :] * u0
    conv = taps(pltpu.roll(u, 2, 0), pltpu.roll(u, 1, 0), u)
    top = u[0:SUBLANES, :]
    row = lax.broadcasted_iota(jnp.int32, (SUBLANES, 1), 0)
    prev1 = carry_ref[1:2, :]
    prev2 = carry_ref[0:1, :]
    top1 = jnp.where(row == 0, prev1, pltpu.roll(top, 1, 0))
    top2 = jnp.where(row == 0, prev2, jnp.where(row == 1, prev1, pltpu.roll(top, 2, 0)))
    conv = jnp.concatenate([taps(top2, top1, top), conv[SUBLANES:, :]], axis=0)
    carry_ref[0:2, :] = u[tm - 2:tm, :]
    return conv_ref[:, 0:k].astype(F32) * conv


def _fox_kernel(q_ref, qa_ref, k_ref, ka_ref, v_ref, o_ref, vt_ref, s_ref, m_ref, l_ref, acc_ref, *,
                t):
    n_tiles = q_ref.shape[0] // t
    assert n_tiles % 2 == 0
    d = FOX_HEAD_DIM
    n_heads = q_ref.shape[1] // d
    j = pl.program_id(2)

    @pl.when(j == 0)
    def _():
        for g in range(n_heads):
            for blk in range(v_ref.shape[0] // t):
                v = v_ref[blk * t:(blk + 1) * t, g * d:(g + 1) * d]
                vt_ref[g, blk, :d] = v.astype(F32).T.astype(BF16)
                vt_ref[g, blk, d:] = jnp.ones((vt_ref.shape[2] - d, t), BF16)

    def scores(tile, kb, g):
        q = jnp.concatenate([q_ref[tile * t:(tile + 1) * t, g * d:(g + 1) * d],
                             qa_ref[tile * t:(tile + 1) * t, :]], axis=1)
        rows = pl.ds(pl.multiple_of(kb * t, t), t)
        k = jnp.concatenate([k_ref[rows, g * d:(g + 1) * d], ka_ref[rows, g * d:(g + 1) * d]], axis=1)
        return lax.dot_general(k, q, NT_DIMS, preferred_element_type=F32)

    def reset(g):
        m_ref[g] = jnp.full(m_ref.shape[1:], -jnp.inf, F32)
        l_ref[g] = jnp.zeros(l_ref.shape[1:], F32)
        acc_ref[g] = jnp.zeros(acc_ref.shape[1:], F32)

    def update(g, m_new, alpha, pv):
        l_ref[g] = alpha * l_ref[g] + pv[d:d + 1]
        acc_ref[g] = alpha * acc_ref[g] + pv[:d]
        m_ref[g] = m_new

    def consume(kb, slot, g):
        s = s_ref[slot, g]
        m_prev = m_ref[g]
        m_new = jnp.maximum(m_prev, jnp.max(s, axis=0, keepdims=True))
        p = jnp.exp2(s - m_new)
        pv = jnp.dot(vt_ref[g, kb], p.astype(BF16), preferred_element_type=F32)
        update(g, m_new, jnp.exp2(m_prev - m_new), pv)

    def causal(s):
        kpos = lax.broadcasted_iota(jnp.int32, s.shape, 0)
        qpos = lax.broadcasted_iota(jnp.int32, s.shape, 1)
        return jnp.where(qpos >= kpos, s, -jnp.inf)

    def consume_diagonal(kb, slot, g):
        h = t // 2
        s_up = causal(s_ref[slot, g, :h, :])
        s_lo = causal(s_ref[slot, g, h:, h:])
        m_up = jnp.max(s_up, axis=0, keepdims=True)
        m_blk = jnp.concatenate(
            [m_up[:, :h], jnp.maximum(m_up[:, h:], jnp.max(s_lo, axis=0, keepdims=True))], axis=1)
        m_prev = m_ref[g]
        m_new = jnp.maximum(m_prev, m_blk)
        p_up = jnp.exp2(s_up - m_new).astype(BF16)
        p_lo = jnp.exp2(s_lo - m_new[:, h:]).astype(BF16)
        vt = vt_ref[g, kb]
        pv_up = jnp.dot(vt[:, :h], p_up, preferred_element_type=F32)
        pv_lo = jnp.dot(vt[:, h:], p_lo, preferred_element_type=F32)
        pv = jnp.concatenate([pv_up[:, :h], pv_up[:, h:] + pv_lo], axis=1)
        update(g, m_new, jnp.exp2(m_prev - m_new), pv)

    def advance(tile, kb, slot):
        for g in range(n_heads):
            s_ref[1 - slot, g] = scores(tile, kb + 1, g)
            consume(kb, slot, g)

    def emit(tile, g):
        o_ref[tile * t:(tile + 1) * t, g * d:(g + 1) * d] = (acc_ref[g] / l_ref[g]).T.astype(BF16)

    slot = 0
    for g in range(n_heads):
        reset(g)
        s_ref[slot, g] = scores(0, 0, g)
    for r in range(n_tiles):
        def pairs(pair, carry, r=r, slot=slot):
            advance(r, 2 * pair, slot)
            advance(r, 2 * pair + 1, 1 - slot)
            return carry

        lax.fori_loop(0, (n_tiles // 2) * j + r // 2, pairs, 0)
        diagonal = n_tiles * j + r
        if r % 2 == 1:
            advance(r, diagonal - 1, slot)
            slot = 1 - slot
        for g in range(n_heads):
            if r + 1 < n_tiles:
                s_ref[1 - slot, g] = scores(r + 1, 0, g)
            consume_diagonal(diagonal, slot, g)
            emit(r, g)
            if r + 1 < n_tiles:
                reset(g)
        slot = 1 - slot


def _fox(proj, qa, ka, q_col, k_col, v_col, t=512, tiles_per_step=2, heads_per_step=4):
    b, s, _ = proj.shape
    d = FOX_HEAD_DIM
    w = heads_per_step * d
    rows = tiles_per_step * t
    tiles = lambda col0: pl.BlockSpec((None, rows, w), lambda bi, h, j: (bi, j, col0 // w + h))
    seq = lambda col0: pl.BlockSpec((None, s, w), lambda bi, h, j: (bi, 0, col0 // w + h))
    return pl.pallas_call(
        functools.partial(_fox_kernel, t=t),
        grid=(b, FOX_HEADS // heads_per_step, s // rows),
        in_specs=[tiles(q_col), pl.BlockSpec((None, rows, LANES), lambda bi, h, j: (bi, j, 0)),
                  seq(k_col), seq(0), seq(v_col)],
        out_specs=tiles(0),
        out_shape=jax.ShapeDtypeStruct((b, s, FOX_HEADS * d), BF16),
        scratch_shapes=[pltpu.VMEM((heads_per_step, s // t, d + BF16_ROWS, t), BF16),
                        pltpu.VMEM((2, heads_per_step, t, t), F32),
                        pltpu.VMEM((heads_per_step, 1, t), F32),
                        pltpu.VMEM((heads_per_step, 1, t), F32),
                        pltpu.VMEM((heads_per_step, d, t), F32)],
        compiler_params=_params(3),
        name="fox_attention",
    )(proj, qa, proj, ka, proj)


def _memkv_kernel(mem_ref, g_ref, w_ref, kg_ref, o_ref, y_ref, *, key_tiles):
    j = pl.program_id(0)

    @pl.when(j == 0)
    def _():
        xf = mem_ref[...]
        ms = jnp.mean(xf * xf, axis=-1, keepdims=True)
        y_ref[...] = (xf * lax.rsqrt(ms + EPS) * g_ref[...]).astype(BF16)

    acc = jnp.dot(y_ref[...], w_ref[...].astype(BF16), preferred_element_type=F32)

    @pl.when(j < key_tiles)
    def _():
        _store_group_rms(o_ref, acc, kg_ref[...], MEM_HEAD_DIM)

    @pl.when(j >= key_tiles)
    def _():
        o_ref[...] = acc.astype(BF16)


def _memkv(mem2d, g, w, kg, tn=512):
    m, d = mem2d.shape
    width = MEM_HEADS * MEM_HEAD_DIM
    key_tiles = width // tn
    return pl.pallas_call(
        functools.partial(_memkv_kernel, key_tiles=key_tiles),
        grid=(2 * key_tiles,),
        in_specs=[
            pl.BlockSpec((m, d), lambda j: (0, 0)),
            pl.BlockSpec((1, d), lambda j: (0, 0)),
            pl.BlockSpec((d, tn), lambda j: (0, j)),
            pl.BlockSpec((1, tn), lambda j: (0, jnp.minimum(j, key_tiles - 1))),
        ],
        out_specs=pl.BlockSpec((m, tn), lambda j: (0, j)),
        out_shape=jax.ShapeDtypeStruct((m, 2 * width), BF16),
        scratch_shapes=[pltpu.VMEM((m, d), BF16)],
        compiler_params=_params(1),
        name="mem_kv",
    )(mem2d, g, w, kg)


def _memattn_kernel(q_ref, kv_ref, o_ref):
    width = MEM_HEADS * MEM_HEAD_DIM
    for h in range(MEM_HEADS):
        lo, hi = h * MEM_HEAD_DIM, (h + 1) * MEM_HEAD_DIM
        q = q_ref[:, lo:hi]
        k = kv_ref[:, lo:hi]
        v = kv_ref[:, width + lo:width + hi]
        s = lax.dot_general(q, k, NT_DIMS, preferred_element_type=F32)
        p = jnp.exp(s - jnp.max(s, axis=-1, keepdims=True))
        l = jnp.sum(p, axis=-1, keepdims=True)
        o = jnp.dot(p.astype(BF16), v, preferred_element_type=F32) / l
        o_ref[:, lo:hi] = o.astype(BF16)


def _memattn(proj, q_col, kv, tm=1024):
    b, s, _ = proj.shape
    width = MEM_HEADS * MEM_HEAD_DIM
    n_mem = kv.shape[1]
    return pl.pallas_call(
        _memattn_kernel,
        grid=(b, s // tm),
        in_specs=[
            pl.BlockSpec((None, tm, width), lambda i, j: (i, j, q_col // width)),
            pl.BlockSpec((None, n_mem, 2 * width), lambda i, j: (i, 0, 0)),
        ],
        out_specs=pl.BlockSpec((None, tm, width), lambda i, j: (i, j, 0)),
        out_shape=jax.ShapeDtypeStruct((b, s, width), BF16),
        compiler_params=_params(2),
        name="mem_attention",
    )(proj, kv)


def _merge_kernel(conv_ref, cw_ref, b_ref, c_ref, wa_ref, wb_ref, wc_ref, g_ref,
                  o_ref, wabf_ref, wbbf_ref, wcbf_ref, carry_ref, *, tiles_per_seq):
    i = pl.program_id(1)
    tn = o_ref.shape[1]

    @pl.when(i == 0)
    def _():
        wabf_ref[...] = wa_ref[...].astype(BF16)
        wbbf_ref[...] = wb_ref[...].astype(BF16)
        wcbf_ref[...] = wc_ref[...].astype(BF16)

    @pl.when(i % tiles_per_seq == 0)
    def _():
        carry_ref[...] = jnp.zeros_like(carry_ref)

    y_conv = _gated_conv_tile(conv_ref, cw_ref, carry_ref).astype(BF16)
    oa = jnp.dot(y_conv, wabf_ref[...], preferred_element_type=F32)
    ob = jnp.dot(b_ref[...], wbbf_ref[...], preferred_element_type=F32)
    oc = jnp.dot(c_ref[...], wcbf_ref[...], preferred_element_type=F32)
    gate = lambda br: g_ref[:, br * tn:(br + 1) * tn].astype(F32)
    o_ref[...] = (gate(0) * oa + gate(1) * ob + gate(2) * oc).astype(BF16)


def _merge(proj, conv_col, conv_w, yb, yc, wa, wb, wc, seq_len, d_model, tm=512, tn=COL_TILE):
    m, k = yb.shape
    lhs = pl.BlockSpec((tm, k), lambda j, i: (i, 0))
    wsp = pl.BlockSpec((k, tn), lambda j, i: (0, j))
    return pl.pallas_call(
        functools.partial(_merge_kernel, tiles_per_seq=seq_len // tm),
        grid=(d_model // tn, m // tm),
        in_specs=[pl.BlockSpec((tm, 3 * k), lambda j, i: (i, conv_col // (3 * k))),
                  pl.BlockSpec((CONV_TAPS, k), lambda j, i: (0, 0)),
                  lhs, lhs, wsp, wsp, wsp,
                  pl.BlockSpec((tm, 3 * tn), lambda j, i: (i, j))],
        out_specs=pl.BlockSpec((tm, tn), lambda j, i: (i, j)),
        out_shape=jax.ShapeDtypeStruct((m, d_model), BF16),
        scratch_shapes=[pltpu.VMEM((k, tn), BF16)] * 3 + [pltpu.VMEM((SUBLANES, k), F32)],
        compiler_params=_params(2),
        name="branch_merge",
    )(proj, conv_w, yb, yc, wa, wb, wc, proj)


def _outproj_kernel(a_ref, w_ref, x_ref, g_ref, x2_ref, h2_ref, wbf_ref):
    @pl.when(pl.program_id(0) == 0)
    def _():
        wbf_ref[...] = w_ref[...].astype(BF16)

    x2 = x_ref[...] + jnp.dot(a_ref[...], wbf_ref[...], preferred_element_type=F32)
    x2_ref[...] = x2
    ms = jnp.mean(x2 * x2, axis=-1, keepdims=True)
    h2_ref[...] = (x2 * lax.rsqrt(ms + EPS) * g_ref[...]).astype(BF16)


def _outproj(a, w, x2d, g, tm=512):
    m, d = x2d.shape
    row = pl.BlockSpec((tm, d), lambda i: (i, 0))
    return pl.pallas_call(
        _outproj_kernel,
        grid=(m // tm,),
        in_specs=[row, pl.BlockSpec((d, d), lambda i: (0, 0), pipeline_mode=pl.Buffered(1)), row,
                  pl.BlockSpec((1, d), lambda i: (0, 0))],
        out_specs=[row, row],
        out_shape=[jax.ShapeDtypeStruct((m, d), F32), jax.ShapeDtypeStruct((m, d), BF16)],
        scratch_shapes=[pltpu.VMEM((d, d), BF16)],
        compiler_params=_params(1),
        name="out_proj_norm2",
    )(a, w, x2d, g)


def _mlp_kernel(h_ref, wu_ref, wd_ref, x_ref, o_ref):
    f = pl.program_id(1)
    rows = x_ref.shape[0]

    @pl.when(f == 0)
    def _():
        o_ref[...] = jnp.zeros_like(o_ref)

    a = jnp.dot(h_ref[...], wu_ref[...].astype(BF16), preferred_element_type=F32)
    a = jnp.square(jnp.maximum(a, 0.0)).astype(BF16)
    o_ref[...] += jnp.dot(a, wd_ref[...].astype(BF16), preferred_element_type=F32)
    slab = pl.ds(pl.multiple_of(f * rows, rows), rows)
    o_ref[slab, :] += x_ref[...]


def _mlp(h2, w_up, w_down, x2, tm=1024, tf=512):
    m, d = h2.shape
    d_ff = w_up.shape[1]
    nf = d_ff // tf
    rows = tm // nf
    return pl.pallas_call(
        _mlp_kernel,
        grid=(m // tm, nf),
        in_specs=[
            pl.BlockSpec((tm, d), lambda i, f: (i, 0)),
            pl.BlockSpec((d, tf), lambda i, f: (0, f)),
            pl.BlockSpec((tf, d), lambda i, f: (f, 0)),
            pl.BlockSpec((rows, d), lambda i, f: (i * nf + f, 0)),
        ],
        out_specs=pl.BlockSpec((tm, d), lambda i, f: (i, 0)),
        out_shape=jax.ShapeDtypeStruct((m, d), F32),
        compiler_params=_params(2),
        name="relu2_mlp",
    )(h2, w_up, w_down, x2)


def kernel(x, mem, norm1_g, w_in, b_f, conv_w, fox_q_g, fox_k_g, mem_norm_g, w_mem_kv, mem_q_g, mem_k_g, w_conv_out, w_fox_out, w_mem_out, w_out, norm2_g, w_up, w_down):
    b, s, d = x.shape
    m = b * s
    conv_width = conv_w.shape[1]
    fox_width = FOX_HEADS * FOX_HEAD_DIM
    mem_width = MEM_HEADS * MEM_HEAD_DIM
    tile = COL_TILE
    q_row = 3 * conv_width
    k_row = q_row + fox_width
    v_row = k_row + fox_width
    f_row = v_row + fox_width
    mq_row = f_row + FOX_HEADS
    gate_row = mq_row + mem_width
    n_gate = N_BRANCHES * d
    ch_tiles = d // tile
    gate_rows = [gate_row + br * d + n * tile for n in range(ch_tiles) for br in range(N_BRANCHES)]
    rest_rows = [r for r in range(0, f_row, tile)] + [r for r in range(mq_row, gate_row, tile)]
    row_starts = tuple(gate_rows + rest_rows)
    conv_col, q_col, k_col, v_col, mq_col = (n_gate + r for r in (0, q_row, k_row, v_row, f_row))
    kinds = ([GATE] * len(gate_rows) + [PLAIN] * (q_row // tile) + [FOX_HEAD_DIM] * (2 * fox_width // tile)
             + [PLAIN] * (fox_width // tile) + [MEM_HEAD_DIM] * (mem_width // tile))
    gains = jnp.ones((len(row_starts) * tile,), F32)
    gains = gains.at[q_col:k_col].set(jnp.tile(fox_q_g * (LOG2_E / math.sqrt(FOX_HEAD_DIM)), FOX_HEADS))
    gains = gains.at[k_col:v_col].set(jnp.tile(fox_k_g, FOX_HEADS))
    gains = gains.at[mq_col:].set(jnp.tile(mem_q_g * (1.0 / math.sqrt(MEM_HEAD_DIM)), MEM_HEADS))

    row = lambda v: v.reshape(1, -1)
    bf = jnp.pad(b_f, (0, LANES - FOX_HEADS)).reshape(1, LANES)
    w_t = w_in.T

    h1, qa, ka = _norm1(x, row(norm1_g), w_t, f_row, bf)
    proj = _proj(h1.reshape(m, d), w_t, row_starts, kinds, gains.reshape(-1, tile))
    proj3 = proj.reshape(b, s, -1)

    y_fox = _fox(proj3, qa, ka, q_col, k_col, v_col)
    kv = _memkv(mem.reshape(-1, d), row(mem_norm_g), w_mem_kv, row(jnp.tile(mem_k_g, MEM_HEADS)))
    y_mem = _memattn(proj3, mq_col, kv.reshape(b, -1, 2 * mem_width))

    merged = _merge(proj, conv_col, conv_w, y_fox.reshape(m, -1), y_mem.reshape(m, -1),
                    w_conv_out, w_fox_out, w_mem_out, s, d)
    x2, h2 = _outproj(merged, w_out, x.reshape(m, d), row(norm2_g))
    out = _mlp(h2, w_up, w_down, x2)
    return out.reshape(b, s, d)
```

```python
import functools
import math

import jax
import jax.numpy as jnp
import numpy as np
from jax import lax
from jax.experimental import pallas as pl
from jax.experimental.pallas import tpu as pltpu

F32 = jnp.float32
BF16 = jnp.bfloat16

EPS = 1e-6
LOG2_E = math.log2(math.e)
LANES = 128
SUBLANES = 8
BF16_ROWS = 2 * SUBLANES
CONV_TAPS = 3
FOX_HEADS = 8
FOX_HEAD_DIM = 128
MEM_HEADS = 4
MEM_HEAD_DIM = 256
N_BRANCHES = 3

VMEM_LIMIT_BYTES = 56 * 1024 * 1024
COL_TILE = 1024
NT_DIMS = (((1,), (1,)), ((), ()))


def _params(n_axes):
    return pltpu.CompilerParams(
        dimension_semantics=("arbitrary",) * n_axes,
        vmem_limit_bytes=VMEM_LIMIT_BYTES)


def _store_group_rms(o_ref, a, gain, width):
    for s in range(0, a.shape[1], width):
        blk = a[:, s:s + width]
        ms = jnp.mean(blk * blk, axis=-1, keepdims=True)
        o_ref[:, s:s + width] = (blk * lax.rsqrt(ms + EPS) * gain[:, s:s + width]).astype(o_ref.dtype)


def _split3(c):
    hi = c.astype(BF16).astype(F32)
    r1 = c - hi
    mid = r1.astype(BF16).astype(F32)
    lo = (r1 - mid).astype(BF16).astype(F32)
    return hi, mid, lo


ONE_LANE = 3 * FOX_HEADS


def _decay_selector():
    sel = np.zeros((LANES, FOX_HEADS, LANES), np.float32)
    for hd in range(FOX_HEADS):
        for piece in range(3):
            sel[ONE_LANE, hd, piece * FOX_HEADS + hd] = 1.0
            sel[piece * FOX_HEADS + hd, hd, ONE_LANE + piece] = -1.0
    return jnp.asarray(sel.reshape(LANES, -1), BF16)


def _norm1_kernel(x_ref, g_ref, wf_ref, bf_ref, sel_ref, h_ref, qa_ref, ka_ref, carry_ref):
    tm = x_ref.shape[0]

    @pl.when(pl.program_id(1) == 0)
    def _():
        carry_ref[...] = jnp.zeros_like(carry_ref)

    xf = x_ref[...]
    ms = jnp.mean(xf * xf, axis=-1, keepdims=True)
    y = xf * lax.rsqrt(ms + EPS) * g_ref[...]
    y_hi = y.astype(BF16)
    h_ref[...] = y_hi

    wf = wf_ref[...]
    wf_hi = wf.astype(BF16).astype(F32)
    w2 = jnp.concatenate(
        [wf_hi, wf - wf_hi, jnp.zeros((LANES - 2 * FOX_HEADS, wf.shape[1]), F32)], axis=0).astype(BF16)
    z2 = lax.dot_general(y_hi, w2, NT_DIMS, preferred_element_type=F32)
    z = z2 + pltpu.roll(z2, LANES - FOX_HEADS, 1) + bf_ref[...]
    log_f = -LOG2_E * (jnp.maximum(-z, 0.0) + jnp.log1p(jnp.exp(-jnp.abs(z))))

    row = lax.broadcasted_iota(jnp.int32, (tm, tm), 0)
    col = lax.broadcasted_iota(jnp.int32, (tm, tm), 1)
    tri = jnp.where(row >= col, 1.0, 0.0).astype(BF16)
    f_hi, f_mid, f_lo = _split3(log_f)
    part = jnp.dot(tri, jnp.concatenate([f_hi, f_mid], axis=1).astype(BF16),
                   preferred_element_type=F32)
    c = (part[:, :LANES] + part[:, LANES:]
         + jnp.dot(tri, f_lo.astype(BF16), preferred_element_type=F32) + carry_ref[0:1, :])
    carry_ref[0:1, :] = c[tm - 1:tm, :]

    lane = lax.broadcasted_iota(jnp.int32, (tm, LANES), 1)
    c_hi, c_mid, c_lo = _split3(c)
    packed = jnp.where(
        lane < FOX_HEADS, c_hi,
        jnp.where(lane < 2 * FOX_HEADS, pltpu.roll(c_mid, FOX_HEADS, 1),
                  jnp.where(lane < ONE_LANE, pltpu.roll(c_lo, 2 * FOX_HEADS, 1),
                            jnp.where(lane < ONE_LANE + 3, 1.0, 0.0)))).astype(BF16)
    qa_ref[...] = packed
    ka_ref[...] = jnp.dot(packed, sel_ref[...], preferred_element_type=F32).astype(BF16)


def _norm1(x, g, w_t, f_row, bf, tm=1024):
    b, s, d = x.shape
    return pl.pallas_call(
        _norm1_kernel,
        grid=(b, s // tm),
        in_specs=[
            pl.BlockSpec((None, tm, d), lambda i, j: (i, j, 0)),
            pl.BlockSpec((1, d), lambda i, j: (0, 0)),
            pl.BlockSpec((FOX_HEADS, d), lambda i, j: (f_row // FOX_HEADS, 0)),
            pl.BlockSpec((1, LANES), lambda i, j: (0, 0)),
            pl.BlockSpec((LANES, FOX_HEADS * LANES), lambda i, j: (0, 0)),
        ],
        out_specs=[
            pl.BlockSpec((None, tm, d), lambda i, j: (i, j, 0)),
            pl.BlockSpec((None, tm, LANES), lambda i, j: (i, j, 0)),
            pl.BlockSpec((None, tm, FOX_HEADS * LANES), lambda i, j: (i, j, 0)),
        ],
        out_shape=[
            jax.ShapeDtypeStruct((b, s, d), BF16),
            jax.ShapeDtypeStruct((b, s, LANES), BF16),
            jax.ShapeDtypeStruct((b, s, FOX_HEADS * LANES), BF16),
        ],
        scratch_shapes=[pltpu.VMEM((SUBLANES, LANES), F32)],
        compiler_params=_params(2),
        name="norm1_decay",
    )(x, g, w_t, bf, _decay_selector())


PLAIN, GATE = 0, -1


def _proj_kernel(h_ref, w_ref, gain_ref, o_ref, wbf_ref, *, kinds):
    @pl.when(pl.program_id(1) == 0)
    def _():
        wbf_ref[...] = w_ref[...].astype(BF16)

    def tile(kind):
        acc = lax.dot_general(h_ref[...], wbf_ref[...], NT_DIMS, preferred_element_type=F32)
        if kind == PLAIN:
            o_ref[...] = acc.astype(BF16)
        elif kind == GATE:
            o_ref[...] = (0.5 * jnp.tanh(0.5 * acc) + 0.5).astype(BF16)
        else:
            _store_group_rms(o_ref, acc, gain_ref[...], kind)

    j = pl.program_id(0)
    for kind in sorted(set(kinds)):
        hit = functools.reduce(jnp.logical_or, [j == t for t, kd in enumerate(kinds) if kd == kind])
        pl.when(hit)(functools.partial(tile, kind))


def _proj(h, w_t, row_starts, kinds, gains, tm=1024, tn=COL_TILE):
    m, k = h.shape
    n_tiles = len(row_starts)
    assert all(r % SUBLANES == 0 for r in row_starts)

    def w_rows(j, i):
        start = jnp.int32(row_starts[0])
        for t in range(1, n_tiles):
            start = jnp.where(j == t, row_starts[t], start)
        return (pl.multiple_of(start, SUBLANES), 0)

    return pl.pallas_call(
        functools.partial(_proj_kernel, kinds=tuple(kinds)),
        grid=(n_tiles, m // tm),
        in_specs=[
            pl.BlockSpec((tm, k), lambda j, i: (i, 0)),
            pl.BlockSpec((pl.Element(tn), pl.Element(k)), w_rows),
            pl.BlockSpec((None, 1, tn), lambda j, i: (j, 0, 0)),
        ],
        out_specs=pl.BlockSpec((tm, tn), lambda j, i: (i, j)),
        out_shape=jax.ShapeDtypeStruct((m, n_tiles * tn), BF16),
        scratch_shapes=[pltpu.VMEM((tn, k), BF16)],
        compiler_params=_params(2),
        name="in_proj",
    )(h, w_t, gains.reshape(n_tiles, 1, tn))


def _gated_conv_tile(conv_ref, w_ref, carry_ref):
    tm = conv_ref.shape[0]
    k = w_ref.shape[1]
    u = conv_ref[:, k:2 * k].astype(F32) * conv_ref[:, 2 * k:3 * k].astype(F32)
    w = w_ref[...]
    taps = lambda u2, u1, u0: w[0:1, :] * u2 + w[1:2, :] * u1 + w[2:3, :] * u0
    conv = taps(pltpu.roll(u, 2, 0), pltpu.roll(u, 1, 0), u)
    top = u[0:SUBLANES, :]
    row = lax.broadcasted_iota(jnp.int32, (SUBLANES, 1), 0)
    prev1 = carry_ref[1:2, :]
    prev2 = carry_ref[0:1, :]
    top1 = jnp.where(row == 0, prev1, pltpu.roll(top, 1, 0))
    top2 = jnp.where(row == 0, prev2, jnp.where(row == 1, prev1, pltpu.roll(top, 2, 0)))
    conv = jnp.concatenate([taps(top2, top1, top), conv[SUBLANES:, :]], axis=0)
    carry_ref[0:2, :] = u[tm - 2:tm, :]
    return conv_ref[:, 0:k].astype(F32) * conv


def _fox_kernel(q_ref, qa_ref, k_ref, ka_ref, v_ref, o_ref, vt_ref, s_ref, m_ref, l_ref, acc_ref, *,
                t):
    n_tiles = q_ref.shape[0] // t
    assert n_tiles % 2 == 0
    d = FOX_HEAD_DIM
    n_heads = q_ref.shape[1] // d
    j = pl.program_id(2)

    @pl.when(j == 0)
    def _():
        for g in range(n_heads):
            for blk in range(v_ref.shape[0] // t):
                v = v_ref[blk * t:(blk + 1) * t, g * d:(g + 1) * d]
                vt_ref[g, blk, :d] = v.astype(F32).T.astype(BF16)
                vt_ref[g, blk, d:] = jnp.ones((vt_ref.shape[2] - d, t), BF16)

    def scores(tile, kb, g):
        q = jnp.concatenate([q_ref[tile * t:(tile + 1) * t, g * d:(g + 1) * d],
                             qa_ref[tile * t:(tile + 1) * t, :]], axis=1)
        rows = pl.ds(pl.multiple_of(kb * t, t), t)
        k = jnp.concatenate([k_ref[rows, g * d:(g + 1) * d], ka_ref[rows, g * d:(g + 1) * d]], axis=1)
        return lax.dot_general(k, q, NT_DIMS, preferred_element_type=F32)

    def reset(g):
        m_ref[g] = jnp.full(m_ref.shape[1:], -jnp.inf, F32)
        l_ref[g] = jnp.zeros(l_ref.shape[1:], F32)
        acc_ref[g] = jnp.zeros(acc_ref.shape[1:], F32)

    def update(g, m_new, alpha, pv):
        l_ref[g] = alpha * l_ref[g] + pv[d:d + 1]
        acc_ref[g] = alpha * acc_ref[g] + pv[:d]
        m_ref[g] = m_new

    def consume(kb, slot, g):
        s = s_ref[slot, g]
        m_prev = m_ref[g]
        m_new = jnp.maximum(m_prev, jnp.max(s, axis=0, keepdims=True))
        p = jnp.exp2(s - m_new)
        pv = jnp.dot(vt_ref[g, kb], p.astype(BF16), preferred_element_type=F32)
        update(g, m_new, jnp.exp2(m_prev - m_new), pv)

    def causal(s):
        kpos = lax.broadcasted_iota(jnp.int32, s.shape, 0)
        qpos = lax.broadcasted_iota(jnp.int32, s.shape, 1)
        return jnp.where(qpos >= kpos, s, -jnp.inf)

    def consume_diagonal(kb, slot, g):
        h = t // 2
        s_up = causal(s_ref[slot, g, :h, :])
        s_lo = causal(s_ref[slot, g, h:, h:])
        m_up = jnp.max(s_up, axis=0, keepdims=True)
        m_blk = jnp.concatenate(
            [m_up[:, :h], jnp.maximum(m_up[:, h:], jnp.max(s_lo, axis=0, keepdims=True))], axis=1)
        m_prev = m_ref[g]
        m_new = jnp.maximum(m_prev, m_blk)
        p_up = jnp.exp2(s_up - m_new).astype(BF16)
        p_lo = jnp.exp2(s_lo - m_new[:, h:]).astype(BF16)
        vt = vt_ref[g, kb]
        pv_up = jnp.dot(vt[:, :h], p_up, preferred_element_type=F32)
        pv_lo = jnp.dot(vt[:, h:], p_lo, preferred_element_type=F32)
        pv = jnp.concatenate([pv_up[:, :h], pv_up[:, h:] + pv_lo], axis=1)
        update(g, m_new, jnp.exp2(m_prev - m_new), pv)

    def advance(tile, kb, slot):
        for g in range(n_heads):
            s_ref[1 - slot, g] = scores(tile, kb + 1, g)
            consume(kb, slot, g)

    def emit(tile, g):
        o_ref[tile * t:(tile + 1) * t, g * d:(g + 1) * d] = (acc_ref[g] / l_ref[g]).T.astype(BF16)

    slot = 0
    for g in range(n_heads):
        reset(g)
        s_ref[slot, g] = scores(0, 0, g)
    for r in range(n_tiles):
        def pairs(pair, carry, r=r, slot=slot):
            advance(r, 2 * pair, slot)
            advance(r, 2 * pair + 1, 1 - slot)
            return carry

        lax.fori_loop(0, (n_tiles // 2) * j + r // 2, pairs, 0)
        diagonal = n_tiles * j + r
        if r % 2 == 1:
            advance(r, diagonal - 1, slot)
            slot = 1 - slot
        for g in range(n_heads):
            if r + 1 < n_tiles:
                s_ref[1 - slot, g] = scores(r + 1, 0, g)
            consume_diagonal(diagonal, slot, g)
            emit(r, g)
            if r + 1 < n_tiles:
                reset(g)
        slot = 1 - slot


def _fox(proj, qa, ka, q_col, k_col, v_col, t=512, tiles_per_step=2, heads_per_step=4):
    b, s, _ = proj.shape
    d = FOX_HEAD_DIM
    w = heads_per_step * d
    rows = tiles_per_step * t
    tiles = lambda col0: pl.BlockSpec((None, rows, w), lambda bi, h, j: (bi, j, col0 // w + h))
    seq = lambda col0: pl.BlockSpec((None, s, w), lambda bi, h, j: (bi, 0, col0 // w + h))
    return pl.pallas_call(
        functools.partial(_fox_kernel, t=t),
        grid=(b, FOX_HEADS // heads_per_step, s // rows),
        in_specs=[tiles(q_col), pl.BlockSpec((None, rows, LANES), lambda bi, h, j: (bi, j, 0)),
                  seq(k_col), seq(0), seq(v_col)],
        out_specs=tiles(0),
        out_shape=jax.ShapeDtypeStruct((b, s, FOX_HEADS * d), BF16),
        scratch_shapes=[pltpu.VMEM((heads_per_step, s // t, d + BF16_ROWS, t), BF16),
                        pltpu.VMEM((2, heads_per_step, t, t), F32),
                        pltpu.VMEM((heads_per_step, 1, t), F32),
                        pltpu.VMEM((heads_per_step, 1, t), F32),
                        pltpu.VMEM((heads_per_step, d, t), F32)],
        compiler_params=_params(3),
        name="fox_attention",
    )(proj, qa, proj, ka, proj)


def _memkv_kernel(mem_ref, g_ref, w_ref, kg_ref, o_ref, y_ref, *, key_tiles):
    j = pl.program_id(0)

    @pl.when(j == 0)
    def _():
        xf = mem_ref[...]
        ms = jnp.mean(xf * xf, axis=-1, keepdims=True)
        y_ref[...] = (xf * lax.rsqrt(ms + EPS) * g_ref[...]).astype(BF16)

    acc = jnp.dot(y_ref[...], w_ref[...].astype(BF16), preferred_element_type=F32)

    @pl.when(j < key_tiles)
    def _():
        _store_group_rms(o_ref, acc, kg_ref[...], MEM_HEAD_DIM)

    @pl.when(j >= key_tiles)
    def _():
        o_ref[...] = acc.astype(BF16)


def _memkv(mem2d, g, w, kg, tn=1024):
    m, d = mem2d.shape
    width = MEM_HEADS * MEM_HEAD_DIM
    key_tiles = width // tn
    return pl.pallas_call(
        functools.partial(_memkv_kernel, key_tiles=key_tiles),
        grid=(2 * key_tiles,),
        in_specs=[
            pl.BlockSpec((m, d), lambda j: (0, 0)),
            pl.BlockSpec((1, d), lambda j: (0, 0)),
            pl.BlockSpec((d, tn), lambda j: (0, j)),
            pl.BlockSpec((1, tn), lambda j: (0, jnp.minimum(j, key_tiles - 1))),
        ],
        out_specs=pl.BlockSpec((m, tn), lambda j: (0, j)),
        out_shape=jax.ShapeDtypeStruct((m, 2 * width), BF16),
        scratch_shapes=[pltpu.VMEM((m, d), BF16)],
        compiler_params=_params(1),
        name="mem_kv",
    )(mem2d, g, w, kg)


def _memattn_kernel(q_ref, kv_ref, o_ref):
    width = MEM_HEADS * MEM_HEAD_DIM
    for h in range(MEM_HEADS):
        lo, hi = h * MEM_HEAD_DIM, (h + 1) * MEM_HEAD_DIM
        q = q_ref[:, lo:hi]
        k = kv_ref[:, lo:hi]
        v = kv_ref[:, width + lo:width + hi]
        s = lax.dot_general(q, k, NT_DIMS, preferred_element_type=F32)
        p = jnp.exp(s - jnp.max(s, axis=-1, keepdims=True))
        l = jnp.sum(p, axis=-1, keepdims=True)
        o = jnp.dot(p.astype(BF16), v, preferred_element_type=F32) / l
        o_ref[:, lo:hi] = o.astype(BF16)


def _memattn(proj, q_col, kv, tm=1024):
    b, s, _ = proj.shape
    width = MEM_HEADS * MEM_HEAD_DIM
    n_mem = kv.shape[1]
    return pl.pallas_call(
        _memattn_kernel,
        grid=(b, s // tm),
        in_specs=[
            pl.BlockSpec((None, tm, width), lambda i, j: (i, j, q_col // width)),
            pl.BlockSpec((None, n_mem, 2 * width), lambda i, j: (i, 0, 0)),
        ],
        out_specs=pl.BlockSpec((None, tm, width), lambda i, j: (i, j, 0)),
        out_shape=jax.ShapeDtypeStruct((b, s, width), BF16),
        compiler_params=_params(2),
        name="mem_attention",
    )(proj, kv)


def _merge_kernel(conv_ref, cw_ref, b_ref, c_ref, wa_ref, wb_ref, wc_ref, g_ref,
                  o_ref, wabf_ref, wbbf_ref, wcbf_ref, carry_ref, *, tiles_per_seq):
    i = pl.program_id(1)
    tn = o_ref.shape[1]

    @pl.when(i == 0)
    def _():
        wabf_ref[...] = wa_ref[...].astype(BF16)
        wbbf_ref[...] = wb_ref[...].astype(BF16)
        wcbf_ref[...] = wc_ref[...].astype(BF16)

    @pl.when(i % tiles_per_seq == 0)
    def _():
        carry_ref[...] = jnp.zeros_like(carry_ref)

    y_conv = _gated_conv_tile(conv_ref, cw_ref, carry_ref).astype(BF16)
    oa = jnp.dot(y_conv, wabf_ref[...], preferred_element_type=F32)
    ob = jnp.dot(b_ref[...], wbbf_ref[...], preferred_element_type=F32)
    oc = jnp.dot(c_ref[...], wcbf_ref[...], preferred_element_type=F32)
    gate = lambda br: g_ref[:, br * tn:(br + 1) * tn].astype(F32)
    o_ref[...] = (gate(0) * oa + gate(1) * ob + gate(2) * oc).astype(BF16)


def _merge(proj, conv_col, conv_w, yb, yc, wa, wb, wc, seq_len, d_model, tm=512, tn=COL_TILE):
    m, k = yb.shape
    lhs = pl.BlockSpec((tm, k), lambda j, i: (i, 0))
    wsp = pl.BlockSpec((k, tn), lambda j, i: (0, j))
    return pl.pallas_call(
        functools.partial(_merge_kernel, tiles_per_seq=seq_len // tm),
        grid=(d_model // tn, m // tm),
        in_specs=[pl.BlockSpec((tm, 3 * k), lambda j, i: (i, conv_col // (3 * k))),
                  pl.BlockSpec((CONV_TAPS, k), lambda j, i: (0, 0)),
                  lhs, lhs, wsp, wsp, wsp,
                  pl.BlockSpec((tm, 3 * tn), lambda j, i: (i, j))],
        out_specs=pl.BlockSpec((tm, tn), lambda j, i: (i, j)),
        out_shape=jax.ShapeDtypeStruct((m, d_model), BF16),
        scratch_shapes=[pltpu.VMEM((k, tn), BF16)] * 3 + [pltpu.VMEM((SUBLANES, k), F32)],
        compiler_params=_params(2),
        name="branch_merge",
    )(proj, conv_w, yb, yc, wa, wb, wc, proj)


def _outproj_kernel(a_ref, w_ref, x_ref, g_ref, x2_ref, h2_ref, wbf_ref):
    @pl.when(pl.program_id(0) == 0)
    def _():
        wbf_ref[...] = w_ref[...].astype(BF16)

    x2 = x_ref[...] + jnp.dot(a_ref[...], wbf_ref[...], preferred_element_type=F32)
    x2_ref[...] = x2
    ms = jnp.mean(x2 * x2, axis=-1, keepdims=True)
    h2_ref[...] = (x2 * lax.rsqrt(ms + EPS) * g_ref[...]).astype(BF16)


def _outproj(a, w, x2d, g, tm=512):
    m, d = x2d.shape
    row = pl.BlockSpec((tm, d), lambda i: (i, 0))
    return pl.pallas_call(
        _outproj_kernel,
        grid=(m // tm,),
        in_specs=[row, pl.BlockSpec((d, d), lambda i: (0, 0), pipeline_mode=pl.Buffered(1)), row,
                  pl.BlockSpec((1, d), lambda i: (0, 0))],
        out_specs=[row, row],
        out_shape=[jax.ShapeDtypeStruct((m, d), F32), jax.ShapeDtypeStruct((m, d), BF16)],
        scratch_shapes=[pltpu.VMEM((d, d), BF16)],
        compiler_params=_params(1),
        name="out_proj_norm2",
    )(a, w, x2d, g)


def _mlp_kernel(h_ref, wu_ref, wd_ref, x_ref, o_ref):
    f = pl.program_id(1)
    rows = x_ref.shape[0]

    @pl.when(f == 0)
    def _():
        o_ref[...] = jnp.zeros_like(o_ref)

    a = jnp.dot(h_ref[...], wu_ref[...].astype(BF16), preferred_element_type=F32)
    a = jnp.square(jnp.maximum(a, 0.0)).astype(BF16)
    o_ref[...] += jnp.dot(a, wd_ref[...].astype(BF16), preferred_element_type=F32)
    slab = pl.ds(pl.multiple_of(f * rows, rows), rows)
    o_ref[slab, :] += x_ref[...]


def _mlp(h2, w_up, w_down, x2, tm=1024, tf=512):
    m, d = h2.shape
    d_ff = w_up.shape[1]
    nf = d_ff // tf
    rows = tm // nf
    return pl.pallas_call(
        _mlp_kernel,
        grid=(m // tm, nf),
        in_specs=[
            pl.BlockSpec((tm, d), lambda i, f: (i, 0)),
            pl.BlockSpec((d, tf), lambda i, f: (0, f)),
            pl.BlockSpec((tf, d), lambda i, f: (f, 0)),
            pl.BlockSpec((rows, d), lambda i, f: (i * nf + f, 0)),
        ],
        out_specs=pl.BlockSpec((tm, d), lambda i, f: (i, 0)),
        out_shape=jax.ShapeDtypeStruct((m, d), F32),
        compiler_params=_params(2),
        name="relu2_mlp",
    )(h2, w_up, w_down, x2)


def kernel(x, mem, norm1_g, w_in, b_f, conv_w, fox_q_g, fox_k_g, mem_norm_g, w_mem_kv, mem_q_g, mem_k_g, w_conv_out, w_fox_out, w_mem_out, w_out, norm2_g, w_up, w_down):
    b, s, d = x.shape
    m = b * s
    conv_width = conv_w.shape[1]
    fox_width = FOX_HEADS * FOX_HEAD_DIM
    mem_width = MEM_HEADS * MEM_HEAD_DIM
    tile = COL_TILE
    q_row = 3 * conv_width
    k_row = q_row + fox_width
    v_row = k_row + fox_width
    f_row = v_row + fox_width
    mq_row = f_row + FOX_HEADS
    gate_row = mq_row + mem_width
    n_gate = N_BRANCHES * d
    ch_tiles = d // tile
    gate_rows = [gate_row + br * d + n * tile for n in range(ch_tiles) for br in range(N_BRANCHES)]
    rest_rows = [r for r in range(0, f_row, tile)] + [r for r in range(mq_row, gate_row, tile)]
    row_starts = tuple(gate_rows + rest_rows)
    conv_col, q_col, k_col, v_col, mq_col = (n_gate + r for r in (0, q_row, k_row, v_row, f_row))
    kinds = ([GATE] * len(gate_rows) + [PLAIN] * (q_row // tile) + [FOX_HEAD_DIM] * (2 * fox_width // tile)
             + [PLAIN] * (fox_width // tile) + [MEM_HEAD_DIM] * (mem_width // tile))
    gains = jnp.ones((len(row_starts) * tile,), F32)
    gains = gains.at[q_col:k_col].set(jnp.tile(fox_q_g * (LOG2_E / math.sqrt(FOX_HEAD_DIM)), FOX_HEADS))
    gains = gains.at[k_col:v_col].set(jnp.tile(fox_k_g, FOX_HEADS))
    gains = gains.at[mq_col:].set(jnp.tile(mem_q_g * (1.0 / math.sqrt(MEM_HEAD_DIM)), MEM_HEADS))

    row = lambda v: v.reshape(1, -1)
    bf = jnp.pad(b_f, (0, LANES - FOX_HEADS)).reshape(1, LANES)
    w_t = w_in.T

    h1, qa, ka = _norm1(x, row(norm1_g), w_t, f_row, bf)
    proj = _proj(h1.reshape(m, d), w_t, row_starts, kinds, gains.reshape(-1, tile))
    proj3 = proj.reshape(b, s, -1)

    y_fox = _fox(proj3, qa, ka, q_col, k_col, v_col)
    kv = _memkv(mem.reshape(-1, d), row(mem_norm_g), w_mem_kv, row(jnp.tile(mem_k_g, MEM_HEADS)))
    y_mem = _memattn(proj3, mq_col, kv.reshape(b, -1, 2 * mem_width))

    merged = _merge(proj, conv_col, conv_w, y_fox.reshape(m, -1), y_mem.reshape(m, -1),
                    w_conv_out, w_fox_out, w_mem_out, s, d)
    x2, h2 = _outproj(merged, w_out, x.reshape(m, d), row(norm2_g))
    out = _mlp(h2, w_up, w_down, x2)
    return out.reshape(b, s, d)
```

```python
import functools
import math

import jax
import jax.numpy as jnp
import numpy as np
from jax import lax
from jax.experimental import pallas as pl
from jax.experimental.pallas import tpu as pltpu

F32 = jnp.float32
BF16 = jnp.bfloat16

EPS = 1e-6
LOG2_E = math.log2(math.e)
LANES = 128
SUBLANES = 8
BF16_ROWS = 2 * SUBLANES
CONV_TAPS = 3
FOX_HEADS = 8
FOX_HEAD_DIM = 128
MEM_HEADS = 4
MEM_HEAD_DIM = 256
N_BRANCHES = 3

VMEM_LIMIT_BYTES = 56 * 1024 * 1024
COL_TILE = 1024
NT_DIMS = (((1,), (1,)), ((), ()))


def _params(n_axes):
    return pltpu.CompilerParams(
        dimension_semantics=("arbitrary",) * n_axes,
        vmem_limit_bytes=VMEM_LIMIT_BYTES)


def _store_group_rms(o_ref, a, gain, width):
    for s in range(0, a.shape[1], width):
        blk = a[:, s:s + width]
        ms = jnp.mean(blk * blk, axis=-1, keepdims=True)
        o_ref[:, s:s + width] = (blk * lax.rsqrt(ms + EPS) * gain[:, s:s + width]).astype(o_ref.dtype)


def _split3(c):
    hi = c.astype(BF16).astype(F32)
    r1 = c - hi
    mid = r1.astype(BF16).astype(F32)
    lo = (r1 - mid).astype(BF16).astype(F32)
    return hi, mid, lo


ONE_LANE = 3 * FOX_HEADS


def _decay_selector():
    sel = np.zeros((LANES, FOX_HEADS, LANES), np.float32)
    for hd in range(FOX_HEADS):
        for piece in range(3):
            sel[ONE_LANE, hd, piece * FOX_HEADS + hd] = 1.0
            sel[piece * FOX_HEADS + hd, hd, ONE_LANE + piece] = -1.0
    return jnp.asarray(sel.reshape(LANES, -1), BF16)


def _norm1_kernel(x_ref, g_ref, wf_ref, bf_ref, sel_ref, h_ref, qa_ref, ka_ref, carry_ref):
    tm = x_ref.shape[0]

    @pl.when(pl.program_id(1) == 0)
    def _():
        carry_ref[...] = jnp.zeros_like(carry_ref)

    xf = x_ref[...]
    ms = jnp.mean(xf * xf, axis=-1, keepdims=True)
    y = xf * lax.rsqrt(ms + EPS) * g_ref[...]
    y_hi = y.astype(BF16)
    h_ref[...] = y_hi

    wf = wf_ref[...]
    wf_hi = wf.astype(BF16).astype(F32)
    w2 = jnp.concatenate(
        [wf_hi, wf - wf_hi, jnp.zeros((LANES - 2 * FOX_HEADS, wf.shape[1]), F32)], axis=0).astype(BF16)
    z2 = lax.dot_general(y_hi, w2, NT_DIMS, preferred_element_type=F32)
    z = z2 + pltpu.roll(z2, LANES - FOX_HEADS, 1) + bf_ref[...]
    log_f = -LOG2_E * (jnp.maximum(-z, 0.0) + jnp.log1p(jnp.exp(-jnp.abs(z))))

    row = lax.broadcasted_iota(jnp.int32, (tm, tm), 0)
    col = lax.broadcasted_iota(jnp.int32, (tm, tm), 1)
    tri = jnp.where(row >= col, 1.0, 0.0).astype(BF16)
    f_hi, f_mid, f_lo = _split3(log_f)
    part = jnp.dot(tri, jnp.concatenate([f_hi, f_mid], axis=1).astype(BF16),
                   preferred_element_type=F32)
    c = (part[:, :LANES] + part[:, LANES:]
         + jnp.dot(tri, f_lo.astype(BF16), preferred_element_type=F32) + carry_ref[0:1, :])
    carry_ref[0:1, :] = c[tm - 1:tm, :]

    lane = lax.broadcasted_iota(jnp.int32, (tm, LANES), 1)
    c_hi, c_mid, c_lo = _split3(c)
    packed = jnp.where(
        lane < FOX_HEADS, c_hi,
        jnp.where(lane < 2 * FOX_HEADS, pltpu.roll(c_mid, FOX_HEADS, 1),
                  jnp.where(lane < ONE_LANE, pltpu.roll(c_lo, 2 * FOX_HEADS, 1),
                            jnp.where(lane < ONE_LANE + 3, 1.0, 0.0)))).astype(BF16)
    qa_ref[...] = packed
    ka_ref[...] = jnp.dot(packed, sel_ref[...], preferred_element_type=F32).astype(BF16)


def _norm1(x, g, w_t, f_row, bf, tm=1024):
    b, s, d = x.shape
    return pl.pallas_call(
        _norm1_kernel,
        grid=(b, s // tm),
        in_specs=[
            pl.BlockSpec((None, tm, d), lambda i, j: (i, j, 0)),
            pl.BlockSpec((1, d), lambda i, j: (0, 0)),
            pl.BlockSpec((FOX_HEADS, d), lambda i, j: (f_row // FOX_HEADS, 0)),
            pl.BlockSpec((1, LANES), lambda i, j: (0, 0)),
            pl.BlockSpec((LANES, FOX_HEADS * LANES), lambda i, j: (0, 0)),
        ],
        out_specs=[
            pl.BlockSpec((None, tm, d), lambda i, j: (i, j, 0)),
            pl.BlockSpec((None, tm, LANES), lambda i, j: (i, j, 0)),
            pl.BlockSpec((None, tm, FOX_HEADS * LANES), lambda i, j: (i, j, 0)),
        ],
        out_shape=[
            jax.ShapeDtypeStruct((b, s, d), BF16),
            jax.ShapeDtypeStruct((b, s, LANES), BF16),
            jax.ShapeDtypeStruct((b, s, FOX_HEADS * LANES), BF16),
        ],
        scratch_shapes=[pltpu.VMEM((SUBLANES, LANES), F32)],
        compiler_params=_params(2),
        name="norm1_decay",
    )(x, g, w_t, bf, _decay_selector())


PLAIN, GATE = 0, -1


def _proj_kernel(h_ref, w_ref, gain_ref, o_ref, wbf_ref, *, kinds):
    @pl.when(pl.program_id(1) == 0)
    def _():
        wbf_ref[...] = w_ref[...].astype(BF16)

    def tile(kind):
        acc = lax.dot_general(h_ref[...], wbf_ref[...], NT_DIMS, preferred_element_type=F32)
        if kind == PLAIN:
            o_ref[...] = acc.astype(BF16)
        elif kind == GATE:
            o_ref[...] = (0.5 * jnp.tanh(0.5 * acc) + 0.5).astype(BF16)
        else:
            _store_group_rms(o_ref, acc, gain_ref[...], kind)

    j = pl.program_id(0)
    for kind in sorted(set(kinds)):
        hit = functools.reduce(jnp.logical_or, [j == t for t, kd in enumerate(kinds) if kd == kind])
        pl.when(hit)(functools.partial(tile, kind))


def _proj(h, w_t, row_starts, kinds, gains, tm=1024, tn=COL_TILE):
    m, k = h.shape
    n_tiles = len(row_starts)
    assert all(r % SUBLANES == 0 for r in row_starts)

    def w_rows(j, i):
        start = jnp.int32(row_starts[0])
        for t in range(1, n_tiles):
            start = jnp.where(j == t, row_starts[t], start)
        return (pl.multiple_of(start, SUBLANES), 0)

    return pl.pallas_call(
        functools.partial(_proj_kernel, kinds=tuple(kinds)),
        grid=(n_tiles, m // tm),
        in_specs=[
            pl.BlockSpec((tm, k), lambda j, i: (i, 0)),
            pl.BlockSpec((pl.Element(tn), pl.Element(k)), w_rows),
            pl.BlockSpec((None, 1, tn), lambda j, i: (j, 0, 0)),
        ],
        out_specs=pl.BlockSpec((tm, tn), lambda j, i: (i, j)),
        out_shape=jax.ShapeDtypeStruct((m, n_tiles * tn), BF16),
        scratch_shapes=[pltpu.VMEM((tn, k), BF16)],
        compiler_params=_params(2),
        name="in_proj",
    )(h, w_t, gains.reshape(n_tiles, 1, tn))


def _gated_conv_tile(conv_ref, w_ref, carry_ref):
    tm = conv_ref.shape[0]
    k = w_ref.shape[1]
    u = conv_ref[:, k:2 * k].astype(F32) * conv_ref[:, 2 * k:3 * k].astype(F32)
    w = w_ref[...]
    taps = lambda u2, u1, u0: w[0:1, :] * u2 + w[1:2, :] * u1 + w[2:3, :] * u0
    conv = taps(pltpu.roll(u, 2, 0), pltpu.roll(u, 1, 0), u)
    top = u[0:SUBLANES, :]
    row = lax.broadcasted_iota(jnp.int32, (SUBLANES, 1), 0)
    prev1 = carry_ref[1:2, :]
    prev2 = carry_ref[0:1, :]
    top1 = jnp.where(row == 0, prev1, pltpu.roll(top, 1, 0))
    top2 = jnp.where(row == 0, prev2, jnp.where(row == 1, prev1, pltpu.roll(top, 2, 0)))
    conv = jnp.concatenate([taps(top2, top1, top), conv[SUBLANES:, :]], axis=0)
    carry_ref[0:2, :] = u[tm - 2:tm, :]
    return conv_ref[:, 0:k].astype(F32) * conv


def _fox_kernel(q_ref, qa_ref, k_ref, ka_ref, v_ref, o_ref, vt_ref, s_ref, m_ref, l_ref, acc_ref, *,
                t):
    n_tiles = q_ref.shape[0] // t
    assert n_tiles % 2 == 0
    d = FOX_HEAD_DIM
    n_heads = q_ref.shape[1] // d
    j = pl.program_id(2)

    @pl.when(j == 0)
    def _():
        for g in range(n_heads):
            for blk in range(v_ref.shape[0] // t):
                v = v_ref[blk * t:(blk + 1) * t, g * d:(g + 1) * d]
                vt_ref[g, blk, :d] = v.astype(F32).T.astype(BF16)
                vt_ref[g, blk, d:] = jnp.ones((vt_ref.shape[2] - d, t), BF16)

    def scores(tile, kb, g):
        q = jnp.concatenate([q_ref[tile * t:(tile + 1) * t, g * d:(g + 1) * d],
                             qa_ref[tile * t:(tile + 1) * t, :]], axis=1)
        rows = pl.ds(pl.multiple_of(kb * t, t), t)
        k = jnp.concatenate([k_ref[rows, g * d:(g + 1) * d], ka_ref[rows, g * d:(g + 1) * d]], axis=1)
        return lax.dot_general(k, q, NT_DIMS, preferred_element_type=F32)

    def reset(g):
        m_ref[g] = jnp.full(m_ref.shape[1:], -jnp.inf, F32)
        l_ref[g] = jnp.zeros(l_ref.shape[1:], F32)
        acc_ref[g] = jnp.zeros(acc_ref.shape[1:], F32)

    def update(g, m_new, alpha, pv):
        l_ref[g] = alpha * l_ref[g] + pv[d:d + 1]
        acc_ref[g] = alpha * acc_ref[g] + pv[:d]
        m_ref[g] = m_new

    def consume(kb, slot, g):
        s = s_ref[slot, g]
        m_prev = m_ref[g]
        m_new = jnp.maximum(m_prev, jnp.max(s, axis=0, keepdims=True))
        p = jnp.exp2(s - m_new)
        pv = jnp.dot(vt_ref[g, kb], p.astype(BF16), preferred_element_type=F32)
        update(g, m_new, jnp.exp2(m_prev - m_new), pv)

    def causal(s):
        kpos = lax.broadcasted_iota(jnp.int32, s.shape, 0)
        qpos = lax.broadcasted_iota(jnp.int32, s.shape, 1)
        return jnp.where(qpos >= kpos, s, -jnp.inf)

    def consume_diagonal(kb, slot, g):
        h = t // 2
        s_up = causal(s_ref[slot, g, :h, :])
        s_lo = causal(s_ref[slot, g, h:, h:])
        m_up = jnp.max(s_up, axis=0, keepdims=True)
        m_blk = jnp.concatenate(
            [m_up[:, :h], jnp.maximum(m_up[:, h:], jnp.max(s_lo, axis=0, keepdims=True))], axis=1)
        m_prev = m_ref[g]
        m_new = jnp.maximum(m_prev, m_blk)
        p_up = jnp.exp2(s_up - m_new).astype(BF16)
        p_lo = jnp.exp2(s_lo - m_new[:, h:]).astype(BF16)
        vt = vt_ref[g, kb]
        pv_up = jnp.dot(vt[:, :h], p_up, preferred_element_type=F32)
        pv_lo = jnp.dot(vt[:, h:], p_lo, preferred_element_type=F32)
        pv = jnp.concatenate([pv_up[:, :h], pv_up[:, h:] + pv_lo], axis=1)
        update(g, m_new, jnp.exp2(m_prev - m_new), pv)

    def advance(tile, kb, slot):
        for g in range(n_heads):
            s_ref[1 - slot, g] = scores(tile, kb + 1, g)
            consume(kb, slot, g)

    def emit(tile, g):
        o_ref[tile * t:(tile + 1) * t, g * d:(g + 1) * d] = (acc_ref[g] / l_ref[g]).T.astype(BF16)

    slot = 0
    for g in range(n_heads):
        reset(g)
        s_ref[slot, g] = scores(0, 0, g)
    for r in range(n_tiles):
        def pairs(pair, carry, r=r, slot=slot):
            advance(r, 2 * pair, slot)
            advance(r, 2 * pair + 1, 1 - slot)
            return carry

        lax.fori_loop(0, (n_tiles // 2) * j + r // 2, pairs, 0)
        diagonal = n_tiles * j + r
        if r % 2 == 1:
            advance(r, diagonal - 1, slot)
            slot = 1 - slot
        for g in range(n_heads):
            if r + 1 < n_tiles:
                s_ref[1 - slot, g] = scores(r + 1, 0, g)
            consume_diagonal(diagonal, slot, g)
            emit(r, g)
            if r + 1 < n_tiles:
                reset(g)
        slot = 1 - slot


def _fox(proj, qa, ka, q_col, k_col, v_col, t=512, tiles_per_step=2, heads_per_step=4):
    b, s, _ = proj.shape
    d = FOX_HEAD_DIM
    w = heads_per_step * d
    rows = tiles_per_step * t
    tiles = lambda col0: pl.BlockSpec((None, rows, w), lambda bi, h, j: (bi, j, col0 // w + h))
    seq = lambda col0: pl.BlockSpec((None, s, w), lambda bi, h, j: (bi, 0, col0 // w + h))
    return pl.pallas_call(
        functools.partial(_fox_kernel, t=t),
        grid=(b, FOX_HEADS // heads_per_step, s // rows),
        in_specs=[tiles(q_col), pl.BlockSpec((None, rows, LANES), lambda bi, h, j: (bi, j, 0)),
                  seq(k_col), seq(0), seq(v_col)],
        out_specs=tiles(0),
        out_shape=jax.ShapeDtypeStruct((b, s, FOX_HEADS * d), BF16),
        scratch_shapes=[pltpu.VMEM((heads_per_step, s // t, d + BF16_ROWS, t), BF16),
                        pltpu.VMEM((2, heads_per_step, t, t), F32),
                        pltpu.VMEM((heads_per_step, 1, t), F32),
                        pltpu.VMEM((heads_per_step, 1, t), F32),
                        pltpu.VMEM((heads_per_step, d, t), F32)],
        compiler_params=_params(3),
        name="fox_attention",
    )(proj, qa, proj, ka, proj)


def _memkv_kernel(mem_ref, g_ref, w_ref, kg_ref, o_ref):
    xf = mem_ref[...]
    ms = jnp.mean(xf * xf, axis=-1, keepdims=True)
    y = (xf * lax.rsqrt(ms + EPS) * g_ref[...]).astype(BF16)
    acc = jnp.dot(y, w_ref[...].astype(BF16), preferred_element_type=F32)

    @pl.when(pl.program_id(0) == 0)
    def _():
        _store_group_rms(o_ref, acc, kg_ref[...], MEM_HEAD_DIM)

    @pl.when(pl.program_id(0) == 1)
    def _():
        o_ref[...] = acc.astype(BF16)


def _memkv(mem2d, g, w, kg):
    m, d = mem2d.shape
    width = MEM_HEADS * MEM_HEAD_DIM
    return pl.pallas_call(
        _memkv_kernel,
        grid=(2,),
        in_specs=[
            pl.BlockSpec((m, d), lambda j: (0, 0)),
            pl.BlockSpec((1, d), lambda j: (0, 0)),
            pl.BlockSpec((d, width), lambda j: (0, j)),
            pl.BlockSpec((1, width), lambda j: (0, 0)),
        ],
        out_specs=pl.BlockSpec((m, width), lambda j: (0, j)),
        out_shape=jax.ShapeDtypeStruct((m, 2 * width), BF16),
        compiler_params=_params(1),
        name="mem_kv",
    )(mem2d, g, w, kg)


def _memattn_kernel(q_ref, kv_ref, o_ref):
    width = MEM_HEADS * MEM_HEAD_DIM
    for h in range(MEM_HEADS):
        lo, hi = h * MEM_HEAD_DIM, (h + 1) * MEM_HEAD_DIM
        q = q_ref[:, lo:hi]
        k = kv_ref[:, lo:hi]
        v = kv_ref[:, width + lo:width + hi]
        s = lax.dot_general(q, k, NT_DIMS, preferred_element_type=F32)
        p = jnp.exp(s - jnp.max(s, axis=-1, keepdims=True))
        l = jnp.sum(p, axis=-1, keepdims=True)
        o = jnp.dot(p.astype(BF16), v, preferred_element_type=F32) / l
        o_ref[:, lo:hi] = o.astype(BF16)


def _memattn(proj, q_col, kv, tm=1024):
    b, s, _ = proj.shape
    width = MEM_HEADS * MEM_HEAD_DIM
    n_mem = kv.shape[1]
    return pl.pallas_call(
        _memattn_kernel,
        grid=(b, s // tm),
        in_specs=[
            pl.BlockSpec((None, tm, width), lambda i, j: (i, j, q_col // width)),
            pl.BlockSpec((None, n_mem, 2 * width), lambda i, j: (i, 0, 0)),
        ],
        out_specs=pl.BlockSpec((None, tm, width), lambda i, j: (i, j, 0)),
        out_shape=jax.ShapeDtypeStruct((b, s, width), BF16),
        compiler_params=_params(2),
        name="mem_attention",
    )(proj, kv)


def _merge_kernel(conv_ref, cw_ref, b_ref, c_ref, wa_ref, wb_ref, wc_ref, g_ref,
                  o_ref, wabf_ref, wbbf_ref, wcbf_ref, carry_ref, *, tiles_per_seq):
    i = pl.program_id(1)
    tn = o_ref.shape[1]

    @pl.when(i == 0)
    def _():
        wabf_ref[...] = wa_ref[...].astype(BF16)
        wbbf_ref[...] = wb_ref[...].astype(BF16)
        wcbf_ref[...] = wc_ref[...].astype(BF16)

    @pl.when(i % tiles_per_seq == 0)
    def _():
        carry_ref[...] = jnp.zeros_like(carry_ref)

    y_conv = _gated_conv_tile(conv_ref, cw_ref, carry_ref).astype(BF16)
    oa = jnp.dot(y_conv, wabf_ref[...], preferred_element_type=F32)
    ob = jnp.dot(b_ref[...], wbbf_ref[...], preferred_element_type=F32)
    oc = jnp.dot(c_ref[...], wcbf_ref[...], preferred_element_type=F32)
    gate = lambda br: g_ref[:, br * tn:(br + 1) * tn].astype(F32)
    o_ref[...] = (gate(0) * oa + gate(1) * ob + gate(2) * oc).astype(BF16)


def _merge(proj, conv_col, conv_w, yb, yc, wa, wb, wc, seq_len, d_model, tm=512, tn=COL_TILE):
    m, k = yb.shape
    lhs = pl.BlockSpec((tm, k), lambda j, i: (i, 0))
    wsp = pl.BlockSpec((k, tn), lambda j, i: (0, j))
    return pl.pallas_call(
        functools.partial(_merge_kernel, tiles_per_seq=seq_len // tm),
        grid=(d_model // tn, m // tm),
        in_specs=[pl.BlockSpec((tm, 3 * k), lambda j, i: (i, conv_col // (3 * k))),
                  pl.BlockSpec((CONV_TAPS, k), lambda j, i: (0, 0)),
                  lhs, lhs, wsp, wsp, wsp,
                  pl.BlockSpec((tm, 3 * tn), lambda j, i: (i, j))],
        out_specs=pl.BlockSpec((tm, tn), lambda j, i: (i, j)),
        out_shape=jax.ShapeDtypeStruct((m, d_model), BF16),
        scratch_shapes=[pltpu.VMEM((k, tn), BF16)] * 3 + [pltpu.VMEM((SUBLANES, k), F32)],
        compiler_params=_params(2),
        name="branch_merge",
    )(proj, conv_w, yb, yc, wa, wb, wc, proj)


def _outproj_kernel(a_ref, w_ref, x_ref, g_ref, x2_ref, h2_ref, wbf_ref):
    @pl.when(pl.program_id(0) == 0)
    def _():
        wbf_ref[...] = w_ref[...].astype(BF16)

    x2 = x_ref[...] + jnp.dot(a_ref[...], wbf_ref[...], preferred_element_type=F32)
    x2_ref[...] = x2
    ms = jnp.mean(x2 * x2, axis=-1, keepdims=True)
    h2_ref[...] = (x2 * lax.rsqrt(ms + EPS) * g_ref[...]).astype(BF16)


def _outproj(a, w, x2d, g, tm=512):
    m, d = x2d.shape
    row = pl.BlockSpec((tm, d), lambda i: (i, 0))
    return pl.pallas_call(
        _outproj_kernel,
        grid=(m // tm,),
        in_specs=[row, pl.BlockSpec((d, d), lambda i: (0, 0), pipeline_mode=pl.Buffered(1)), row,
                  pl.BlockSpec((1, d), lambda i: (0, 0))],
        out_specs=[row, row],
        out_shape=[jax.ShapeDtypeStruct((m, d), F32), jax.ShapeDtypeStruct((m, d), BF16)],
        scratch_shapes=[pltpu.VMEM((d, d), BF16)],
        compiler_params=_params(1),
        name="out_proj_norm2",
    )(a, w, x2d, g)


def _mlp_kernel(h_ref, wu_ref, wd_ref, x_ref, o_ref):
    f = pl.program_id(1)
    rows = x_ref.shape[0]

    def chunk(first):
        a = jnp.dot(h_ref[...], wu_ref[...].astype(BF16), preferred_element_type=F32)
        a = jnp.square(jnp.maximum(a, 0.0)).astype(BF16)
        down = jnp.dot(a, wd_ref[...].astype(BF16), preferred_element_type=F32)
        if first:
            o_ref[...] = down
        else:
            o_ref[...] += down
        slab = pl.ds(pl.multiple_of(f * rows, rows), rows)
        o_ref[slab, :] += x_ref[...]

    pl.when(f == 0)(functools.partial(chunk, True))
    pl.when(f != 0)(functools.partial(chunk, False))


def _mlp(h2, w_up, w_down, x2, tm=1024, tf=512):
    m, d = h2.shape
    d_ff = w_up.shape[1]
    nf = d_ff // tf
    rows = tm // nf
    return pl.pallas_call(
        _mlp_kernel,
        grid=(m // tm, nf),
        in_specs=[
            pl.BlockSpec((tm, d), lambda i, f: (i, 0)),
            pl.BlockSpec((d, tf), lambda i, f: (0, f)),
            pl.BlockSpec((tf, d), lambda i, f: (f, 0)),
            pl.BlockSpec((rows, d), lambda i, f: (i * nf + f, 0)),
        ],
        out_specs=pl.BlockSpec((tm, d), lambda i, f: (i, 0)),
        out_shape=jax.ShapeDtypeStruct((m, d), F32),
        compiler_params=_params(2),
        name="relu2_mlp",
    )(h2, w_up, w_down, x2)


def kernel(x, mem, norm1_g, w_in, b_f, conv_w, fox_q_g, fox_k_g, mem_norm_g, w_mem_kv, mem_q_g, mem_k_g, w_conv_out, w_fox_out, w_mem_out, w_out, norm2_g, w_up, w_down):
    b, s, d = x.shape
    m = b * s
    conv_width = conv_w.shape[1]
    fox_width = FOX_HEADS * FOX_HEAD_DIM
    mem_width = MEM_HEADS * MEM_HEAD_DIM
    tile = COL_TILE
    q_row = 3 * conv_width
    k_row = q_row + fox_width
    v_row = k_row + fox_width
    f_row = v_row + fox_width
    mq_row = f_row + FOX_HEADS
    gate_row = mq_row + mem_width
    n_gate = N_BRANCHES * d
    ch_tiles = d // tile
    gate_rows = [gate_row + br * d + n * tile for n in range(ch_tiles) for br in range(N_BRANCHES)]
    rest_rows = [r for r in range(0, f_row, tile)] + [r for r in range(mq_row, gate_row, tile)]
    row_starts = tuple(gate_rows + rest_rows)
    conv_col, q_col, k_col, v_col, mq_col = (n_gate + r for r in (0, q_row, k_row, v_row, f_row))
    kinds = ([GATE] * len(gate_rows) + [PLAIN] * (q_row // tile) + [FOX_HEAD_DIM] * (2 * fox_width // tile)
             + [PLAIN] * (fox_width // tile) + [MEM_HEAD_DIM] * (mem_width // tile))
    gains = jnp.ones((len(row_starts) * tile,), F32)
    gains = gains.at[q_col:k_col].set(jnp.tile(fox_q_g * (LOG2_E / math.sqrt(FOX_HEAD_DIM)), FOX_HEADS))
    gains = gains.at[k_col:v_col].set(jnp.tile(fox_k_g, FOX_HEADS))
    gains = gains.at[mq_col:].set(jnp.tile(mem_q_g * (1.0 / math.sqrt(MEM_HEAD_DIM)), MEM_HEADS))

    row = lambda v: v.reshape(1, -1)
    bf = jnp.pad(b_f, (0, LANES - FOX_HEADS)).reshape(1, LANES)
    w_t = w_in.T

    h1, qa, ka = _norm1(x, row(norm1_g), w_t, f_row, bf)
    proj = _proj(h1.reshape(m, d), w_t, row_starts, kinds, gains.reshape(-1, tile))
    proj3 = proj.reshape(b, s, -1)

    y_fox = _fox(proj3, qa, ka, q_col, k_col, v_col)
    kv = _memkv(mem.reshape(-1, d), row(mem_norm_g), w_mem_kv, row(jnp.tile(mem_k_g, MEM_HEADS)))
    y_mem = _memattn(proj3, mq_col, kv.reshape(b, -1, 2 * mem_width))

    merged = _merge(proj, conv_col, conv_w, y_fox.reshape(m, -1), y_mem.reshape(m, -1),
                    w_conv_out, w_fox_out, w_mem_out, s, d)
    x2, h2 = _outproj(merged, w_out, x.reshape(m, d), row(norm2_g))
    out = _mlp(h2, w_up, w_down, x2)
    return out.reshape(b, s, d)
```

```python
import functools
import math

import jax
import jax.numpy as jnp
import numpy as np
from jax import lax
from jax.experimental import pallas as pl
from jax.experimental.pallas import tpu as pltpu

F32 = jnp.float32
BF16 = jnp.bfloat16

EPS = 1e-6
LOG2_E = math.log2(math.e)
LANES = 128
SUBLANES = 8
BF16_ROWS = 2 * SUBLANES
CONV_TAPS = 3
FOX_HEADS = 8
FOX_HEAD_DIM = 128
MEM_HEADS = 4
MEM_HEAD_DIM = 256
N_BRANCHES = 3

VMEM_LIMIT_BYTES = 56 * 1024 * 1024
COL_TILE = 1024
NT_DIMS = (((1,), (1,)), ((), ()))


def _params(n_axes):
    return pltpu.CompilerParams(
        dimension_semantics=("arbitrary",) * n_axes,
        vmem_limit_bytes=VMEM_LIMIT_BYTES)


def _store_group_rms(o_ref, a, gain, width):
    for s in range(0, a.shape[1], width):
        blk = a[:, s:s + width]
        ms = jnp.mean(blk * blk, axis=-1, keepdims=True)
        o_ref[:, s:s + width] = (blk * lax.rsqrt(ms + EPS) * gain[:, s:s + width]).astype(o_ref.dtype)


def _split3(c):
    hi = c.astype(BF16).astype(F32)
    r1 = c - hi
    mid = r1.astype(BF16).astype(F32)
    lo = (r1 - mid).astype(BF16).astype(F32)
    return hi, mid, lo


ONE_LANE = 3 * FOX_HEADS


def _decay_selector():
    sel = np.zeros((LANES, FOX_HEADS, LANES), np.float32)
    for hd in range(FOX_HEADS):
        for piece in range(3):
            sel[ONE_LANE, hd, piece * FOX_HEADS + hd] = 1.0
            sel[piece * FOX_HEADS + hd, hd, ONE_LANE + piece] = -1.0
    return jnp.asarray(sel.reshape(LANES, -1), BF16)


def _norm1_kernel(x_ref, g_ref, wf_ref, bf_ref, sel_ref, h_ref, qa_ref, ka_ref, carry_ref):
    tm = x_ref.shape[0]

    @pl.when(pl.program_id(1) == 0)
    def _():
        carry_ref[...] = jnp.zeros_like(carry_ref)

    xf = x_ref[...]
    ms = jnp.mean(xf * xf, axis=-1, keepdims=True)
    y = xf * lax.rsqrt(ms + EPS) * g_ref[...]
    y_hi = y.astype(BF16)
    h_ref[...] = y_hi

    wf = wf_ref[...]
    wf_hi = wf.astype(BF16).astype(F32)
    w2 = jnp.concatenate(
        [wf_hi, wf - wf_hi, jnp.zeros((LANES - 2 * FOX_HEADS, wf.shape[1]), F32)], axis=0).astype(BF16)
    z2 = lax.dot_general(y_hi, w2, NT_DIMS, preferred_element_type=F32)
    z = z2 + pltpu.roll(z2, LANES - FOX_HEADS, 1) + bf_ref[...]
    log_f = -LOG2_E * (jnp.maximum(-z, 0.0) + jnp.log1p(jnp.exp(-jnp.abs(z))))

    row = lax.broadcasted_iota(jnp.int32, (tm, tm), 0)
    col = lax.broadcasted_iota(jnp.int32, (tm, tm), 1)
    tri = jnp.where(row >= col, 1.0, 0.0).astype(BF16)
    f_hi, f_mid, f_lo = _split3(log_f)
    part = jnp.dot(tri, jnp.concatenate([f_hi, f_mid], axis=1).astype(BF16),
                   preferred_element_type=F32)
    c = (part[:, :LANES] + part[:, LANES:]
         + jnp.dot(tri, f_lo.astype(BF16), preferred_element_type=F32) + carry_ref[0:1, :])
    carry_ref[0:1, :] = c[tm - 1:tm, :]

    lane = lax.broadcasted_iota(jnp.int32, (tm, LANES), 1)
    c_hi, c_mid, c_lo = _split3(c)
    packed = jnp.where(
        lane < FOX_HEADS, c_hi,
        jnp.where(lane < 2 * FOX_HEADS, pltpu.roll(c_mid, FOX_HEADS, 1),
                  jnp.where(lane < ONE_LANE, pltpu.roll(c_lo, 2 * FOX_HEADS, 1),
                            jnp.where(lane < ONE_LANE + 3, 1.0, 0.0)))).astype(BF16)
    qa_ref[...] = packed
    ka_ref[...] = jnp.dot(packed, sel_ref[...], preferred_element_type=F32).astype(BF16)


def _norm1(x, g, w_t, f_row, bf, tm=1024):
    b, s, d = x.shape
    return pl.pallas_call(
        _norm1_kernel,
        grid=(b, s // tm),
        in_specs=[
            pl.BlockSpec((None, tm, d), lambda i, j: (i, j, 0)),
            pl.BlockSpec((1, d), lambda i, j: (0, 0)),
            pl.BlockSpec((FOX_HEADS, d), lambda i, j: (f_row // FOX_HEADS, 0)),
            pl.BlockSpec((1, LANES), lambda i, j: (0, 0)),
            pl.BlockSpec((LANES, FOX_HEADS * LANES), lambda i, j: (0, 0)),
        ],
        out_specs=[
            pl.BlockSpec((None, tm, d), lambda i, j: (i, j, 0)),
            pl.BlockSpec((None, tm, LANES), lambda i, j: (i, j, 0)),
            pl.BlockSpec((None, tm, FOX_HEADS * LANES), lambda i, j: (i, j, 0)),
        ],
        out_shape=[
            jax.ShapeDtypeStruct((b, s, d), BF16),
            jax.ShapeDtypeStruct((b, s, LANES), BF16),
            jax.ShapeDtypeStruct((b, s, FOX_HEADS * LANES), BF16),
        ],
        scratch_shapes=[pltpu.VMEM((SUBLANES, LANES), F32)],
        compiler_params=_params(2),
        name="norm1_decay",
    )(x, g, w_t, bf, _decay_selector())


PLAIN, GATE = 0, -1


def _proj_kernel(h_ref, w_ref, gain_ref, o_ref, wbf_ref, *, kinds):
    @pl.when(pl.program_id(1) == 0)
    def _():
        wbf_ref[...] = w_ref[...].astype(BF16)

    def tile(kind):
        acc = lax.dot_general(h_ref[...], wbf_ref[...], NT_DIMS, preferred_element_type=F32)
        if kind == PLAIN:
            o_ref[...] = acc.astype(BF16)
        elif kind == GATE:
            o_ref[...] = (0.5 * jnp.tanh(0.5 * acc) + 0.5).astype(BF16)
        else:
            _store_group_rms(o_ref, acc, gain_ref[...], kind)

    j = pl.program_id(0)
    for kind in sorted(set(kinds)):
        hit = functools.reduce(jnp.logical_or, [j == t for t, kd in enumerate(kinds) if kd == kind])
        pl.when(hit)(functools.partial(tile, kind))


def _proj(h, w_t, row_starts, kinds, gains, tm=1024, tn=COL_TILE):
    m, k = h.shape
    n_tiles = len(row_starts)
    assert all(r % SUBLANES == 0 for r in row_starts)

    def w_rows(j, i):
        start = jnp.int32(row_starts[0])
        for t in range(1, n_tiles):
            start = jnp.where(j == t, row_starts[t], start)
        return (pl.multiple_of(start, SUBLANES), 0)

    return pl.pallas_call(
        functools.partial(_proj_kernel, kinds=tuple(kinds)),
        grid=(n_tiles, m // tm),
        in_specs=[
            pl.BlockSpec((tm, k), lambda j, i: (i, 0)),
            pl.BlockSpec((pl.Element(tn), pl.Element(k)), w_rows),
            pl.BlockSpec((None, 1, tn), lambda j, i: (j, 0, 0)),
        ],
        out_specs=pl.BlockSpec((tm, tn), lambda j, i: (i, j)),
        out_shape=jax.ShapeDtypeStruct((m, n_tiles * tn), BF16),
        scratch_shapes=[pltpu.VMEM((tn, k), BF16)],
        compiler_params=_params(2),
        name="in_proj",
    )(h, w_t, gains.reshape(n_tiles, 1, tn))


def _gated_conv_tile(conv_ref, w_ref, carry_ref):
    tm = conv_ref.shape[0]
    k = w_ref.shape[1]
    u = conv_ref[:, k:2 * k].astype(F32) * conv_ref[:, 2 * k:3 * k].astype(F32)
    w = w_ref[...]
    taps = lambda u2, u1, u0: w[0:1, :] * u2 + w[1:2, :] * u1 + w[2:3, :] * u0
    conv = taps(pltpu.roll(u, 2, 0), pltpu.roll(u, 1, 0), u)
    top = u[0:SUBLANES, :]
    row = lax.broadcasted_iota(jnp.int32, (SUBLANES, 1), 0)
    prev1 = carry_ref[1:2, :]
    prev2 = carry_ref[0:1, :]
    top1 = jnp.where(row == 0, prev1, pltpu.roll(top, 1, 0))
    top2 = jnp.where(row == 0, prev2, jnp.where(row == 1, prev1, pltpu.roll(top, 2, 0)))
    conv = jnp.concatenate([taps(top2, top1, top), conv[SUBLANES:, :]], axis=0)
    carry_ref[0:2, :] = u[tm - 2:tm, :]
    return conv_ref[:, 0:k].astype(F32) * conv


def _fox_kernel(q_ref, qa_ref, k_ref, ka_ref, v_ref, o_ref, vt_ref, s_ref, m_ref, l_ref, acc_ref, *,
                t):
    n_tiles = q_ref.shape[0] // t
    assert n_tiles % 2 == 0
    d = FOX_HEAD_DIM
    n_heads = q_ref.shape[1] // d
    j = pl.program_id(2)

    @pl.when(j == 0)
    def _():
        for g in range(n_heads):
            for blk in range(v_ref.shape[0] // t):
                v = v_ref[blk * t:(blk + 1) * t, g * d:(g + 1) * d]
                vt_ref[g, blk, :d] = v.astype(F32).T.astype(BF16)
                vt_ref[g, blk, d:] = jnp.ones((vt_ref.shape[2] - d, t), BF16)

    def scores(tile, kb, g):
        q = jnp.concatenate([q_ref[tile * t:(tile + 1) * t, g * d:(g + 1) * d],
                             qa_ref[tile * t:(tile + 1) * t, :]], axis=1)
        rows = pl.ds(pl.multiple_of(kb * t, t), t)
        k = jnp.concatenate([k_ref[rows, g * d:(g + 1) * d], ka_ref[rows, g * d:(g + 1) * d]], axis=1)
        return lax.dot_general(k, q, NT_DIMS, preferred_element_type=F32)

    def reset(g):
        m_ref[g] = jnp.full(m_ref.shape[1:], -jnp.inf, F32)
        l_ref[g] = jnp.zeros(l_ref.shape[1:], F32)
        acc_ref[g] = jnp.zeros(acc_ref.shape[1:], F32)

    def update(g, m_new, alpha, pv):
        l_ref[g] = alpha * l_ref[g] + pv[d:d + 1]
        acc_ref[g] = alpha * acc_ref[g] + pv[:d]
        m_ref[g] = m_new

    def consume(kb, slot, g):
        s = s_ref[slot, g]
        m_prev = m_ref[g]
        m_new = jnp.maximum(m_prev, jnp.max(s, axis=0, keepdims=True))
        p = jnp.exp2(s - m_new)
        pv = jnp.dot(vt_ref[g, kb], p.astype(BF16), preferred_element_type=F32)
        update(g, m_new, jnp.exp2(m_prev - m_new), pv)

    def causal(s):
        kpos = lax.broadcasted_iota(jnp.int32, s.shape, 0)
        qpos = lax.broadcasted_iota(jnp.int32, s.shape, 1)
        return jnp.where(qpos >= kpos, s, -jnp.inf)

    def consume_diagonal(kb, slot, g):
        h = t // 2
        s_up = causal(s_ref[slot, g, :h, :])
        s_lo = causal(s_ref[slot, g, h:, h:])
        m_up = jnp.max(s_up, axis=0, keepdims=True)
        m_blk = jnp.concatenate(
            [m_up[:, :h], jnp.maximum(m_up[:, h:], jnp.max(s_lo, axis=0, keepdims=True))], axis=1)
        m_prev = m_ref[g]
        m_new = jnp.maximum(m_prev, m_blk)
        p_up = jnp.exp2(s_up - m_new).astype(BF16)
        p_lo = jnp.exp2(s_lo - m_new[:, h:]).astype(BF16)
        vt = vt_ref[g, kb]
        pv_up = jnp.dot(vt[:, :h], p_up, preferred_element_type=F32)
        pv_lo = jnp.dot(vt[:, h:], p_lo, preferred_element_type=F32)
        pv = jnp.concatenate([pv_up[:, :h], pv_up[:, h:] + pv_lo], axis=1)
        update(g, m_new, jnp.exp2(m_prev - m_new), pv)

    def advance(tile, kb, slot):
        for g in range(n_heads):
            s_ref[1 - slot, g] = scores(tile, kb + 1, g)
            consume(kb, slot, g)

    def emit(tile, g):
        o_ref[tile * t:(tile + 1) * t, g * d:(g + 1) * d] = (acc_ref[g] / l_ref[g]).T.astype(BF16)

    slot = 0
    for g in range(n_heads):
        reset(g)
        s_ref[slot, g] = scores(0, 0, g)
    for r in range(n_tiles):
        def pairs(pair, carry, r=r, slot=slot):
            advance(r, 2 * pair, slot)
            advance(r, 2 * pair + 1, 1 - slot)
            return carry

        lax.fori_loop(0, (n_tiles // 2) * j + r // 2, pairs, 0)
        diagonal = n_tiles * j + r
        if r % 2 == 1:
            advance(r, diagonal - 1, slot)
            slot = 1 - slot
        for g in range(n_heads):
            if r + 1 < n_tiles:
                s_ref[1 - slot, g] = scores(r + 1, 0, g)
            consume_diagonal(diagonal, slot, g)
            emit(r, g)
            if r + 1 < n_tiles:
                reset(g)
        slot = 1 - slot


def _fox(proj, qa, ka, q_col, k_col, v_col, t=512, tiles_per_step=2, heads_per_step=4):
    b, s, _ = proj.shape
    d = FOX_HEAD_DIM
    w = heads_per_step * d
    rows = tiles_per_step * t
    tiles = lambda col0: pl.BlockSpec((None, rows, w), lambda bi, h, j: (bi, j, col0 // w + h))
    seq = lambda col0: pl.BlockSpec((None, s, w), lambda bi, h, j: (bi, 0, col0 // w + h))
    return pl.pallas_call(
        functools.partial(_fox_kernel, t=t),
        grid=(b, FOX_HEADS // heads_per_step, s // rows),
        in_specs=[tiles(q_col), pl.BlockSpec((None, rows, LANES), lambda bi, h, j: (bi, j, 0)),
                  seq(k_col), seq(0), seq(v_col)],
        out_specs=tiles(0),
        out_shape=jax.ShapeDtypeStruct((b, s, FOX_HEADS * d), BF16),
        scratch_shapes=[pltpu.VMEM((heads_per_step, s // t, d + BF16_ROWS, t), BF16),
                        pltpu.VMEM((2, heads_per_step, t, t), F32),
                        pltpu.VMEM((heads_per_step, 1, t), F32),
                        pltpu.VMEM((heads_per_step, 1, t), F32),
                        pltpu.VMEM((heads_per_step, d, t), F32)],
        compiler_params=_params(3),
        name="fox_attention",
    )(proj, qa, proj, ka, proj)


def _memkv_kernel(mem_ref, g_ref, w_ref, kg_ref, o_ref):
    xf = mem_ref[...]
    ms = jnp.mean(xf * xf, axis=-1, keepdims=True)
    y = (xf * lax.rsqrt(ms + EPS) * g_ref[...]).astype(BF16)
    acc = jnp.dot(y, w_ref[...].astype(BF16), preferred_element_type=F32)

    @pl.when(pl.program_id(0) == 0)
    def _():
        _store_group_rms(o_ref, acc, kg_ref[...], MEM_HEAD_DIM)

    @pl.when(pl.program_id(0) == 1)
    def _():
        o_ref[...] = acc.astype(BF16)


def _memkv(mem2d, g, w, kg):
    m, d = mem2d.shape
    width = MEM_HEADS * MEM_HEAD_DIM
    return pl.pallas_call(
        _memkv_kernel,
        grid=(2,),
        in_specs=[
            pl.BlockSpec((m, d), lambda j: (0, 0)),
            pl.BlockSpec((1, d), lambda j: (0, 0)),
            pl.BlockSpec((d, width), lambda j: (0, j)),
            pl.BlockSpec((1, width), lambda j: (0, 0)),
        ],
        out_specs=pl.BlockSpec((m, width), lambda j: (0, j)),
        out_shape=jax.ShapeDtypeStruct((m, 2 * width), BF16),
        compiler_params=_params(1),
        name="mem_kv",
    )(mem2d, g, w, kg)


def _memattn_kernel(q_ref, kv_ref, o_ref):
    width = MEM_HEADS * MEM_HEAD_DIM
    for h in range(MEM_HEADS):
        lo, hi = h * MEM_HEAD_DIM, (h + 1) * MEM_HEAD_DIM
        q = q_ref[:, lo:hi]
        k = kv_ref[:, lo:hi]
        v = kv_ref[:, width + lo:width + hi]
        s = lax.dot_general(q, k, NT_DIMS, preferred_element_type=F32)
        p = jnp.exp(s - jnp.max(s, axis=-1, keepdims=True))
        l = jnp.sum(p, axis=-1, keepdims=True)
        o = jnp.dot(p.astype(BF16), v, preferred_element_type=F32) / l
        o_ref[:, lo:hi] = o.astype(BF16)


def _memattn(proj, q_col, kv, tm=1024):
    b, s, _ = proj.shape
    width = MEM_HEADS * MEM_HEAD_DIM
    n_mem = kv.shape[1]
    return pl.pallas_call(
        _memattn_kernel,
        grid=(b, s // tm),
        in_specs=[
            pl.BlockSpec((None, tm, width), lambda i, j: (i, j, q_col // width)),
            pl.BlockSpec((None, n_mem, 2 * width), lambda i, j: (i, 0, 0)),
        ],
        out_specs=pl.BlockSpec((None, tm, width), lambda i, j: (i, j, 0)),
        out_shape=jax.ShapeDtypeStruct((b, s, width), BF16),
        compiler_params=_params(2),
        name="mem_attention",
    )(proj, kv)


def _merge_kernel(conv_ref, cw_ref, b_ref, c_ref, wa_ref, wb_ref, wc_ref, g_ref,
                  o_ref, wabf_ref, wbbf_ref, wcbf_ref, carry_ref, *, tiles_per_seq):
    i = pl.program_id(1)
    tn = o_ref.shape[1]

    @pl.when(i == 0)
    def _():
        wabf_ref[...] = wa_ref[...].astype(BF16)
        wbbf_ref[...] = wb_ref[...].astype(BF16)
        wcbf_ref[...] = wc_ref[...].astype(BF16)

    @pl.when(i % tiles_per_seq == 0)
    def _():
        carry_ref[...] = jnp.zeros_like(carry_ref)

    y_conv = _gated_conv_tile(conv_ref, cw_ref, carry_ref).astype(BF16)
    oa = jnp.dot(y_conv, wabf_ref[...], preferred_element_type=F32)
    ob = jnp.dot(b_ref[...], wbbf_ref[...], preferred_element_type=F32)
    oc = jnp.dot(c_ref[...], wcbf_ref[...], preferred_element_type=F32)
    gate = lambda br: g_ref[:, br * tn:(br + 1) * tn].astype(F32)
    o_ref[...] = (gate(0) * oa + gate(1) * ob + gate(2) * oc).astype(BF16)


def _merge(proj, conv_col, conv_w, yb, yc, wa, wb, wc, seq_len, d_model, tm=512, tn=COL_TILE):
    m, k = yb.shape
    lhs = pl.BlockSpec((tm, k), lambda j, i: (i, 0))
    wsp = pl.BlockSpec((k, tn), lambda j, i: (0, j))
    body = functools.partial(_merge_kernel, tiles_per_seq=seq_len // tm)
    inner = pltpu.emit_pipeline(
        body,
        grid=(d_model // tn, m // tm),
        in_specs=[pl.BlockSpec((tm, 3 * k), lambda j, i: (i, conv_col // (3 * k))),
                  pl.BlockSpec((CONV_TAPS, k), lambda j, i: (0, 0)),
                  lhs, lhs, wsp, wsp, wsp,
                  pl.BlockSpec((tm, 3 * tn), lambda j, i: (i, j))],
        out_specs=[pl.BlockSpec((tm, tn), lambda j, i: (i, j))],
    )

    def outer(*refs):
        inner(*refs[:9], scratches=refs[9:])

    any_spec = pl.BlockSpec(memory_space=pl.ANY)
    return pl.pallas_call(
        outer,
        in_specs=[any_spec] * 8,
        out_specs=any_spec,
        out_shape=jax.ShapeDtypeStruct((m, d_model), BF16),
        scratch_shapes=[pltpu.VMEM((k, tn), BF16)] * 3 + [pltpu.VMEM((SUBLANES, k), F32)],
        compiler_params=pltpu.CompilerParams(vmem_limit_bytes=VMEM_LIMIT_BYTES),
        name="branch_merge",
    )(proj, conv_w, yb, yc, wa, wb, wc, proj)


def _outproj_kernel(a_ref, w_ref, x_ref, g_ref, x2_ref, h2_ref, wbf_ref):
    @pl.when(pl.program_id(0) == 0)
    def _():
        wbf_ref[...] = w_ref[...].astype(BF16)

    x2 = x_ref[...] + jnp.dot(a_ref[...], wbf_ref[...], preferred_element_type=F32)
    x2_ref[...] = x2
    ms = jnp.mean(x2 * x2, axis=-1, keepdims=True)
    h2_ref[...] = (x2 * lax.rsqrt(ms + EPS) * g_ref[...]).astype(BF16)


def _outproj(a, w, x2d, g, tm=512):
    m, d = x2d.shape
    row = pl.BlockSpec((tm, d), lambda i: (i, 0))
    return pl.pallas_call(
        _outproj_kernel,
        grid=(m // tm,),
        in_specs=[row, pl.BlockSpec((d, d), lambda i: (0, 0), pipeline_mode=pl.Buffered(1)), row,
                  pl.BlockSpec((1, d), lambda i: (0, 0))],
        out_specs=[row, row],
        out_shape=[jax.ShapeDtypeStruct((m, d), F32), jax.ShapeDtypeStruct((m, d), BF16)],
        scratch_shapes=[pltpu.VMEM((d, d), BF16)],
        compiler_params=_params(1),
        name="out_proj_norm2",
    )(a, w, x2d, g)


def _mlp_kernel(h_ref, wu_ref, wd_ref, x_ref, o_ref):
    f = pl.program_id(1)
    rows = x_ref.shape[0]

    @pl.when(f == 0)
    def _():
        o_ref[...] = jnp.zeros_like(o_ref)

    a = jnp.dot(h_ref[...], wu_ref[...].astype(BF16), preferred_element_type=F32)
    a = jnp.square(jnp.maximum(a, 0.0)).astype(BF16)
    o_ref[...] += jnp.dot(a, wd_ref[...].astype(BF16), preferred_element_type=F32)
    slab = pl.ds(pl.multiple_of(f * rows, rows), rows)
    o_ref[slab, :] += x_ref[...]


def _mlp(h2, w_up, w_down, x2, tm=1024, tf=512):
    m, d = h2.shape
    d_ff = w_up.shape[1]
    nf = d_ff // tf
    rows = tm // nf
    return pl.pallas_call(
        _mlp_kernel,
        grid=(m // tm, nf),
        in_specs=[
            pl.BlockSpec((tm, d), lambda i, f: (i, 0)),
            pl.BlockSpec((d, tf), lambda i, f: (0, f)),
            pl.BlockSpec((tf, d), lambda i, f: (f, 0)),
            pl.BlockSpec((rows, d), lambda i, f: (i * nf + f, 0)),
        ],
        out_specs=pl.BlockSpec((tm, d), lambda i, f: (i, 0)),
        out_shape=jax.ShapeDtypeStruct((m, d), F32),
        compiler_params=_params(2),
        name="relu2_mlp",
    )(h2, w_up, w_down, x2)


def kernel(x, mem, norm1_g, w_in, b_f, conv_w, fox_q_g, fox_k_g, mem_norm_g, w_mem_kv, mem_q_g, mem_k_g, w_conv_out, w_fox_out, w_mem_out, w_out, norm2_g, w_up, w_down):
    b, s, d = x.shape
    m = b * s
    conv_width = conv_w.shape[1]
    fox_width = FOX_HEADS * FOX_HEAD_DIM
    mem_width = MEM_HEADS * MEM_HEAD_DIM
    tile = COL_TILE
    q_row = 3 * conv_width
    k_row = q_row + fox_width
    v_row = k_row + fox_width
    f_row = v_row + fox_width
    mq_row = f_row + FOX_HEADS
    gate_row = mq_row + mem_width
    n_gate = N_BRANCHES * d
    ch_tiles = d // tile
    gate_rows = [gate_row + br * d + n * tile for n in range(ch_tiles) for br in range(N_BRANCHES)]
    rest_rows = [r for r in range(0, f_row, tile)] + [r for r in range(mq_row, gate_row, tile)]
    row_starts = tuple(gate_rows + rest_rows)
    conv_col, q_col, k_col, v_col, mq_col = (n_gate + r for r in (0, q_row, k_row, v_row, f_row))
    kinds = ([GATE] * len(gate_rows) + [PLAIN] * (q_row // tile) + [FOX_HEAD_DIM] * (2 * fox_width // tile)
             + [PLAIN] * (fox_width // tile) + [MEM_HEAD_DIM] * (mem_width // tile))
    gains = jnp.ones((len(row_starts) * tile,), F32)
    gains = gains.at[q_col:k_col].set(jnp.tile(fox_q_g * (LOG2_E / math.sqrt(FOX_HEAD_DIM)), FOX_HEADS))
    gains = gains.at[k_col:v_col].set(jnp.tile(fox_k_g, FOX_HEADS))
    gains = gains.at[mq_col:].set(jnp.tile(mem_q_g * (1.0 / math.sqrt(MEM_HEAD_DIM)), MEM_HEADS))

    row = lambda v: v.reshape(1, -1)
    bf = jnp.pad(b_f, (0, LANES - FOX_HEADS)).reshape(1, LANES)
    w_t = w_in.T

    h1, qa, ka = _norm1(x, row(norm1_g), w_t, f_row, bf)
    proj = _proj(h1.reshape(m, d), w_t, row_starts, kinds, gains.reshape(-1, tile))
    proj3 = proj.reshape(b, s, -1)

    y_fox = _fox(proj3, qa, ka, q_col, k_col, v_col)
    kv = _memkv(mem.reshape(-1, d), row(mem_norm_g), w_mem_kv, row(jnp.tile(mem_k_g, MEM_HEADS)))
    y_mem = _memattn(proj3, mq_col, kv.reshape(b, -1, 2 * mem_width))

    merged = _merge(proj, conv_col, conv_w, y_fox.reshape(m, -1), y_mem.reshape(m, -1),
                    w_conv_out, w_fox_out, w_mem_out, s, d)
    x2, h2 = _outproj(merged, w_out, x.reshape(m, d), row(norm2_g))
    out = _mlp(h2, w_up, w_down, x2)
    return out.reshape(b, s, d)
```

```python
import functools
import math

import jax
import jax.numpy as jnp
import numpy as np
from jax import lax
from jax.experimental import pallas as pl
from jax.experimental.pallas import tpu as pltpu

F32 = jnp.float32
BF16 = jnp.bfloat16

EPS = 1e-6
LOG2_E = math.log2(math.e)
LANES = 128
SUBLANES = 8
BF16_ROWS = 2 * SUBLANES
CONV_TAPS = 3
FOX_HEADS = 8
FOX_HEAD_DIM = 128
MEM_HEADS = 4
MEM_HEAD_DIM = 256
N_BRANCHES = 3

VMEM_LIMIT_BYTES = 56 * 1024 * 1024
COL_TILE = 1024
NT_DIMS = (((1,), (1,)), ((), ()))


def _params(n_axes):
    return pltpu.CompilerParams(
        dimension_semantics=("arbitrary",) * n_axes,
        vmem_limit_bytes=VMEM_LIMIT_BYTES)


def _store_group_rms(o_ref, a, gain, width):
    for s in range(0, a.shape[1], width):
        blk = a[:, s:s + width]
        ms = jnp.mean(blk * blk, axis=-1, keepdims=True)
        o_ref[:, s:s + width] = (blk * lax.rsqrt(ms + EPS) * gain[:, s:s + width]).astype(o_ref.dtype)


def _split3(c):
    hi = c.astype(BF16).astype(F32)
    r1 = c - hi
    mid = r1.astype(BF16).astype(F32)
    lo = (r1 - mid).astype(BF16).astype(F32)
    return hi, mid, lo


ONE_LANE = 3 * FOX_HEADS


def _decay_selector():
    sel = np.zeros((LANES, FOX_HEADS, LANES), np.float32)
    for hd in range(FOX_HEADS):
        for piece in range(3):
            sel[ONE_LANE, hd, piece * FOX_HEADS + hd] = 1.0
            sel[piece * FOX_HEADS + hd, hd, ONE_LANE + piece] = -1.0
    return jnp.asarray(sel.reshape(LANES, -1), BF16)


def _norm1_kernel(x_ref, g_ref, wf_ref, bf_ref, sel_ref, h_ref, qa_ref, ka_ref, carry_ref):
    tm = x_ref.shape[0]

    @pl.when(pl.program_id(1) == 0)
    def _():
        carry_ref[...] = jnp.zeros_like(carry_ref)

    xf = x_ref[...]
    ms = jnp.mean(xf * xf, axis=-1, keepdims=True)
    y = xf * lax.rsqrt(ms + EPS) * g_ref[...]
    y_hi = y.astype(BF16)
    h_ref[...] = y_hi

    wf = wf_ref[...]
    wf_hi = wf.astype(BF16).astype(F32)
    w2 = jnp.concatenate(
        [wf_hi, wf - wf_hi, jnp.zeros((LANES - 2 * FOX_HEADS, wf.shape[1]), F32)], axis=0).astype(BF16)
    z2 = lax.dot_general(y_hi, w2, NT_DIMS, preferred_element_type=F32)
    z = z2 + pltpu.roll(z2, LANES - FOX_HEADS, 1) + bf_ref[...]
    log_f = -LOG2_E * (jnp.maximum(-z, 0.0) + jnp.log1p(jnp.exp(-jnp.abs(z))))

    row = lax.broadcasted_iota(jnp.int32, (tm, tm), 0)
    col = lax.broadcasted_iota(jnp.int32, (tm, tm), 1)
    tri = jnp.where(row >= col, 1.0, 0.0).astype(BF16)
    f_hi, f_mid, f_lo = _split3(log_f)
    part = jnp.dot(tri, jnp.concatenate([f_hi, f_mid], axis=1).astype(BF16),
                   preferred_element_type=F32)
    c = (part[:, :LANES] + part[:, LANES:]
         + jnp.dot(tri, f_lo.astype(BF16), preferred_element_type=F32) + carry_ref[0:1, :])
    carry_ref[0:1, :] = c[tm - 1:tm, :]

    lane = lax.broadcasted_iota(jnp.int32, (tm, LANES), 1)
    c_hi, c_mid, c_lo = _split3(c)
    packed = jnp.where(
        lane < FOX_HEADS, c_hi,
        jnp.where(lane < 2 * FOX_HEADS, pltpu.roll(c_mid, FOX_HEADS, 1),
                  jnp.where(lane < ONE_LANE, pltpu.roll(c_lo, 2 * FOX_HEADS, 1),
                            jnp.where(lane < ONE_LANE + 3, 1.0, 0.0)))).astype(BF16)
    qa_ref[...] = packed
    ka_ref[...] = jnp.dot(packed, sel_ref[...], preferred_element_type=F32).astype(BF16)


def _norm1(x, g, w_t, f_row, bf, tm=1024):
    b, s, d = x.shape
    return pl.pallas_call(
        _norm1_kernel,
        grid=(b, s // tm),
        in_specs=[
            pl.BlockSpec((None, tm, d), lambda i, j: (i, j, 0)),
            pl.BlockSpec((1, d), lambda i, j: (0, 0)),
            pl.BlockSpec((FOX_HEADS, d), lambda i, j: (f_row // FOX_HEADS, 0)),
            pl.BlockSpec((1, LANES), lambda i, j: (0, 0)),
            pl.BlockSpec((LANES, FOX_HEADS * LANES), lambda i, j: (0, 0)),
        ],
        out_specs=[
            pl.BlockSpec((None, tm, d), lambda i, j: (i, j, 0)),
            pl.BlockSpec((None, tm, LANES), lambda i, j: (i, j, 0)),
            pl.BlockSpec((None, tm, FOX_HEADS * LANES), lambda i, j: (i, j, 0)),
        ],
        out_shape=[
            jax.ShapeDtypeStruct((b, s, d), BF16),
            jax.ShapeDtypeStruct((b, s, LANES), BF16),
            jax.ShapeDtypeStruct((b, s, FOX_HEADS * LANES), BF16),
        ],
        scratch_shapes=[pltpu.VMEM((SUBLANES, LANES), F32)],
        compiler_params=_params(2),
        name="norm1_decay",
    )(x, g, w_t, bf, _decay_selector())


PLAIN, GATE = 0, -1


def _proj_kernel(h_ref, w_ref, gain_ref, o_ref, wbf_ref, *, kinds):
    @pl.when(pl.program_id(1) == 0)
    def _():
        wbf_ref[...] = w_ref[...].astype(BF16)

    def tile(kind):
        acc = lax.dot_general(h_ref[...], wbf_ref[...], NT_DIMS, preferred_element_type=F32)
        if kind == PLAIN:
            o_ref[...] = acc.astype(BF16)
        elif kind == GATE:
            o_ref[...] = (0.5 * jnp.tanh(0.5 * acc) + 0.5).astype(BF16)
        else:
            _store_group_rms(o_ref, acc, gain_ref[...], kind)

    j = pl.program_id(0)
    for kind in sorted(set(kinds)):
        hit = functools.reduce(jnp.logical_or, [j == t for t, kd in enumerate(kinds) if kd == kind])
        pl.when(hit)(functools.partial(tile, kind))


def _proj(h, w_t, row_starts, kinds, gains, tm=1024, tn=COL_TILE):
    m, k = h.shape
    n_tiles = len(row_starts)
    assert all(r % SUBLANES == 0 for r in row_starts)

    def w_rows(j, i):
        start = jnp.int32(row_starts[0])
        for t in range(1, n_tiles):
            start = jnp.where(j == t, row_starts[t], start)
        return (pl.multiple_of(start, SUBLANES), 0)

    return pl.pallas_call(
        functools.partial(_proj_kernel, kinds=tuple(kinds)),
        grid=(n_tiles, m // tm),
        in_specs=[
            pl.BlockSpec((tm, k), lambda j, i: (i, 0)),
            pl.BlockSpec((pl.Element(tn), pl.Element(k)), w_rows),
            pl.BlockSpec((None, 1, tn), lambda j, i: (j, 0, 0)),
        ],
        out_specs=pl.BlockSpec((tm, tn), lambda j, i: (i, j)),
        out_shape=jax.ShapeDtypeStruct((m, n_tiles * tn), BF16),
        scratch_shapes=[pltpu.VMEM((tn, k), BF16)],
        compiler_params=_params(2),
        name="in_proj",
    )(h, w_t, gains.reshape(n_tiles, 1, tn))


def _gated_conv_tile(conv_ref, w_ref, carry_ref):
    tm = conv_ref.shape[0]
    k = w_ref.shape[1]
    u = conv_ref[:, k:2 * k].astype(F32) * conv_ref[:, 2 * k:3 * k].astype(F32)
    w = w_ref[...]
    taps = lambda u2, u1, u0: w[0:1, :] * u2 + w[1:2, :] * u1 + w[2:3, :] * u0
    conv = taps(pltpu.roll(u, 2, 0), pltpu.roll(u, 1, 0), u)
    top = u[0:SUBLANES, :]
    row = lax.broadcasted_iota(jnp.int32, (SUBLANES, 1), 0)
    prev1 = carry_ref[1:2, :]
    prev2 = carry_ref[0:1, :]
    top1 = jnp.where(row == 0, prev1, pltpu.roll(top, 1, 0))
    top2 = jnp.where(row == 0, prev2, jnp.where(row == 1, prev1, pltpu.roll(top, 2, 0)))
    conv = jnp.concatenate([taps(top2, top1, top), conv[SUBLANES:, :]], axis=0)
    carry_ref[0:2, :] = u[tm - 2:tm, :]
    return conv_ref[:, 0:k].astype(F32) * conv


def _fox_kernel(q_ref, qa_ref, k_ref, ka_ref, v_ref, o_ref, vt_ref, s_ref, m_ref, l_ref, acc_ref, *,
                t):
    n_tiles = q_ref.shape[0] // t
    assert n_tiles % 2 == 0
    d = FOX_HEAD_DIM
    n_heads = q_ref.shape[1] // d
    j = pl.program_id(2)

    @pl.when(j == 0)
    def _():
        for g in range(n_heads):
            for blk in range(v_ref.shape[0] // t):
                v = v_ref[blk * t:(blk + 1) * t, g * d:(g + 1) * d]
                vt_ref[g, blk, :d] = v.astype(F32).T.astype(BF16)
                vt_ref[g, blk, d:] = jnp.ones((vt_ref.shape[2] - d, t), BF16)

    def scores(tile, kb, g):
        q = jnp.concatenate([q_ref[tile * t:(tile + 1) * t, g * d:(g + 1) * d],
                             qa_ref[tile * t:(tile + 1) * t, :]], axis=1)
        rows = pl.ds(pl.multiple_of(kb * t, t), t)
        k = jnp.concatenate([k_ref[rows, g * d:(g + 1) * d], ka_ref[rows, g * d:(g + 1) * d]], axis=1)
        return lax.dot_general(k, q, NT_DIMS, preferred_element_type=F32)

    def reset(g):
        m_ref[g] = jnp.full(m_ref.shape[1:], -jnp.inf, F32)
        l_ref[g] = jnp.zeros(l_ref.shape[1:], F32)
        acc_ref[g] = jnp.zeros(acc_ref.shape[1:], F32)

    def update(g, m_new, alpha, pv):
        l_ref[g] = alpha * l_ref[g] + pv[d:d + 1]
        acc_ref[g] = alpha * acc_ref[g] + pv[:d]
        m_ref[g] = m_new

    def consume(kb, slot, g):
        s = s_ref[slot, g]
        m_prev = m_ref[g]
        m_new = jnp.maximum(m_prev, jnp.max(s, axis=0, keepdims=True))
        p = jnp.exp2(s - m_new)
        pv = jnp.dot(vt_ref[g, kb], p.astype(BF16), preferred_element_type=F32)
        update(g, m_new, jnp.exp2(m_prev - m_new), pv)

    def causal(s):
        kpos = lax.broadcasted_iota(jnp.int32, s.shape, 0)
        qpos = lax.broadcasted_iota(jnp.int32, s.shape, 1)
        return jnp.where(qpos >= kpos, s, -jnp.inf)

    def consume_diagonal(kb, slot, g):
        h = t // 2
        s_up = causal(s_ref[slot, g, :h, :])
        s_lo = causal(s_ref[slot, g, h:, h:])
        m_up = jnp.max(s_up, axis=0, keepdims=True)
        m_blk = jnp.concatenate(
            [m_up[:, :h], jnp.maximum(m_up[:, h:], jnp.max(s_lo, axis=0, keepdims=True))], axis=1)
        m_prev = m_ref[g]
        m_new = jnp.maximum(m_prev, m_blk)
        p_up = jnp.exp2(s_up - m_new).astype(BF16)
        p_lo = jnp.exp2(s_lo - m_new[:, h:]).astype(BF16)
        vt = vt_ref[g, kb]
        pv_up = jnp.dot(vt[:, :h], p_up, preferred_element_type=F32)
        pv_lo = jnp.dot(vt[:, h:], p_lo, preferred_element_type=F32)
        pv = jnp.concatenate([pv_up[:, :h], pv_up[:, h:] + pv_lo], axis=1)
        update(g, m_new, jnp.exp2(m_prev - m_new), pv)

    def advance(tile, kb, slot):
        for g in range(n_heads):
            s_ref[1 - slot, g] = scores(tile, kb + 1, g)
            consume(kb, slot, g)

    def emit(tile, g):
        o_ref[tile * t:(tile + 1) * t, g * d:(g + 1) * d] = (acc_ref[g] / l_ref[g]).T.astype(BF16)

    slot = 0
    for g in range(n_heads):
        reset(g)
        s_ref[slot, g] = scores(0, 0, g)
    for r in range(n_tiles):
        def pairs(pair, carry, r=r, slot=slot):
            advance(r, 2 * pair, slot)
            advance(r, 2 * pair + 1, 1 - slot)
            return carry

        lax.fori_loop(0, (n_tiles // 2) * j + r // 2, pairs, 0)
        diagonal = n_tiles * j + r
        if r % 2 == 1:
            advance(r, diagonal - 1, slot)
            slot = 1 - slot
        for g in range(n_heads):
            if r + 1 < n_tiles:
                s_ref[1 - slot, g] = scores(r + 1, 0, g)
            consume_diagonal(diagonal, slot, g)
            emit(r, g)
            if r + 1 < n_tiles:
                reset(g)
        slot = 1 - slot


def _fox(proj, qa, ka, q_col, k_col, v_col, t=512, tiles_per_step=2, heads_per_step=4):
    b, s, _ = proj.shape
    d = FOX_HEAD_DIM
    w = heads_per_step * d
    rows = tiles_per_step * t
    tiles = lambda col0: pl.BlockSpec((None, rows, w), lambda bi, h, j: (bi, j, col0 // w + h))
    seq = lambda col0: pl.BlockSpec((None, s, w), lambda bi, h, j: (bi, 0, col0 // w + h))
    return pl.pallas_call(
        functools.partial(_fox_kernel, t=t),
        grid=(b, FOX_HEADS // heads_per_step, s // rows),
        in_specs=[tiles(q_col), pl.BlockSpec((None, rows, LANES), lambda bi, h, j: (bi, j, 0)),
                  seq(k_col), seq(0), seq(v_col)],
        out_specs=tiles(0),
        out_shape=jax.ShapeDtypeStruct((b, s, FOX_HEADS * d), BF16),
        scratch_shapes=[pltpu.VMEM((heads_per_step, s // t, d + BF16_ROWS, t), BF16),
                        pltpu.VMEM((2, heads_per_step, t, t), F32),
                        pltpu.VMEM((heads_per_step, 1, t), F32),
                        pltpu.VMEM((heads_per_step, 1, t), F32),
                        pltpu.VMEM((heads_per_step, d, t), F32)],
        compiler_params=_params(3),
        name="fox_attention",
    )(proj, qa, proj, ka, proj)


def _memkv_kernel(mem_ref, g_ref, w_ref, kg_ref, o_ref):
    xf = mem_ref[...]
    ms = jnp.mean(xf * xf, axis=-1, keepdims=True)
    y = (xf * lax.rsqrt(ms + EPS) * g_ref[...]).astype(BF16)
    acc = jnp.dot(y, w_ref[...].astype(BF16), preferred_element_type=F32)

    @pl.when(pl.program_id(0) == 0)
    def _():
        _store_group_rms(o_ref, acc, kg_ref[...], MEM_HEAD_DIM)

    @pl.when(pl.program_id(0) == 1)
    def _():
        o_ref[...] = acc.astype(BF16)


def _memkv(mem2d, g, w, kg):
    m, d = mem2d.shape
    width = MEM_HEADS * MEM_HEAD_DIM
    return pl.pallas_call(
        _memkv_kernel,
        grid=(2,),
        in_specs=[
            pl.BlockSpec((m, d), lambda j: (0, 0)),
            pl.BlockSpec((1, d), lambda j: (0, 0)),
            pl.BlockSpec((d, width), lambda j: (0, j)),
            pl.BlockSpec((1, width), lambda j: (0, 0)),
        ],
        out_specs=pl.BlockSpec((m, width), lambda j: (0, j)),
        out_shape=jax.ShapeDtypeStruct((m, 2 * width), BF16),
        compiler_params=_params(1),
        name="mem_kv",
    )(mem2d, g, w, kg)


def _memattn_kernel(q_ref, kv_ref, o_ref):
    width = MEM_HEADS * MEM_HEAD_DIM
    for h in range(MEM_HEADS):
        lo, hi = h * MEM_HEAD_DIM, (h + 1) * MEM_HEAD_DIM
        q = q_ref[:, lo:hi]
        k = kv_ref[:, lo:hi]
        v = kv_ref[:, width + lo:width + hi]
        s = lax.dot_general(q, k, NT_DIMS, preferred_element_type=F32)
        p = jnp.exp(s - jnp.max(s, axis=-1, keepdims=True))
        l = jnp.sum(p, axis=-1, keepdims=True)
        o = jnp.dot(p.astype(BF16), v, preferred_element_type=F32) / l
        o_ref[:, lo:hi] = o.astype(BF16)


def _memattn(proj, q_col, kv, tm=1024):
    b, s, _ = proj.shape
    width = MEM_HEADS * MEM_HEAD_DIM
    n_mem = kv.shape[1]
    return pl.pallas_call(
        _memattn_kernel,
        grid=(b, s // tm),
        in_specs=[
            pl.BlockSpec((None, tm, width), lambda i, j: (i, j, q_col // width)),
            pl.BlockSpec((None, n_mem, 2 * width), lambda i, j: (i, 0, 0)),
        ],
        out_specs=pl.BlockSpec((None, tm, width), lambda i, j: (i, j, 0)),
        out_shape=jax.ShapeDtypeStruct((b, s, width), BF16),
        compiler_params=_params(2),
        name="mem_attention",
    )(proj, kv)


def _merge_kernel(conv_ref, cw_ref, b_ref, c_ref, wa_ref, wb_ref, wc_ref, g_ref,
                  o_ref, wabf_ref, wbbf_ref, wcbf_ref, carry_ref, *, tiles_per_seq):
    i = pl.program_id(1)
    tn = o_ref.shape[1]

    @pl.when(i == 0)
    def _():
        wabf_ref[...] = wa_ref[...].astype(BF16)
        wbbf_ref[...] = wb_ref[...].astype(BF16)
        wcbf_ref[...] = wc_ref[...].astype(BF16)

    @pl.when(i % tiles_per_seq == 0)
    def _():
        carry_ref[...] = jnp.zeros_like(carry_ref)

    y_conv = _gated_conv_tile(conv_ref, cw_ref, carry_ref).astype(BF16)
    oa = jnp.dot(y_conv, wabf_ref[...], preferred_element_type=F32)
    ob = jnp.dot(b_ref[...], wbbf_ref[...], preferred_element_type=F32)
    oc = jnp.dot(c_ref[...], wcbf_ref[...], preferred_element_type=F32)
    gate = lambda br: g_ref[:, br * tn:(br + 1) * tn].astype(F32)
    o_ref[...] = (gate(0) * oa + gate(1) * ob + gate(2) * oc).astype(BF16)


def _merge(proj, conv_col, conv_w, yb, yc, wa, wb, wc, seq_len, d_model, tm=512, tn=COL_TILE):
    m, k = yb.shape
    lhs = pl.BlockSpec((tm, k), lambda j, i: (i, 0))
    wsp = pl.BlockSpec((k, tn), lambda j, i: (0, j))
    return pl.pallas_call(
        functools.partial(_merge_kernel, tiles_per_seq=seq_len // tm),
        grid=(d_model // tn, m // tm),
        in_specs=[pl.BlockSpec((tm, 3 * k), lambda j, i: (i, conv_col // (3 * k))),
                  pl.BlockSpec((CONV_TAPS, k), lambda j, i: (0, 0)),
                  lhs, lhs, wsp, wsp, wsp,
                  pl.BlockSpec((tm, 3 * tn), lambda j, i: (i, j))],
        out_specs=pl.BlockSpec((tm, tn), lambda j, i: (i, j)),
        out_shape=jax.ShapeDtypeStruct((m, d_model), BF16),
        scratch_shapes=[pltpu.VMEM((k, tn), BF16)] * 3 + [pltpu.VMEM((SUBLANES, k), F32)],
        compiler_params=_params(2),
        name="branch_merge",
    )(proj, conv_w, yb, yc, wa, wb, wc, proj)


W_CHUNKS = 4


def _outproj_kernel(a_ref, w_hbm, x_ref, g_ref, x2_ref, h2_ref, wbf_ref, stage_ref, sem):
    rows = wbf_ref.shape[0] // W_CHUNKS

    def chunk_copy(c):
        return pltpu.make_async_copy(w_hbm.at[pl.ds(c * rows, rows), :], stage_ref.at[c % 2],
                                     sem.at[c % 2])

    def finish(acc):
        x2 = x_ref[...] + acc
        x2_ref[...] = x2
        ms = jnp.mean(x2 * x2, axis=-1, keepdims=True)
        h2_ref[...] = (x2 * lax.rsqrt(ms + EPS) * g_ref[...]).astype(BF16)

    @pl.when(pl.program_id(0) == 0)
    def _():
        chunk_copy(0).start()
        chunk_copy(1).start()
        acc = None
        for c in range(W_CHUNKS):
            k_rows = slice(c * rows, (c + 1) * rows)
            chunk_copy(c).wait()
            wbf_ref[k_rows, :] = stage_ref[c % 2].astype(BF16)
            if c + 2 < W_CHUNKS:
                chunk_copy(c + 2).start()
            part = jnp.dot(a_ref[:, k_rows], wbf_ref[k_rows, :], preferred_element_type=F32)
            acc = part if acc is None else acc + part
        finish(acc)

    @pl.when(pl.program_id(0) != 0)
    def _():
        finish(jnp.dot(a_ref[...], wbf_ref[...], preferred_element_type=F32))


def _outproj(a, w, x2d, g, tm=512):
    m, d = x2d.shape
    row = pl.BlockSpec((tm, d), lambda i: (i, 0))
    return pl.pallas_call(
        _outproj_kernel,
        grid=(m // tm,),
        in_specs=[row, pl.BlockSpec(memory_space=pl.ANY), row,
                  pl.BlockSpec((1, d), lambda i: (0, 0))],
        out_specs=[row, row],
        out_shape=[jax.ShapeDtypeStruct((m, d), F32), jax.ShapeDtypeStruct((m, d), BF16)],
        scratch_shapes=[pltpu.VMEM((d, d), BF16), pltpu.VMEM((2, d // W_CHUNKS, d), F32),
                        pltpu.SemaphoreType.DMA((2,))],
        compiler_params=_params(1),
        name="out_proj_norm2",
    )(a, w, x2d, g)


def _mlp_kernel(h_ref, wu_ref, wd_ref, x_ref, o_ref):
    f = pl.program_id(1)
    rows = x_ref.shape[0]

    @pl.when(f == 0)
    def _():
        o_ref[...] = jnp.zeros_like(o_ref)

    a = jnp.dot(h_ref[...], wu_ref[...].astype(BF16), preferred_element_type=F32)
    a = jnp.square(jnp.maximum(a, 0.0)).astype(BF16)
    o_ref[...] += jnp.dot(a, wd_ref[...].astype(BF16), preferred_element_type=F32)
    slab = pl.ds(pl.multiple_of(f * rows, rows), rows)
    o_ref[slab, :] += x_ref[...]


def _mlp(h2, w_up, w_down, x2, tm=1024, tf=512):
    m, d = h2.shape
    d_ff = w_up.shape[1]
    nf = d_ff // tf
    rows = tm // nf
    return pl.pallas_call(
        _mlp_kernel,
        grid=(m // tm, nf),
        in_specs=[
            pl.BlockSpec((tm, d), lambda i, f: (i, 0)),
            pl.BlockSpec((d, tf), lambda i, f: (0, f)),
            pl.BlockSpec((tf, d), lambda i, f: (f, 0)),
            pl.BlockSpec((rows, d), lambda i, f: (i * nf + f, 0)),
        ],
        out_specs=pl.BlockSpec((tm, d), lambda i, f: (i, 0)),
        out_shape=jax.ShapeDtypeStruct((m, d), F32),
        compiler_params=_params(2),
        name="relu2_mlp",
    )(h2, w_up, w_down, x2)


def kernel(x, mem, norm1_g, w_in, b_f, conv_w, fox_q_g, fox_k_g, mem_norm_g, w_mem_kv, mem_q_g, mem_k_g, w_conv_out, w_fox_out, w_mem_out, w_out, norm2_g, w_up, w_down):
    b, s, d = x.shape
    m = b * s
    conv_width = conv_w.shape[1]
    fox_width = FOX_HEADS * FOX_HEAD_DIM
    mem_width = MEM_HEADS * MEM_HEAD_DIM
    tile = COL_TILE
    q_row = 3 * conv_width
    k_row = q_row + fox_width
    v_row = k_row + fox_width
    f_row = v_row + fox_width
    mq_row = f_row + FOX_HEADS
    gate_row = mq_row + mem_width
    n_gate = N_BRANCHES * d
    ch_tiles = d // tile
    gate_rows = [gate_row + br * d + n * tile for n in range(ch_tiles) for br in range(N_BRANCHES)]
    rest_rows = [r for r in range(0, f_row, tile)] + [r for r in range(mq_row, gate_row, tile)]
    row_starts = tuple(gate_rows + rest_rows)
    conv_col, q_col, k_col, v_col, mq_col = (n_gate + r for r in (0, q_row, k_row, v_row, f_row))
    kinds = ([GATE] * len(gate_rows) + [PLAIN] * (q_row // tile) + [FOX_HEAD_DIM] * (2 * fox_width // tile)
             + [PLAIN] * (fox_width // tile) + [MEM_HEAD_DIM] * (mem_width // tile))
    gains = jnp.ones((len(row_starts) * tile,), F32)
    gains = gains.at[q_col:k_col].set(jnp.tile(fox_q_g * (LOG2_E / math.sqrt(FOX_HEAD_DIM)), FOX_HEADS))
    gains = gains.at[k_col:v_col].set(jnp.tile(fox_k_g, FOX_HEADS))
    gains = gains.at[mq_col:].set(jnp.tile(mem_q_g * (1.0 / math.sqrt(MEM_HEAD_DIM)), MEM_HEADS))

    row = lambda v: v.reshape(1, -1)
    bf = jnp.pad(b_f, (0, LANES - FOX_HEADS)).reshape(1, LANES)
    w_t = w_in.T

    h1, qa, ka = _norm1(x, row(norm1_g), w_t, f_row, bf)
    proj = _proj(h1.reshape(m, d), w_t, row_starts, kinds, gains.reshape(-1, tile))
    proj3 = proj.reshape(b, s, -1)

    y_fox = _fox(proj3, qa, ka, q_col, k_col, v_col)
    kv = _memkv(mem.reshape(-1, d), row(mem_norm_g), w_mem_kv, row(jnp.tile(mem_k_g, MEM_HEADS)))
    y_mem = _memattn(proj3, mq_col, kv.reshape(b, -1, 2 * mem_width))

    merged = _merge(proj, conv_col, conv_w, y_fox.reshape(m, -1), y_mem.reshape(m, -1),
                    w_conv_out, w_fox_out, w_mem_out, s, d)
    x2, h2 = _outproj(merged, w_out, x.reshape(m, d), row(norm2_g))
    out = _mlp(h2, w_up, w_down, x2)
    return out.reshape(b, s, d)
```

```python
import functools
import math

import jax
import jax.numpy as jnp
import numpy as np
from jax import lax
from jax.experimental import pallas as pl
from jax.experimental.pallas import tpu as pltpu

F32 = jnp.float32
BF16 = jnp.bfloat16

EPS = 1e-6
LOG2_E = math.log2(math.e)
LANES = 128
SUBLANES = 8
BF16_ROWS = 2 * SUBLANES
CONV_TAPS = 3
FOX_HEADS = 8
FOX_HEAD_DIM = 128
MEM_HEADS = 4
MEM_HEAD_DIM = 256
N_BRANCHES = 3

VMEM_LIMIT_BYTES = 56 * 1024 * 1024
COL_TILE = 1024
NT_DIMS = (((1,), (1,)), ((), ()))


def _params(n_axes):
    return pltpu.CompilerParams(
        dimension_semantics=("arbitrary",) * n_axes,
        vmem_limit_bytes=VMEM_LIMIT_BYTES)


def _store_group_rms(o_ref, a, gain, width):
    for s in range(0, a.shape[1], width):
        blk = a[:, s:s + width]
        ms = jnp.mean(blk * blk, axis=-1, keepdims=True)
        o_ref[:, s:s + width] = (blk * lax.rsqrt(ms + EPS) * gain[:, s:s + width]).astype(o_ref.dtype)


def _split3(c):
    hi = c.astype(BF16).astype(F32)
    r1 = c - hi
    mid = r1.astype(BF16).astype(F32)
    lo = (r1 - mid).astype(BF16).astype(F32)
    return hi, mid, lo


ONE_LANE = 3 * FOX_HEADS


def _decay_selector():
    sel = np.zeros((LANES, FOX_HEADS, LANES), np.float32)
    for hd in range(FOX_HEADS):
        for piece in range(3):
            sel[ONE_LANE, hd, piece * FOX_HEADS + hd] = 1.0
            sel[piece * FOX_HEADS + hd, hd, ONE_LANE + piece] = -1.0
    return jnp.asarray(sel.reshape(LANES, -1), BF16)


def _norm1_kernel(x_ref, g_ref, wf_ref, bf_ref, sel_ref, h_ref, qa_ref, ka_ref, carry_ref):
    tm = x_ref.shape[0]

    @pl.when(pl.program_id(1) == 0)
    def _():
        carry_ref[...] = jnp.zeros_like(carry_ref)

    wf = wf_ref[...]
    wf_hi = wf.astype(BF16).astype(F32)
    w2 = jnp.concatenate(
        [wf_hi, wf - wf_hi, jnp.zeros((LANES - 2 * FOX_HEADS, wf.shape[1]), F32)], axis=0).astype(BF16)
    row = lax.broadcasted_iota(jnp.int32, (CUMSUM_ROWS, CUMSUM_ROWS), 0)
    col = lax.broadcasted_iota(jnp.int32, (CUMSUM_ROWS, CUMSUM_ROWS), 1)
    tri = jnp.where(row >= col, 1.0, 0.0).astype(BF16)
    lane = lax.broadcasted_iota(jnp.int32, (CUMSUM_ROWS, LANES), 1)

    def pack3(hi, mid, lo, rest):
        return jnp.where(
            lane < FOX_HEADS, hi,
            jnp.where(lane < 2 * FOX_HEADS, pltpu.roll(mid, FOX_HEADS, 1),
                      jnp.where(lane < ONE_LANE, pltpu.roll(lo, 2 * FOX_HEADS, 1), rest))).astype(BF16)

    blocks = [pl.ds(blk * CUMSUM_ROWS, CUMSUM_ROWS) for blk in range(tm // CUMSUM_ROWS)]

    def logits(rows):
        xf = x_ref[rows, :]
        ms = jnp.mean(xf * xf, axis=-1, keepdims=True)
        y_hi = (xf * lax.rsqrt(ms + EPS) * g_ref[...]).astype(BF16)
        h_ref[rows, :] = y_hi
        return lax.dot_general(y_hi, w2, NT_DIMS, preferred_element_type=F32)

    def local_cumsum(z2):
        z = z2 + pltpu.roll(z2, LANES - FOX_HEADS, 1) + bf_ref[...]
        log_f = -LOG2_E * (jnp.maximum(-z, 0.0) + jnp.log1p(jnp.exp(-jnp.abs(z))))
        return jnp.dot(tri, pack3(*_split3(log_f), 0.0), preferred_element_type=F32)

    def decay_columns(rows, loc, run):
        c = (loc + pltpu.roll(loc, LANES - FOX_HEADS, 1)
             + pltpu.roll(loc, LANES - 2 * FOX_HEADS, 1) + run)
        packed = pack3(*_split3(c), jnp.where(lane < ONE_LANE + 3, 1.0, 0.0))
        qa_ref[rows, :] = packed
        ka_ref[rows, :] = jnp.dot(packed, sel_ref[...], preferred_element_type=F32).astype(BF16)
        return c[CUMSUM_ROWS - 1:CUMSUM_ROWS, :]

    z2s = [logits(rows) for rows in blocks]
    locs = [local_cumsum(z2) for z2 in z2s]
    run = carry_ref[0:1, :]
    for rows, loc in zip(blocks, locs):
        run = decay_columns(rows, loc, run)
    carry_ref[0:1, :] = run


def _norm1(x, g, w_t, f_row, bf, tm=1024):
    b, s, d = x.shape
    return pl.pallas_call(
        _norm1_kernel,
        grid=(b, s // tm),
        in_specs=[
            pl.BlockSpec((None, tm, d), lambda i, j: (i, j, 0)),
            pl.BlockSpec((1, d), lambda i, j: (0, 0)),
            pl.BlockSpec((FOX_HEADS, d), lambda i, j: (f_row // FOX_HEADS, 0)),
            pl.BlockSpec((1, LANES), lambda i, j: (0, 0)),
            pl.BlockSpec((LANES, FOX_HEADS * LANES), lambda i, j: (0, 0)),
        ],
        out_specs=[
            pl.BlockSpec((None, tm, d), lambda i, j: (i, j, 0)),
            pl.BlockSpec((None, tm, LANES), lambda i, j: (i, j, 0)),
            pl.BlockSpec((None, tm, FOX_HEADS * LANES), lambda i, j: (i, j, 0)),
        ],
        out_shape=[
            jax.ShapeDtypeStruct((b, s, d), BF16),
            jax.ShapeDtypeStruct((b, s, LANES), BF16),
            jax.ShapeDtypeStruct((b, s, FOX_HEADS * LANES), BF16),
        ],
        scratch_shapes=[pltpu.VMEM((SUBLANES, LANES), F32)],
        compiler_params=_params(2),
        name="norm1_decay",
    )(x, g, w_t, bf, _decay_selector())


CUMSUM_ROWS = 256
PLAIN, GATE = 0, -1


def _proj_kernel(h_ref, w_ref, gain_ref, o_ref, wbf_ref, *, kinds):
    @pl.when(pl.program_id(1) == 0)
    def _():
        wbf_ref[...] = w_ref[...].astype(BF16)

    def tile(kind):
        acc = lax.dot_general(h_ref[...], wbf_ref[...], NT_DIMS, preferred_element_type=F32)
        if kind == PLAIN:
            o_ref[...] = acc.astype(BF16)
        elif kind == GATE:
            o_ref[...] = (0.5 * jnp.tanh(0.5 * acc) + 0.5).astype(BF16)
        else:
            _store_group_rms(o_ref, acc, gain_ref[...], kind)

    j = pl.program_id(0)
    for kind in sorted(set(kinds)):
        hit = functools.reduce(jnp.logical_or, [j == t for t, kd in enumerate(kinds) if kd == kind])
        pl.when(hit)(functools.partial(tile, kind))


def _proj(h, w_t, row_starts, kinds, gains, tm=1024, tn=COL_TILE):
    m, k = h.shape
    n_tiles = len(row_starts)
    assert all(r % SUBLANES == 0 for r in row_starts)

    def w_rows(j, i):
        start = jnp.int32(row_starts[0])
        for t in range(1, n_tiles):
            start = jnp.where(j == t, row_starts[t], start)
        return (pl.multiple_of(start, SUBLANES), 0)

    return pl.pallas_call(
        functools.partial(_proj_kernel, kinds=tuple(kinds)),
        grid=(n_tiles, m // tm),
        in_specs=[
            pl.BlockSpec((tm, k), lambda j, i: (i, 0)),
            pl.BlockSpec((pl.Element(tn), pl.Element(k)), w_rows),
            pl.BlockSpec((None, 1, tn), lambda j, i: (j, 0, 0)),
        ],
        out_specs=pl.BlockSpec((tm, tn), lambda j, i: (i, j)),
        out_shape=jax.ShapeDtypeStruct((m, n_tiles * tn), BF16),
        scratch_shapes=[pltpu.VMEM((tn, k), BF16)],
        compiler_params=_params(2),
        name="in_proj",
    )(h, w_t, gains.reshape(n_tiles, 1, tn))


def _gated_conv_tile(conv_ref, w_ref, carry_ref):
    tm = conv_ref.shape[0]
    k = w_ref.shape[1]
    u = conv_ref[:, k:2 * k].astype(F32) * conv_ref[:, 2 * k:3 * k].astype(F32)
    w = w_ref[...]
    taps = lambda u2, u1, u0: w[0:1, :] * u2 + w[1:2, :] * u1 + w[2:3, :] * u0
    conv = taps(pltpu.roll(u, 2, 0), pltpu.roll(u, 1, 0), u)
    top = u[0:SUBLANES, :]
    row = lax.broadcasted_iota(jnp.int32, (SUBLANES, 1), 0)
    prev1 = carry_ref[1:2, :]
    prev2 = carry_ref[0:1, :]
    top1 = jnp.where(row == 0, prev1, pltpu.roll(top, 1, 0))
    top2 = jnp.where(row == 0, prev2, jnp.where(row == 1, prev1, pltpu.roll(top, 2, 0)))
    conv = jnp.concatenate([taps(top2, top1, top), conv[SUBLANES:, :]], axis=0)
    carry_ref[0:2, :] = u[tm - 2:tm, :]
    return conv_ref[:, 0:k].astype(F32) * conv


def _fox_kernel(q_ref, qa_ref, k_ref, ka_ref, v_ref, o_ref, vt_ref, s_ref, m_ref, l_ref, acc_ref, *,
                t):
    n_tiles = q_ref.shape[0] // t
    assert n_tiles % 2 == 0
    d = FOX_HEAD_DIM
    n_heads = q_ref.shape[1] // d
    j = pl.program_id(2)

    @pl.when(j == 0)
    def _():
        for g in range(n_heads):
            for blk in range(v_ref.shape[0] // t):
                v = v_ref[blk * t:(blk + 1) * t, g * d:(g + 1) * d]
                vt_ref[g, blk, :d] = v.astype(F32).T.astype(BF16)
                vt_ref[g, blk, d:] = jnp.ones((vt_ref.shape[2] - d, t), BF16)

    def scores(tile, kb, g):
        q = jnp.concatenate([q_ref[tile * t:(tile + 1) * t, g * d:(g + 1) * d],
                             qa_ref[tile * t:(tile + 1) * t, :]], axis=1)
        rows = pl.ds(pl.multiple_of(kb * t, t), t)
        k = jnp.concatenate([k_ref[rows, g * d:(g + 1) * d], ka_ref[rows, g * d:(g + 1) * d]], axis=1)
        return lax.dot_general(k, q, NT_DIMS, preferred_element_type=F32)

    def reset(g):
        m_ref[g] = jnp.full(m_ref.shape[1:], -jnp.inf, F32)
        l_ref[g] = jnp.zeros(l_ref.shape[1:], F32)
        acc_ref[g] = jnp.zeros(acc_ref.shape[1:], F32)

    def update(g, m_new, alpha, pv):
        l_ref[g] = alpha * l_ref[g] + pv[d:d + 1]
        acc_ref[g] = alpha * acc_ref[g] + pv[:d]
        m_ref[g] = m_new

    def consume(kb, slot, g):
        s = s_ref[slot, g]
        m_prev = m_ref[g]
        m_new = jnp.maximum(m_prev, jnp.max(s, axis=0, keepdims=True))
        p = jnp.exp2(s - m_new)
        pv = jnp.dot(vt_ref[g, kb], p.astype(BF16), preferred_element_type=F32)
        update(g, m_new, jnp.exp2(m_prev - m_new), pv)

    def causal(s):
        kpos = lax.broadcasted_iota(jnp.int32, s.shape, 0)
        qpos = lax.broadcasted_iota(jnp.int32, s.shape, 1)
        return jnp.where(qpos >= kpos, s, -jnp.inf)

    def consume_diagonal(kb, slot, g):
        h = t // 2
        s_up = causal(s_ref[slot, g, :h, :])
        s_lo = causal(s_ref[slot, g, h:, h:])
        m_up = jnp.max(s_up, axis=0, keepdims=True)
        m_blk = jnp.concatenate(
            [m_up[:, :h], jnp.maximum(m_up[:, h:], jnp.max(s_lo, axis=0, keepdims=True))], axis=1)
        m_prev = m_ref[g]
        m_new = jnp.maximum(m_prev, m_blk)
        p_up = jnp.exp2(s_up - m_new).astype(BF16)
        p_lo = jnp.exp2(s_lo - m_new[:, h:]).astype(BF16)
        vt = vt_ref[g, kb]
        pv_up = jnp.dot(vt[:, :h], p_up, preferred_element_type=F32)
        pv_lo = jnp.dot(vt[:, h:], p_lo, preferred_element_type=F32)
        pv = jnp.concatenate([pv_up[:, :h], pv_up[:, h:] + pv_lo], axis=1)
        update(g, m_new, jnp.exp2(m_prev - m_new), pv)

    def advance(tile, kb, slot):
        for g in range(n_heads):
            s_ref[1 - slot, g] = scores(tile, kb + 1, g)
            consume(kb, slot, g)

    def emit(tile, g):
        o_ref[tile * t:(tile + 1) * t, g * d:(g + 1) * d] = (acc_ref[g] / l_ref[g]).T.astype(BF16)

    slot = 0
    for g in range(n_heads):
        reset(g)
        s_ref[slot, g] = scores(0, 0, g)
    for r in range(n_tiles):
        def pairs(pair, carry, r=r, slot=slot):
            advance(r, 2 * pair, slot)
            advance(r, 2 * pair + 1, 1 - slot)
            return carry

        lax.fori_loop(0, (n_tiles // 2) * j + r // 2, pairs, 0)
        diagonal = n_tiles * j + r
        if r % 2 == 1:
            advance(r, diagonal - 1, slot)
            slot = 1 - slot
        for g in range(n_heads):
            if r + 1 < n_tiles:
                s_ref[1 - slot, g] = scores(r + 1, 0, g)
            consume_diagonal(diagonal, slot, g)
            emit(r, g)
            if r + 1 < n_tiles:
                reset(g)
        slot = 1 - slot


def _fox(proj, qa, ka, q_col, k_col, v_col, t=512, tiles_per_step=2, heads_per_step=4):
    b, s, _ = proj.shape
    d = FOX_HEAD_DIM
    w = heads_per_step * d
    rows = tiles_per_step * t
    tiles = lambda col0: pl.BlockSpec((None, rows, w), lambda bi, h, j: (bi, j, col0 // w + h))
    seq = lambda col0: pl.BlockSpec((None, s, w), lambda bi, h, j: (bi, 0, col0 // w + h))
    return pl.pallas_call(
        functools.partial(_fox_kernel, t=t),
        grid=(b, FOX_HEADS // heads_per_step, s // rows),
        in_specs=[tiles(q_col), pl.BlockSpec((None, rows, LANES), lambda bi, h, j: (bi, j, 0)),
                  seq(k_col), seq(0), seq(v_col)],
        out_specs=tiles(0),
        out_shape=jax.ShapeDtypeStruct((b, s, FOX_HEADS * d), BF16),
        scratch_shapes=[pltpu.VMEM((heads_per_step, s // t, d + BF16_ROWS, t), BF16),
                        pltpu.VMEM((2, heads_per_step, t, t), F32),
                        pltpu.VMEM((heads_per_step, 1, t), F32),
                        pltpu.VMEM((heads_per_step, 1, t), F32),
                        pltpu.VMEM((heads_per_step, d, t), F32)],
        compiler_params=_params(3),
        name="fox_attention",
    )(proj, qa, proj, ka, proj)


def _memkv_kernel(mem_ref, g_ref, w_ref, kg_ref, o_ref):
    xf = mem_ref[...]
    ms = jnp.mean(xf * xf, axis=-1, keepdims=True)
    y = (xf * lax.rsqrt(ms + EPS) * g_ref[...]).astype(BF16)
    acc = jnp.dot(y, w_ref[...].astype(BF16), preferred_element_type=F32)

    @pl.when(pl.program_id(0) == 0)
    def _():
        _store_group_rms(o_ref, acc, kg_ref[...], MEM_HEAD_DIM)

    @pl.when(pl.program_id(0) == 1)
    def _():
        o_ref[...] = acc.astype(BF16)


def _memkv(mem2d, g, w, kg):
    m, d = mem2d.shape
    width = MEM_HEADS * MEM_HEAD_DIM
    return pl.pallas_call(
        _memkv_kernel,
        grid=(2,),
        in_specs=[
            pl.BlockSpec((m, d), lambda j: (0, 0)),
            pl.BlockSpec((1, d), lambda j: (0, 0)),
            pl.BlockSpec((d, width), lambda j: (0, j)),
            pl.BlockSpec((1, width), lambda j: (0, 0)),
        ],
        out_specs=pl.BlockSpec((m, width), lambda j: (0, j)),
        out_shape=jax.ShapeDtypeStruct((m, 2 * width), BF16),
        compiler_params=_params(1),
        name="mem_kv",
    )(mem2d, g, w, kg)


def _memattn_kernel(q_ref, kv_ref, o_ref):
    width = MEM_HEADS * MEM_HEAD_DIM
    for h in range(MEM_HEADS):
        lo, hi = h * MEM_HEAD_DIM, (h + 1) * MEM_HEAD_DIM
        q = q_ref[:, lo:hi]
        k = kv_ref[:, lo:hi]
        v = kv_ref[:, width + lo:width + hi]
        s = lax.dot_general(q, k, NT_DIMS, preferred_element_type=F32)
        p = jnp.exp(s - jnp.max(s, axis=-1, keepdims=True))
        l = jnp.sum(p, axis=-1, keepdims=True)
        o = jnp.dot(p.astype(BF16), v, preferred_element_type=F32) / l
        o_ref[:, lo:hi] = o.astype(BF16)


def _memattn(proj, q_col, kv, tm=1024):
    b, s, _ = proj.shape
    width = MEM_HEADS * MEM_HEAD_DIM
    n_mem = kv.shape[1]
    return pl.pallas_call(
        _memattn_kernel,
        grid=(b, s // tm),
        in_specs=[
            pl.BlockSpec((None, tm, width), lambda i, j: (i, j, q_col // width)),
            pl.BlockSpec((None, n_mem, 2 * width), lambda i, j: (i, 0, 0)),
        ],
        out_specs=pl.BlockSpec((None, tm, width), lambda i, j: (i, j, 0)),
        out_shape=jax.ShapeDtypeStruct((b, s, width), BF16),
        compiler_params=_params(2),
        name="mem_attention",
    )(proj, kv)


def _merge_kernel(conv_ref, cw_ref, b_ref, c_ref, wa_ref, wb_ref, wc_ref, g_ref,
                  o_ref, wabf_ref, wbbf_ref, wcbf_ref, carry_ref, *, tiles_per_seq):
    i = pl.program_id(1)
    tn = o_ref.shape[1]

    @pl.when(i == 0)
    def _():
        wabf_ref[...] = wa_ref[...].astype(BF16)
        wbbf_ref[...] = wb_ref[...].astype(BF16)
        wcbf_ref[...] = wc_ref[...].astype(BF16)

    @pl.when(i % tiles_per_seq == 0)
    def _():
        carry_ref[...] = jnp.zeros_like(carry_ref)

    y_conv = _gated_conv_tile(conv_ref, cw_ref, carry_ref).astype(BF16)
    oa = jnp.dot(y_conv, wabf_ref[...], preferred_element_type=F32)
    ob = jnp.dot(b_ref[...], wbbf_ref[...], preferred_element_type=F32)
    oc = jnp.dot(c_ref[...], wcbf_ref[...], preferred_element_type=F32)
    gate = lambda br: g_ref[:, br * tn:(br + 1) * tn].astype(F32)
    o_ref[...] = (gate(0) * oa + gate(1) * ob + gate(2) * oc).astype(BF16)


def _merge(proj, conv_col, conv_w, yb, yc, wa, wb, wc, seq_len, d_model, tm=512, tn=COL_TILE):
    m, k = yb.shape
    lhs = pl.BlockSpec((tm, k), lambda j, i: (i, 0))
    wsp = pl.BlockSpec((k, tn), lambda j, i: (0, j))
    return pl.pallas_call(
        functools.partial(_merge_kernel, tiles_per_seq=seq_len // tm),
        grid=(d_model // tn, m // tm),
        in_specs=[pl.BlockSpec((tm, 3 * k), lambda j, i: (i, conv_col // (3 * k))),
                  pl.BlockSpec((CONV_TAPS, k), lambda j, i: (0, 0)),
                  lhs, lhs, wsp, wsp, wsp,
                  pl.BlockSpec((tm, 3 * tn), lambda j, i: (i, j))],
        out_specs=pl.BlockSpec((tm, tn), lambda j, i: (i, j)),
        out_shape=jax.ShapeDtypeStruct((m, d_model), BF16),
        scratch_shapes=[pltpu.VMEM((k, tn), BF16)] * 3 + [pltpu.VMEM((SUBLANES, k), F32)],
        compiler_params=_params(2),
        name="branch_merge",
    )(proj, conv_w, yb, yc, wa, wb, wc, proj)


def _outproj_kernel(a_ref, w_ref, x_ref, g_ref, x2_ref, h2_ref, wbf_ref):
    @pl.when(pl.program_id(0) == 0)
    def _():
        wbf_ref[...] = w_ref[...].astype(BF16)

    x2 = x_ref[...] + jnp.dot(a_ref[...], wbf_ref[...], preferred_element_type=F32)
    x2_ref[...] = x2
    ms = jnp.mean(x2 * x2, axis=-1, keepdims=True)
    h2_ref[...] = (x2 * lax.rsqrt(ms + EPS) * g_ref[...]).astype(BF16)


def _outproj(a, w, x2d, g, tm=512):
    m, d = x2d.shape
    row = pl.BlockSpec((tm, d), lambda i: (i, 0))
    return pl.pallas_call(
        _outproj_kernel,
        grid=(m // tm,),
        in_specs=[row, pl.BlockSpec((d, d), lambda i: (0, 0), pipeline_mode=pl.Buffered(1)), row,
                  pl.BlockSpec((1, d), lambda i: (0, 0))],
        out_specs=[row, row],
        out_shape=[jax.ShapeDtypeStruct((m, d), F32), jax.ShapeDtypeStruct((m, d), BF16)],
        scratch_shapes=[pltpu.VMEM((d, d), BF16)],
        compiler_params=_params(1),
        name="out_proj_norm2",
    )(a, w, x2d, g)


def _mlp_kernel(h_ref, wu_ref, wd_ref, x_ref, o_ref):
    f = pl.program_id(1)
    rows = x_ref.shape[0]

    @pl.when(f == 0)
    def _():
        o_ref[...] = jnp.zeros_like(o_ref)

    a = jnp.dot(h_ref[...], wu_ref[...].astype(BF16), preferred_element_type=F32)
    a = jnp.square(jnp.maximum(a, 0.0)).astype(BF16)
    o_ref[...] += jnp.dot(a, wd_ref[...].astype(BF16), preferred_element_type=F32)
    slab = pl.ds(pl.multiple_of(f * rows, rows), rows)
    o_ref[slab, :] += x_ref[...]


def _mlp(h2, w_up, w_down, x2, tm=1024, tf=512):
    m, d = h2.shape
    d_ff = w_up.shape[1]
    nf = d_ff // tf
    rows = tm // nf
    return pl.pallas_call(
        _mlp_kernel,
        grid=(m // tm, nf),
        in_specs=[
            pl.BlockSpec((tm, d), lambda i, f: (i, 0)),
            pl.BlockSpec((d, tf), lambda i, f: (0, f)),
            pl.BlockSpec((tf, d), lambda i, f: (f, 0)),
            pl.BlockSpec((rows, d), lambda i, f: (i * nf + f, 0)),
        ],
        out_specs=pl.BlockSpec((tm, d), lambda i, f: (i, 0)),
        out_shape=jax.ShapeDtypeStruct((m, d), F32),
        compiler_params=_params(2),
        name="relu2_mlp",
    )(h2, w_up, w_down, x2)


def kernel(x, mem, norm1_g, w_in, b_f, conv_w, fox_q_g, fox_k_g, mem_norm_g, w_mem_kv, mem_q_g, mem_k_g, w_conv_out, w_fox_out, w_mem_out, w_out, norm2_g, w_up, w_down):
    b, s, d = x.shape
    m = b * s
    conv_width = conv_w.shape[1]
    fox_width = FOX_HEADS * FOX_HEAD_DIM
    mem_width = MEM_HEADS * MEM_HEAD_DIM
    tile = COL_TILE
    q_row = 3 * conv_width
    k_row = q_row + fox_width
    v_row = k_row + fox_width
    f_row = v_row + fox_width
    mq_row = f_row + FOX_HEADS
    gate_row = mq_row + mem_width
    n_gate = N_BRANCHES * d
    ch_tiles = d // tile
    gate_rows = [gate_row + br * d + n * tile for n in range(ch_tiles) for br in range(N_BRANCHES)]
    rest_rows = [r for r in range(0, f_row, tile)] + [r for r in range(mq_row, gate_row, tile)]
    row_starts = tuple(gate_rows + rest_rows)
    conv_col, q_col, k_col, v_col, mq_col = (n_gate + r for r in (0, q_row, k_row, v_row, f_row))
    kinds = ([GATE] * len(gate_rows) + [PLAIN] * (q_row // tile) + [FOX_HEAD_DIM] * (2 * fox_width // tile)
             + [PLAIN] * (fox_width // tile) + [MEM_HEAD_DIM] * (mem_width // tile))
    gains = jnp.ones((len(row_starts) * tile,), F32)
    gains = gains.at[q_col:k_col].set(jnp.tile(fox_q_g * (LOG2_E / math.sqrt(FOX_HEAD_DIM)), FOX_HEADS))
    gains = gains.at[k_col:v_col].set(jnp.tile(fox_k_g, FOX_HEADS))
    gains = gains.at[mq_col:].set(jnp.tile(mem_q_g * (1.0 / math.sqrt(MEM_HEAD_DIM)), MEM_HEADS))

    row = lambda v: v.reshape(1, -1)
    bf = jnp.pad(b_f, (0, LANES - FOX_HEADS)).reshape(1, LANES)
    w_t = w_in.T

    h1, qa, ka = _norm1(x, row(norm1_g), w_t, f_row, bf)
    proj = _proj(h1.reshape(m, d), w_t, row_starts, kinds, gains.reshape(-1, tile))
    proj3 = proj.reshape(b, s, -1)

    y_fox = _fox(proj3, qa, ka, q_col, k_col, v_col)
    kv = _memkv(mem.reshape(-1, d), row(mem_norm_g), w_mem_kv, row(jnp.tile(mem_k_g, MEM_HEADS)))
    y_mem = _memattn(proj3, mq_col, kv.reshape(b, -1, 2 * mem_width))

    merged = _merge(proj, conv_col, conv_w, y_fox.reshape(m, -1), y_mem.reshape(m, -1),
                    w_conv_out, w_fox_out, w_mem_out, s, d)
    x2, h2 = _outproj(merged, w_out, x.reshape(m, d), row(norm2_g))
    out = _mlp(h2, w_up, w_down, x2)
    return out.reshape(b, s, d)
```

```python
import functools
import math

import jax
import jax.numpy as jnp
import numpy as np
from jax import lax
from jax.experimental import pallas as pl
from jax.experimental.pallas import tpu as pltpu

F32 = jnp.float32
BF16 = jnp.bfloat16

EPS = 1e-6
LOG2_E = math.log2(math.e)
LANES = 128
SUBLANES = 8
BF16_ROWS = 2 * SUBLANES
CONV_TAPS = 3
FOX_HEADS = 8
FOX_HEAD_DIM = 128
MEM_HEADS = 4
MEM_HEAD_DIM = 256
N_BRANCHES = 3

VMEM_LIMIT_BYTES = 56 * 1024 * 1024
COL_TILE = 1024
NT_DIMS = (((1,), (1,)), ((), ()))


def _params(n_axes):
    return pltpu.CompilerParams(
        dimension_semantics=("arbitrary",) * n_axes,
        vmem_limit_bytes=VMEM_LIMIT_BYTES)


def _store_group_rms(o_ref, a, gain, width):
    for s in range(0, a.shape[1], width):
        blk = a[:, s:s + width]
        ms = jnp.mean(blk * blk, axis=-1, keepdims=True)
        o_ref[:, s:s + width] = (blk * lax.rsqrt(ms + EPS) * gain[:, s:s + width]).astype(o_ref.dtype)


def _split3(c):
    hi = c.astype(BF16).astype(F32)
    r1 = c - hi
    mid = r1.astype(BF16).astype(F32)
    lo = (r1 - mid).astype(BF16).astype(F32)
    return hi, mid, lo


ONE_LANE = 3 * FOX_HEADS


def _decay_selector():
    sel = np.zeros((LANES, FOX_HEADS, LANES), np.float32)
    for hd in range(FOX_HEADS):
        for piece in range(3):
            sel[ONE_LANE, hd, piece * FOX_HEADS + hd] = 1.0
            sel[piece * FOX_HEADS + hd, hd, ONE_LANE + piece] = -1.0
    return jnp.asarray(sel.reshape(LANES, -1), BF16)


def _norm1_kernel(x_ref, g_ref, wf_ref, bf_ref, sel_ref, h_ref, qa_ref, ka_ref, carry_ref):
    tm = x_ref.shape[0]

    @pl.when(pl.program_id(1) == 0)
    def _():
        carry_ref[...] = jnp.zeros_like(carry_ref)

    wf = wf_ref[...]
    wf_hi = wf.astype(BF16).astype(F32)
    w2 = jnp.concatenate(
        [wf_hi, wf - wf_hi, jnp.zeros((LANES - 2 * FOX_HEADS, wf.shape[1]), F32)], axis=0).astype(BF16)
    row = lax.broadcasted_iota(jnp.int32, (CUMSUM_ROWS, CUMSUM_ROWS), 0)
    col = lax.broadcasted_iota(jnp.int32, (CUMSUM_ROWS, CUMSUM_ROWS), 1)
    tri = jnp.where(row >= col, 1.0, 0.0).astype(BF16)
    lane = lax.broadcasted_iota(jnp.int32, (CUMSUM_ROWS, LANES), 1)

    def pack3(hi, mid, lo, rest):
        return jnp.where(
            lane < FOX_HEADS, hi,
            jnp.where(lane < 2 * FOX_HEADS, pltpu.roll(mid, FOX_HEADS, 1),
                      jnp.where(lane < ONE_LANE, pltpu.roll(lo, 2 * FOX_HEADS, 1), rest))).astype(BF16)

    blocks = [pl.ds(blk * CUMSUM_ROWS, CUMSUM_ROWS) for blk in range(tm // CUMSUM_ROWS)]

    def logits(rows):
        xf = x_ref[rows, :]
        ms = jnp.mean(xf * xf, axis=-1, keepdims=True)
        y_hi = (xf * lax.rsqrt(ms + EPS) * g_ref[...]).astype(BF16)
        h_ref[rows, :] = y_hi
        return lax.dot_general(y_hi, w2, NT_DIMS, preferred_element_type=F32)

    def local_cumsum(z2):
        z = z2 + pltpu.roll(z2, LANES - FOX_HEADS, 1) + bf_ref[...]
        log_f = -LOG2_E * (jnp.maximum(-z, 0.0) + jnp.log1p(jnp.exp(-jnp.abs(z))))
        return jnp.dot(tri, pack3(*_split3(log_f), 0.0), preferred_element_type=F32)

    def decay_columns(rows, loc, run):
        c = (loc + pltpu.roll(loc, LANES - FOX_HEADS, 1)
             + pltpu.roll(loc, LANES - 2 * FOX_HEADS, 1) + run)
        packed = pack3(*_split3(c), jnp.where(lane < ONE_LANE + 3, 1.0, 0.0))
        qa_ref[rows, :] = packed
        ka_ref[rows, :] = jnp.dot(packed, sel_ref[...], preferred_element_type=F32).astype(BF16)
        return c[CUMSUM_ROWS - 1:CUMSUM_ROWS, :]

    z2s = [logits(rows) for rows in blocks]
    locs = [local_cumsum(z2) for z2 in z2s]
    run = carry_ref[0:1, :]
    for rows, loc in zip(blocks, locs):
        run = decay_columns(rows, loc, run)
    carry_ref[0:1, :] = run


def _norm1(x, g, w_t, f_row, bf, tm=1024):
    b, s, d = x.shape
    return pl.pallas_call(
        _norm1_kernel,
        grid=(b, s // tm),
        in_specs=[
            pl.BlockSpec((None, tm, d), lambda i, j: (i, j, 0)),
            pl.BlockSpec((1, d), lambda i, j: (0, 0)),
            pl.BlockSpec((FOX_HEADS, d), lambda i, j: (f_row // FOX_HEADS, 0)),
            pl.BlockSpec((1, LANES), lambda i, j: (0, 0)),
            pl.BlockSpec((LANES, FOX_HEADS * LANES), lambda i, j: (0, 0)),
        ],
        out_specs=[
            pl.BlockSpec((None, tm, d), lambda i, j: (i, j, 0)),
            pl.BlockSpec((None, tm, LANES), lambda i, j: (i, j, 0)),
            pl.BlockSpec((None, tm, FOX_HEADS * LANES), lambda i, j: (i, j, 0)),
        ],
        out_shape=[
            jax.ShapeDtypeStruct((b, s, d), BF16),
            jax.ShapeDtypeStruct((b, s, LANES), BF16),
            jax.ShapeDtypeStruct((b, s, FOX_HEADS * LANES), BF16),
        ],
        scratch_shapes=[pltpu.VMEM((SUBLANES, LANES), F32)],
        compiler_params=_params(2),
        name="norm1_decay",
    )(x, g, w_t, bf, _decay_selector())


CUMSUM_ROWS = 256
PLAIN, GATE = 0, -1


def _proj_kernel(h_ref, w_ref, gain_ref, wu_ref, wd_ref, o_ref, wubf_ref, wdbf_ref, wbf_ref, *,
                 kinds, cast_tiles):
    @pl.when(pl.program_id(1) == 0)
    def _():
        wbf_ref[...] = w_ref[...].astype(BF16)

    def tile(kind, cast):
        acc = lax.dot_general(h_ref[...], wbf_ref[...], NT_DIMS, preferred_element_type=F32)
        if cast:
            wubf_ref[...] = wu_ref[...].astype(BF16)
            wdbf_ref[...] = wd_ref[...].astype(BF16)
        if kind == PLAIN:
            o_ref[...] = acc.astype(BF16)
        elif kind == GATE:
            o_ref[...] = (0.5 * jnp.tanh(0.5 * acc) + 0.5).astype(BF16)
        else:
            _store_group_rms(o_ref, acc, gain_ref[...], kind)

    j = pl.program_id(0)
    jobs = [(kind, t < cast_tiles) for t, kind in enumerate(kinds)]
    for job in sorted(set(jobs)):
        hit = functools.reduce(jnp.logical_or, [j == t for t, jb in enumerate(jobs) if jb == job])
        pl.when(hit)(functools.partial(tile, *job))


def _proj(h, w_t, row_starts, kinds, gains, w_up, w_down, tm=1024, tn=COL_TILE, cast_tiles=8):
    m, k = h.shape
    n_tiles = len(row_starts)
    assert all(r % SUBLANES == 0 for r in row_starts)
    cast_steps = cast_tiles * (m // tm)
    up_rows, down_rows = w_up.shape[0] // cast_steps, w_down.shape[0] // cast_steps
    assert cast_tiles <= n_tiles and up_rows % BF16_ROWS == 0 and down_rows % BF16_ROWS == 0
    assert up_rows * cast_steps == w_up.shape[0] and down_rows * cast_steps == w_down.shape[0]

    def w_rows(j, i):
        start = jnp.int32(row_starts[0])
        for t in range(1, n_tiles):
            start = jnp.where(j == t, row_starts[t], start)
        return (pl.multiple_of(start, SUBLANES), 0)

    slab = lambda j, i: (jnp.minimum(j * (m // tm) + i, cast_steps - 1), 0)
    up_spec = pl.BlockSpec((up_rows, w_up.shape[1]), slab)
    down_spec = pl.BlockSpec((down_rows, w_down.shape[1]), slab)
    return pl.pallas_call(
        functools.partial(_proj_kernel, kinds=tuple(kinds), cast_tiles=cast_tiles),
        grid=(n_tiles, m // tm),
        in_specs=[
            pl.BlockSpec((tm, k), lambda j, i: (i, 0)),
            pl.BlockSpec((pl.Element(tn), pl.Element(k)), w_rows),
            pl.BlockSpec((None, 1, tn), lambda j, i: (j, 0, 0)),
            up_spec, down_spec,
        ],
        out_specs=[pl.BlockSpec((tm, tn), lambda j, i: (i, j)), up_spec, down_spec],
        out_shape=[jax.ShapeDtypeStruct((m, n_tiles * tn), BF16),
                   jax.ShapeDtypeStruct(w_up.shape, BF16), jax.ShapeDtypeStruct(w_down.shape, BF16)],
        scratch_shapes=[pltpu.VMEM((tn, k), BF16)],
        compiler_params=_params(2),
        name="in_proj",
    )(h, w_t, gains.reshape(n_tiles, 1, tn), w_up, w_down)


def _gated_conv_tile(conv_ref, w_ref, carry_ref):
    tm = conv_ref.shape[0]
    k = w_ref.shape[1]
    u = conv_ref[:, k:2 * k].astype(F32) * conv_ref[:, 2 * k:3 * k].astype(F32)
    w = w_ref[...]
    taps = lambda u2, u1, u0: w[0:1, :] * u2 + w[1:2, :] * u1 + w[2:3, :] * u0
    conv = taps(pltpu.roll(u, 2, 0), pltpu.roll(u, 1, 0), u)
    top = u[0:SUBLANES, :]
    row = lax.broadcasted_iota(jnp.int32, (SUBLANES, 1), 0)
    prev1 = carry_ref[1:2, :]
    prev2 = carry_ref[0:1, :]
    top1 = jnp.where(row == 0, prev1, pltpu.roll(top, 1, 0))
    top2 = jnp.where(row == 0, prev2, jnp.where(row == 1, prev1, pltpu.roll(top, 2, 0)))
    conv = jnp.concatenate([taps(top2, top1, top), conv[SUBLANES:, :]], axis=0)
    carry_ref[0:2, :] = u[tm - 2:tm, :]
    return conv_ref[:, 0:k].astype(F32) * conv


def _fox_kernel(q_ref, qa_ref, k_ref, ka_ref, v_ref, o_ref, vt_ref, s_ref, m_ref, l_ref, acc_ref, *,
                t):
    n_tiles = q_ref.shape[0] // t
    assert n_tiles % 2 == 0
    d = FOX_HEAD_DIM
    n_heads = q_ref.shape[1] // d
    j = pl.program_id(2)

    @pl.when(j == 0)
    def _():
        for g in range(n_heads):
            for blk in range(v_ref.shape[0] // t):
                v = v_ref[blk * t:(blk + 1) * t, g * d:(g + 1) * d]
                vt_ref[g, blk, :d] = v.astype(F32).T.astype(BF16)
                vt_ref[g, blk, d:] = jnp.ones((vt_ref.shape[2] - d, t), BF16)

    def scores(tile, kb, g):
        q = jnp.concatenate([q_ref[tile * t:(tile + 1) * t, g * d:(g + 1) * d],
                             qa_ref[tile * t:(tile + 1) * t, :]], axis=1)
        rows = pl.ds(pl.multiple_of(kb * t, t), t)
        k = jnp.concatenate([k_ref[rows, g * d:(g + 1) * d], ka_ref[rows, g * d:(g + 1) * d]], axis=1)
        return lax.dot_general(k, q, NT_DIMS, preferred_element_type=F32)

    def reset(g):
        m_ref[g] = jnp.full(m_ref.shape[1:], -jnp.inf, F32)
        l_ref[g] = jnp.zeros(l_ref.shape[1:], F32)
        acc_ref[g] = jnp.zeros(acc_ref.shape[1:], F32)

    def update(g, m_new, alpha, pv):
        l_ref[g] = alpha * l_ref[g] + pv[d:d + 1]
        acc_ref[g] = alpha * acc_ref[g] + pv[:d]
        m_ref[g] = m_new

    def consume(kb, slot, g):
        s = s_ref[slot, g]
        m_prev = m_ref[g]
        m_new = jnp.maximum(m_prev, jnp.max(s, axis=0, keepdims=True))
        p = jnp.exp2(s - m_new)
        pv = jnp.dot(vt_ref[g, kb], p.astype(BF16), preferred_element_type=F32)
        update(g, m_new, jnp.exp2(m_prev - m_new), pv)

    def causal(s):
        kpos = lax.broadcasted_iota(jnp.int32, s.shape, 0)
        qpos = lax.broadcasted_iota(jnp.int32, s.shape, 1)
        return jnp.where(qpos >= kpos, s, -jnp.inf)

    def consume_diagonal(kb, slot, g):
        h = t // 2
        s_up = causal(s_ref[slot, g, :h, :])
        s_lo = causal(s_ref[slot, g, h:, h:])
        m_up = jnp.max(s_up, axis=0, keepdims=True)
        m_blk = jnp.concatenate(
            [m_up[:, :h], jnp.maximum(m_up[:, h:], jnp.max(s_lo, axis=0, keepdims=True))], axis=1)
        m_prev = m_ref[g]
        m_new = jnp.maximum(m_prev, m_blk)
        p_up = jnp.exp2(s_up - m_new).astype(BF16)
        p_lo = jnp.exp2(s_lo - m_new[:, h:]).astype(BF16)
        vt = vt_ref[g, kb]
        pv_up = jnp.dot(vt[:, :h], p_up, preferred_element_type=F32)
        pv_lo = jnp.dot(vt[:, h:], p_lo, preferred_element_type=F32)
        pv = jnp.concatenate([pv_up[:, :h], pv_up[:, h:] + pv_lo], axis=1)
        update(g, m_new, jnp.exp2(m_prev - m_new), pv)

    def advance(tile, kb, slot):
        for g in range(n_heads):
            s_ref[1 - slot, g] = scores(tile, kb + 1, g)
            consume(kb, slot, g)

    def emit(tile, g):
        o_ref[tile * t:(tile + 1) * t, g * d:(g + 1) * d] = (acc_ref[g] / l_ref[g]).T.astype(BF16)

    slot = 0
    for g in range(n_heads):
        reset(g)
        s_ref[slot, g] = scores(0, 0, g)
    for r in range(n_tiles):
        def pairs(pair, carry, r=r, slot=slot):
            advance(r, 2 * pair, slot)
            advance(r, 2 * pair + 1, 1 - slot)
            return carry

        lax.fori_loop(0, (n_tiles // 2) * j + r // 2, pairs, 0)
        diagonal = n_tiles * j + r
        if r % 2 == 1:
            advance(r, diagonal - 1, slot)
            slot = 1 - slot
        for g in range(n_heads):
            if r + 1 < n_tiles:
                s_ref[1 - slot, g] = scores(r + 1, 0, g)
            consume_diagonal(diagonal, slot, g)
            emit(r, g)
            if r + 1 < n_tiles:
                reset(g)
        slot = 1 - slot


def _fox(proj, qa, ka, q_col, k_col, v_col, t=512, tiles_per_step=2, heads_per_step=4):
    b, s, _ = proj.shape
    d = FOX_HEAD_DIM
    w = heads_per_step * d
    rows = tiles_per_step * t
    tiles = lambda col0: pl.BlockSpec((None, rows, w), lambda bi, h, j: (bi, j, col0 // w + h))
    seq = lambda col0: pl.BlockSpec((None, s, w), lambda bi, h, j: (bi, 0, col0 // w + h))
    return pl.pallas_call(
        functools.partial(_fox_kernel, t=t),
        grid=(b, FOX_HEADS // heads_per_step, s // rows),
        in_specs=[tiles(q_col), pl.BlockSpec((None, rows, LANES), lambda bi, h, j: (bi, j, 0)),
                  seq(k_col), seq(0), seq(v_col)],
        out_specs=tiles(0),
        out_shape=jax.ShapeDtypeStruct((b, s, FOX_HEADS * d), BF16),
        scratch_shapes=[pltpu.VMEM((heads_per_step, s // t, d + BF16_ROWS, t), BF16),
                        pltpu.VMEM((2, heads_per_step, t, t), F32),
                        pltpu.VMEM((heads_per_step, 1, t), F32),
                        pltpu.VMEM((heads_per_step, 1, t), F32),
                        pltpu.VMEM((heads_per_step, d, t), F32)],
        compiler_params=_params(3),
        name="fox_attention",
    )(proj, qa, proj, ka, proj)


def _memkv_kernel(mem_ref, g_ref, w_ref, kg_ref, o_ref):
    xf = mem_ref[...]
    ms = jnp.mean(xf * xf, axis=-1, keepdims=True)
    y = (xf * lax.rsqrt(ms + EPS) * g_ref[...]).astype(BF16)
    acc = jnp.dot(y, w_ref[...].astype(BF16), preferred_element_type=F32)

    @pl.when(pl.program_id(0) == 0)
    def _():
        _store_group_rms(o_ref, acc, kg_ref[...], MEM_HEAD_DIM)

    @pl.when(pl.program_id(0) == 1)
    def _():
        o_ref[...] = acc.astype(BF16)


def _memkv(mem2d, g, w, kg):
    m, d = mem2d.shape
    width = MEM_HEADS * MEM_HEAD_DIM
    return pl.pallas_call(
        _memkv_kernel,
        grid=(2,),
        in_specs=[
            pl.BlockSpec((m, d), lambda j: (0, 0)),
            pl.BlockSpec((1, d), lambda j: (0, 0)),
            pl.BlockSpec((d, width), lambda j: (0, j)),
            pl.BlockSpec((1, width), lambda j: (0, 0)),
        ],
        out_specs=pl.BlockSpec((m, width), lambda j: (0, j)),
        out_shape=jax.ShapeDtypeStruct((m, 2 * width), BF16),
        compiler_params=_params(1),
        name="mem_kv",
    )(mem2d, g, w, kg)


def _memattn_kernel(q_ref, kv_ref, o_ref):
    width = MEM_HEADS * MEM_HEAD_DIM
    for h in range(MEM_HEADS):
        lo, hi = h * MEM_HEAD_DIM, (h + 1) * MEM_HEAD_DIM
        q = q_ref[:, lo:hi]
        k = kv_ref[:, lo:hi]
        v = kv_ref[:, width + lo:width + hi]
        s = lax.dot_general(q, k, NT_DIMS, preferred_element_type=F32)
        p = jnp.exp(s - jnp.max(s, axis=-1, keepdims=True))
        l = jnp.sum(p, axis=-1, keepdims=True)
        o = jnp.dot(p.astype(BF16), v, preferred_element_type=F32) / l
        o_ref[:, lo:hi] = o.astype(BF16)


def _memattn(proj, q_col, kv, tm=1024):
    b, s, _ = proj.shape
    width = MEM_HEADS * MEM_HEAD_DIM
    n_mem = kv.shape[1]
    return pl.pallas_call(
        _memattn_kernel,
        grid=(b, s // tm),
        in_specs=[
            pl.BlockSpec((None, tm, width), lambda i, j: (i, j, q_col // width)),
            pl.BlockSpec((None, n_mem, 2 * width), lambda i, j: (i, 0, 0)),
        ],
        out_specs=pl.BlockSpec((None, tm, width), lambda i, j: (i, j, 0)),
        out_shape=jax.ShapeDtypeStruct((b, s, width), BF16),
        compiler_params=_params(2),
        name="mem_attention",
    )(proj, kv)


def _merge_kernel(conv_ref, cw_ref, b_ref, c_ref, wa_ref, wb_ref, wc_ref, g_ref,
                  o_ref, wabf_ref, wbbf_ref, wcbf_ref, carry_ref, *, tiles_per_seq):
    i = pl.program_id(1)
    tn = o_ref.shape[1]

    @pl.when(i == 0)
    def _():
        wabf_ref[...] = wa_ref[...].astype(BF16)
        wbbf_ref[...] = wb_ref[...].astype(BF16)
        wcbf_ref[...] = wc_ref[...].astype(BF16)

    @pl.when(i % tiles_per_seq == 0)
    def _():
        carry_ref[...] = jnp.zeros_like(carry_ref)

    y_conv = _gated_conv_tile(conv_ref, cw_ref, carry_ref).astype(BF16)
    oa = jnp.dot(y_conv, wabf_ref[...], preferred_element_type=F32)
    ob = jnp.dot(b_ref[...], wbbf_ref[...], preferred_element_type=F32)
    oc = jnp.dot(c_ref[...], wcbf_ref[...], preferred_element_type=F32)
    gate = lambda br: g_ref[:, br * tn:(br + 1) * tn].astype(F32)
    o_ref[...] = (gate(0) * oa + gate(1) * ob + gate(2) * oc).astype(BF16)


def _merge(proj, conv_col, conv_w, yb, yc, wa, wb, wc, seq_len, d_model, tm=512, tn=COL_TILE):
    m, k = yb.shape
    lhs = pl.BlockSpec((tm, k), lambda j, i: (i, 0))
    wsp = pl.BlockSpec((k, tn), lambda j, i: (0, j))
    return pl.pallas_call(
        functools.partial(_merge_kernel, tiles_per_seq=seq_len // tm),
        grid=(d_model // tn, m // tm),
        in_specs=[pl.BlockSpec((tm, 3 * k), lambda j, i: (i, conv_col // (3 * k))),
                  pl.BlockSpec((CONV_TAPS, k), lambda j, i: (0, 0)),
                  lhs, lhs, wsp, wsp, wsp,
                  pl.BlockSpec((tm, 3 * tn), lambda j, i: (i, j))],
        out_specs=pl.BlockSpec((tm, tn), lambda j, i: (i, j)),
        out_shape=jax.ShapeDtypeStruct((m, d_model), BF16),
        scratch_shapes=[pltpu.VMEM((k, tn), BF16)] * 3 + [pltpu.VMEM((SUBLANES, k), F32)],
        compiler_params=_params(2),
        name="branch_merge",
    )(proj, conv_w, yb, yc, wa, wb, wc, proj)


def _outproj_kernel(a_ref, w_ref, x_ref, g_ref, x2_ref, h2_ref, wbf_ref):
    @pl.when(pl.program_id(0) == 0)
    def _():
        wbf_ref[...] = w_ref[...].astype(BF16)

    x2 = x_ref[...] + jnp.dot(a_ref[...], wbf_ref[...], preferred_element_type=F32)
    x2_ref[...] = x2
    ms = jnp.mean(x2 * x2, axis=-1, keepdims=True)
    h2_ref[...] = (x2 * lax.rsqrt(ms + EPS) * g_ref[...]).astype(BF16)


def _outproj(a, w, x2d, g, tm=512):
    m, d = x2d.shape
    row = pl.BlockSpec((tm, d), lambda i: (i, 0))
    return pl.pallas_call(
        _outproj_kernel,
        grid=(m // tm,),
        in_specs=[row, pl.BlockSpec((d, d), lambda i: (0, 0), pipeline_mode=pl.Buffered(1)), row,
                  pl.BlockSpec((1, d), lambda i: (0, 0))],
        out_specs=[row, row],
        out_shape=[jax.ShapeDtypeStruct((m, d), F32), jax.ShapeDtypeStruct((m, d), BF16)],
        scratch_shapes=[pltpu.VMEM((d, d), BF16)],
        compiler_params=_params(1),
        name="out_proj_norm2",
    )(a, w, x2d, g)


def _mlp_kernel(h_ref, wu_ref, wd_ref, x_ref, o_ref):
    f = pl.program_id(1)
    rows = x_ref.shape[0]

    @pl.when(f == 0)
    def _():
        o_ref[...] = jnp.zeros_like(o_ref)

    a = jnp.dot(h_ref[...], wu_ref[...], preferred_element_type=F32)
    a = jnp.square(jnp.maximum(a, 0.0)).astype(BF16)
    o_ref[...] += jnp.dot(a, wd_ref[...], preferred_element_type=F32)
    slab = pl.ds(pl.multiple_of(f * rows, rows), rows)
    o_ref[slab, :] += x_ref[...]


def _mlp(h2, w_up, w_down, x2, tm=1024, tf=1024):
    m, d = h2.shape
    d_ff = w_up.shape[1]
    nf = d_ff // tf
    rows = tm // nf
    return pl.pallas_call(
        _mlp_kernel,
        grid=(m // tm, nf),
        in_specs=[
            pl.BlockSpec((tm, d), lambda i, f: (i, 0)),
            pl.BlockSpec((d, tf), lambda i, f: (0, f)),
            pl.BlockSpec((tf, d), lambda i, f: (f, 0)),
            pl.BlockSpec((rows, d), lambda i, f: (i * nf + f, 0)),
        ],
        out_specs=pl.BlockSpec((tm, d), lambda i, f: (i, 0)),
        out_shape=jax.ShapeDtypeStruct((m, d), F32),
        compiler_params=_params(2),
        name="relu2_mlp",
    )(h2, w_up, w_down, x2)


def kernel(x, mem, norm1_g, w_in, b_f, conv_w, fox_q_g, fox_k_g, mem_norm_g, w_mem_kv, mem_q_g, mem_k_g, w_conv_out, w_fox_out, w_mem_out, w_out, norm2_g, w_up, w_down):
    b, s, d = x.shape
    m = b * s
    conv_width = conv_w.shape[1]
    fox_width = FOX_HEADS * FOX_HEAD_DIM
    mem_width = MEM_HEADS * MEM_HEAD_DIM
    tile = COL_TILE
    q_row = 3 * conv_width
    k_row = q_row + fox_width
    v_row = k_row + fox_width
    f_row = v_row + fox_width
    mq_row = f_row + FOX_HEADS
    gate_row = mq_row + mem_width
    n_gate = N_BRANCHES * d
    ch_tiles = d // tile
    gate_rows = [gate_row + br * d + n * tile for n in range(ch_tiles) for br in range(N_BRANCHES)]
    rest_rows = [r for r in range(0, f_row, tile)] + [r for r in range(mq_row, gate_row, tile)]
    row_starts = tuple(gate_rows + rest_rows)
    conv_col, q_col, k_col, v_col, mq_col = (n_gate + r for r in (0, q_row, k_row, v_row, f_row))
    kinds = ([GATE] * len(gate_rows) + [PLAIN] * (q_row // tile) + [FOX_HEAD_DIM] * (2 * fox_width // tile)
             + [PLAIN] * (fox_width // tile) + [MEM_HEAD_DIM] * (mem_width // tile))
    gains = jnp.ones((len(row_starts) * tile,), F32)
    gains = gains.at[q_col:k_col].set(jnp.tile(fox_q_g * (LOG2_E / math.sqrt(FOX_HEAD_DIM)), FOX_HEADS))
    gains = gains.at[k_col:v_col].set(jnp.tile(fox_k_g, FOX_HEADS))
    gains = gains.at[mq_col:].set(jnp.tile(mem_q_g * (1.0 / math.sqrt(MEM_HEAD_DIM)), MEM_HEADS))

    row = lambda v: v.reshape(1, -1)
    bf = jnp.pad(b_f, (0, LANES - FOX_HEADS)).reshape(1, LANES)
    w_t = w_in.T

    h1, qa, ka = _norm1(x, row(norm1_g), w_t, f_row, bf)
    proj, w_up_bf, w_down_bf = _proj(h1.reshape(m, d), w_t, row_starts, kinds, gains.reshape(-1, tile),
                                     w_up, w_down)
    proj3 = proj.reshape(b, s, -1)

    y_fox = _fox(proj3, qa, ka, q_col, k_col, v_col)
    kv = _memkv(mem.reshape(-1, d), row(mem_norm_g), w_mem_kv, row(jnp.tile(mem_k_g, MEM_HEADS)))
    y_mem = _memattn(proj3, mq_col, kv.reshape(b, -1, 2 * mem_width))

    merged = _merge(proj, conv_col, conv_w, y_fox.reshape(m, -1), y_mem.reshape(m, -1),
                    w_conv_out, w_fox_out, w_mem_out, s, d)
    x2, h2 = _outproj(merged, w_out, x.reshape(m, d), row(norm2_g))
    out = _mlp(h2, w_up_bf, w_down_bf, x2)
    return out.reshape(b, s, d)
```

```python
import functools
import math

import jax
import jax.numpy as jnp
import numpy as np
from jax import lax
from jax.experimental import pallas as pl
from jax.experimental.pallas import tpu as pltpu

F32 = jnp.float32
BF16 = jnp.bfloat16

EPS = 1e-6
LOG2_E = math.log2(math.e)
LANES = 128
SUBLANES = 8
BF16_ROWS = 2 * SUBLANES
CONV_TAPS = 3
FOX_HEADS = 8
FOX_HEAD_DIM = 128
MEM_HEADS = 4
MEM_HEAD_DIM = 256
N_BRANCHES = 3

VMEM_LIMIT_BYTES = 56 * 1024 * 1024
COL_TILE = 1024
NT_DIMS = (((1,), (1,)), ((), ()))


def _params(n_axes):
    return pltpu.CompilerParams(
        dimension_semantics=("arbitrary",) * n_axes,
        vmem_limit_bytes=VMEM_LIMIT_BYTES)


def _store_group_rms(o_ref, a, gain, width):
    for s in range(0, a.shape[1], width):
        blk = a[:, s:s + width]
        ms = jnp.mean(blk * blk, axis=-1, keepdims=True)
        o_ref[:, s:s + width] = (blk * lax.rsqrt(ms + EPS) * gain[:, s:s + width]).astype(o_ref.dtype)


def _split3(c):
    hi = c.astype(BF16).astype(F32)
    r1 = c - hi
    mid = r1.astype(BF16).astype(F32)
    lo = (r1 - mid).astype(BF16).astype(F32)
    return hi, mid, lo


ONE_LANE = 3 * FOX_HEADS


def _decay_selector():
    sel = np.zeros((LANES, FOX_HEADS, LANES), np.float32)
    for hd in range(FOX_HEADS):
        for piece in range(3):
            sel[ONE_LANE, hd, piece * FOX_HEADS + hd] = 1.0
            sel[piece * FOX_HEADS + hd, hd, ONE_LANE + piece] = -1.0
    return jnp.asarray(sel.reshape(LANES, -1), BF16)


def _norm1_kernel(x_ref, g_ref, wf_ref, bf_ref, sel_ref, h_ref, qa_ref, ka_ref, carry_ref):
    tm = x_ref.shape[0]

    @pl.when(pl.program_id(1) == 0)
    def _():
        carry_ref[...] = jnp.zeros_like(carry_ref)

    wf = wf_ref[...]
    wf_hi = wf.astype(BF16).astype(F32)
    w2 = jnp.concatenate(
        [wf_hi, wf - wf_hi, jnp.zeros((LANES - 2 * FOX_HEADS, wf.shape[1]), F32)], axis=0).astype(BF16)
    row = lax.broadcasted_iota(jnp.int32, (CUMSUM_ROWS, CUMSUM_ROWS), 0)
    col = lax.broadcasted_iota(jnp.int32, (CUMSUM_ROWS, CUMSUM_ROWS), 1)
    tri = jnp.where(row >= col, 1.0, 0.0).astype(BF16)
    lane = lax.broadcasted_iota(jnp.int32, (CUMSUM_ROWS, LANES), 1)

    def pack3(hi, mid, lo, rest):
        return jnp.where(
            lane < FOX_HEADS, hi,
            jnp.where(lane < 2 * FOX_HEADS, pltpu.roll(mid, FOX_HEADS, 1),
                      jnp.where(lane < ONE_LANE, pltpu.roll(lo, 2 * FOX_HEADS, 1), rest))).astype(BF16)

    blocks = [pl.ds(blk * CUMSUM_ROWS, CUMSUM_ROWS) for blk in range(tm // CUMSUM_ROWS)]

    def logits(rows):
        xf = x_ref[rows, :]
        ms = jnp.mean(xf * xf, axis=-1, keepdims=True)
        y_hi = (xf * lax.rsqrt(ms + EPS) * g_ref[...]).astype(BF16)
        h_ref[rows, :] = y_hi
        return lax.dot_general(y_hi, w2, NT_DIMS, preferred_element_type=F32)

    def local_cumsum(z2):
        z = z2 + pltpu.roll(z2, LANES - FOX_HEADS, 1) + bf_ref[...]
        log_f = -LOG2_E * (jnp.maximum(-z, 0.0) + jnp.log1p(jnp.exp(-jnp.abs(z))))
        return jnp.dot(tri, pack3(*_split3(log_f), 0.0), preferred_element_type=F32)

    def decay_columns(rows, loc, run):
        c = (loc + pltpu.roll(loc, LANES - FOX_HEADS, 1)
             + pltpu.roll(loc, LANES - 2 * FOX_HEADS, 1) + run)
        packed = pack3(*_split3(c), jnp.where(lane < ONE_LANE + 3, 1.0, 0.0))
        qa_ref[rows, :] = packed
        ka_ref[rows, :] = jnp.dot(packed, sel_ref[...], preferred_element_type=F32).astype(BF16)
        return c[CUMSUM_ROWS - 1:CUMSUM_ROWS, :]

    z2s = [logits(rows) for rows in blocks]
    locs = [local_cumsum(z2) for z2 in z2s]
    run = carry_ref[0:1, :]
    for rows, loc in zip(blocks, locs):
        run = decay_columns(rows, loc, run)
    carry_ref[0:1, :] = run


def _norm1(x, g, w_t, f_row, bf, tm=1024):
    b, s, d = x.shape
    return pl.pallas_call(
        _norm1_kernel,
        grid=(b, s // tm),
        in_specs=[
            pl.BlockSpec((None, tm, d), lambda i, j: (i, j, 0)),
            pl.BlockSpec((1, d), lambda i, j: (0, 0)),
            pl.BlockSpec((FOX_HEADS, d), lambda i, j: (f_row // FOX_HEADS, 0)),
            pl.BlockSpec((1, LANES), lambda i, j: (0, 0)),
            pl.BlockSpec((LANES, FOX_HEADS * LANES), lambda i, j: (0, 0)),
        ],
        out_specs=[
            pl.BlockSpec((None, tm, d), lambda i, j: (i, j, 0)),
            pl.BlockSpec((None, tm, LANES), lambda i, j: (i, j, 0)),
            pl.BlockSpec((None, tm, FOX_HEADS * LANES), lambda i, j: (i, j, 0)),
        ],
        out_shape=[
            jax.ShapeDtypeStruct((b, s, d), BF16),
            jax.ShapeDtypeStruct((b, s, LANES), BF16),
            jax.ShapeDtypeStruct((b, s, FOX_HEADS * LANES), BF16),
        ],
        scratch_shapes=[pltpu.VMEM((SUBLANES, LANES), F32)],
        compiler_params=_params(2),
        name="norm1_decay",
    )(x, g, w_t, bf, _decay_selector())


CUMSUM_ROWS = 256
PLAIN, GATE = 0, -1


def _proj_kernel(h_ref, w_ref, gain_ref, wu_ref, wd_ref, o_ref, wubf_ref, wdbf_ref, wbf_ref, *,
                 kinds, cast_tiles):
    @pl.when(pl.program_id(1) == 0)
    def _():
        wbf_ref[...] = w_ref[...].astype(BF16)

    def tile(kind, cast):
        acc = lax.dot_general(h_ref[...], wbf_ref[...], NT_DIMS, preferred_element_type=F32)
        if cast:
            wubf_ref[...] = wu_ref[...].astype(BF16)
            wdbf_ref[...] = wd_ref[...].astype(BF16)
        if kind == PLAIN:
            o_ref[...] = acc.astype(BF16)
        elif kind == GATE:
            o_ref[...] = (0.5 * jnp.tanh(0.5 * acc) + 0.5).astype(BF16)
        else:
            _store_group_rms(o_ref, acc, gain_ref[...], kind)

    j = pl.program_id(0)
    jobs = [(kind, t < cast_tiles) for t, kind in enumerate(kinds)]
    for job in sorted(set(jobs)):
        hit = functools.reduce(jnp.logical_or, [j == t for t, jb in enumerate(jobs) if jb == job])
        pl.when(hit)(functools.partial(tile, *job))


def _proj(h, w_t, row_starts, kinds, gains, w_up, w_down, tm=1024, tn=COL_TILE, cast_tiles=4):
    m, k = h.shape
    n_tiles = len(row_starts)
    assert all(r % SUBLANES == 0 for r in row_starts)
    cast_steps = cast_tiles * (m // tm)
    up_rows, down_rows = w_up.shape[0] // cast_steps, w_down.shape[0] // cast_steps
    assert cast_tiles <= n_tiles and up_rows % BF16_ROWS == 0 and down_rows % BF16_ROWS == 0
    assert up_rows * cast_steps == w_up.shape[0] and down_rows * cast_steps == w_down.shape[0]

    def w_rows(j, i):
        start = jnp.int32(row_starts[0])
        for t in range(1, n_tiles):
            start = jnp.where(j == t, row_starts[t], start)
        return (pl.multiple_of(start, SUBLANES), 0)

    slab = lambda j, i: (jnp.minimum(j * (m // tm) + i, cast_steps - 1), 0)
    up_spec = pl.BlockSpec((up_rows, w_up.shape[1]), slab)
    down_spec = pl.BlockSpec((down_rows, w_down.shape[1]), slab)
    return pl.pallas_call(
        functools.partial(_proj_kernel, kinds=tuple(kinds), cast_tiles=cast_tiles),
        grid=(n_tiles, m // tm),
        in_specs=[
            pl.BlockSpec((tm, k), lambda j, i: (i, 0)),
            pl.BlockSpec((pl.Element(tn), pl.Element(k)), w_rows),
            pl.BlockSpec((None, 1, tn), lambda j, i: (j, 0, 0)),
            up_spec, down_spec,
        ],
        out_specs=[pl.BlockSpec((tm, tn), lambda j, i: (i, j)), up_spec, down_spec],
        out_shape=[jax.ShapeDtypeStruct((m, n_tiles * tn), BF16),
                   jax.ShapeDtypeStruct(w_up.shape, BF16), jax.ShapeDtypeStruct(w_down.shape, BF16)],
        scratch_shapes=[pltpu.VMEM((tn, k), BF16)],
        compiler_params=_params(2),
        name="in_proj",
    )(h, w_t, gains.reshape(n_tiles, 1, tn), w_up, w_down)


def _gated_conv_tile(conv_ref, w_ref, carry_ref):
    tm = conv_ref.shape[0]
    k = w_ref.shape[1]
    u = conv_ref[:, k:2 * k].astype(F32) * conv_ref[:, 2 * k:3 * k].astype(F32)
    w = w_ref[...]
    taps = lambda u2, u1, u0: w[0:1, :] * u2 + w[1:2, :] * u1 + w[2:3, :] * u0
    conv = taps(pltpu.roll(u, 2, 0), pltpu.roll(u, 1, 0), u)
    top = u[0:SUBLANES, :]
    row = lax.broadcasted_iota(jnp.int32, (SUBLANES, 1), 0)
    prev1 = carry_ref[1:2, :]
    prev2 = carry_ref[0:1, :]
    top1 = jnp.where(row == 0, prev1, pltpu.roll(top, 1, 0))
    top2 = jnp.where(row == 0, prev2, jnp.where(row == 1, prev1, pltpu.roll(top, 2, 0)))
    conv = jnp.concatenate([taps(top2, top1, top), conv[SUBLANES:, :]], axis=0)
    carry_ref[0:2, :] = u[tm - 2:tm, :]
    return conv_ref[:, 0:k].astype(F32) * conv


def _fox_kernel(q_ref, qa_ref, k_ref, ka_ref, v_ref, o_ref, vt_ref, s_ref, m_ref, l_ref, acc_ref, *,
                t):
    n_tiles = q_ref.shape[0] // t
    assert n_tiles % 2 == 0
    d = FOX_HEAD_DIM
    n_heads = q_ref.shape[1] // d
    j = pl.program_id(2)

    @pl.when(j == 0)
    def _():
        for g in range(n_heads):
            for blk in range(v_ref.shape[0] // t):
                v = v_ref[blk * t:(blk + 1) * t, g * d:(g + 1) * d]
                vt_ref[g, blk, :d] = v.astype(F32).T.astype(BF16)
                vt_ref[g, blk, d:] = jnp.ones((vt_ref.shape[2] - d, t), BF16)

    def scores(tile, kb, g):
        q = jnp.concatenate([q_ref[tile * t:(tile + 1) * t, g * d:(g + 1) * d],
                             qa_ref[tile * t:(tile + 1) * t, :]], axis=1)
        rows = pl.ds(pl.multiple_of(kb * t, t), t)
        k = jnp.concatenate([k_ref[rows, g * d:(g + 1) * d], ka_ref[rows, g * d:(g + 1) * d]], axis=1)
        return lax.dot_general(k, q, NT_DIMS, preferred_element_type=F32)

    def reset(g):
        m_ref[g] = jnp.full(m_ref.shape[1:], -jnp.inf, F32)
        l_ref[g] = jnp.zeros(l_ref.shape[1:], F32)
        acc_ref[g] = jnp.zeros(acc_ref.shape[1:], F32)

    def update(g, m_new, alpha, pv):
        l_ref[g] = alpha * l_ref[g] + pv[d:d + 1]
        acc_ref[g] = alpha * acc_ref[g] + pv[:d]
        m_ref[g] = m_new

    def consume(kb, slot, g):
        s = s_ref[slot, g]
        m_prev = m_ref[g]
        m_new = jnp.maximum(m_prev, jnp.max(s, axis=0, keepdims=True))
        p = jnp.exp2(s - m_new)
        pv = jnp.dot(vt_ref[g, kb], p.astype(BF16), preferred_element_type=F32)
        update(g, m_new, jnp.exp2(m_prev - m_new), pv)

    def causal(s):
        kpos = lax.broadcasted_iota(jnp.int32, s.shape, 0)
        qpos = lax.broadcasted_iota(jnp.int32, s.shape, 1)
        return jnp.where(qpos >= kpos, s, -jnp.inf)

    def consume_diagonal(kb, slot, g):
        h = t // 2
        s_up = causal(s_ref[slot, g, :h, :])
        s_lo = causal(s_ref[slot, g, h:, h:])
        m_up = jnp.max(s_up, axis=0, keepdims=True)
        m_blk = jnp.concatenate(
            [m_up[:, :h], jnp.maximum(m_up[:, h:], jnp.max(s_lo, axis=0, keepdims=True))], axis=1)
        m_prev = m_ref[g]
        m_new = jnp.maximum(m_prev, m_blk)
        p_up = jnp.exp2(s_up - m_new).astype(BF16)
        p_lo = jnp.exp2(s_lo - m_new[:, h:]).astype(BF16)
        vt = vt_ref[g, kb]
        pv_up = jnp.dot(vt[:, :h], p_up, preferred_element_type=F32)
        pv_lo = jnp.dot(vt[:, h:], p_lo, preferred_element_type=F32)
        pv = jnp.concatenate([pv_up[:, :h], pv_up[:, h:] + pv_lo], axis=1)
        update(g, m_new, jnp.exp2(m_prev - m_new), pv)

    def advance(tile, kb, slot):
        for g in range(n_heads):
            s_ref[1 - slot, g] = scores(tile, kb + 1, g)
            consume(kb, slot, g)

    def emit(tile, g):
        o_ref[tile * t:(tile + 1) * t, g * d:(g + 1) * d] = (acc_ref[g] / l_ref[g]).T.astype(BF16)

    slot = 0
    for g in range(n_heads):
        reset(g)
        s_ref[slot, g] = scores(0, 0, g)
    for r in range(n_tiles):
        def pairs(pair, carry, r=r, slot=slot):
            advance(r, 2 * pair, slot)
            advance(r, 2 * pair + 1, 1 - slot)
            return carry

        lax.fori_loop(0, (n_tiles // 2) * j + r // 2, pairs, 0)
        diagonal = n_tiles * j + r
        if r % 2 == 1:
            advance(r, diagonal - 1, slot)
            slot = 1 - slot
        for g in range(n_heads):
            if r + 1 < n_tiles:
                s_ref[1 - slot, g] = scores(r + 1, 0, g)
            consume_diagonal(diagonal, slot, g)
            emit(r, g)
            if r + 1 < n_tiles:
                reset(g)
        slot = 1 - slot


def _fox(proj, qa, ka, q_col, k_col, v_col, t=512, tiles_per_step=2, heads_per_step=4):
    b, s, _ = proj.shape
    d = FOX_HEAD_DIM
    w = heads_per_step * d
    rows = tiles_per_step * t
    tiles = lambda col0: pl.BlockSpec((None, rows, w), lambda bi, h, j: (bi, j, col0 // w + h))
    seq = lambda col0: pl.BlockSpec((None, s, w), lambda bi, h, j: (bi, 0, col0 // w + h))
    return pl.pallas_call(
        functools.partial(_fox_kernel, t=t),
        grid=(b, FOX_HEADS // heads_per_step, s // rows),
        in_specs=[tiles(q_col), pl.BlockSpec((None, rows, LANES), lambda bi, h, j: (bi, j, 0)),
                  seq(k_col), seq(0), seq(v_col)],
        out_specs=tiles(0),
        out_shape=jax.ShapeDtypeStruct((b, s, FOX_HEADS * d), BF16),
        scratch_shapes=[pltpu.VMEM((heads_per_step, s // t, d + BF16_ROWS, t), BF16),
                        pltpu.VMEM((2, heads_per_step, t, t), F32),
                        pltpu.VMEM((heads_per_step, 1, t), F32),
                        pltpu.VMEM((heads_per_step, 1, t), F32),
                        pltpu.VMEM((heads_per_step, d, t), F32)],
        compiler_params=_params(3),
        name="fox_attention",
    )(proj, qa, proj, ka, proj)


def _memkv_kernel(mem_ref, g_ref, w_ref, kg_ref, o_ref):
    xf = mem_ref[...]
    ms = jnp.mean(xf * xf, axis=-1, keepdims=True)
    y = (xf * lax.rsqrt(ms + EPS) * g_ref[...]).astype(BF16)
    acc = jnp.dot(y, w_ref[...].astype(BF16), preferred_element_type=F32)

    @pl.when(pl.program_id(0) == 0)
    def _():
        _store_group_rms(o_ref, acc, kg_ref[...], MEM_HEAD_DIM)

    @pl.when(pl.program_id(0) == 1)
    def _():
        o_ref[...] = acc.astype(BF16)


def _memkv(mem2d, g, w, kg):
    m, d = mem2d.shape
    width = MEM_HEADS * MEM_HEAD_DIM
    return pl.pallas_call(
        _memkv_kernel,
        grid=(2,),
        in_specs=[
            pl.BlockSpec((m, d), lambda j: (0, 0)),
            pl.BlockSpec((1, d), lambda j: (0, 0)),
            pl.BlockSpec((d, width), lambda j: (0, j)),
            pl.BlockSpec((1, width), lambda j: (0, 0)),
        ],
        out_specs=pl.BlockSpec((m, width), lambda j: (0, j)),
        out_shape=jax.ShapeDtypeStruct((m, 2 * width), BF16),
        compiler_params=_params(1),
        name="mem_kv",
    )(mem2d, g, w, kg)


def _memattn_kernel(q_ref, kv_ref, o_ref):
    width = MEM_HEADS * MEM_HEAD_DIM
    for h in range(MEM_HEADS):
        lo, hi = h * MEM_HEAD_DIM, (h + 1) * MEM_HEAD_DIM
        q = q_ref[:, lo:hi]
        k = kv_ref[:, lo:hi]
        v = kv_ref[:, width + lo:width + hi]
        s = lax.dot_general(q, k, NT_DIMS, preferred_element_type=F32)
        p = jnp.exp(s - jnp.max(s, axis=-1, keepdims=True))
        l = jnp.sum(p, axis=-1, keepdims=True)
        o = jnp.dot(p.astype(BF16), v, preferred_element_type=F32) / l
        o_ref[:, lo:hi] = o.astype(BF16)


def _memattn(proj, q_col, kv, tm=1024):
    b, s, _ = proj.shape
    width = MEM_HEADS * MEM_HEAD_DIM
    n_mem = kv.shape[1]
    return pl.pallas_call(
        _memattn_kernel,
        grid=(b, s // tm),
        in_specs=[
            pl.BlockSpec((None, tm, width), lambda i, j: (i, j, q_col // width)),
            pl.BlockSpec((None, n_mem, 2 * width), lambda i, j: (i, 0, 0)),
        ],
        out_specs=pl.BlockSpec((None, tm, width), lambda i, j: (i, j, 0)),
        out_shape=jax.ShapeDtypeStruct((b, s, width), BF16),
        compiler_params=_params(2),
        name="mem_attention",
    )(proj, kv)


def _merge_kernel(conv_ref, cw_ref, b_ref, c_ref, wa_ref, wb_ref, wc_ref, g_ref,
                  o_ref, wabf_ref, wbbf_ref, wcbf_ref, carry_ref, *, tiles_per_seq):
    i = pl.program_id(1)
    tn = o_ref.shape[1]

    @pl.when(i == 0)
    def _():
        wabf_ref[...] = wa_ref[...].astype(BF16)
        wbbf_ref[...] = wb_ref[...].astype(BF16)
        wcbf_ref[...] = wc_ref[...].astype(BF16)

    @pl.when(i % tiles_per_seq == 0)
    def _():
        carry_ref[...] = jnp.zeros_like(carry_ref)

    y_conv = _gated_conv_tile(conv_ref, cw_ref, carry_ref).astype(BF16)
    oa = jnp.dot(y_conv, wabf_ref[...], preferred_element_type=F32)
    ob = jnp.dot(b_ref[...], wbbf_ref[...], preferred_element_type=F32)
    oc = jnp.dot(c_ref[...], wcbf_ref[...], preferred_element_type=F32)
    gate = lambda br: g_ref[:, br * tn:(br + 1) * tn].astype(F32)
    o_ref[...] = (gate(0) * oa + gate(1) * ob + gate(2) * oc).astype(BF16)


def _merge(proj, conv_col, conv_w, yb, yc, wa, wb, wc, seq_len, d_model, tm=512, tn=COL_TILE):
    m, k = yb.shape
    lhs = pl.BlockSpec((tm, k), lambda j, i: (i, 0))
    wsp = pl.BlockSpec((k, tn), lambda j, i: (0, j))
    return pl.pallas_call(
        functools.partial(_merge_kernel, tiles_per_seq=seq_len // tm),
        grid=(d_model // tn, m // tm),
        in_specs=[pl.BlockSpec((tm, 3 * k), lambda j, i: (i, conv_col // (3 * k))),
                  pl.BlockSpec((CONV_TAPS, k), lambda j, i: (0, 0)),
                  lhs, lhs, wsp, wsp, wsp,
                  pl.BlockSpec((tm, 3 * tn), lambda j, i: (i, j))],
        out_specs=pl.BlockSpec((tm, tn), lambda j, i: (i, j)),
        out_shape=jax.ShapeDtypeStruct((m, d_model), BF16),
        scratch_shapes=[pltpu.VMEM((k, tn), BF16)] * 3 + [pltpu.VMEM((SUBLANES, k), F32)],
        compiler_params=_params(2),
        name="branch_merge",
    )(proj, conv_w, yb, yc, wa, wb, wc, proj)


def _outproj_kernel(a_ref, w_ref, x_ref, g_ref, x2_ref, h2_ref, wbf_ref):
    @pl.when(pl.program_id(0) == 0)
    def _():
        wbf_ref[...] = w_ref[...].astype(BF16)

    x2 = x_ref[...] + jnp.dot(a_ref[...], wbf_ref[...], preferred_element_type=F32)
    x2_ref[...] = x2
    ms = jnp.mean(x2 * x2, axis=-1, keepdims=True)
    h2_ref[...] = (x2 * lax.rsqrt(ms + EPS) * g_ref[...]).astype(BF16)


def _outproj(a, w, x2d, g, tm=512):
    m, d = x2d.shape
    row = pl.BlockSpec((tm, d), lambda i: (i, 0))
    return pl.pallas_call(
        _outproj_kernel,
        grid=(m // tm,),
        in_specs=[row, pl.BlockSpec((d, d), lambda i: (0, 0), pipeline_mode=pl.Buffered(1)), row,
                  pl.BlockSpec((1, d), lambda i: (0, 0))],
        out_specs=[row, row],
        out_shape=[jax.ShapeDtypeStruct((m, d), F32), jax.ShapeDtypeStruct((m, d), BF16)],
        scratch_shapes=[pltpu.VMEM((d, d), BF16)],
        compiler_params=_params(1),
        name="out_proj_norm2",
    )(a, w, x2d, g)


def _mlp_kernel(h_ref, wu_ref, wd_ref, x_ref, o_ref):
    f = pl.program_id(1)
    rows = x_ref.shape[0]

    @pl.when(f == 0)
    def _():
        o_ref[...] = jnp.zeros_like(o_ref)

    a = jnp.dot(h_ref[...], wu_ref[...], preferred_element_type=F32)
    a = jnp.square(jnp.maximum(a, 0.0)).astype(BF16)
    o_ref[...] += jnp.dot(a, wd_ref[...], preferred_element_type=F32)
    slab = pl.ds(pl.multiple_of(f * rows, rows), rows)
    o_ref[slab, :] += x_ref[...]


def _mlp(h2, w_up, w_down, x2, tm=1024, tf=1024):
    m, d = h2.shape
    d_ff = w_up.shape[1]
    nf = d_ff // tf
    rows = tm // nf
    return pl.pallas_call(
        _mlp_kernel,
        grid=(m // tm, nf),
        in_specs=[
            pl.BlockSpec((tm, d), lambda i, f: (i, 0)),
            pl.BlockSpec((d, tf), lambda i, f: (0, f)),
            pl.BlockSpec((tf, d), lambda i, f: (f, 0)),
            pl.BlockSpec((rows, d), lambda i, f: (i * nf + f, 0)),
        ],
        out_specs=pl.BlockSpec((tm, d), lambda i, f: (i, 0)),
        out_shape=jax.ShapeDtypeStruct((m, d), F32),
        compiler_params=_params(2),
        name="relu2_mlp",
    )(h2, w_up, w_down, x2)


def kernel(x, mem, norm1_g, w_in, b_f, conv_w, fox_q_g, fox_k_g, mem_norm_g, w_mem_kv, mem_q_g, mem_k_g, w_conv_out, w_fox_out, w_mem_out, w_out, norm2_g, w_up, w_down):
    b, s, d = x.shape
    m = b * s
    conv_width = conv_w.shape[1]
    fox_width = FOX_HEADS * FOX_HEAD_DIM
    mem_width = MEM_HEADS * MEM_HEAD_DIM
    tile = COL_TILE
    q_row = 3 * conv_width
    k_row = q_row + fox_width
    v_row = k_row + fox_width
    f_row = v_row + fox_width
    mq_row = f_row + FOX_HEADS
    gate_row = mq_row + mem_width
    n_gate = N_BRANCHES * d
    ch_tiles = d // tile
    gate_rows = [gate_row + br * d + n * tile for n in range(ch_tiles) for br in range(N_BRANCHES)]
    rest_rows = [r for r in range(0, f_row, tile)] + [r for r in range(mq_row, gate_row, tile)]
    row_starts = tuple(gate_rows + rest_rows)
    conv_col, q_col, k_col, v_col, mq_col = (n_gate + r for r in (0, q_row, k_row, v_row, f_row))
    kinds = ([GATE] * len(gate_rows) + [PLAIN] * (q_row // tile) + [FOX_HEAD_DIM] * (2 * fox_width // tile)
             + [PLAIN] * (fox_width // tile) + [MEM_HEAD_DIM] * (mem_width // tile))
    gains = jnp.ones((len(row_starts) * tile,), F32)
    gains = gains.at[q_col:k_col].set(jnp.tile(fox_q_g * (LOG2_E / math.sqrt(FOX_HEAD_DIM)), FOX_HEADS))
    gains = gains.at[k_col:v_col].set(jnp.tile(fox_k_g, FOX_HEADS))
    gains = gains.at[mq_col:].set(jnp.tile(mem_q_g * (1.0 / math.sqrt(MEM_HEAD_DIM)), MEM_HEADS))

    row = lambda v: v.reshape(1, -1)
    bf = jnp.pad(b_f, (0, LANES - FOX_HEADS)).reshape(1, LANES)
    w_t = w_in.T

    h1, qa, ka = _norm1(x, row(norm1_g), w_t, f_row, bf)
    proj, w_up_bf, w_down_bf = _proj(h1.reshape(m, d), w_t, row_starts, kinds, gains.reshape(-1, tile),
                                     w_up, w_down)
    proj3 = proj.reshape(b, s, -1)

    y_fox = _fox(proj3, qa, ka, q_col, k_col, v_col)
    kv = _memkv(mem.reshape(-1, d), row(mem_norm_g), w_mem_kv, row(jnp.tile(mem_k_g, MEM_HEADS)))
    y_mem = _memattn(proj3, mq_col, kv.reshape(b, -1, 2 * mem_width))

    merged = _merge(proj, conv_col, conv_w, y_fox.reshape(m, -1), y_mem.reshape(m, -1),
                    w_conv_out, w_fox_out, w_mem_out, s, d)
    x2, h2 = _outproj(merged, w_out, x.reshape(m, d), row(norm2_g))
    out = _mlp(h2, w_up_bf, w_down_bf, x2)
    return out.reshape(b, s, d)
```

```python
import functools
import math

import jax
import jax.numpy as jnp
import numpy as np
from jax import lax
from jax.experimental import pallas as pl
from jax.experimental.pallas import tpu as pltpu

F32 = jnp.float32
BF16 = jnp.bfloat16

EPS = 1e-6
LOG2_E = math.log2(math.e)
LANES = 128
SUBLANES = 8
BF16_ROWS = 2 * SUBLANES
CONV_TAPS = 3
FOX_HEADS = 8
FOX_HEAD_DIM = 128
MEM_HEADS = 4
MEM_HEAD_DIM = 256
N_BRANCHES = 3

VMEM_LIMIT_BYTES = 56 * 1024 * 1024
COL_TILE = 1024
NT_DIMS = (((1,), (1,)), ((), ()))


def _params(n_axes):
    return pltpu.CompilerParams(
        dimension_semantics=("arbitrary",) * n_axes,
        vmem_limit_bytes=VMEM_LIMIT_BYTES)


def _store_group_rms(o_ref, a, gain, width):
    for s in range(0, a.shape[1], width):
        blk = a[:, s:s + width]
        ms = jnp.mean(blk * blk, axis=-1, keepdims=True)
        o_ref[:, s:s + width] = (blk * lax.rsqrt(ms + EPS) * gain[:, s:s + width]).astype(o_ref.dtype)


def _split3(c):
    hi = c.astype(BF16).astype(F32)
    r1 = c - hi
    mid = r1.astype(BF16).astype(F32)
    lo = (r1 - mid).astype(BF16).astype(F32)
    return hi, mid, lo


ONE_LANE = 3 * FOX_HEADS


def _decay_selector():
    sel = np.zeros((LANES, FOX_HEADS, LANES), np.float32)
    for hd in range(FOX_HEADS):
        for piece in range(3):
            sel[ONE_LANE, hd, piece * FOX_HEADS + hd] = 1.0
            sel[piece * FOX_HEADS + hd, hd, ONE_LANE + piece] = -1.0
    return jnp.asarray(sel.reshape(LANES, -1), BF16)


def _norm1_kernel(x_ref, g_ref, wf_ref, bf_ref, sel_ref, h_ref, qa_ref, ka_ref, carry_ref):
    tm = x_ref.shape[0]

    @pl.when(pl.program_id(1) == 0)
    def _():
        carry_ref[...] = jnp.zeros_like(carry_ref)

    wf = wf_ref[...]
    wf_hi = wf.astype(BF16).astype(F32)
    w2 = jnp.concatenate(
        [wf_hi, wf - wf_hi, jnp.zeros((LANES - 2 * FOX_HEADS, wf.shape[1]), F32)], axis=0).astype(BF16)
    row = lax.broadcasted_iota(jnp.int32, (CUMSUM_ROWS, CUMSUM_ROWS), 0)
    col = lax.broadcasted_iota(jnp.int32, (CUMSUM_ROWS, CUMSUM_ROWS), 1)
    tri = jnp.where(row >= col, 1.0, 0.0).astype(BF16)
    lane = lax.broadcasted_iota(jnp.int32, (CUMSUM_ROWS, LANES), 1)

    def pack3(hi, mid, lo, rest):
        return jnp.where(
            lane < FOX_HEADS, hi,
            jnp.where(lane < 2 * FOX_HEADS, pltpu.roll(mid, FOX_HEADS, 1),
                      jnp.where(lane < ONE_LANE, pltpu.roll(lo, 2 * FOX_HEADS, 1), rest))).astype(BF16)

    blocks = [pl.ds(blk * CUMSUM_ROWS, CUMSUM_ROWS) for blk in range(tm // CUMSUM_ROWS)]

    def logits(rows):
        xf = x_ref[rows, :]
        ms = jnp.mean(xf * xf, axis=-1, keepdims=True)
        y_hi = (xf * lax.rsqrt(ms + EPS) * g_ref[...]).astype(BF16)
        h_ref[rows, :] = y_hi
        return lax.dot_general(y_hi, w2, NT_DIMS, preferred_element_type=F32)

    def local_cumsum(z2):
        z = z2 + pltpu.roll(z2, LANES - FOX_HEADS, 1) + bf_ref[...]
        log_f = -LOG2_E * (jnp.maximum(-z, 0.0) + jnp.log1p(jnp.exp(-jnp.abs(z))))
        return jnp.dot(tri, pack3(*_split3(log_f), 0.0), preferred_element_type=F32)

    def decay_columns(rows, loc, run):
        c = (loc + pltpu.roll(loc, LANES - FOX_HEADS, 1)
             + pltpu.roll(loc, LANES - 2 * FOX_HEADS, 1) + run)
        packed = pack3(*_split3(c), jnp.where(lane < ONE_LANE + 3, 1.0, 0.0))
        qa_ref[rows, :] = packed
        ka_ref[rows, :] = jnp.dot(packed, sel_ref[...], preferred_element_type=F32).astype(BF16)
        return c[CUMSUM_ROWS - 1:CUMSUM_ROWS, :]

    z2s = [logits(rows) for rows in blocks]
    locs = [local_cumsum(z2) for z2 in z2s]
    run = carry_ref[0:1, :]
    for rows, loc in zip(blocks, locs):
        run = decay_columns(rows, loc, run)
    carry_ref[0:1, :] = run


def _norm1(x, g, w_t, f_row, bf, tm=1024):
    b, s, d = x.shape
    return pl.pallas_call(
        _norm1_kernel,
        grid=(b, s // tm),
        in_specs=[
            pl.BlockSpec((None, tm, d), lambda i, j: (i, j, 0)),
            pl.BlockSpec((1, d), lambda i, j: (0, 0)),
            pl.BlockSpec((FOX_HEADS, d), lambda i, j: (f_row // FOX_HEADS, 0)),
            pl.BlockSpec((1, LANES), lambda i, j: (0, 0)),
            pl.BlockSpec((LANES, FOX_HEADS * LANES), lambda i, j: (0, 0)),
        ],
        out_specs=[
            pl.BlockSpec((None, tm, d), lambda i, j: (i, j, 0)),
            pl.BlockSpec((None, tm, LANES), lambda i, j: (i, j, 0)),
            pl.BlockSpec((None, tm, FOX_HEADS * LANES), lambda i, j: (i, j, 0)),
        ],
        out_shape=[
            jax.ShapeDtypeStruct((b, s, d), BF16),
            jax.ShapeDtypeStruct((b, s, LANES), BF16),
            jax.ShapeDtypeStruct((b, s, FOX_HEADS * LANES), BF16),
        ],
        scratch_shapes=[pltpu.VMEM((SUBLANES, LANES), F32)],
        compiler_params=_params(2),
        name="norm1_decay",
    )(x, g, w_t, bf, _decay_selector())


CUMSUM_ROWS = 256
MXU_COLS = 256
PLAIN, GATE = 0, -1


def _proj_kernel(h_ref, w_ref, gain_ref, wu_ref, wd_ref, o_ref, wubf_ref, wdbf_ref, wbf_ref, *,
                 kinds, cast_tiles):
    @pl.when(pl.program_id(1) == 0)
    def _():
        wbf_ref[...] = w_ref[...].astype(BF16)

    def epilogue(kind, acc):
        return acc.astype(BF16) if kind == PLAIN else (0.5 * jnp.tanh(0.5 * acc) + 0.5).astype(BF16)

    def tile_with_cast(kind):
        tn = o_ref.shape[1]
        chunks = tn // MXU_COLS
        uc, dr = wu_ref.shape[1] // chunks, wd_ref.shape[0] // chunks
        for c in range(chunks):
            cols = pl.ds(c * MXU_COLS, MXU_COLS)
            acc = lax.dot_general(h_ref[...], wbf_ref[cols, :], NT_DIMS, preferred_element_type=F32)
            wubf_ref[:, c * uc:(c + 1) * uc] = wu_ref[:, c * uc:(c + 1) * uc].astype(BF16)
            wdbf_ref[c * dr:(c + 1) * dr, :] = wd_ref[c * dr:(c + 1) * dr, :].astype(BF16)
            o_ref[:, cols] = epilogue(kind, acc)

    def tile(kind, cast):
        if cast:
            assert kind in (PLAIN, GATE)
            return tile_with_cast(kind)
        acc = lax.dot_general(h_ref[...], wbf_ref[...], NT_DIMS, preferred_element_type=F32)
        if kind == PLAIN:
            o_ref[...] = acc.astype(BF16)
        elif kind == GATE:
            o_ref[...] = (0.5 * jnp.tanh(0.5 * acc) + 0.5).astype(BF16)
        else:
            _store_group_rms(o_ref, acc, gain_ref[...], kind)

    j = pl.program_id(0)
    jobs = [(kind, t < cast_tiles) for t, kind in enumerate(kinds)]
    for job in sorted(set(jobs)):
        hit = functools.reduce(jnp.logical_or, [j == t for t, jb in enumerate(jobs) if jb == job])
        pl.when(hit)(functools.partial(tile, *job))


def _proj(h, w_t, row_starts, kinds, gains, w_up, w_down, tm=1024, tn=COL_TILE, cast_tiles=8):
    m, k = h.shape
    n_tiles = len(row_starts)
    assert all(r % SUBLANES == 0 for r in row_starts)
    cast_steps = cast_tiles * (m // tm)
    up_rows, down_rows = w_up.shape[0] // cast_steps, w_down.shape[0] // cast_steps
    assert cast_tiles <= n_tiles and up_rows % BF16_ROWS == 0 and down_rows % BF16_ROWS == 0
    assert up_rows * cast_steps == w_up.shape[0] and down_rows * cast_steps == w_down.shape[0]

    def w_rows(j, i):
        start = jnp.int32(row_starts[0])
        for t in range(1, n_tiles):
            start = jnp.where(j == t, row_starts[t], start)
        return (pl.multiple_of(start, SUBLANES), 0)

    slab = lambda j, i: (jnp.minimum(j * (m // tm) + i, cast_steps - 1), 0)
    up_spec = pl.BlockSpec((up_rows, w_up.shape[1]), slab)
    down_spec = pl.BlockSpec((down_rows, w_down.shape[1]), slab)
    return pl.pallas_call(
        functools.partial(_proj_kernel, kinds=tuple(kinds), cast_tiles=cast_tiles),
        grid=(n_tiles, m // tm),
        in_specs=[
            pl.BlockSpec((tm, k), lambda j, i: (i, 0)),
            pl.BlockSpec((pl.Element(tn), pl.Element(k)), w_rows),
            pl.BlockSpec((None, 1, tn), lambda j, i: (j, 0, 0)),
            up_spec, down_spec,
        ],
        out_specs=[pl.BlockSpec((tm, tn), lambda j, i: (i, j)), up_spec, down_spec],
        out_shape=[jax.ShapeDtypeStruct((m, n_tiles * tn), BF16),
                   jax.ShapeDtypeStruct(w_up.shape, BF16), jax.ShapeDtypeStruct(w_down.shape, BF16)],
        scratch_shapes=[pltpu.VMEM((tn, k), BF16)],
        compiler_params=_params(2),
        name="in_proj",
    )(h, w_t, gains.reshape(n_tiles, 1, tn), w_up, w_down)


def _gated_conv_tile(conv_ref, w_ref, carry_ref):
    tm = conv_ref.shape[0]
    k = w_ref.shape[1]
    u = conv_ref[:, k:2 * k].astype(F32) * conv_ref[:, 2 * k:3 * k].astype(F32)
    w = w_ref[...]
    taps = lambda u2, u1, u0: w[0:1, :] * u2 + w[1:2, :] * u1 + w[2:3, :] * u0
    conv = taps(pltpu.roll(u, 2, 0), pltpu.roll(u, 1, 0), u)
    top = u[0:SUBLANES, :]
    row = lax.broadcasted_iota(jnp.int32, (SUBLANES, 1), 0)
    prev1 = carry_ref[1:2, :]
    prev2 = carry_ref[0:1, :]
    top1 = jnp.where(row == 0, prev1, pltpu.roll(top, 1, 0))
    top2 = jnp.where(row == 0, prev2, jnp.where(row == 1, prev1, pltpu.roll(top, 2, 0)))
    conv = jnp.concatenate([taps(top2, top1, top), conv[SUBLANES:, :]], axis=0)
    carry_ref[0:2, :] = u[tm - 2:tm, :]
    return conv_ref[:, 0:k].astype(F32) * conv


def _fox_kernel(q_ref, qa_ref, k_ref, ka_ref, v_ref, o_ref, vt_ref, s_ref, m_ref, l_ref, acc_ref, *,
                t):
    n_tiles = q_ref.shape[0] // t
    assert n_tiles % 2 == 0
    d = FOX_HEAD_DIM
    n_heads = q_ref.shape[1] // d
    j = pl.program_id(2)

    @pl.when(j == 0)
    def _():
        for g in range(n_heads):
            for blk in range(v_ref.shape[0] // t):
                v = v_ref[blk * t:(blk + 1) * t, g * d:(g + 1) * d]
                vt_ref[g, blk, :d] = v.astype(F32).T.astype(BF16)
                vt_ref[g, blk, d:] = jnp.ones((vt_ref.shape[2] - d, t), BF16)

    def scores(tile, kb, g):
        q = jnp.concatenate([q_ref[tile * t:(tile + 1) * t, g * d:(g + 1) * d],
                             qa_ref[tile * t:(tile + 1) * t, :]], axis=1)
        rows = pl.ds(pl.multiple_of(kb * t, t), t)
        k = jnp.concatenate([k_ref[rows, g * d:(g + 1) * d], ka_ref[rows, g * d:(g + 1) * d]], axis=1)
        return lax.dot_general(k, q, NT_DIMS, preferred_element_type=F32)

    def reset(g):
        m_ref[g] = jnp.full(m_ref.shape[1:], -jnp.inf, F32)
        l_ref[g] = jnp.zeros(l_ref.shape[1:], F32)
        acc_ref[g] = jnp.zeros(acc_ref.shape[1:], F32)

    def update(g, m_new, alpha, pv):
        l_ref[g] = alpha * l_ref[g] + pv[d:d + 1]
        acc_ref[g] = alpha * acc_ref[g] + pv[:d]
        m_ref[g] = m_new

    def consume(kb, slot, g):
        s = s_ref[slot, g]
        m_prev = m_ref[g]
        m_new = jnp.maximum(m_prev, jnp.max(s, axis=0, keepdims=True))
        p = jnp.exp2(s - m_new)
        pv = jnp.dot(vt_ref[g, kb], p.astype(BF16), preferred_element_type=F32)
        update(g, m_new, jnp.exp2(m_prev - m_new), pv)

    def causal(s):
        kpos = lax.broadcasted_iota(jnp.int32, s.shape, 0)
        qpos = lax.broadcasted_iota(jnp.int32, s.shape, 1)
        return jnp.where(qpos >= kpos, s, -jnp.inf)

    def consume_diagonal(kb, slot, g):
        h = t // 2
        s_up = causal(s_ref[slot, g, :h, :])
        s_lo = causal(s_ref[slot, g, h:, h:])
        m_up = jnp.max(s_up, axis=0, keepdims=True)
        m_blk = jnp.concatenate(
            [m_up[:, :h], jnp.maximum(m_up[:, h:], jnp.max(s_lo, axis=0, keepdims=True))], axis=1)
        m_prev = m_ref[g]
        m_new = jnp.maximum(m_prev, m_blk)
        p_up = jnp.exp2(s_up - m_new).astype(BF16)
        p_lo = jnp.exp2(s_lo - m_new[:, h:]).astype(BF16)
        vt = vt_ref[g, kb]
        pv_up = jnp.dot(vt[:, :h], p_up, preferred_element_type=F32)
        pv_lo = jnp.dot(vt[:, h:], p_lo, preferred_element_type=F32)
        pv = jnp.concatenate([pv_up[:, :h], pv_up[:, h:] + pv_lo], axis=1)
        update(g, m_new, jnp.exp2(m_prev - m_new), pv)

    def advance(tile, kb, slot):
        for g in range(n_heads):
            s_ref[1 - slot, g] = scores(tile, kb + 1, g)
            consume(kb, slot, g)

    def emit(tile, g):
        o_ref[tile * t:(tile + 1) * t, g * d:(g + 1) * d] = (acc_ref[g] / l_ref[g]).T.astype(BF16)

    slot = 0
    for g in range(n_heads):
        reset(g)
        s_ref[slot, g] = scores(0, 0, g)
    for r in range(n_tiles):
        def pairs(pair, carry, r=r, slot=slot):
            advance(r, 2 * pair, slot)
            advance(r, 2 * pair + 1, 1 - slot)
            return carry

        lax.fori_loop(0, (n_tiles // 2) * j + r // 2, pairs, 0)
        diagonal = n_tiles * j + r
        if r % 2 == 1:
            advance(r, diagonal - 1, slot)
            slot = 1 - slot
        for g in range(n_heads):
            if r + 1 < n_tiles:
                s_ref[1 - slot, g] = scores(r + 1, 0, g)
            consume_diagonal(diagonal, slot, g)
            emit(r, g)
            if r + 1 < n_tiles:
                reset(g)
        slot = 1 - slot


def _fox(proj, qa, ka, q_col, k_col, v_col, t=512, tiles_per_step=2, heads_per_step=4):
    b, s, _ = proj.shape
    d = FOX_HEAD_DIM
    w = heads_per_step * d
    rows = tiles_per_step * t
    tiles = lambda col0: pl.BlockSpec((None, rows, w), lambda bi, h, j: (bi, j, col0 // w + h))
    seq = lambda col0: pl.BlockSpec((None, s, w), lambda bi, h, j: (bi, 0, col0 // w + h))
    return pl.pallas_call(
        functools.partial(_fox_kernel, t=t),
        grid=(b, FOX_HEADS // heads_per_step, s // rows),
        in_specs=[tiles(q_col), pl.BlockSpec((None, rows, LANES), lambda bi, h, j: (bi, j, 0)),
                  seq(k_col), seq(0), seq(v_col)],
        out_specs=tiles(0),
        out_shape=jax.ShapeDtypeStruct((b, s, FOX_HEADS * d), BF16),
        scratch_shapes=[pltpu.VMEM((heads_per_step, s // t, d + BF16_ROWS, t), BF16),
                        pltpu.VMEM((2, heads_per_step, t, t), F32),
                        pltpu.VMEM((heads_per_step, 1, t), F32),
                        pltpu.VMEM((heads_per_step, 1, t), F32),
                        pltpu.VMEM((heads_per_step, d, t), F32)],
        compiler_params=_params(3),
        name="fox_attention",
    )(proj, qa, proj, ka, proj)


def _memkv_kernel(mem_ref, g_ref, w_ref, kg_ref, o_ref):
    xf = mem_ref[...]
    ms = jnp.mean(xf * xf, axis=-1, keepdims=True)
    y = (xf * lax.rsqrt(ms + EPS) * g_ref[...]).astype(BF16)
    acc = jnp.dot(y, w_ref[...].astype(BF16), preferred_element_type=F32)

    @pl.when(pl.program_id(0) == 0)
    def _():
        _store_group_rms(o_ref, acc, kg_ref[...], MEM_HEAD_DIM)

    @pl.when(pl.program_id(0) == 1)
    def _():
        o_ref[...] = acc.astype(BF16)


def _memkv(mem2d, g, w, kg):
    m, d = mem2d.shape
    width = MEM_HEADS * MEM_HEAD_DIM
    return pl.pallas_call(
        _memkv_kernel,
        grid=(2,),
        in_specs=[
            pl.BlockSpec((m, d), lambda j: (0, 0)),
            pl.BlockSpec((1, d), lambda j: (0, 0)),
            pl.BlockSpec((d, width), lambda j: (0, j)),
            pl.BlockSpec((1, width), lambda j: (0, 0)),
        ],
        out_specs=pl.BlockSpec((m, width), lambda j: (0, j)),
        out_shape=jax.ShapeDtypeStruct((m, 2 * width), BF16),
        compiler_params=_params(1),
        name="mem_kv",
    )(mem2d, g, w, kg)


def _memattn_kernel(q_ref, kv_ref, o_ref):
    width = MEM_HEADS * MEM_HEAD_DIM
    for h in range(MEM_HEADS):
        lo, hi = h * MEM_HEAD_DIM, (h + 1) * MEM_HEAD_DIM
        q = q_ref[:, lo:hi]
        k = kv_ref[:, lo:hi]
        v = kv_ref[:, width + lo:width + hi]
        s = lax.dot_general(q, k, NT_DIMS, preferred_element_type=F32)
        p = jnp.exp(s - jnp.max(s, axis=-1, keepdims=True))
        l = jnp.sum(p, axis=-1, keepdims=True)
        o = jnp.dot(p.astype(BF16), v, preferred_element_type=F32) / l
        o_ref[:, lo:hi] = o.astype(BF16)


def _memattn(proj, q_col, kv, tm=1024):
    b, s, _ = proj.shape
    width = MEM_HEADS * MEM_HEAD_DIM
    n_mem = kv.shape[1]
    return pl.pallas_call(
        _memattn_kernel,
        grid=(b, s // tm),
        in_specs=[
            pl.BlockSpec((None, tm, width), lambda i, j: (i, j, q_col // width)),
            pl.BlockSpec((None, n_mem, 2 * width), lambda i, j: (i, 0, 0)),
        ],
        out_specs=pl.BlockSpec((None, tm, width), lambda i, j: (i, j, 0)),
        out_shape=jax.ShapeDtypeStruct((b, s, width), BF16),
        compiler_params=_params(2),
        name="mem_attention",
    )(proj, kv)


def _merge_kernel(conv_ref, cw_ref, b_ref, c_ref, wa_ref, wb_ref, wc_ref, g_ref,
                  o_ref, wabf_ref, wbbf_ref, wcbf_ref, carry_ref, *, tiles_per_seq):
    i = pl.program_id(1)
    tn = o_ref.shape[1]

    @pl.when(i == 0)
    def _():
        wabf_ref[...] = wa_ref[...].astype(BF16)
        wbbf_ref[...] = wb_ref[...].astype(BF16)
        wcbf_ref[...] = wc_ref[...].astype(BF16)

    @pl.when(i % tiles_per_seq == 0)
    def _():
        carry_ref[...] = jnp.zeros_like(carry_ref)

    y_conv = _gated_conv_tile(conv_ref, cw_ref, carry_ref).astype(BF16)
    oa = jnp.dot(y_conv, wabf_ref[...], preferred_element_type=F32)
    ob = jnp.dot(b_ref[...], wbbf_ref[...], preferred_element_type=F32)
    oc = jnp.dot(c_ref[...], wcbf_ref[...], preferred_element_type=F32)
    gate = lambda br: g_ref[:, br * tn:(br + 1) * tn].astype(F32)
    o_ref[...] = (gate(0) * oa + gate(1) * ob + gate(2) * oc).astype(BF16)


def _merge(proj, conv_col, conv_w, yb, yc, wa, wb, wc, seq_len, d_model, tm=512, tn=COL_TILE):
    m, k = yb.shape
    lhs = pl.BlockSpec((tm, k), lambda j, i: (i, 0))
    wsp = pl.BlockSpec((k, tn), lambda j, i: (0, j))
    return pl.pallas_call(
        functools.partial(_merge_kernel, tiles_per_seq=seq_len // tm),
        grid=(d_model // tn, m // tm),
        in_specs=[pl.BlockSpec((tm, 3 * k), lambda j, i: (i, conv_col // (3 * k))),
                  pl.BlockSpec((CONV_TAPS, k), lambda j, i: (0, 0)),
                  lhs, lhs, wsp, wsp, wsp,
                  pl.BlockSpec((tm, 3 * tn), lambda j, i: (i, j))],
        out_specs=pl.BlockSpec((tm, tn), lambda j, i: (i, j)),
        out_shape=jax.ShapeDtypeStruct((m, d_model), BF16),
        scratch_shapes=[pltpu.VMEM((k, tn), BF16)] * 3 + [pltpu.VMEM((SUBLANES, k), F32)],
        compiler_params=_params(2),
        name="branch_merge",
    )(proj, conv_w, yb, yc, wa, wb, wc, proj)


def _outproj_kernel(a_ref, w_ref, x_ref, g_ref, x2_ref, h2_ref, wbf_ref):
    @pl.when(pl.program_id(0) == 0)
    def _():
        wbf_ref[...] = w_ref[...].astype(BF16)

    x2 = x_ref[...] + jnp.dot(a_ref[...], wbf_ref[...], preferred_element_type=F32)
    x2_ref[...] = x2
    ms = jnp.mean(x2 * x2, axis=-1, keepdims=True)
    h2_ref[...] = (x2 * lax.rsqrt(ms + EPS) * g_ref[...]).astype(BF16)


def _outproj(a, w, x2d, g, tm=512):
    m, d = x2d.shape
    row = pl.BlockSpec((tm, d), lambda i: (i, 0))
    return pl.pallas_call(
        _outproj_kernel,
        grid=(m // tm,),
        in_specs=[row, pl.BlockSpec((d, d), lambda i: (0, 0), pipeline_mode=pl.Buffered(1)), row,
                  pl.BlockSpec((1, d), lambda i: (0, 0))],
        out_specs=[row, row],
        out_shape=[jax.ShapeDtypeStruct((m, d), F32), jax.ShapeDtypeStruct((m, d), BF16)],
        scratch_shapes=[pltpu.VMEM((d, d), BF16)],
        compiler_params=_params(1),
        name="out_proj_norm2",
    )(a, w, x2d, g)


def _mlp_kernel(h_ref, wu_ref, wd_ref, x_ref, o_ref):
    f = pl.program_id(1)
    rows = x_ref.shape[0]

    @pl.when(f == 0)
    def _():
        o_ref[...] = jnp.zeros_like(o_ref)

    a = jnp.dot(h_ref[...], wu_ref[...], preferred_element_type=F32)
    a = jnp.square(jnp.maximum(a, 0.0)).astype(BF16)
    o_ref[...] += jnp.dot(a, wd_ref[...], preferred_element_type=F32)
    slab = pl.ds(pl.multiple_of(f * rows, rows), rows)
    o_ref[slab, :] += x_ref[...]


def _mlp(h2, w_up, w_down, x2, tm=1024, tf=1024):
    m, d = h2.shape
    d_ff = w_up.shape[1]
    nf = d_ff // tf
    rows = tm // nf
    return pl.pallas_call(
        _mlp_kernel,
        grid=(m // tm, nf),
        in_specs=[
            pl.BlockSpec((tm, d), lambda i, f: (i, 0)),
            pl.BlockSpec((d, tf), lambda i, f: (0, f)),
            pl.BlockSpec((tf, d), lambda i, f: (f, 0)),
            pl.BlockSpec((rows, d), lambda i, f: (i * nf + f, 0)),
        ],
        out_specs=pl.BlockSpec((tm, d), lambda i, f: (i, 0)),
        out_shape=jax.ShapeDtypeStruct((m, d), F32),
        compiler_params=_params(2),
        name="relu2_mlp",
    )(h2, w_up, w_down, x2)


def kernel(x, mem, norm1_g, w_in, b_f, conv_w, fox_q_g, fox_k_g, mem_norm_g, w_mem_kv, mem_q_g, mem_k_g, w_conv_out, w_fox_out, w_mem_out, w_out, norm2_g, w_up, w_down):
    b, s, d = x.shape
    m = b * s
    conv_width = conv_w.shape[1]
    fox_width = FOX_HEADS * FOX_HEAD_DIM
    mem_width = MEM_HEADS * MEM_HEAD_DIM
    tile = COL_TILE
    q_row = 3 * conv_width
    k_row = q_row + fox_width
    v_row = k_row + fox_width
    f_row = v_row + fox_width
    mq_row = f_row + FOX_HEADS
    gate_row = mq_row + mem_width
    n_gate = N_BRANCHES * d
    ch_tiles = d // tile
    gate_rows = [gate_row + br * d + n * tile for n in range(ch_tiles) for br in range(N_BRANCHES)]
    rest_rows = [r for r in range(0, f_row, tile)] + [r for r in range(mq_row, gate_row, tile)]
    row_starts = tuple(gate_rows + rest_rows)
    conv_col, q_col, k_col, v_col, mq_col = (n_gate + r for r in (0, q_row, k_row, v_row, f_row))
    kinds = ([GATE] * len(gate_rows) + [PLAIN] * (q_row // tile) + [FOX_HEAD_DIM] * (2 * fox_width // tile)
             + [PLAIN] * (fox_width // tile) + [MEM_HEAD_DIM] * (mem_width // tile))
    gains = jnp.ones((len(row_starts) * tile,), F32)
    gains = gains.at[q_col:k_col].set(jnp.tile(fox_q_g * (LOG2_E / math.sqrt(FOX_HEAD_DIM)), FOX_HEADS))
    gains = gains.at[k_col:v_col].set(jnp.tile(fox_k_g, FOX_HEADS))
    gains = gains.at[mq_col:].set(jnp.tile(mem_q_g * (1.0 / math.sqrt(MEM_HEAD_DIM)), MEM_HEADS))

    row = lambda v: v.reshape(1, -1)
    bf = jnp.pad(b_f, (0, LANES - FOX_HEADS)).reshape(1, LANES)
    w_t = w_in.T

    h1, qa, ka = _norm1(x, row(norm1_g), w_t, f_row, bf)
    proj, w_up_bf, w_down_bf = _proj(h1.reshape(m, d), w_t, row_starts, kinds, gains.reshape(-1, tile),
                                     w_up, w_down)
    proj3 = proj.reshape(b, s, -1)

    y_fox = _fox(proj3, qa, ka, q_col, k_col, v_col)
    kv = _memkv(mem.reshape(-1, d), row(mem_norm_g), w_mem_kv, row(jnp.tile(mem_k_g, MEM_HEADS)))
    y_mem = _memattn(proj3, mq_col, kv.reshape(b, -1, 2 * mem_width))

    merged = _merge(proj, conv_col, conv_w, y_fox.reshape(m, -1), y_mem.reshape(m, -1),
                    w_conv_out, w_fox_out, w_mem_out, s, d)
    x2, h2 = _outproj(merged, w_out, x.reshape(m, d), row(norm2_g))
    out = _mlp(h2, w_up_bf, w_down_bf, x2)
    return out.reshape(b, s, d)
```

```python
import functools
import math

import jax
import jax.numpy as jnp
import numpy as np
from jax import lax
from jax.experimental import pallas as pl
from jax.experimental.pallas import tpu as pltpu

F32 = jnp.float32
BF16 = jnp.bfloat16

EPS = 1e-6
LOG2_E = math.log2(math.e)
LANES = 128
SUBLANES = 8
BF16_ROWS = 2 * SUBLANES
CONV_TAPS = 3
FOX_HEADS = 8
FOX_HEAD_DIM = 128
MEM_HEADS = 4
MEM_HEAD_DIM = 256
N_BRANCHES = 3

VMEM_LIMIT_BYTES = 56 * 1024 * 1024
COL_TILE = 1024
NT_DIMS = (((1,), (1,)), ((), ()))


def _params(n_axes):
    return pltpu.CompilerParams(
        dimension_semantics=("arbitrary",) * n_axes,
        vmem_limit_bytes=VMEM_LIMIT_BYTES)


def _store_group_rms(o_ref, a, gain, width):
    for s in range(0, a.shape[1], width):
        blk = a[:, s:s + width]
        ms = jnp.mean(blk * blk, axis=-1, keepdims=True)
        o_ref[:, s:s + width] = (blk * lax.rsqrt(ms + EPS) * gain[:, s:s + width]).astype(o_ref.dtype)


def _split3(c):
    hi = c.astype(BF16).astype(F32)
    r1 = c - hi
    mid = r1.astype(BF16).astype(F32)
    lo = (r1 - mid).astype(BF16).astype(F32)
    return hi, mid, lo


ONE_LANE = 3 * FOX_HEADS


def _decay_selector():
    sel = np.zeros((LANES, FOX_HEADS, LANES), np.float32)
    for hd in range(FOX_HEADS):
        for piece in range(3):
            sel[ONE_LANE, hd, piece * FOX_HEADS + hd] = 1.0
            sel[piece * FOX_HEADS + hd, hd, ONE_LANE + piece] = -1.0
    return jnp.asarray(sel.reshape(LANES, -1), BF16)


def _norm1_kernel(x_ref, g_ref, wf_ref, bf_ref, sel_ref, h_ref, qa_ref, ka_ref, carry_ref):
    tm = x_ref.shape[0]

    @pl.when(pl.program_id(1) == 0)
    def _():
        carry_ref[...] = jnp.zeros_like(carry_ref)

    wf = wf_ref[...]
    wf_hi = wf.astype(BF16).astype(F32)
    w2 = jnp.concatenate(
        [wf_hi, wf - wf_hi, jnp.zeros((LANES - 2 * FOX_HEADS, wf.shape[1]), F32)], axis=0).astype(BF16)
    row = lax.broadcasted_iota(jnp.int32, (CUMSUM_ROWS, CUMSUM_ROWS), 0)
    col = lax.broadcasted_iota(jnp.int32, (CUMSUM_ROWS, CUMSUM_ROWS), 1)
    tri = jnp.where(row >= col, 1.0, 0.0).astype(BF16)
    lane = lax.broadcasted_iota(jnp.int32, (CUMSUM_ROWS, LANES), 1)

    def pack3(hi, mid, lo, rest):
        return jnp.where(
            lane < FOX_HEADS, hi,
            jnp.where(lane < 2 * FOX_HEADS, pltpu.roll(mid, FOX_HEADS, 1),
                      jnp.where(lane < ONE_LANE, pltpu.roll(lo, 2 * FOX_HEADS, 1), rest))).astype(BF16)

    blocks = [pl.ds(blk * CUMSUM_ROWS, CUMSUM_ROWS) for blk in range(tm // CUMSUM_ROWS)]

    def logits(rows):
        xf = x_ref[rows, :]
        ms = jnp.mean(xf * xf, axis=-1, keepdims=True)
        y_hi = (xf * lax.rsqrt(ms + EPS) * g_ref[...]).astype(BF16)
        h_ref[rows, :] = y_hi
        return lax.dot_general(y_hi, w2, NT_DIMS, preferred_element_type=F32)

    def local_cumsum(z2):
        z = z2 + pltpu.roll(z2, LANES - FOX_HEADS, 1) + bf_ref[...]
        log_f = -LOG2_E * (jnp.maximum(-z, 0.0) + jnp.log1p(jnp.exp(-jnp.abs(z))))
        return jnp.dot(tri, pack3(*_split3(log_f), 0.0), preferred_element_type=F32)

    def decay_columns(rows, loc, run):
        c = (loc + pltpu.roll(loc, LANES - FOX_HEADS, 1)
             + pltpu.roll(loc, LANES - 2 * FOX_HEADS, 1) + run)
        packed = pack3(*_split3(c), jnp.where(lane < ONE_LANE + 3, 1.0, 0.0))
        qa_ref[rows, :] = packed
        ka_ref[rows, :] = jnp.dot(packed, sel_ref[...], preferred_element_type=F32).astype(BF16)
        return c[CUMSUM_ROWS - 1:CUMSUM_ROWS, :]

    z2s = [logits(rows) for rows in blocks]
    locs = [local_cumsum(z2) for z2 in z2s]
    run = carry_ref[0:1, :]
    for rows, loc in zip(blocks, locs):
        run = decay_columns(rows, loc, run)
    carry_ref[0:1, :] = run


def _norm1(x, g, w_t, f_row, bf, tm=1024):
    b, s, d = x.shape
    return pl.pallas_call(
        _norm1_kernel,
        grid=(b, s // tm),
        in_specs=[
            pl.BlockSpec((None, tm, d), lambda i, j: (i, j, 0)),
            pl.BlockSpec((1, d), lambda i, j: (0, 0)),
            pl.BlockSpec((FOX_HEADS, d), lambda i, j: (f_row // FOX_HEADS, 0)),
            pl.BlockSpec((1, LANES), lambda i, j: (0, 0)),
            pl.BlockSpec((LANES, FOX_HEADS * LANES), lambda i, j: (0, 0)),
        ],
        out_specs=[
            pl.BlockSpec((None, tm, d), lambda i, j: (i, j, 0)),
            pl.BlockSpec((None, tm, LANES), lambda i, j: (i, j, 0)),
            pl.BlockSpec((None, tm, FOX_HEADS * LANES), lambda i, j: (i, j, 0)),
        ],
        out_shape=[
            jax.ShapeDtypeStruct((b, s, d), BF16),
            jax.ShapeDtypeStruct((b, s, LANES), BF16),
            jax.ShapeDtypeStruct((b, s, FOX_HEADS * LANES), BF16),
        ],
        scratch_shapes=[pltpu.VMEM((SUBLANES, LANES), F32)],
        compiler_params=_params(2),
        name="norm1_decay",
    )(x, g, w_t, bf, _decay_selector())


CUMSUM_ROWS = 256
PLAIN, GATE = 0, -1


def _proj_kernel(h_ref, w_ref, gain_ref, wu_ref, wd_ref, o_ref, wubf_ref, wdbf_ref, wbf_ref, *,
                 kinds, cast_tiles):
    @pl.when(pl.program_id(1) == 0)
    def _():
        wbf_ref[...] = w_ref[...].astype(BF16)

    def tile(kind, cast):
        acc = lax.dot_general(h_ref[...], wbf_ref[...], NT_DIMS, preferred_element_type=F32)
        if cast:
            wubf_ref[...] = wu_ref[...].astype(BF16)
            wdbf_ref[...] = wd_ref[...].astype(BF16)
        if kind == PLAIN:
            o_ref[...] = acc.astype(BF16)
        elif kind == GATE:
            o_ref[...] = (0.5 * jnp.tanh(0.5 * acc) + 0.5).astype(BF16)
        else:
            _store_group_rms(o_ref, acc, gain_ref[...], kind)

    j = pl.program_id(0)
    jobs = [(kind, t >= len(kinds) - cast_tiles) for t, kind in enumerate(kinds)]
    for job in sorted(set(jobs)):
        hit = functools.reduce(jnp.logical_or, [j == t for t, jb in enumerate(jobs) if jb == job])
        pl.when(hit)(functools.partial(tile, *job))


def _proj(h, w_t, row_starts, kinds, gains, w_up, w_down, tm=1024, tn=COL_TILE, cast_tiles=8):
    m, k = h.shape
    n_tiles = len(row_starts)
    assert all(r % SUBLANES == 0 for r in row_starts)
    cast_steps = cast_tiles * (m // tm)
    up_rows, down_rows = w_up.shape[0] // cast_steps, w_down.shape[0] // cast_steps
    assert cast_tiles <= n_tiles and up_rows % BF16_ROWS == 0 and down_rows % BF16_ROWS == 0
    assert up_rows * cast_steps == w_up.shape[0] and down_rows * cast_steps == w_down.shape[0]

    def w_rows(j, i):
        start = jnp.int32(row_starts[0])
        for t in range(1, n_tiles):
            start = jnp.where(j == t, row_starts[t], start)
        return (pl.multiple_of(start, SUBLANES), 0)

    first = n_tiles - cast_tiles
    slab = lambda j, i: (jnp.where(j >= first, (j - first) * (m // tm) + i, 0), 0)
    up_spec = pl.BlockSpec((up_rows, w_up.shape[1]), slab)
    down_spec = pl.BlockSpec((down_rows, w_down.shape[1]), slab)
    return pl.pallas_call(
        functools.partial(_proj_kernel, kinds=tuple(kinds), cast_tiles=cast_tiles),
        grid=(n_tiles, m // tm),
        in_specs=[
            pl.BlockSpec((tm, k), lambda j, i: (i, 0)),
            pl.BlockSpec((pl.Element(tn), pl.Element(k)), w_rows),
            pl.BlockSpec((None, 1, tn), lambda j, i: (j, 0, 0)),
            up_spec, down_spec,
        ],
        out_specs=[pl.BlockSpec((tm, tn), lambda j, i: (i, j)), up_spec, down_spec],
        out_shape=[jax.ShapeDtypeStruct((m, n_tiles * tn), BF16),
                   jax.ShapeDtypeStruct(w_up.shape, BF16), jax.ShapeDtypeStruct(w_down.shape, BF16)],
        scratch_shapes=[pltpu.VMEM((tn, k), BF16)],
        compiler_params=_params(2),
        name="in_proj",
    )(h, w_t, gains.reshape(n_tiles, 1, tn), w_up, w_down)


def _gated_conv_tile(conv_ref, w_ref, carry_ref):
    tm = conv_ref.shape[0]
    k = w_ref.shape[1]
    u = conv_ref[:, k:2 * k].astype(F32) * conv_ref[:, 2 * k:3 * k].astype(F32)
    w = w_ref[...]
    taps = lambda u2, u1, u0: w[0:1, :] * u2 + w[1:2, :] * u1 + w[2:3, :] * u0
    conv = taps(pltpu.roll(u, 2, 0), pltpu.roll(u, 1, 0), u)
    top = u[0:SUBLANES, :]
    row = lax.broadcasted_iota(jnp.int32, (SUBLANES, 1), 0)
    prev1 = carry_ref[1:2, :]
    prev2 = carry_ref[0:1, :]
    top1 = jnp.where(row == 0, prev1, pltpu.roll(top, 1, 0))
    top2 = jnp.where(row == 0, prev2, jnp.where(row == 1, prev1, pltpu.roll(top, 2, 0)))
    conv = jnp.concatenate([taps(top2, top1, top), conv[SUBLANES:, :]], axis=0)
    carry_ref[0:2, :] = u[tm - 2:tm, :]
    return conv_ref[:, 0:k].astype(F32) * conv


def _fox_kernel(q_ref, qa_ref, k_ref, ka_ref, v_ref, o_ref, vt_ref, s_ref, m_ref, l_ref, acc_ref, *,
                t):
    n_tiles = q_ref.shape[0] // t
    assert n_tiles % 2 == 0
    d = FOX_HEAD_DIM
    n_heads = q_ref.shape[1] // d
    j = pl.program_id(2)

    @pl.when(j == 0)
    def _():
        for g in range(n_heads):
            for blk in range(v_ref.shape[0] // t):
                v = v_ref[blk * t:(blk + 1) * t, g * d:(g + 1) * d]
                vt_ref[g, blk, :d] = v.astype(F32).T.astype(BF16)
                vt_ref[g, blk, d:] = jnp.ones((vt_ref.shape[2] - d, t), BF16)

    def scores(tile, kb, g):
        q = jnp.concatenate([q_ref[tile * t:(tile + 1) * t, g * d:(g + 1) * d],
                             qa_ref[tile * t:(tile + 1) * t, :]], axis=1)
        rows = pl.ds(pl.multiple_of(kb * t, t), t)
        k = jnp.concatenate([k_ref[rows, g * d:(g + 1) * d], ka_ref[rows, g * d:(g + 1) * d]], axis=1)
        return lax.dot_general(k, q, NT_DIMS, preferred_element_type=F32)

    def reset(g):
        m_ref[g] = jnp.full(m_ref.shape[1:], -jnp.inf, F32)
        l_ref[g] = jnp.zeros(l_ref.shape[1:], F32)
        acc_ref[g] = jnp.zeros(acc_ref.shape[1:], F32)

    def update(g, m_new, alpha, pv):
        l_ref[g] = alpha * l_ref[g] + pv[d:d + 1]
        acc_ref[g] = alpha * acc_ref[g] + pv[:d]
        m_ref[g] = m_new

    def consume(kb, slot, g):
        s = s_ref[slot, g]
        m_prev = m_ref[g]
        m_new = jnp.maximum(m_prev, jnp.max(s, axis=0, keepdims=True))
        p = jnp.exp2(s - m_new)
        pv = jnp.dot(vt_ref[g, kb], p.astype(BF16), preferred_element_type=F32)
        update(g, m_new, jnp.exp2(m_prev - m_new), pv)

    def causal(s):
        kpos = lax.broadcasted_iota(jnp.int32, s.shape, 0)
        qpos = lax.broadcasted_iota(jnp.int32, s.shape, 1)
        return jnp.where(qpos >= kpos, s, -jnp.inf)

    def consume_diagonal(kb, slot, g):
        h = t // 2
        s_up = causal(s_ref[slot, g, :h, :])
        s_lo = causal(s_ref[slot, g, h:, h:])
        m_up = jnp.max(s_up, axis=0, keepdims=True)
        m_blk = jnp.concatenate(
            [m_up[:, :h], jnp.maximum(m_up[:, h:], jnp.max(s_lo, axis=0, keepdims=True))], axis=1)
        m_prev = m_ref[g]
        m_new = jnp.maximum(m_prev, m_blk)
        p_up = jnp.exp2(s_up - m_new).astype(BF16)
        p_lo = jnp.exp2(s_lo - m_new[:, h:]).astype(BF16)
        vt = vt_ref[g, kb]
        pv_up = jnp.dot(vt[:, :h], p_up, preferred_element_type=F32)
        pv_lo = jnp.dot(vt[:, h:], p_lo, preferred_element_type=F32)
        pv = jnp.concatenate([pv_up[:, :h], pv_up[:, h:] + pv_lo], axis=1)
        update(g, m_new, jnp.exp2(m_prev - m_new), pv)

    def advance(tile, kb, slot):
        for g in range(n_heads):
            s_ref[1 - slot, g] = scores(tile, kb + 1, g)
            consume(kb, slot, g)

    def emit(tile, g):
        o_ref[tile * t:(tile + 1) * t, g * d:(g + 1) * d] = (acc_ref[g] / l_ref[g]).T.astype(BF16)

    slot = 0
    for g in range(n_heads):
        reset(g)
        s_ref[slot, g] = scores(0, 0, g)
    for r in range(n_tiles):
        def pairs(pair, carry, r=r, slot=slot):
            advance(r, 2 * pair, slot)
            advance(r, 2 * pair + 1, 1 - slot)
            return carry

        lax.fori_loop(0, (n_tiles // 2) * j + r // 2, pairs, 0)
        diagonal = n_tiles * j + r
        if r % 2 == 1:
            advance(r, diagonal - 1, slot)
            slot = 1 - slot
        for g in range(n_heads):
            if r + 1 < n_tiles:
                s_ref[1 - slot, g] = scores(r + 1, 0, g)
            consume_diagonal(diagonal, slot, g)
            emit(r, g)
            if r + 1 < n_tiles:
                reset(g)
        slot = 1 - slot


def _fox(proj, qa, ka, q_col, k_col, v_col, t=512, tiles_per_step=2, heads_per_step=4):
    b, s, _ = proj.shape
    d = FOX_HEAD_DIM
    w = heads_per_step * d
    rows = tiles_per_step * t
    tiles = lambda col0: pl.BlockSpec((None, rows, w), lambda bi, h, j: (bi, j, col0 // w + h))
    seq = lambda col0: pl.BlockSpec((None, s, w), lambda bi, h, j: (bi, 0, col0 // w + h))
    return pl.pallas_call(
        functools.partial(_fox_kernel, t=t),
        grid=(b, FOX_HEADS // heads_per_step, s // rows),
        in_specs=[tiles(q_col), pl.BlockSpec((None, rows, LANES), lambda bi, h, j: (bi, j, 0)),
                  seq(k_col), seq(0), seq(v_col)],
        out_specs=tiles(0),
        out_shape=jax.ShapeDtypeStruct((b, s, FOX_HEADS * d), BF16),
        scratch_shapes=[pltpu.VMEM((heads_per_step, s // t, d + BF16_ROWS, t), BF16),
                        pltpu.VMEM((2, heads_per_step, t, t), F32),
                        pltpu.VMEM((heads_per_step, 1, t), F32),
                        pltpu.VMEM((heads_per_step, 1, t), F32),
                        pltpu.VMEM((heads_per_step, d, t), F32)],
        compiler_params=_params(3),
        name="fox_attention",
    )(proj, qa, proj, ka, proj)


def _memkv_kernel(mem_ref, g_ref, w_ref, kg_ref, o_ref):
    xf = mem_ref[...]
    ms = jnp.mean(xf * xf, axis=-1, keepdims=True)
    y = (xf * lax.rsqrt(ms + EPS) * g_ref[...]).astype(BF16)
    acc = jnp.dot(y, w_ref[...].astype(BF16), preferred_element_type=F32)

    @pl.when(pl.program_id(0) == 0)
    def _():
        _store_group_rms(o_ref, acc, kg_ref[...], MEM_HEAD_DIM)

    @pl.when(pl.program_id(0) == 1)
    def _():
        o_ref[...] = acc.astype(BF16)


def _memkv(mem2d, g, w, kg):
    m, d = mem2d.shape
    width = MEM_HEADS * MEM_HEAD_DIM
    return pl.pallas_call(
        _memkv_kernel,
        grid=(2,),
        in_specs=[
            pl.BlockSpec((m, d), lambda j: (0, 0)),
            pl.BlockSpec((1, d), lambda j: (0, 0)),
            pl.BlockSpec((d, width), lambda j: (0, j)),
            pl.BlockSpec((1, width), lambda j: (0, 0)),
        ],
        out_specs=pl.BlockSpec((m, width), lambda j: (0, j)),
        out_shape=jax.ShapeDtypeStruct((m, 2 * width), BF16),
        compiler_params=_params(1),
        name="mem_kv",
    )(mem2d, g, w, kg)


def _memattn_kernel(q_ref, kv_ref, o_ref):
    width = MEM_HEADS * MEM_HEAD_DIM
    for h in range(MEM_HEADS):
        lo, hi = h * MEM_HEAD_DIM, (h + 1) * MEM_HEAD_DIM
        q = q_ref[:, lo:hi]
        k = kv_ref[:, lo:hi]
        v = kv_ref[:, width + lo:width + hi]
        s = lax.dot_general(q, k, NT_DIMS, preferred_element_type=F32)
        p = jnp.exp(s - jnp.max(s, axis=-1, keepdims=True))
        l = jnp.sum(p, axis=-1, keepdims=True)
        o = jnp.dot(p.astype(BF16), v, preferred_element_type=F32) / l
        o_ref[:, lo:hi] = o.astype(BF16)


def _memattn(proj, q_col, kv, tm=1024):
    b, s, _ = proj.shape
    width = MEM_HEADS * MEM_HEAD_DIM
    n_mem = kv.shape[1]
    return pl.pallas_call(
        _memattn_kernel,
        grid=(b, s // tm),
        in_specs=[
            pl.BlockSpec((None, tm, width), lambda i, j: (i, j, q_col // width)),
            pl.BlockSpec((None, n_mem, 2 * width), lambda i, j: (i, 0, 0)),
        ],
        out_specs=pl.BlockSpec((None, tm, width), lambda i, j: (i, j, 0)),
        out_shape=jax.ShapeDtypeStruct((b, s, width), BF16),
        compiler_params=_params(2),
        name="mem_attention",
    )(proj, kv)


def _merge_kernel(conv_ref, cw_ref, b_ref, c_ref, wa_ref, wb_ref, wc_ref, g_ref,
                  o_ref, wabf_ref, wbbf_ref, wcbf_ref, carry_ref, *, tiles_per_seq):
    i = pl.program_id(1)
    tn = o_ref.shape[1]

    @pl.when(i == 0)
    def _():
        wabf_ref[...] = wa_ref[...].astype(BF16)
        wbbf_ref[...] = wb_ref[...].astype(BF16)
        wcbf_ref[...] = wc_ref[...].astype(BF16)

    @pl.when(i % tiles_per_seq == 0)
    def _():
        carry_ref[...] = jnp.zeros_like(carry_ref)

    y_conv = _gated_conv_tile(conv_ref, cw_ref, carry_ref).astype(BF16)
    oa = jnp.dot(y_conv, wabf_ref[...], preferred_element_type=F32)
    ob = jnp.dot(b_ref[...], wbbf_ref[...], preferred_element_type=F32)
    oc = jnp.dot(c_ref[...], wcbf_ref[...], preferred_element_type=F32)
    gate = lambda br: g_ref[:, br * tn:(br + 1) * tn].astype(F32)
    o_ref[...] = (gate(0) * oa + gate(1) * ob + gate(2) * oc).astype(BF16)


def _merge(proj, conv_col, conv_w, yb, yc, wa, wb, wc, seq_len, d_model, tm=512, tn=COL_TILE):
    m, k = yb.shape
    lhs = pl.BlockSpec((tm, k), lambda j, i: (i, 0))
    wsp = pl.BlockSpec((k, tn), lambda j, i: (0, j))
    return pl.pallas_call(
        functools.partial(_merge_kernel, tiles_per_seq=seq_len // tm),
        grid=(d_model // tn, m // tm),
        in_specs=[pl.BlockSpec((tm, 3 * k), lambda j, i: (i, conv_col // (3 * k))),
                  pl.BlockSpec((CONV_TAPS, k), lambda j, i: (0, 0)),
                  lhs, lhs, wsp, wsp, wsp,
                  pl.BlockSpec((tm, 3 * tn), lambda j, i: (i, j))],
        out_specs=pl.BlockSpec((tm, tn), lambda j, i: (i, j)),
        out_shape=jax.ShapeDtypeStruct((m, d_model), BF16),
        scratch_shapes=[pltpu.VMEM((k, tn), BF16)] * 3 + [pltpu.VMEM((SUBLANES, k), F32)],
        compiler_params=_params(2),
        name="branch_merge",
    )(proj, conv_w, yb, yc, wa, wb, wc, proj)


def _outproj_kernel(a_ref, w_ref, x_ref, g_ref, x2_ref, h2_ref, wbf_ref):
    @pl.when(pl.program_id(0) == 0)
    def _():
        wbf_ref[...] = w_ref[...].astype(BF16)

    x2 = x_ref[...] + jnp.dot(a_ref[...], wbf_ref[...], preferred_element_type=F32)
    x2_ref[...] = x2
    ms = jnp.mean(x2 * x2, axis=-1, keepdims=True)
    h2_ref[...] = (x2 * lax.rsqrt(ms + EPS) * g_ref[...]).astype(BF16)


def _outproj(a, w, x2d, g, tm=512):
    m, d = x2d.shape
    row = pl.BlockSpec((tm, d), lambda i: (i, 0))
    return pl.pallas_call(
        _outproj_kernel,
        grid=(m // tm,),
        in_specs=[row, pl.BlockSpec((d, d), lambda i: (0, 0), pipeline_mode=pl.Buffered(1)), row,
                  pl.BlockSpec((1, d), lambda i: (0, 0))],
        out_specs=[row, row],
        out_shape=[jax.ShapeDtypeStruct((m, d), F32), jax.ShapeDtypeStruct((m, d), BF16)],
        scratch_shapes=[pltpu.VMEM((d, d), BF16)],
        compiler_params=_params(1),
        name="out_proj_norm2",
    )(a, w, x2d, g)


def _mlp_kernel(h_ref, wu_ref, wd_ref, x_ref, o_ref):
    f = pl.program_id(1)
    rows = x_ref.shape[0]

    @pl.when(f == 0)
    def _():
        o_ref[...] = jnp.zeros_like(o_ref)

    a = jnp.dot(h_ref[...], wu_ref[...], preferred_element_type=F32)
    a = jnp.square(jnp.maximum(a, 0.0)).astype(BF16)
    o_ref[...] += jnp.dot(a, wd_ref[...], preferred_element_type=F32)
    slab = pl.ds(pl.multiple_of(f * rows, rows), rows)
    o_ref[slab, :] += x_ref[...]


def _mlp(h2, w_up, w_down, x2, tm=1024, tf=1024):
    m, d = h2.shape
    d_ff = w_up.shape[1]
    nf = d_ff // tf
    rows = tm // nf
    return pl.pallas_call(
        _mlp_kernel,
        grid=(m // tm, nf),
        in_specs=[
            pl.BlockSpec((tm, d), lambda i, f: (i, 0)),
            pl.BlockSpec((d, tf), lambda i, f: (0, f)),
            pl.BlockSpec((tf, d), lambda i, f: (f, 0)),
            pl.BlockSpec((rows, d), lambda i, f: (i * nf + f, 0)),
        ],
        out_specs=pl.BlockSpec((tm, d), lambda i, f: (i, 0)),
        out_shape=jax.ShapeDtypeStruct((m, d), F32),
        compiler_params=_params(2),
        name="relu2_mlp",
    )(h2, w_up, w_down, x2)


def kernel(x, mem, norm1_g, w_in, b_f, conv_w, fox_q_g, fox_k_g, mem_norm_g, w_mem_kv, mem_q_g, mem_k_g, w_conv_out, w_fox_out, w_mem_out, w_out, norm2_g, w_up, w_down):
    b, s, d = x.shape
    m = b * s
    conv_width = conv_w.shape[1]
    fox_width = FOX_HEADS * FOX_HEAD_DIM
    mem_width = MEM_HEADS * MEM_HEAD_DIM
    tile = COL_TILE
    q_row = 3 * conv_width
    k_row = q_row + fox_width
    v_row = k_row + fox_width
    f_row = v_row + fox_width
    mq_row = f_row + FOX_HEADS
    gate_row = mq_row + mem_width
    n_gate = N_BRANCHES * d
    ch_tiles = d // tile
    gate_rows = [gate_row + br * d + n * tile for n in range(ch_tiles) for br in range(N_BRANCHES)]
    rest_rows = [r for r in range(0, f_row, tile)] + [r for r in range(mq_row, gate_row, tile)]
    row_starts = tuple(gate_rows + rest_rows)
    conv_col, q_col, k_col, v_col, mq_col = (n_gate + r for r in (0, q_row, k_row, v_row, f_row))
    kinds = ([GATE] * len(gate_rows) + [PLAIN] * (q_row // tile) + [FOX_HEAD_DIM] * (2 * fox_width // tile)
             + [PLAIN] * (fox_width // tile) + [MEM_HEAD_DIM] * (mem_width // tile))
    gains = jnp.ones((len(row_starts) * tile,), F32)
    gains = gains.at[q_col:k_col].set(jnp.tile(fox_q_g * (LOG2_E / math.sqrt(FOX_HEAD_DIM)), FOX_HEADS))
    gains = gains.at[k_col:v_col].set(jnp.tile(fox_k_g, FOX_HEADS))
    gains = gains.at[mq_col:].set(jnp.tile(mem_q_g * (1.0 / math.sqrt(MEM_HEAD_DIM)), MEM_HEADS))

    row = lambda v: v.reshape(1, -1)
    bf = jnp.pad(b_f, (0, LANES - FOX_HEADS)).reshape(1, LANES)
    w_t = w_in.T

    h1, qa, ka = _norm1(x, row(norm1_g), w_t, f_row, bf)
    proj, w_up_bf, w_down_bf = _proj(h1.reshape(m, d), w_t, row_starts, kinds, gains.reshape(-1, tile),
                                     w_up, w_down)
    proj3 = proj.reshape(b, s, -1)

    y_fox = _fox(proj3, qa, ka, q_col, k_col, v_col)
    kv = _memkv(mem.reshape(-1, d), row(mem_norm_g), w_mem_kv, row(jnp.tile(mem_k_g, MEM_HEADS)))
    y_mem = _memattn(proj3, mq_col, kv.reshape(b, -1, 2 * mem_width))

    merged = _merge(proj, conv_col, conv_w, y_fox.reshape(m, -1), y_mem.reshape(m, -1),
                    w_conv_out, w_fox_out, w_mem_out, s, d)
    x2, h2 = _outproj(merged, w_out, x.reshape(m, d), row(norm2_g))
    out = _mlp(h2, w_up_bf, w_down_bf, x2)
    return out.reshape(b, s, d)
```
